```python
import jax, jax.numpy as jnp
from jax import lax
import numpy as np

D_MODEL = 4096
BATCH = 1
SEQ = 16384
DEPTH = 2

D_MIX = D_MODEL
GROUP_W = D_MIX // 4

GLA_HEADS = 4
GLA_DK = GROUP_W // (2 * GLA_HEADS)
GLA_DV = GROUP_W // GLA_HEADS
GLA_GATE_RANK = 16
GLA_TAU = 16.0
GLA_CHUNK = 64

GM_GROUPS = 8
GM_CH = GROUP_W // GM_GROUPS
GM_CHUNK = 128

SC_WIDTH = 3

MLA_HEADS = 8
MLA_NOPE = 128
MLA_ROPE = 64
MLA_V = GROUP_W // MLA_HEADS
MLA_Q_LORA = 768
MLA_KV_LORA = 256
ROPE_BASE = 10000.0
ATTN_BLOCK = 128

MOE_GROUPS = 4
MOE_PER_GROUP = 8
N_EXPERTS = MOE_GROUPS * MOE_PER_GROUP
MOE_TOPK = 2
D_EXPERT = 512

ALPHA = (2.0 * DEPTH) ** 0.25
BETA = (8.0 * DEPTH) ** -0.25

PROJ_SIZES = (
    GLA_HEADS * GLA_DK,
    GLA_HEADS * GLA_DK,
    GLA_HEADS * GLA_DV,
    GROUP_W,
    GLA_GATE_RANK,
    2 * GROUP_W,
    3 * GROUP_W,
    MLA_Q_LORA,
    MLA_KV_LORA,
    MLA_ROPE,
)
D_IN = 2 * GLA_HEADS * GLA_DK + GLA_HEADS * GLA_DV + GROUP_W + GLA_GATE_RANK + 2 * GROUP_W + 3 * GROUP_W + MLA_Q_LORA + MLA_KV_LORA + MLA_ROPE

kernel_name = "hymba_style_gla_gmlp_conv_mla_hmoe_deepnorm"

F32 = jnp.float32


def rms_norm(t, g, eps=1e-6):
    tf = t.astype(F32)
    y = tf * lax.rsqrt(jnp.mean(tf * tf, axis=-1, keepdims=True) + eps) * g.astype(F32)
    return y.astype(t.dtype)


def layer_norm(t, g, b, eps=1e-5):
    tf = t.astype(F32)
    mu = jnp.mean(tf, axis=-1, keepdims=True)
    d = tf - mu
    var = jnp.mean(d * d, axis=-1, keepdims=True)
    y = d * lax.rsqrt(var + eps) * g.astype(F32) + b.astype(F32)
    return y.astype(t.dtype)


def split_cols(p, sizes):
    offs = []
    acc = 0
    for s in sizes[:-1]:
        acc += s
        offs.append(acc)
    return jnp.split(p, offs, axis=-1)


def gla_mixer(q, k, v, r, g_low, wa2, ba, norm_g):
    bsz, s_len, _ = q.shape
    nc = s_len // GLA_CHUNK

    def to_chunks(t, d):
        return t.astype(F32).reshape(bsz, nc, GLA_CHUNK, GLA_HEADS, d).transpose(1, 0, 3, 2, 4)

    log_gate = jax.nn.log_sigmoid((g_low @ wa2 + ba).astype(F32)) / GLA_TAU
    qc = to_chunks(q, GLA_DK) * (GLA_DK ** -0.5)
    kc = to_chunks(k, GLA_DK)
    vc = to_chunks(v, GLA_DV)
    gc = to_chunks(log_gate, GLA_DK)
    causal = jnp.tril(jnp.ones((GLA_CHUNK, GLA_CHUNK), dtype=bool))

    def step(state, inp):
        qi, ki, vi, gi = inp
        b = jnp.cumsum(gi, axis=2)
        o_inter = jnp.einsum('bhck,bhkv->bhcv', qi * jnp.exp(b), state)
        diff = b[:, :, :, None, :] - b[:, :, None, :, :]
        decay = jnp.exp(jnp.where(causal[:, :, None], diff, -jnp.inf))
        att = jnp.einsum('bhik,bhjk,bhijk->bhij', qi, ki, decay)
        o_intra = jnp.einsum('bhij,bhjv->bhiv', att, vi)
        b_last = b[:, :, -1:, :]
        new_state = jnp.exp(b_last[:, :, 0, :])[..., None] * state + jnp.einsum(
            'bhjk,bhjv->bhkv', ki * jnp.exp(b_last - b), vi)
        return new_state, o_inter + o_intra

    s0 = jnp.zeros((bsz, GLA_HEADS, GLA_DK, GLA_DV), F32)
    _, o = lax.scan(step, s0, (qc, kc, vc, gc))
    o = o.transpose(1, 0, 3, 2, 4).reshape(bsz, s_len, GLA_HEADS, GLA_DV)
    o = rms_norm(o, norm_g.reshape(GLA_HEADS, GLA_DV))
    return (o.reshape(bsz, s_len, GROUP_W) * jax.nn.silu(r.astype(F32))).astype(q.dtype)


def gmlp_mixer(z, ln_g, ln_b, ws, bs, norm_g):
    bsz, s_len, _ = z.shape
    z = jax.nn.gelu(z, approximate=False)
    u, v = jnp.split(z, 2, axis=-1)
    v = layer_norm(v, ln_g, ln_b)
    nc = s_len // GM_CHUNK
    vb = v.reshape(bsz, nc, GM_CHUNK, GM_GROUPS, GM_CH)
    w = ws * jnp.tril(jnp.ones((GM_CHUNK, GM_CHUNK), ws.dtype))
    mixed = jnp.einsum('gts,bnsgc->bntgc', w, vb) + bs.T[:, :, None]
    out = u * mixed.reshape(bsz, s_len, GROUP_W)
    return rms_norm(out, norm_g)


def short_conv_mixer(p, conv_w, norm_g):
    b_gate, c_gate, h = jnp.split(p, 3, axis=-1)
    z = c_gate * h
    y = lax.conv_general_dilated(
        z, conv_w[:, None, :].astype(z.dtype), window_strides=(1,),
        padding=((SC_WIDTH - 1, 0),), dimension_numbers=('NWC', 'WIO', 'NWC'),
        feature_group_count=GROUP_W)
    return rms_norm(b_gate * y, norm_g)


def apply_rope(t, cos, sin):
    t1, t2 = jnp.split(t, 2, axis=-1)
    return jnp.concatenate([t1 * cos - t2 * sin, t1 * sin + t2 * cos], axis=-1)


def mla_mixer(c_q, c_kv, k_rope, q_norm, kv_norm, wuq, wukv, norm_g):
    bsz, s_len, _ = c_q.shape
    dt = c_q.dtype
    q = (rms_norm(c_q, q_norm) @ wuq).reshape(bsz, s_len, MLA_HEADS, MLA_NOPE + MLA_ROPE)
    q_nope, q_rope = q[..., :MLA_NOPE], q[..., MLA_NOPE:]
    kv = (rms_norm(c_kv, kv_norm) @ wukv).reshape(bsz, s_len, MLA_HEADS, MLA_NOPE + MLA_V)
    k_nope, v = kv[..., :MLA_NOPE], kv[..., MLA_NOPE:]

    pos = jnp.arange(s_len, dtype=F32)
    inv_freq = ROPE_BASE ** (-jnp.arange(0, MLA_ROPE, 2, dtype=F32) / MLA_ROPE)
    ang = pos[:, None] * inv_freq[None, :]
    cos, sin = jnp.cos(ang), jnp.sin(ang)
    q_rope = apply_rope(q_rope.astype(F32), cos[:, None, :], sin[:, None, :]).astype(dt)
    k_rope = apply_rope(k_rope.astype(F32), cos, sin).astype(dt)

    scale = (MLA_NOPE + MLA_ROPE) ** -0.5
    nb = s_len // ATTN_BLOCK
    qn_b = q_nope.reshape(bsz, nb, ATTN_BLOCK, MLA_HEADS, MLA_NOPE).transpose(1, 0, 3, 2, 4)
    qr_b = q_rope.reshape(bsz, nb, ATTN_BLOCK, MLA_HEADS, MLA_ROPE).transpose(1, 0, 3, 2, 4)
    k_pos = jnp.arange(s_len)

    def block(args):
        qn, qr, blk = args
        s = jnp.einsum('bhqd,bkhd->bhqk', qn, k_nope) + jnp.einsum('bhqd,bkd->bhqk', qr, k_rope)
        s = s.astype(F32) * scale
        q_pos = blk * ATTN_BLOCK + jnp.arange(ATTN_BLOCK)
        s = jnp.where(k_pos[None, :] <= q_pos[:, None], s, -jnp.inf)
        p = jax.nn.softmax(s, axis=-1).astype(v.dtype)
        return jnp.einsum('bhqk,bkhv->bqhv', p, v)

    o = lax.map(block, (qn_b, qr_b, jnp.arange(nb)))
    o = o.transpose(1, 0, 2, 3, 4).reshape(bsz, s_len, GROUP_W)
    return rms_norm(o, norm_g)


def hier_moe(h, wrg, brg, wre, bre, w_gate, w_up, w_down):
    bsz, s_len, _ = h.shape
    lg = (h @ wrg).astype(F32) + brg.astype(F32)
    pg = jax.nn.softmax(lg, axis=-1)
    _, gsel = lax.top_k(lg, 1)
    pg_sel = jnp.take_along_axis(pg, gsel, axis=-1)
    le = ((h @ wre).astype(F32) + bre.astype(F32)).reshape(bsz, s_len, MOE_GROUPS, MOE_PER_GROUP)
    gidx = jnp.broadcast_to(gsel[..., None], (bsz, s_len, 1, MOE_PER_GROUP))
    le_sel = jnp.take_along_axis(le, gidx, axis=2)[:, :, 0, :]
    top_v, top_i = lax.top_k(le_sel, MOE_TOPK)
    w = jax.nn.softmax(top_v, axis=-1) * pg_sel
    eid = gsel * MOE_PER_GROUP + top_i
    gates = jnp.sum(jax.nn.one_hot(eid, N_EXPERTS, dtype=F32) * w[..., None], axis=-2).astype(h.dtype)
    y = jnp.zeros_like(h)
    for e in range(N_EXPERTS):
        a = jax.nn.silu(h @ w_gate[e]) * (h @ w_up[e])
        y = y + gates[..., e:e + 1] * (a @ w_down[e])
    return y


def setup_inputs(seed: int = 0) -> dict:
    key = jax.random.key(seed)
    ks = jax.random.split(key, 29)
    L = DEPTH

    def nrm(k, shape, scale):
        return jax.random.normal(k, shape, F32) * scale

    def gain(k, shape):
        return 1.0 + 0.02 * jax.random.normal(k, shape, F32)

    return {
        "x": nrm(ks[0], (BATCH, SEQ, D_MODEL), 1.0),
        "w_in": nrm(ks[1], (L, D_MODEL, D_IN), D_MODEL ** -0.5),
        "gla_wa2": nrm(ks[2], (L, GLA_GATE_RANK, GLA_HEADS * GLA_DK), GLA_GATE_RANK ** -0.5),
        "gla_ba": nrm(ks[3], (L, GLA_HEADS * GLA_DK), 0.1),
        "gla_norm": gain(ks[4], (L, GROUP_W)),
        "gm_ln_g": gain(ks[5], (L, GROUP_W)),
        "gm_ln_b": nrm(ks[6], (L, GROUP_W), 0.02),
        "gm_ws": nrm(ks[7], (L, GM_GROUPS, GM_CHUNK, GM_CHUNK), GM_CHUNK ** -0.5),
        "gm_bs": gain(ks[8], (L, GM_GROUPS, GM_CHUNK)),
        "gm_norm": gain(ks[9], (L, GROUP_W)),
        "sc_conv": nrm(ks[10], (L, SC_WIDTH, GROUP_W), SC_WIDTH ** -0.5),
        "sc_norm": gain(ks[11], (L, GROUP_W)),
        "mla_q_norm": gain(ks[12], (L, MLA_Q_LORA)),
        "mla_kv_norm": gain(ks[13], (L, MLA_KV_LORA)),
        "mla_wuq": nrm(ks[14], (L, MLA_Q_LORA, MLA_HEADS * (MLA_NOPE + MLA_ROPE)), MLA_Q_LORA ** -0.5),
        "mla_wukv": nrm(ks[15], (L, MLA_KV_LORA, MLA_HEADS * (MLA_NOPE + MLA_V)), MLA_KV_LORA ** -0.5),
        "mla_norm": gain(ks[16], (L, GROUP_W)),
        "w_o": nrm(ks[17], (L, D_MIX, D_MODEL), BETA * D_MIX ** -0.5),
        "ln1_g": gain(ks[18], (L, D_MODEL)),
        "ln1_b": nrm(ks[19], (L, D_MODEL), 0.02),
        "router_g_w": nrm(ks[20], (L, D_MODEL, MOE_GROUPS), D_MODEL ** -0.5),
        "router_g_b": nrm(ks[21], (L, MOE_GROUPS), 0.01),
        "router_e_w": nrm(ks[22], (L, D_MODEL, N_EXPERTS), D_MODEL ** -0.5),
        "router_e_b": nrm(ks[23], (L, N_EXPERTS), 0.01),
        "exp_w_gate": nrm(ks[24], (L, N_EXPERTS, D_MODEL, D_EXPERT), D_MODEL ** -0.5),
        "exp_w_up": nrm(ks[25], (L, N_EXPERTS, D_MODEL, D_EXPERT), D_MODEL ** -0.5),
        "exp_w_down": nrm(ks[26], (L, N_EXPERTS, D_EXPERT, D_MODEL), BETA * D_EXPERT ** -0.5),
        "ln2_g": gain(ks[27], (L, D_MODEL)),
        "ln2_b": nrm(ks[28], (L, D_MODEL), 0.02),
    }


def reference(x, w_in, gla_wa2, gla_ba, gla_norm, gm_ln_g, gm_ln_b, gm_ws, gm_bs, gm_norm,
              sc_conv, sc_norm, mla_q_norm, mla_kv_norm, mla_wuq, mla_wukv, mla_norm, w_o,
              ln1_g, ln1_b, router_g_w, router_g_b, router_e_w, router_e_b,
              exp_w_gate, exp_w_up, exp_w_down, ln2_g, ln2_b):
    for l in range(DEPTH):
        p = x @ w_in[l]
        (q, k, v, r, g_low, z_gm, p_sc, c_q, c_kv, k_rope) = split_cols(p, PROJ_SIZES)
        out_a = gla_mixer(q, k, v, r, g_low, gla_wa2[l], gla_ba[l], gla_norm[l])
        out_b = gmlp_mixer(z_gm, gm_ln_g[l], gm_ln_b[l], gm_ws[l], gm_bs[l], gm_norm[l])
        out_c = short_conv_mixer(p_sc, sc_conv[l], sc_norm[l])
        out_d = mla_mixer(c_q, c_kv, k_rope, mla_q_norm[l], mla_kv_norm[l], mla_wuq[l], mla_wukv[l], mla_norm[l])
        mix = jnp.concatenate([out_a, out_b, out_c, out_d], axis=-1) @ w_o[l]
        x = layer_norm(ALPHA * x + mix, ln1_g[l], ln1_b[l])
        ffn = hier_moe(x, router_g_w[l], router_g_b[l], router_e_w[l], router_e_b[l],
                       exp_w_gate[l], exp_w_up[l], exp_w_down[l])
        x = layer_norm(ALPHA * x + ffn, ln2_g[l], ln2_b[l])
    return x
```

```python
import functools

import jax
import jax.numpy as jnp
from jax import lax
from jax.experimental import pallas as pl
from jax.experimental.pallas import tpu as pltpu

F32 = jnp.float32
BF16 = jnp.bfloat16

D_MODEL = 4096
DEPTH = 2
GROUP_W = 1024

GLA_HEADS = 4
GLA_DK = 128
GLA_DV = 256
GLA_GATE_RANK = 16
GLA_TAU = 16.0
GLA_CHUNK = 64
GLA_SUB = 16

GM_GROUPS = 8
GM_CH = 128
GM_CHUNK = 128

MLA_HEADS = 8
MLA_NOPE = 128
MLA_ROPE = 64
MLA_V = 128
MLA_Q_LORA = 768
MLA_KV_LORA = 256
ROPE_BASE = 10000.0
MLA_QK_PAD = 256

MOE_GROUPS = 4
MOE_PER_GROUP = 8
N_EXPERTS = 32
D_EXPERT = 512

ALPHA = (2.0 * DEPTH) ** 0.25

LANE = 128
MIB = 1024 * 1024

COL_CQ = 0
COL_CKV = 768
COL_Q = 1024
COL_K = 1536
COL_V = 2048
COL_R = 3072
COL_U = 4096
COL_VG = 5120
COL_CB = 6144
COL_CC = 7168
COL_CH = 8192
COL_KR = 9216
COL_KRS = 9344
COL_GL = 9472
P_COLS = 9728

MOE_TILE = 256
GATHER_ROWS = 1024


def _cparams(sem, vmem_mib):
    return pltpu.CompilerParams(dimension_semantics=sem, vmem_limit_bytes=vmem_mib * MIB)


def _dot(a, b):
    return jnp.dot(a, b, preferred_element_type=F32)


def _dot_nt(a, b):
    return lax.dot_general(a, b, (((1,), (1,)), ((), ())), preferred_element_type=F32)


def _dot_tn(a, b):
    return lax.dot_general(a, b, (((0,), (0,)), ((), ())), preferred_element_type=F32)


def _mm_in_kernel(x_ref, w_ref, o_ref, xb_ref):
    @pl.when(pl.program_id(1) == 0)
    def _():
        xb_ref[...] = x_ref[...].astype(BF16)

    o_ref[...] = _dot(xb_ref[...], w_ref[...]).astype(o_ref.dtype)


def _mm_in(x, w):
    s, k = x.shape
    n = w.shape[1]
    tm = min(512, s)
    tn = 512
    return pl.pallas_call(
        _mm_in_kernel,
        grid=(s // tm, n // tn),
        in_specs=[pl.BlockSpec((tm, k), lambda i, j: (i, 0)),
                  pl.BlockSpec((k, tn), lambda i, j: (0, j))],
        out_specs=pl.BlockSpec((tm, tn), lambda i, j: (i, j)),
        out_shape=jax.ShapeDtypeStruct((s, n), BF16),
        scratch_shapes=[pltpu.VMEM((tm, k), BF16)],
        compiler_params=_cparams(("parallel", "arbitrary"), 48),
        name="mm_in",
    )(x, w)


def _gla_kernel(q_ref, k_ref, v_ref, r_ref, gl_ref, wa2_ref, ba_ref, ng_ref, o_ref, s_ref, *, n_chunks):
    c_len = GLA_CHUNK

    @pl.when(pl.program_id(0) == 0)
    def _():
        s_ref[...] = jnp.zeros_like(s_ref)

    row = lax.broadcasted_iota(jnp.int32, (c_len, c_len), 0)
    col = lax.broadcasted_iota(jnp.int32, (c_len, c_len), 1)
    tril = jnp.where(col <= row, 1.0, 0.0).astype(BF16)
    sub_row = lax.broadcasted_iota(jnp.int32, (GLA_SUB, c_len), 0)
    sub_col = lax.broadcasted_iota(jnp.int32, (GLA_SUB, c_len), 1)
    n_sub = c_len // GLA_SUB

    def chunk(c, carry):
        rows = pl.ds(pl.multiple_of(c * c_len, c_len), c_len)
        logit = _dot(gl_ref[rows, :], wa2_ref[...]) + ba_ref[...]
        g = (jnp.minimum(logit, 0.0) - jnp.log(1.0 + jnp.exp(-jnp.abs(logit)))) * (1.0 / GLA_TAU)
        g_hi = g.astype(BF16)
        g_lo = (g - g_hi.astype(F32)).astype(BF16)
        b_all = _dot(tril, g_hi) + _dot(tril, g_lo)

        for h in range(GLA_HEADS):
            hs = slice(GLA_DK * h, GLA_DK * (h + 1))
            vs = slice(GLA_DV * h, GLA_DV * (h + 1))
            bh = b_all[:, hs]
            qh = q_ref[rows, hs].astype(F32) * (GLA_DK ** -0.5)
            kh = k_ref[rows, hs].astype(F32)
            vh = v_ref[rows, vs]
            state = s_ref[h]

            o = _dot((qh * jnp.exp(bh)).astype(BF16), state.astype(BF16))

            att_rows = []
            for blk in range(n_sub):
                sl = slice(GLA_SUB * blk, GLA_SUB * (blk + 1))
                b_blk = bh[sl]
                q_blk = qh[sl]
                att = jnp.zeros((GLA_SUB, c_len), F32)
                if blk > 0:
                    ref = bh[GLA_SUB * blk:GLA_SUB * blk + 1, :]
                    qs = (q_blk * jnp.exp(b_blk - ref)).astype(BF16)
                    ks = (kh * jnp.exp(jnp.minimum(ref - bh, 0.0))).astype(BF16)
                    att = jnp.where(sub_col < GLA_SUB * blk, _dot_nt(qs, ks), 0.0)
                for jj in range(GLA_SUB):
                    j = GLA_SUB * blk + jj
                    t = q_blk * kh[j:j + 1, :] * jnp.exp(b_blk - bh[j:j + 1, :])
                    rs = jnp.sum(t, axis=-1, keepdims=True)
                    att = jnp.where((sub_col == j) & (sub_row >= jj), rs, att)
                att_rows.append(att)
            att_full = jnp.concatenate(att_rows, axis=0).astype(BF16)
            o = o + _dot(att_full, vh)

            b_last = bh[c_len - 1:c_len, :]
            kd = (kh * jnp.exp(b_last - bh)).astype(BF16)
            decay_col = jnp.transpose(jnp.broadcast_to(jnp.exp(b_last), (GLA_DK, GLA_DK)))
            s_ref[h] = state * jnp.concatenate([decay_col, decay_col], axis=1) + _dot_tn(kd, vh)

            var = jnp.mean(o * o, axis=-1, keepdims=True)
            on = o * lax.rsqrt(var + 1e-6) * ng_ref[:, vs]
            rr = r_ref[rows, vs].astype(F32)
            o_ref[rows, vs] = (on * (rr / (1.0 + jnp.exp(-rr)))).astype(o_ref.dtype)
        return carry

    lax.fori_loop(0, n_chunks, chunk, 0)


def _gla(p, wa2p, ba, ng):
    s = p.shape[0]
    t = min(256, s)
    col = lambda width, off: pl.BlockSpec((t, width), lambda i: (i, off // width))
    full = lambda shape: pl.BlockSpec(shape, lambda i: (0,) * len(shape))
    return pl.pallas_call(
        functools.partial(_gla_kernel, n_chunks=t // GLA_CHUNK),
        grid=(s // t,),
        in_specs=[col(512, COL_Q), col(512, COL_K), col(1024, COL_V), col(1024, COL_R), col(LANE, COL_GL),
                  full(wa2p.shape), full(ba.shape), full(ng.shape)],
        out_specs=pl.BlockSpec((t, GROUP_W), lambda i: (i, 0)),
        out_shape=jax.ShapeDtypeStruct((s, GROUP_W), BF16),
        scratch_shapes=[pltpu.VMEM((GLA_HEADS, GLA_DK, GLA_DV), F32)],
        compiler_params=_cparams(("arbitrary",), 32),
        name="gla",
    )(p, p, p, p, p, wa2p, ba, ng)


def _gelu(x):
    return 0.5 * x * (1.0 + lax.erf(x * 0.7071067811865476))


def _gmlp_kernel(u_ref, v_ref, lg_ref, lb_ref, ws_ref, bias_ref, ng_ref, o_ref, buf_ref, *, n_chunks):
    row = lax.broadcasted_iota(jnp.int32, (GM_CHUNK, GM_CHUNK), 0)
    col = lax.broadcasted_iota(jnp.int32, (GM_CHUNK, GM_CHUNK), 1)
    causal = col <= row
    v = _gelu(v_ref[...].astype(F32))
    mu = jnp.mean(v, axis=-1, keepdims=True)
    d = v - mu
    var = jnp.mean(d * d, axis=-1, keepdims=True)
    vb = (d * lax.rsqrt(var + 1e-5) * lg_ref[...] + lb_ref[...]).astype(BF16)
    for g in range(GM_GROUPS):
        cs = slice(GM_CH * g, GM_CH * (g + 1))
        w = jnp.where(causal, ws_ref[g], 0.0).astype(BF16)
        for c in range(n_chunks):
            rs = slice(GM_CHUNK * c, GM_CHUNK * (c + 1))
            mixed = _dot(w, vb[rs, cs]) + bias_ref[:, cs]
            buf_ref[rs, cs] = _gelu(u_ref[rs, cs].astype(F32)) * mixed
    out = buf_ref[...]
    ms = jnp.mean(out * out, axis=-1, keepdims=True)
    o_ref[...] = (out * lax.rsqrt(ms + 1e-6) * ng_ref[...]).astype(o_ref.dtype)


def _gmlp(p, lg, lb, ws, bias, ng):
    s = p.shape[0]
    t = min(256, s)
    col = lambda off: pl.BlockSpec((t, GROUP_W), lambda i: (i, off // GROUP_W))
    full = lambda shape: pl.BlockSpec(shape, lambda i: (0,) * len(shape))
    return pl.pallas_call(
        functools.partial(_gmlp_kernel, n_chunks=t // GM_CHUNK),
        grid=(s // t,),
        in_specs=[col(COL_U), col(COL_VG), full(lg.shape), full(lb.shape), full(ws.shape), full(bias.shape),
                  full(ng.shape)],
        out_specs=pl.BlockSpec((t, GROUP_W), lambda i: (i, 0)),
        out_shape=jax.ShapeDtypeStruct((s, GROUP_W), BF16),
        scratch_shapes=[pltpu.VMEM((t, GROUP_W), F32)],
        compiler_params=_cparams(("parallel",), 32),
        name="gmlp",
    )(p, p, lg, lb, ws, bias, ng)


CONV_HALO = 8


def _sconv_kernel(b_ref, c_ref, h_ref, w_ref, ng_ref, o_ref, z_ref):
    t = b_ref.shape[0]

    @pl.when(pl.program_id(0) == 0)
    def _():
        z_ref[0:CONV_HALO, :] = jnp.zeros((CONV_HALO, GROUP_W), F32)

    z = c_ref[...].astype(F32) * h_ref[...].astype(F32)
    z_ref[CONV_HALO:CONV_HALO + t, :] = z
    z1 = z_ref[CONV_HALO - 1:CONV_HALO - 1 + t, :]
    z2 = z_ref[CONV_HALO - 2:CONV_HALO - 2 + t, :]
    y = w_ref[0:1, :] * z2 + w_ref[1:2, :] * z1 + w_ref[2:3, :] * z
    z_ref[0:CONV_HALO, :] = z[t - CONV_HALO:t, :]
    out = b_ref[...].astype(F32) * y
    ms = jnp.mean(out * out, axis=-1, keepdims=True)
    o_ref[...] = (out * lax.rsqrt(ms + 1e-6) * ng_ref[...]).astype(o_ref.dtype)


def _sconv(p, w, ng):
    s = p.shape[0]
    t = min(256, s)
    col = lambda off: pl.BlockSpec((t, GROUP_W), lambda i: (i, off // GROUP_W))
    full = lambda shape: pl.BlockSpec(shape, lambda i: (0,) * len(shape))
    return pl.pallas_call(
        _sconv_kernel,
        grid=(s // t,),
        in_specs=[col(COL_CB), col(COL_CC), col(COL_CH), full(w.shape), full(ng.shape)],
        out_specs=pl.BlockSpec((t, GROUP_W), lambda i: (i, 0)),
        out_shape=jax.ShapeDtypeStruct((s, GROUP_W), BF16),
        scratch_shapes=[pltpu.VMEM((t + CONV_HALO, GROUP_W), F32)],
        compiler_params=_cparams(("arbitrary",), 32),
        name="sconv",
    )(p, p, p, w, ng)


def _mla_proj_kernel(cq_ref, ckv_ref, kr_ref, krs_ref, qn_ref, kvn_ref, wq_ref, wqs_ref, wk_ref, wv_ref,
                     cos_ref, sin_ref, q_out, k_out, v_out):
    def rms(ref, g_ref):
        t = ref[...].astype(F32)
        return (t * lax.rsqrt(jnp.mean(t * t, axis=-1, keepdims=True) + 1e-6) * g_ref[...]).astype(BF16)

    cqn = rms(cq_ref, qn_ref)
    ckvn = rms(ckv_ref, kvn_ref)
    qm = _dot(cqn, wq_ref[...])
    qsw = _dot(cqn, wqs_ref[...])
    kn = _dot(ckvn, wk_ref[...])
    v_out[...] = _dot(ckvn, wv_ref[...]).astype(v_out.dtype)
    cos = cos_ref[...]
    sin = sin_ref[...]
    scale = (MLA_NOPE + MLA_ROPE) ** -0.5
    kr_rot = (kr_ref[...].astype(F32) * cos + krs_ref[...].astype(F32) * sin).astype(k_out.dtype)
    for h in range(MLA_HEADS):
        lo = MLA_QK_PAD * h
        q_out[:, lo:lo + LANE] = (qm[:, lo:lo + LANE] * scale).astype(q_out.dtype)
        q_out[:, lo + LANE:lo + 2 * LANE] = (
            (qm[:, lo + LANE:lo + 2 * LANE] * cos + qsw[:, LANE * h:LANE * (h + 1)] * sin) * scale
        ).astype(q_out.dtype)
        k_out[:, lo:lo + LANE] = kn[:, LANE * h:LANE * (h + 1)].astype(k_out.dtype)
        k_out[:, lo + LANE:lo + 2 * LANE] = kr_rot


def _mla_proj(p, qn, kvn, wq, wqs, wk, wv, cos, sin):
    s = p.shape[0]
    t = min(256, s)
    full = lambda shape: pl.BlockSpec(shape, lambda i: (0,) * len(shape))
    rowb = lambda width: pl.BlockSpec((t, width), lambda i: (i, 0))
    qk_w = MLA_HEADS * MLA_QK_PAD
    return pl.pallas_call(
        _mla_proj_kernel,
        grid=(s // t,),
        in_specs=[pl.BlockSpec((t, MLA_Q_LORA), lambda i: (i, COL_CQ // MLA_Q_LORA)),
                  pl.BlockSpec((t, MLA_KV_LORA), lambda i: (i, COL_CKV // MLA_KV_LORA)),
                  pl.BlockSpec((t, LANE), lambda i: (i, COL_KR // LANE)),
                  pl.BlockSpec((t, LANE), lambda i: (i, COL_KRS // LANE)),
                  full(qn.shape), full(kvn.shape), full(wq.shape), full(wqs.shape), full(wk.shape), full(wv.shape),
                  rowb(LANE), rowb(LANE)],
        out_specs=[rowb(qk_w), rowb(qk_w), rowb(GROUP_W)],
        out_shape=[jax.ShapeDtypeStruct((s, qk_w), BF16), jax.ShapeDtypeStruct((s, qk_w), BF16),
                   jax.ShapeDtypeStruct((s, GROUP_W), BF16)],
        compiler_params=_cparams(("parallel",), 40),
        name="mla_proj",
    )(p, p, p, p, qn, kvn, wq, wqs, wk, wv, cos, sin)


def _flash_kernel(qi_ref, ki_ref, q_ref, k_ref, v_ref, ng_ref, o_ref, acc_ref, m_ref, l_ref, *, tq, tk):
    step = pl.program_id(0)
    qi = qi_ref[step]
    ki = ki_ref[step]

    @pl.when(ki == 0)
    def _():
        m_ref[...] = jnp.full(m_ref.shape, -jnp.inf, F32)
        l_ref[...] = jnp.zeros_like(l_ref)
        acc_ref[...] = jnp.zeros_like(acc_ref)

    row = qi * tq + lax.broadcasted_iota(jnp.int32, (tq, tk), 0)
    col = ki * tk + lax.broadcasted_iota(jnp.int32, (tq, tk), 1)
    visible = col <= row
    for h in range(MLA_HEADS):
        qs = slice(MLA_QK_PAD * h, MLA_QK_PAD * (h + 1))
        vs = slice(MLA_V * h, MLA_V * (h + 1))
        sc = jnp.where(visible, _dot_nt(q_ref[:, qs], k_ref[:, qs]), -jnp.inf)
        m_old = m_ref[h]
        m_new = jnp.maximum(m_old, jnp.max(sc, axis=-1, keepdims=True))
        pr = jnp.exp(sc - m_new)
        fade = jnp.exp(m_old - m_new)
        l_ref[h] = fade * l_ref[h] + jnp.sum(pr, axis=-1, keepdims=True)
        acc_ref[:, vs] = fade * acc_ref[:, vs] + _dot(pr.astype(BF16), v_ref[:, vs])
        m_ref[h] = m_new

    @pl.when(ki == qi)
    def _():
        for h in range(MLA_HEADS):
            vs = slice(MLA_V * h, MLA_V * (h + 1))
            acc_ref[:, vs] = acc_ref[:, vs] / l_ref[h]
        o = acc_ref[...]
        ms = jnp.mean(o * o, axis=-1, keepdims=True)
        o_ref[...] = (o * lax.rsqrt(ms + 1e-6) * ng_ref[...]).astype(o_ref.dtype)


def _flash(q, k, v, ng):
    s = q.shape[0]
    tq = tk = min(512, s)
    nq = s // tq
    pairs = [(a, b) for a in range(nq) for b in range(a + 1)]
    qi_tab = jnp.asarray([a for a, _ in pairs], jnp.int32)
    ki_tab = jnp.asarray([b for _, b in pairs], jnp.int32)
    qk_w = MLA_HEADS * MLA_QK_PAD
    grid_spec = pltpu.PrefetchScalarGridSpec(
        num_scalar_prefetch=2,
        grid=(len(pairs),),
        in_specs=[pl.BlockSpec((tq, qk_w), lambda t, qi, ki: (qi[t], 0)),
                  pl.BlockSpec((tk, qk_w), lambda t, qi, ki: (ki[t], 0)),
                  pl.BlockSpec((tk, GROUP_W), lambda t, qi, ki: (ki[t], 0)),
                  pl.BlockSpec((1, GROUP_W), lambda t, qi, ki: (0, 0))],
        out_specs=pl.BlockSpec((tq, GROUP_W), lambda t, qi, ki: (qi[t], 0)),
        scratch_shapes=[pltpu.VMEM((tq, GROUP_W), F32),
                        pltpu.VMEM((MLA_HEADS, tq, 1), F32),
                        pltpu.VMEM((MLA_HEADS, tq, 1), F32)],
    )
    return pl.pallas_call(
        functools.partial(_flash_kernel, tq=tq, tk=tk),
        grid_spec=grid_spec,
        out_shape=jax.ShapeDtypeStruct((s, GROUP_W), BF16),
        compiler_params=_cparams(("arbitrary",), 48),
        name="flash",
    )(qi_tab, ki_tab, q, k, v, ng)


def _mm_out_kernel(a0_ref, a1_ref, a2_ref, a3_ref, w_ref, x_ref, o_ref):
    acc = _dot(a0_ref[...], w_ref[0:GROUP_W, :])
    acc = acc + _dot(a1_ref[...], w_ref[GROUP_W:2 * GROUP_W, :])
    acc = acc + _dot(a2_ref[...], w_ref[2 * GROUP_W:3 * GROUP_W, :])
    acc = acc + _dot(a3_ref[...], w_ref[3 * GROUP_W:4 * GROUP_W, :])
    o_ref[...] = ALPHA * x_ref[...] + acc


def _mm_out(a0, a1, a2, a3, w, x):
    s = x.shape[0]
    tm = min(512, s)
    tn = 1024
    a_spec = pl.BlockSpec((tm, GROUP_W), lambda i, j: (i, 0))
    return pl.pallas_call(
        _mm_out_kernel,
        grid=(s // tm, D_MODEL // tn),
        in_specs=[a_spec, a_spec, a_spec, a_spec,
                  pl.BlockSpec((D_MODEL, tn), lambda i, j: (0, j)),
                  pl.BlockSpec((tm, tn), lambda i, j: (i, j))],
        out_specs=pl.BlockSpec((tm, tn), lambda i, j: (i, j)),
        out_shape=jax.ShapeDtypeStruct((s, D_MODEL), F32),
        compiler_params=_cparams(("parallel", "arbitrary"), 48),
        name="mm_out",
    )(a0, a1, a2, a3, w, x)


def _layer_norm(t, g, b):
    mu = jnp.mean(t, axis=-1, keepdims=True)
    d = t - mu
    var = jnp.mean(d * d, axis=-1, keepdims=True)
    return d * lax.rsqrt(var + 1e-5) * g + b


def _ln_route_kernel(y_ref, g_ref, b_ref, wh_ref, wl_ref, rb_ref, x_out, route_out):
    x1 = _layer_norm(y_ref[...], g_ref[...], b_ref[...])
    x_out[...] = x1
    x_hi = x1.astype(BF16)
    x_lo = (x1 - x_hi.astype(F32)).astype(BF16)
    logits = _dot(x_hi, wh_ref[...]) + _dot(x_lo, wh_ref[...]) + _dot(x_hi, wl_ref[...]) + rb_ref[...]
    lane = lax.broadcasted_iota(jnp.int32, logits.shape, 1).astype(F32)
    neg = -jnp.inf
    big = 1e9
    is_grp = lane < MOE_GROUPS
    lg = jnp.where(is_grp, logits, neg)
    mg = jnp.max(lg, axis=-1, keepdims=True)
    gsel = jnp.min(jnp.where(lg == mg, lane, big), axis=-1, keepdims=True)
    pg_sel = 1.0 / jnp.sum(jnp.where(is_grp, jnp.exp(lg - mg), 0.0), axis=-1, keepdims=True)
    lo = MOE_GROUPS + MOE_PER_GROUP * gsel
    le = jnp.where((lane >= lo) & (lane < lo + MOE_PER_GROUP), logits, neg)
    v1 = jnp.max(le, axis=-1, keepdims=True)
    i1 = jnp.min(jnp.where(le == v1, lane, big), axis=-1, keepdims=True)
    le2 = jnp.where(lane == i1, neg, le)
    v2 = jnp.max(le2, axis=-1, keepdims=True)
    i2 = jnp.min(jnp.where(le2 == v2, lane, big), axis=-1, keepdims=True)
    e = jnp.exp(v2 - v1)
    w1 = pg_sel / (1.0 + e)
    w2 = pg_sel * e / (1.0 + e)
    route = jnp.where(lane == 0.0, i1 - MOE_GROUPS,
                      jnp.where(lane == 1.0, i2 - MOE_GROUPS,
                                jnp.where(lane == 2.0, w1, jnp.where(lane == 3.0, w2, 0.0))))
    route_out[...] = route


def _ln_route(y, g, b, wh, wl, rb):
    s = y.shape[0]
    tm = min(256, s)
    full = lambda shape: pl.BlockSpec(shape, lambda i: (0,) * len(shape))
    rowb = lambda width: pl.BlockSpec((tm, width), lambda i: (i, 0))
    return pl.pallas_call(
        _ln_route_kernel,
        grid=(s // tm,),
        in_specs=[rowb(D_MODEL), full(g.shape), full(b.shape), full(wh.shape), full(wl.shape), full(rb.shape)],
        out_specs=[rowb(D_MODEL), rowb(LANE)],
        out_shape=[jax.ShapeDtypeStruct((s, D_MODEL), F32), jax.ShapeDtypeStruct((s, LANE), F32)],
        compiler_params=_cparams(("parallel",), 40),
        name="ln_route",
    )(y, g, b, wh, wl, rb)


def _gather_kernel(idx_ref, src_ref, dst_ref, sem, *, rows):
    base = pl.program_id(0) * rows

    def row_copy(i, src_row):
        return pltpu.make_async_copy(src_ref.at[pl.ds(src_row, 1)], dst_ref.at[pl.ds(base + i, 1)], sem)

    def start(i, carry):
        row_copy(i, idx_ref[i]).start()
        return carry

    def wait(i, carry):
        row_copy(i, 0).wait()
        return carry

    lax.fori_loop(0, rows, start, 0)
    lax.fori_loop(0, rows, wait, 0)


def _gather_rows(src, idx):
    n = idx.shape[0]
    rows = GATHER_ROWS
    return pl.pallas_call(
        functools.partial(_gather_kernel, rows=rows),
        grid=(n // rows,),
        in_specs=[pl.BlockSpec((rows,), lambda i: (i,), memory_space=pltpu.SMEM),
                  pl.BlockSpec(memory_space=pl.ANY)],
        out_specs=pl.BlockSpec(memory_space=pl.ANY),
        out_shape=jax.ShapeDtypeStruct((n, src.shape[1]), src.dtype),
        scratch_shapes=[pltpu.SemaphoreType.DMA(())],
        compiler_params=_cparams(("arbitrary",), 16),
        name="gather",
    )(idx, src)


def _moe_kernel(te_ref, nt_ref, xs_ref, wg_ref, wu_ref, wd_ref, ws_ref, o_ref):
    used = pl.program_id(0) < nt_ref[0]

    @pl.when(used)
    def _():
        h = xs_ref[...].astype(BF16)
        gate = _dot(h, wg_ref[0])
        up = _dot(h, wu_ref[0])
        a = (gate / (1.0 + jnp.exp(-gate)) * up).astype(BF16)
        o_ref[...] = _dot(a, wd_ref[0]) * ws_ref[...]

    @pl.when(jnp.logical_not(used))
    def _():
        o_ref[...] = jnp.zeros_like(o_ref)


def _moe(tile_expert, n_tiles, xs, wg, wu, wd, w_slot):
    p_rows = xs.shape[0]
    tm = MOE_TILE
    grid_spec = pltpu.PrefetchScalarGridSpec(
        num_scalar_prefetch=2,
        grid=(p_rows // tm,),
        in_specs=[pl.BlockSpec((tm, D_MODEL), lambda t, te, nt: (t, 0)),
                  pl.BlockSpec((1, D_MODEL, D_EXPERT), lambda t, te, nt: (te[t], 0, 0)),
                  pl.BlockSpec((1, D_MODEL, D_EXPERT), lambda t, te, nt: (te[t], 0, 0)),
                  pl.BlockSpec((1, D_EXPERT, D_MODEL), lambda t, te, nt: (te[t], 0, 0)),
                  pl.BlockSpec((tm, 1), lambda t, te, nt: (t, 0))],
        out_specs=pl.BlockSpec((tm, D_MODEL), lambda t, te, nt: (t, 0)),
    )
    return pl.pallas_call(
        _moe_kernel,
        grid_spec=grid_spec,
        out_shape=jax.ShapeDtypeStruct((p_rows, D_MODEL), F32),
        compiler_params=_cparams(("arbitrary",), 56),
        name="moe",
    )(tile_expert, n_tiles, xs, wg, wu, wd, w_slot)


def _ln_add_kernel(x_ref, ya_ref, yb_ref, g_ref, b_ref, o_ref):
    o_ref[...] = _layer_norm(ALPHA * x_ref[...] + (ya_ref[...] + yb_ref[...]), g_ref[...], b_ref[...])


def _ln_add(x, y2, g, b):
    s = x.shape[0]
    tm = min(256, s)
    full = lambda shape: pl.BlockSpec(shape, lambda i: (0,) * len(shape))
    return pl.pallas_call(
        _ln_add_kernel,
        grid=(s // tm,),
        in_specs=[pl.BlockSpec((tm, D_MODEL), lambda i: (i, 0)),
                  pl.BlockSpec((tm, D_MODEL), lambda i: (i, 0)),
                  pl.BlockSpec((tm, D_MODEL), lambda i: (i, 1)),
                  full(g.shape), full(b.shape)],
        out_specs=pl.BlockSpec((tm, D_MODEL), lambda i: (i, 0)),
        out_shape=jax.ShapeDtypeStruct((s, D_MODEL), F32),
        compiler_params=_cparams(("parallel",), 48),
        name="ln_add",
    )(x, y2, y2, g, b)


def _pack_w_in(w):
    o = 0
    parts = {}
    for name, width in (("q", 512), ("k", 512), ("v", 1024), ("r", 1024), ("gl", GLA_GATE_RANK), ("u", 1024),
                        ("vg", 1024), ("cb", 1024), ("cc", 1024), ("ch", 1024), ("cq", MLA_Q_LORA),
                        ("ckv", MLA_KV_LORA), ("kr", MLA_ROPE)):
        parts[name] = w[:, o:o + width]
        o += width
    zeros = lambda n: jnp.zeros((w.shape[0], n), w.dtype)
    half = MLA_ROPE // 2
    kr = parts["kr"]
    cols = [parts["cq"], parts["ckv"], parts["q"], parts["k"], parts["v"], parts["r"], parts["u"], parts["vg"],
            parts["cb"], parts["cc"], parts["ch"],
            kr, zeros(LANE - MLA_ROPE),
            kr[:, half:], kr[:, :half], zeros(LANE - MLA_ROPE),
            parts["gl"], zeros(LANE - GLA_GATE_RANK)]
    packed = jnp.concatenate(cols, axis=1)
    packed = jnp.concatenate([packed, zeros(P_COLS - packed.shape[1])], axis=1)
    return packed.astype(BF16)


def _pack_mla_weights(wuq, wukv):
    half = MLA_ROPE // 2
    wq3 = wuq.reshape(MLA_Q_LORA, MLA_HEADS, MLA_NOPE + MLA_ROPE)
    rope = wq3[:, :, MLA_NOPE:]
    zq = jnp.zeros((MLA_Q_LORA, MLA_HEADS, MLA_QK_PAD - MLA_NOPE - MLA_ROPE), wuq.dtype)
    wq = jnp.concatenate([wq3, zq], axis=2).reshape(MLA_Q_LORA, MLA_HEADS * MLA_QK_PAD)
    zs = jnp.zeros((MLA_Q_LORA, MLA_HEADS, LANE - MLA_ROPE), wuq.dtype)
    wqs = jnp.concatenate([rope[:, :, half:], rope[:, :, :half], zs], axis=2).reshape(MLA_Q_LORA, MLA_HEADS * LANE)
    wkv3 = wukv.reshape(MLA_KV_LORA, MLA_HEADS, MLA_NOPE + MLA_V)
    wk = wkv3[:, :, :MLA_NOPE].reshape(MLA_KV_LORA, MLA_HEADS * MLA_NOPE)
    wv = wkv3[:, :, MLA_NOPE:].reshape(MLA_KV_LORA, MLA_HEADS * MLA_V)
    return wq.astype(BF16), wqs.astype(BF16), wk.astype(BF16), wv.astype(BF16)


def _rope_tables(s):
    pos = jnp.arange(s, dtype=F32)
    inv_freq = ROPE_BASE ** (-jnp.arange(0, MLA_ROPE, 2, dtype=F32) / MLA_ROPE)
    ang = pos[:, None] * inv_freq[None, :]
    cos, sin = jnp.cos(ang), jnp.sin(ang)
    z = jnp.zeros((s, LANE - MLA_ROPE), F32)
    return jnp.concatenate([cos, cos, z], axis=1), jnp.concatenate([-sin, sin, z], axis=1)


def _dispatch_plan(route, s):
    n_assign = 2 * s
    p_rows = n_assign + N_EXPERTS * MOE_TILE
    eid = route[:, 0:2].astype(jnp.int32).reshape(n_assign)
    wgt = route[:, 2:4].reshape(n_assign)
    onehot = (eid[:, None] == jnp.arange(N_EXPERTS, dtype=jnp.int32)[None, :]).astype(jnp.int32)
    csum = jnp.cumsum(onehot, axis=0)
    rank = jnp.sum(onehot * (csum - 1), axis=1)
    counts = csum[-1]
    tiles_e = (counts + MOE_TILE - 1) // MOE_TILE
    tile_end = jnp.cumsum(tiles_e)
    tile_start = tile_end - tiles_e
    slot = tile_start[eid] * MOE_TILE + rank
    src_token = jnp.zeros((p_rows,), jnp.int32).at[slot].set(jnp.arange(n_assign, dtype=jnp.int32) // 2)
    w_slot = jnp.zeros((p_rows,), F32).at[slot].set(wgt)
    tile_ids = jnp.arange(p_rows // MOE_TILE, dtype=jnp.int32)
    tile_expert = jnp.minimum(jnp.searchsorted(tile_end, tile_ids, side="right"), N_EXPERTS - 1).astype(jnp.int32)
    n_tiles = tile_end[-1:].astype(jnp.int32)
    return src_token, slot.astype(jnp.int32), w_slot.reshape(p_rows, 1), tile_expert, n_tiles


def kernel(x, w_in, gla_wa2, gla_ba, gla_norm, gm_ln_g, gm_ln_b, gm_ws, gm_bs, gm_norm, sc_conv, sc_norm, mla_q_norm, mla_kv_norm, mla_wuq, mla_wukv, mla_norm, w_o, ln1_g, ln1_b, router_g_w, router_g_b, router_e_w, router_e_b, exp_w_gate, exp_w_up, exp_w_down, ln2_g, ln2_b):
    bsz, s, _ = x.shape
    assert bsz == 1
    xc = x.reshape(s, D_MODEL)
    cos_t, sin_t = _rope_tables(s)
    row = lambda v: v.reshape(1, -1)
    for l in range(DEPTH):
        p = _mm_in(xc, _pack_w_in(w_in[l]))

        wa2p = jnp.concatenate(
            [gla_wa2[l], jnp.zeros((LANE - GLA_GATE_RANK, GLA_HEADS * GLA_DK), F32)], axis=0).astype(BF16)
        out_a = _gla(p, wa2p, row(gla_ba[l]), row(gla_norm[l]))

        gm_bias = jnp.repeat(gm_bs[l].T, GM_CH, axis=1)
        out_b = _gmlp(p, row(gm_ln_g[l]), row(gm_ln_b[l]), gm_ws[l], gm_bias, row(gm_norm[l]))

        out_c = _sconv(p, sc_conv[l], row(sc_norm[l]))

        wq, wqs, wk, wv = _pack_mla_weights(mla_wuq[l], mla_wukv[l])
        q_att, k_att, v_att = _mla_proj(p, row(mla_q_norm[l]), row(mla_kv_norm[l]), wq, wqs, wk, wv, cos_t, sin_t)
        out_d = _flash(q_att, k_att, v_att, row(mla_norm[l]))

        y = _mm_out(out_a, out_b, out_c, out_d, w_o[l].astype(BF16), xc)

        wr = jnp.concatenate([router_g_w[l], router_e_w[l],
                              jnp.zeros((D_MODEL, LANE - MOE_GROUPS - N_EXPERTS), F32)], axis=1)
        wr_hi = wr.astype(BF16)
        wr_lo = (wr - wr_hi.astype(F32)).astype(BF16)
        rb = jnp.concatenate([router_g_b[l], router_e_b[l], jnp.zeros((LANE - MOE_GROUPS - N_EXPERTS,), F32)])
        x1, route = _ln_route(y, row(ln1_g[l]), row(ln1_b[l]), wr_hi, wr_lo, row(rb))

        src_token, slot, w_slot, tile_expert, n_tiles = _dispatch_plan(route, s)
        xs = _gather_rows(x1, src_token)
        ys = _moe(tile_expert, n_tiles, xs, exp_w_gate[l].astype(BF16), exp_w_up[l].astype(BF16),
                  exp_w_down[l].astype(BF16), w_slot)
        y2 = _gather_rows(ys, slot).reshape(s, 2 * D_MODEL)
        xc = _ln_add(x1, y2, row(ln2_g[l]), row(ln2_b[l]))
    return xc.reshape(bsz, s, D_MODEL)
```

```python
import functools

import jax
import jax.numpy as jnp
from jax import lax
from jax.experimental import pallas as pl
from jax.experimental.pallas import tpu as pltpu

F32 = jnp.float32
BF16 = jnp.bfloat16

D_MODEL = 4096
DEPTH = 2
GROUP_W = 1024

GLA_HEADS = 4
GLA_DK = 128
GLA_DV = 256
GLA_GATE_RANK = 16
GLA_TAU = 16.0
GLA_CHUNK = 64
GLA_SUB = 16

GM_GROUPS = 8
GM_CH = 128
GM_CHUNK = 128

MLA_HEADS = 8
MLA_NOPE = 128
MLA_ROPE = 64
MLA_V = 128
MLA_Q_LORA = 768
MLA_KV_LORA = 256
ROPE_BASE = 10000.0
MLA_QK_PAD = 256

MOE_GROUPS = 4
MOE_PER_GROUP = 8
N_EXPERTS = 32
D_EXPERT = 512

ALPHA = (2.0 * DEPTH) ** 0.25
LOG2_E = 1.4426950408889634

LANE = 128
MIB = 1024 * 1024

COL_CQ = 0
COL_CKV = 768
COL_Q = 1024
COL_K = 1536
COL_V = 2048
COL_R = 3072
COL_U = 4096
COL_VG = 5120
COL_CB = 6144
COL_CC = 7168
COL_CH = 8192
COL_KR = 9216
COL_KRS = 9344
COL_GL = 9472
P_COLS = 9728

MOE_TILE = 256
TOK_CHUNKS = D_MODEL // LANE


def _cparams(sem, vmem_mib):
    return pltpu.CompilerParams(dimension_semantics=sem, vmem_limit_bytes=vmem_mib * MIB)


def _dot(a, b):
    return jnp.dot(a, b, preferred_element_type=F32)


def _dot_nt(a, b):
    return lax.dot_general(a, b, (((1,), (1,)), ((), ())), preferred_element_type=F32)


def _dot_tn(a, b):
    return lax.dot_general(a, b, (((0,), (0,)), ((), ())), preferred_element_type=F32)


def _mm_in_kernel(x_ref, w_ref, o_ref, xb_ref):
    @pl.when(pl.program_id(1) == 0)
    def _():
        xb_ref[...] = x_ref[...].astype(BF16)

    o_ref[...] = _dot(xb_ref[...], w_ref[...]).astype(o_ref.dtype)


def _mm_in(x, w):
    s, k = x.shape
    n = w.shape[1]
    tm = min(512, s)
    tn = 512
    return pl.pallas_call(
        _mm_in_kernel,
        grid=(s // tm, n // tn),
        in_specs=[pl.BlockSpec((tm, k), lambda i, j: (i, 0)),
                  pl.BlockSpec((k, tn), lambda i, j: (0, j))],
        out_specs=pl.BlockSpec((tm, tn), lambda i, j: (i, j)),
        out_shape=jax.ShapeDtypeStruct((s, n), BF16),
        scratch_shapes=[pltpu.VMEM((tm, k), BF16)],
        compiler_params=_cparams(("parallel", "arbitrary"), 48),
        name="mm_in",
    )(x, w)


def _gla_kernel(q_ref, k_ref, v_ref, r_ref, gl_ref, wa2_ref, ba_ref, ng_ref, o_ref, s_ref, *, n_chunks):
    c_len = GLA_CHUNK

    @pl.when(pl.program_id(0) == 0)
    def _():
        s_ref[...] = jnp.zeros_like(s_ref)

    row = lax.broadcasted_iota(jnp.int32, (c_len, c_len), 0)
    col = lax.broadcasted_iota(jnp.int32, (c_len, c_len), 1)
    tril = jnp.where(col <= row, 1.0, 0.0).astype(BF16)
    sub_row = lax.broadcasted_iota(jnp.int32, (GLA_SUB, c_len), 0)
    sub_col = lax.broadcasted_iota(jnp.int32, (GLA_SUB, c_len), 1)
    n_sub = c_len // GLA_SUB

    def chunk(c, carry):
        rows = pl.ds(pl.multiple_of(c * c_len, c_len), c_len)
        logit = _dot(gl_ref[rows, :], wa2_ref[...]) + ba_ref[...]
        g = (jnp.minimum(logit, 0.0) - jnp.log(1.0 + jnp.exp(-jnp.abs(logit)))) * (1.0 / GLA_TAU)
        g_hi = g.astype(BF16)
        g_lo = (g - g_hi.astype(F32)).astype(BF16)
        b_all = _dot(tril, g_hi) + _dot(tril, g_lo)

        for h in range(GLA_HEADS):
            hs = slice(GLA_DK * h, GLA_DK * (h + 1))
            vs = slice(GLA_DV * h, GLA_DV * (h + 1))
            bh = b_all[:, hs]
            qh = q_ref[rows, hs].astype(F32) * (GLA_DK ** -0.5)
            kh = k_ref[rows, hs].astype(F32)
            vh = v_ref[rows, vs]
            state = s_ref[h]

            o = _dot((qh * jnp.exp(bh)).astype(BF16), state.astype(BF16))

            att_rows = []
            for blk in range(n_sub):
                sl = slice(GLA_SUB * blk, GLA_SUB * (blk + 1))
                b_blk = bh[sl]
                q_blk = qh[sl]
                att = jnp.zeros((GLA_SUB, c_len), F32)
                if blk > 0:
                    ref = bh[GLA_SUB * blk:GLA_SUB * blk + 1, :]
                    qs = (q_blk * jnp.exp(b_blk - ref)).astype(BF16)
                    ks = (kh * jnp.exp(jnp.minimum(ref - bh, 0.0))).astype(BF16)
                    att = jnp.where(sub_col < GLA_SUB * blk, _dot_nt(qs, ks), 0.0)
                for jj in range(GLA_SUB):
                    j = GLA_SUB * blk + jj
                    t = q_blk * kh[j:j + 1, :] * jnp.exp(b_blk - bh[j:j + 1, :])
                    rs = jnp.sum(t, axis=-1, keepdims=True)
                    att = jnp.where((sub_col == j) & (sub_row >= jj), rs, att)
                att_rows.append(att)
            att_full = jnp.concatenate(att_rows, axis=0).astype(BF16)
            o = o + _dot(att_full, vh)

            b_last = bh[c_len - 1:c_len, :]
            kd = (kh * jnp.exp(b_last - bh)).astype(BF16)
            decay_col = jnp.transpose(jnp.broadcast_to(jnp.exp(b_last), (GLA_DK, GLA_DK)))
            s_ref[h] = state * jnp.concatenate([decay_col, decay_col], axis=1) + _dot_tn(kd, vh)

            var = jnp.mean(o * o, axis=-1, keepdims=True)
            on = o * lax.rsqrt(var + 1e-6) * ng_ref[:, vs]
            rr = r_ref[rows, vs].astype(F32)
            o_ref[rows, vs] = (on * (rr / (1.0 + jnp.exp(-rr)))).astype(o_ref.dtype)
        return carry

    lax.fori_loop(0, n_chunks, chunk, 0)


def _gla(p, wa2p, ba, ng):
    s = p.shape[0]
    t = min(256, s)
    col = lambda width, off: pl.BlockSpec((t, width), lambda i: (i, off // width))
    full = lambda shape: pl.BlockSpec(shape, lambda i: (0,) * len(shape))
    return pl.pallas_call(
        functools.partial(_gla_kernel, n_chunks=t // GLA_CHUNK),
        grid=(s // t,),
        in_specs=[col(512, COL_Q), col(512, COL_K), col(1024, COL_V), col(1024, COL_R), col(LANE, COL_GL),
                  full(wa2p.shape), full(ba.shape), full(ng.shape)],
        out_specs=pl.BlockSpec((t, GROUP_W), lambda i: (i, 0)),
        out_shape=jax.ShapeDtypeStruct((s, GROUP_W), BF16),
        scratch_shapes=[pltpu.VMEM((GLA_HEADS, GLA_DK, GLA_DV), F32)],
        compiler_params=_cparams(("arbitrary",), 32),
        name="gla",
    )(p, p, p, p, p, wa2p, ba, ng)


def _gelu(x):
    return 0.5 * x * (1.0 + lax.erf(x * 0.7071067811865476))


def _gmlp_kernel(u_ref, v_ref, lg_ref, lb_ref, ws_ref, bias_ref, ng_ref, o_ref, buf_ref, *, n_chunks):
    row = lax.broadcasted_iota(jnp.int32, (GM_CHUNK, GM_CHUNK), 0)
    col = lax.broadcasted_iota(jnp.int32, (GM_CHUNK, GM_CHUNK), 1)
    causal = col <= row
    v = _gelu(v_ref[...].astype(F32))
    mu = jnp.mean(v, axis=-1, keepdims=True)
    d = v - mu
    var = jnp.mean(d * d, axis=-1, keepdims=True)
    vb = (d * lax.rsqrt(var + 1e-5) * lg_ref[...] + lb_ref[...]).astype(BF16)
    for g in range(GM_GROUPS):
        cs = slice(GM_CH * g, GM_CH * (g + 1))
        w = jnp.where(causal, ws_ref[g], 0.0).astype(BF16)
        for c in range(n_chunks):
            rs = slice(GM_CHUNK * c, GM_CHUNK * (c + 1))
            mixed = _dot(w, vb[rs, cs]) + bias_ref[:, cs]
            buf_ref[rs, cs] = _gelu(u_ref[rs, cs].astype(F32)) * mixed
    out = buf_ref[...]
    ms = jnp.mean(out * out, axis=-1, keepdims=True)
    o_ref[...] = (out * lax.rsqrt(ms + 1e-6) * ng_ref[...]).astype(o_ref.dtype)


def _gmlp(p, lg, lb, ws, bias, ng):
    s = p.shape[0]
    t = min(256, s)
    col = lambda off: pl.BlockSpec((t, GROUP_W), lambda i: (i, off // GROUP_W))
    full = lambda shape: pl.BlockSpec(shape, lambda i: (0,) * len(shape))
    return pl.pallas_call(
        functools.partial(_gmlp_kernel, n_chunks=t // GM_CHUNK),
        grid=(s // t,),
        in_specs=[col(COL_U), col(COL_VG), full(lg.shape), full(lb.shape), full(ws.shape), full(bias.shape),
                  full(ng.shape)],
        out_specs=pl.BlockSpec((t, GROUP_W), lambda i: (i, 0)),
        out_shape=jax.ShapeDtypeStruct((s, GROUP_W), BF16),
        scratch_shapes=[pltpu.VMEM((t, GROUP_W), F32)],
        compiler_params=_cparams(("parallel",), 32),
        name="gmlp",
    )(p, p, lg, lb, ws, bias, ng)


CONV_HALO = 8


def _sconv_kernel(b_ref, c_ref, h_ref, w_ref, ng_ref, o_ref, z_ref):
    t = b_ref.shape[0]

    @pl.when(pl.program_id(0) == 0)
    def _():
        z_ref[0:CONV_HALO, :] = jnp.zeros((CONV_HALO, GROUP_W), F32)

    z = c_ref[...].astype(F32) * h_ref[...].astype(F32)
    z_ref[CONV_HALO:CONV_HALO + t, :] = z
    z1 = z_ref[CONV_HALO - 1:CONV_HALO - 1 + t, :]
    z2 = z_ref[CONV_HALO - 2:CONV_HALO - 2 + t, :]
    y = w_ref[0:1, :] * z2 + w_ref[1:2, :] * z1 + w_ref[2:3, :] * z
    z_ref[0:CONV_HALO, :] = z[t - CONV_HALO:t, :]
    out = b_ref[...].astype(F32) * y
    ms = jnp.mean(out * out, axis=-1, keepdims=True)
    o_ref[...] = (out * lax.rsqrt(ms + 1e-6) * ng_ref[...]).astype(o_ref.dtype)


def _sconv(p, w, ng):
    s = p.shape[0]
    t = min(256, s)
    col = lambda off: pl.BlockSpec((t, GROUP_W), lambda i: (i, off // GROUP_W))
    full = lambda shape: pl.BlockSpec(shape, lambda i: (0,) * len(shape))
    return pl.pallas_call(
        _sconv_kernel,
        grid=(s // t,),
        in_specs=[col(COL_CB), col(COL_CC), col(COL_CH), full(w.shape), full(ng.shape)],
        out_specs=pl.BlockSpec((t, GROUP_W), lambda i: (i, 0)),
        out_shape=jax.ShapeDtypeStruct((s, GROUP_W), BF16),
        scratch_shapes=[pltpu.VMEM((t + CONV_HALO, GROUP_W), F32)],
        compiler_params=_cparams(("arbitrary",), 32),
        name="sconv",
    )(p, p, p, w, ng)


def _mla_proj_kernel(cq_ref, ckv_ref, kr_ref, krs_ref, qn_ref, kvn_ref, wq_ref, wqs_ref, wk_ref, wv_ref,
                     cos_ref, sin_ref, q_out, k_out, v_out):
    def rms(ref, g_ref):
        t = ref[...].astype(F32)
        return (t * lax.rsqrt(jnp.mean(t * t, axis=-1, keepdims=True) + 1e-6) * g_ref[...]).astype(BF16)

    cqn = rms(cq_ref, qn_ref)
    ckvn = rms(ckv_ref, kvn_ref)
    qm = _dot(cqn, wq_ref[...])
    qsw = _dot(cqn, wqs_ref[...])
    kn = _dot(ckvn, wk_ref[...])
    vv = _dot(ckvn, wv_ref[...])
    cos = cos_ref[...]
    sin = sin_ref[...]
    scale = (MLA_NOPE + MLA_ROPE) ** -0.5 * LOG2_E
    kr_rot = (kr_ref[...].astype(F32) * cos + krs_ref[...].astype(F32) * sin).astype(k_out.dtype)
    ones_col = jnp.where(lax.broadcasted_iota(jnp.int32, (cos.shape[0], LANE), 1) == 0, 1.0, 0.0).astype(v_out.dtype)
    for h in range(MLA_HEADS):
        lo = MLA_QK_PAD * h
        v_out[:, lo:lo + LANE] = vv[:, LANE * h:LANE * (h + 1)].astype(v_out.dtype)
        v_out[:, lo + LANE:lo + 2 * LANE] = ones_col
        q_out[:, lo:lo + LANE] = (qm[:, lo:lo + LANE] * scale).astype(q_out.dtype)
        q_out[:, lo + LANE:lo + 2 * LANE] = (
            (qm[:, lo + LANE:lo + 2 * LANE] * cos + qsw[:, LANE * h:LANE * (h + 1)] * sin) * scale
        ).astype(q_out.dtype)
        k_out[:, lo:lo + LANE] = kn[:, LANE * h:LANE * (h + 1)].astype(k_out.dtype)
        k_out[:, lo + LANE:lo + 2 * LANE] = kr_rot


def _mla_proj(p, qn, kvn, wq, wqs, wk, wv, cos, sin):
    s = p.shape[0]
    t = min(256, s)
    full = lambda shape: pl.BlockSpec(shape, lambda i: (0,) * len(shape))
    rowb = lambda width: pl.BlockSpec((t, width), lambda i: (i, 0))
    qk_w = MLA_HEADS * MLA_QK_PAD
    return pl.pallas_call(
        _mla_proj_kernel,
        grid=(s // t,),
        in_specs=[pl.BlockSpec((t, MLA_Q_LORA), lambda i: (i, COL_CQ // MLA_Q_LORA)),
                  pl.BlockSpec((t, MLA_KV_LORA), lambda i: (i, COL_CKV // MLA_KV_LORA)),
                  pl.BlockSpec((t, LANE), lambda i: (i, COL_KR // LANE)),
                  pl.BlockSpec((t, LANE), lambda i: (i, COL_KRS // LANE)),
                  full(qn.shape), full(kvn.shape), full(wq.shape), full(wqs.shape), full(wk.shape), full(wv.shape),
                  rowb(LANE), rowb(LANE)],
        out_specs=[rowb(qk_w), rowb(qk_w), rowb(qk_w)],
        out_shape=[jax.ShapeDtypeStruct((s, qk_w), BF16)] * 3,
        compiler_params=_cparams(("parallel",), 40),
        name="mla_proj",
    )(p, p, p, p, qn, kvn, wq, wqs, wk, wv, cos, sin)


FLASH_TQ = 512
FLASH_TK = 1024


def _flash_kernel(qi_ref, ki_ref, q_ref, k_ref, v_ref, ng_ref, o_ref, acc_ref, m_ref, *, tq, tk):
    step = pl.program_id(0)
    qi = qi_ref[step]
    ki = ki_ref[step]
    last_ki = (qi * tq) // tk

    @pl.when(ki == 0)
    def _():
        m_ref[...] = jnp.full(m_ref.shape, -jnp.inf, F32)
        acc_ref[...] = jnp.zeros_like(acc_ref)

    def accumulate(masked):
        if masked:
            row = qi * tq + lax.broadcasted_iota(jnp.int32, (tq, tk), 0)
            col = ki * tk + lax.broadcasted_iota(jnp.int32, (tq, tk), 1)
            visible = col <= row
        for h in range(MLA_HEADS):
            hs = slice(MLA_QK_PAD * h, MLA_QK_PAD * (h + 1))
            sc = _dot_nt(q_ref[:, hs], k_ref[:, hs])
            if masked:
                sc = jnp.where(visible, sc, -jnp.inf)
            m_old = m_ref[h]
            m_new = jnp.maximum(m_old, jnp.max(sc, axis=-1, keepdims=True))
            pr = jnp.exp2(sc - m_new).astype(BF16)
            acc_ref[:, hs] = jnp.exp2(m_old - m_new) * acc_ref[:, hs] + _dot(pr, v_ref[:, hs])
            m_ref[h] = m_new

    @pl.when(ki < last_ki)
    def _():
        accumulate(False)

    @pl.when(ki == last_ki)
    def _():
        accumulate(True)
        outs = []
        for h in range(MLA_HEADS):
            lo = MLA_QK_PAD * h
            outs.append(acc_ref[:, lo:lo + MLA_V] / acc_ref[:, lo + MLA_V:lo + MLA_V + 1])
        o = jnp.concatenate(outs, axis=1)
        ms = jnp.mean(o * o, axis=-1, keepdims=True)
        o_ref[...] = (o * lax.rsqrt(ms + 1e-6) * ng_ref[...]).astype(o_ref.dtype)


def _flash(q, k, v, ng):
    s = q.shape[0]
    tk = min(FLASH_TK, s)
    tq = min(FLASH_TQ, tk)
    nq = s // tq
    pairs = [(a, b) for a in range(nq) for b in range((a * tq) // tk + 1)]
    qi_tab = jnp.asarray([a for a, _ in pairs], jnp.int32)
    ki_tab = jnp.asarray([b for _, b in pairs], jnp.int32)
    qk_w = MLA_HEADS * MLA_QK_PAD
    grid_spec = pltpu.PrefetchScalarGridSpec(
        num_scalar_prefetch=2,
        grid=(len(pairs),),
        in_specs=[pl.BlockSpec((tq, qk_w), lambda t, qi, ki: (qi[t], 0)),
                  pl.BlockSpec((tk, qk_w), lambda t, qi, ki: (ki[t], 0)),
                  pl.BlockSpec((tk, qk_w), lambda t, qi, ki: (ki[t], 0)),
                  pl.BlockSpec((1, GROUP_W), lambda t, qi, ki: (0, 0))],
        out_specs=pl.BlockSpec((tq, GROUP_W), lambda t, qi, ki: (qi[t], 0)),
        scratch_shapes=[pltpu.VMEM((tq, qk_w), F32),
                        pltpu.VMEM((MLA_HEADS, tq, 1), F32)],
    )
    return pl.pallas_call(
        functools.partial(_flash_kernel, tq=tq, tk=tk),
        grid_spec=grid_spec,
        out_shape=jax.ShapeDtypeStruct((s, GROUP_W), BF16),
        compiler_params=_cparams(("arbitrary",), 56),
        name="flash",
    )(qi_tab, ki_tab, q, k, v, ng)


def _mm_out_kernel(a0_ref, a1_ref, a2_ref, a3_ref, w_ref, x_ref, o_ref):
    acc = _dot(a0_ref[...], w_ref[0:GROUP_W, :])
    acc = acc + _dot(a1_ref[...], w_ref[GROUP_W:2 * GROUP_W, :])
    acc = acc + _dot(a2_ref[...], w_ref[2 * GROUP_W:3 * GROUP_W, :])
    acc = acc + _dot(a3_ref[...], w_ref[3 * GROUP_W:4 * GROUP_W, :])
    o_ref[...] = ALPHA * x_ref[...] + acc


def _mm_out(a0, a1, a2, a3, w, x):
    s = x.shape[0]
    tm = min(512, s)
    tn = 1024
    a_spec = pl.BlockSpec((tm, GROUP_W), lambda i, j: (i, 0))
    return pl.pallas_call(
        _mm_out_kernel,
        grid=(s // tm, D_MODEL // tn),
        in_specs=[a_spec, a_spec, a_spec, a_spec,
                  pl.BlockSpec((D_MODEL, tn), lambda i, j: (0, j)),
                  pl.BlockSpec((tm, tn), lambda i, j: (i, j))],
        out_specs=pl.BlockSpec((tm, tn), lambda i, j: (i, j)),
        out_shape=jax.ShapeDtypeStruct((s, D_MODEL), F32),
        compiler_params=_cparams(("parallel", "arbitrary"), 48),
        name="mm_out",
    )(a0, a1, a2, a3, w, x)


def _layer_norm(t, g, b):
    mu = jnp.mean(t, axis=-1, keepdims=True)
    d = t - mu
    var = jnp.mean(d * d, axis=-1, keepdims=True)
    return d * lax.rsqrt(var + 1e-5) * g + b


def _ln_route_kernel(y_ref, g_ref, b_ref, wh_ref, wl_ref, rb_ref, x_out, xtok_out, route_out):
    x1 = _layer_norm(y_ref[...], g_ref[...], b_ref[...])
    x_out[...] = x1
    for c in range(TOK_CHUNKS):
        xtok_out[:, c, 0, :] = x1[:, LANE * c:LANE * (c + 1)]
    x_hi = x1.astype(BF16)
    x_lo = (x1 - x_hi.astype(F32)).astype(BF16)
    logits = _dot(x_hi, wh_ref[...]) + _dot(x_lo, wh_ref[...]) + _dot(x_hi, wl_ref[...]) + rb_ref[...]
    lane = lax.broadcasted_iota(jnp.int32, logits.shape, 1).astype(F32)
    neg = -jnp.inf
    big = 1e9
    is_grp = lane < MOE_GROUPS
    lg = jnp.where(is_grp, logits, neg)
    mg = jnp.max(lg, axis=-1, keepdims=True)
    gsel = jnp.min(jnp.where(lg == mg, lane, big), axis=-1, keepdims=True)
    pg_sel = 1.0 / jnp.sum(jnp.where(is_grp, jnp.exp(lg - mg), 0.0), axis=-1, keepdims=True)
    lo = MOE_GROUPS + MOE_PER_GROUP * gsel
    le = jnp.where((lane >= lo) & (lane < lo + MOE_PER_GROUP), logits, neg)
    v1 = jnp.max(le, axis=-1, keepdims=True)
    i1 = jnp.min(jnp.where(le == v1, lane, big), axis=-1, keepdims=True)
    le2 = jnp.where(lane == i1, neg, le)
    v2 = jnp.max(le2, axis=-1, keepdims=True)
    i2 = jnp.min(jnp.where(le2 == v2, lane, big), axis=-1, keepdims=True)
    e = jnp.exp(v2 - v1)
    w1 = pg_sel / (1.0 + e)
    w2 = pg_sel * e / (1.0 + e)
    route = jnp.where(lane == 0.0, i1 - MOE_GROUPS,
                      jnp.where(lane == 1.0, i2 - MOE_GROUPS,
                                jnp.where(lane == 2.0, w1, jnp.where(lane == 3.0, w2, 0.0))))
    route_out[...] = route


def _ln_route(y, g, b, wh, wl, rb):
    s = y.shape[0]
    tm = min(256, s)
    full = lambda shape: pl.BlockSpec(shape, lambda i: (0,) * len(shape))
    rowb = lambda width: pl.BlockSpec((tm, width), lambda i: (i, 0))
    return pl.pallas_call(
        _ln_route_kernel,
        grid=(s // tm,),
        in_specs=[rowb(D_MODEL), full(g.shape), full(b.shape), full(wh.shape), full(wl.shape), full(rb.shape)],
        out_specs=[rowb(D_MODEL), pl.BlockSpec((tm, TOK_CHUNKS, 1, LANE), lambda i: (i, 0, 0, 0)), rowb(LANE)],
        out_shape=[jax.ShapeDtypeStruct((s, D_MODEL), F32),
                   jax.ShapeDtypeStruct((s, TOK_CHUNKS, 1, LANE), F32),
                   jax.ShapeDtypeStruct((s, LANE), F32)],
        compiler_params=_cparams(("parallel",), 48),
        name="ln_route",
    )(y, g, b, wh, wl, rb)


def _moe_kernel(te_ref, tv_ref, nt_ref, src_ref, nxt_ref, dst_ref, x_hbm, wg_ref, wu_ref, wd_ref, ws_ref, y_hbm,
                gbuf, hb, obuf, gsem, ssem):
    t = pl.program_id(0)
    n_used = nt_ref[0]
    slot = lax.rem(t, 2)

    def start_gather(idx_ref, buf_slot):
        def body(i, carry):
            pltpu.make_async_copy(x_hbm.at[idx_ref[i]], gbuf.at[buf_slot, :, pl.ds(i, 1), :],
                                  gsem.at[buf_slot]).start()
            return carry
        lax.fori_loop(0, MOE_TILE, body, 0, unroll=8)

    def wait_gather(buf_slot):
        pltpu.make_async_copy(gbuf.at[buf_slot], gbuf.at[buf_slot], gsem.at[buf_slot]).wait()

    def start_scatter(n_rows):
        def body(i, carry):
            pltpu.make_async_copy(obuf.at[:, pl.ds(i, 1), :], y_hbm.at[dst_ref[i]], ssem).start()
            return carry
        lax.fori_loop(0, n_rows, body, 0)

    def wait_scatter(n_rows):
        rows = obuf.at[:, pl.ds(0, n_rows), :]
        pltpu.make_async_copy(rows, rows, ssem).wait()

    @pl.when(t == 0)
    def _():
        start_gather(src_ref, 0)

    @pl.when(t + 1 < n_used)
    def _():
        start_gather(nxt_ref, 1 - slot)

    @pl.when(t < n_used)
    def _():
        wait_gather(slot)
        for c in range(TOK_CHUNKS):
            hb[:, LANE * c:LANE * (c + 1)] = gbuf[slot, c].astype(BF16)
        h = hb[...]
        gate = _dot(h, wg_ref[0])
        up = _dot(h, wu_ref[0])
        a = (gate / (1.0 + jnp.exp(-gate)) * up).astype(BF16)

        @pl.when(t > 0)
        def _():
            wait_scatter(tv_ref[jnp.maximum(t - 1, 0)])

        ws = ws_ref[...]
        for c2 in range(TOK_CHUNKS // 2):
            yc = _dot(a, wd_ref[0, :, 2 * LANE * c2:2 * LANE * (c2 + 1)]) * ws
            obuf[2 * c2] = yc[:, :LANE]
            obuf[2 * c2 + 1] = yc[:, LANE:]
        start_scatter(tv_ref[t])

        @pl.when(t == n_used - 1)
        def _():
            wait_scatter(tv_ref[t])


def _moe(tile_expert, tile_valid, n_tiles, src_token, dst_row, x_tok, wg, wu, wd, w_slot):
    p_rows = src_token.shape[0]
    tm = MOE_TILE
    n_blocks = p_rows // tm
    smem_blk = lambda fn: pl.BlockSpec((tm,), fn, memory_space=pltpu.SMEM)
    grid_spec = pltpu.PrefetchScalarGridSpec(
        num_scalar_prefetch=3,
        grid=(n_blocks,),
        in_specs=[smem_blk(lambda t, te, tv, nt: (t,)),
                  smem_blk(lambda t, te, tv, nt: (jnp.minimum(t + 1, n_blocks - 1),)),
                  smem_blk(lambda t, te, tv, nt: (t,)),
                  pl.BlockSpec(memory_space=pl.ANY),
                  pl.BlockSpec((1, D_MODEL, D_EXPERT), lambda t, te, tv, nt: (te[t], 0, 0)),
                  pl.BlockSpec((1, D_MODEL, D_EXPERT), lambda t, te, tv, nt: (te[t], 0, 0)),
                  pl.BlockSpec((1, D_EXPERT, D_MODEL), lambda t, te, tv, nt: (te[t], 0, 0)),
                  pl.BlockSpec((tm, 1), lambda t, te, tv, nt: (t, 0))],
        out_specs=pl.BlockSpec(memory_space=pl.ANY),
        scratch_shapes=[pltpu.VMEM((2, TOK_CHUNKS, tm, LANE), F32),
                        pltpu.VMEM((tm, D_MODEL), BF16),
                        pltpu.VMEM((TOK_CHUNKS, tm, LANE), F32),
                        pltpu.SemaphoreType.DMA((2,)),
                        pltpu.SemaphoreType.DMA(())],
    )
    return pl.pallas_call(
        _moe_kernel,
        grid_spec=grid_spec,
        out_shape=jax.ShapeDtypeStruct((2 * x_tok.shape[0], TOK_CHUNKS, 1, LANE), F32),
        compiler_params=_cparams(("arbitrary",), 56),
        name="moe",
    )(tile_expert, tile_valid, n_tiles, src_token, src_token, dst_row, x_tok, wg, wu, wd, w_slot)


def _ln_add_kernel(x_ref, g_ref, b_ref, y_hbm, o_ref, ybuf, vbuf, sem, *, tm):
    i = pl.program_id(0)
    slot = lax.rem(i, 2)

    def start_gather(tile, buf_slot):
        base = tile * (2 * tm)

        def body(j, carry):
            for k in range(2):
                pltpu.make_async_copy(y_hbm.at[base + 2 * j + k], ybuf.at[buf_slot, k, :, pl.ds(j, 1), :],
                                      sem.at[buf_slot]).start()
            return carry
        lax.fori_loop(0, tm, body, 0, unroll=4)

    @pl.when(i == 0)
    def _():
        start_gather(0, 0)

    @pl.when(i + 1 < pl.num_programs(0))
    def _():
        start_gather(i + 1, 1 - slot)

    pltpu.make_async_copy(ybuf.at[slot], ybuf.at[slot], sem.at[slot]).wait()
    for c in range(TOK_CHUNKS):
        cs = slice(LANE * c, LANE * (c + 1))
        vbuf[:, cs] = ALPHA * x_ref[:, cs] + (ybuf[slot, 0, c] + ybuf[slot, 1, c])
    o_ref[...] = _layer_norm(vbuf[...], g_ref[...], b_ref[...])


def _ln_add(x, y_tok, g, b):
    s = x.shape[0]
    tm = min(256, s)
    full = lambda shape: pl.BlockSpec(shape, lambda i: (0,) * len(shape))
    return pl.pallas_call(
        functools.partial(_ln_add_kernel, tm=tm),
        grid=(s // tm,),
        in_specs=[pl.BlockSpec((tm, D_MODEL), lambda i: (i, 0)), full(g.shape), full(b.shape),
                  pl.BlockSpec(memory_space=pl.ANY)],
        out_specs=pl.BlockSpec((tm, D_MODEL), lambda i: (i, 0)),
        out_shape=jax.ShapeDtypeStruct((s, D_MODEL), F32),
        scratch_shapes=[pltpu.VMEM((2, 2, TOK_CHUNKS, tm, LANE), F32),
                        pltpu.VMEM((tm, D_MODEL), F32),
                        pltpu.SemaphoreType.DMA((2,))],
        compiler_params=_cparams(("arbitrary",), 48),
        name="ln_add",
    )(x, g, b, y_tok)


def _pack_w_in(w):
    o = 0
    parts = {}
    for name, width in (("q", 512), ("k", 512), ("v", 1024), ("r", 1024), ("gl", GLA_GATE_RANK), ("u", 1024),
                        ("vg", 1024), ("cb", 1024), ("cc", 1024), ("ch", 1024), ("cq", MLA_Q_LORA),
                        ("ckv", MLA_KV_LORA), ("kr", MLA_ROPE)):
        parts[name] = w[:, o:o + width]
        o += width
    zeros = lambda n: jnp.zeros((w.shape[0], n), w.dtype)
    half = MLA_ROPE // 2
    kr = parts["kr"]
    cols = [parts["cq"], parts["ckv"], parts["q"], parts["k"], parts["v"], parts["r"], parts["u"], parts["vg"],
            parts["cb"], parts["cc"], parts["ch"],
            kr, zeros(LANE - MLA_ROPE),
            kr[:, half:], kr[:, :half], zeros(LANE - MLA_ROPE),
            parts["gl"], zeros(LANE - GLA_GATE_RANK)]
    packed = jnp.concatenate(cols, axis=1)
    packed = jnp.concatenate([packed, zeros(P_COLS - packed.shape[1])], axis=1)
    return packed.astype(BF16)


def _pack_mla_weights(wuq, wukv):
    half = MLA_ROPE // 2
    wq3 = wuq.reshape(MLA_Q_LORA, MLA_HEADS, MLA_NOPE + MLA_ROPE)
    rope = wq3[:, :, MLA_NOPE:]
    zq = jnp.zeros((MLA_Q_LORA, MLA_HEADS, MLA_QK_PAD - MLA_NOPE - MLA_ROPE), wuq.dtype)
    wq = jnp.concatenate([wq3, zq], axis=2).reshape(MLA_Q_LORA, MLA_HEADS * MLA_QK_PAD)
    zs = jnp.zeros((MLA_Q_LORA, MLA_HEADS, LANE - MLA_ROPE), wuq.dtype)
    wqs = jnp.concatenate([rope[:, :, half:], rope[:, :, :half], zs], axis=2).reshape(MLA_Q_LORA, MLA_HEADS * LANE)
    wkv3 = wukv.reshape(MLA_KV_LORA, MLA_HEADS, MLA_NOPE + MLA_V)
    wk = wkv3[:, :, :MLA_NOPE].reshape(MLA_KV_LORA, MLA_HEADS * MLA_NOPE)
    wv = wkv3[:, :, MLA_NOPE:].reshape(MLA_KV_LORA, MLA_HEADS * MLA_V)
    return wq.astype(BF16), wqs.astype(BF16), wk.astype(BF16), wv.astype(BF16)


def _rope_tables(s):
    pos = jnp.arange(s, dtype=F32)
    inv_freq = ROPE_BASE ** (-jnp.arange(0, MLA_ROPE, 2, dtype=F32) / MLA_ROPE)
    ang = pos[:, None] * inv_freq[None, :]
    cos, sin = jnp.cos(ang), jnp.sin(ang)
    z = jnp.zeros((s, LANE - MLA_ROPE), F32)
    return jnp.concatenate([cos, cos, z], axis=1), jnp.concatenate([-sin, sin, z], axis=1)


def _dispatch_plan(route, s):
    n_assign = 2 * s
    p_rows = n_assign + N_EXPERTS * MOE_TILE
    eid = route[:, 0:2].astype(jnp.int32).reshape(n_assign)
    wgt = route[:, 2:4].reshape(n_assign)
    onehot = (eid[:, None] == jnp.arange(N_EXPERTS, dtype=jnp.int32)[None, :]).astype(jnp.int32)
    csum = jnp.cumsum(onehot, axis=0)
    rank = jnp.sum(onehot * (csum - 1), axis=1)
    counts = csum[-1]
    tiles_e = (counts + MOE_TILE - 1) // MOE_TILE
    tile_end = jnp.cumsum(tiles_e)
    tile_start = tile_end - tiles_e
    slot = tile_start[eid] * MOE_TILE + rank
    slot_assign = jnp.full((p_rows,), -1, jnp.int32).at[slot].set(jnp.arange(n_assign, dtype=jnp.int32))
    is_pad = slot_assign < 0
    src_token = jnp.where(is_pad, 0, slot_assign // 2)
    dst_row = jnp.maximum(slot_assign, 0)
    w_slot = jnp.where(is_pad, 0.0, wgt[dst_row])
    n_blocks = p_rows // MOE_TILE
    tile_valid = jnp.sum(jnp.logical_not(is_pad).reshape(n_blocks, MOE_TILE).astype(jnp.int32), axis=1)
    tile_ids = jnp.arange(n_blocks, dtype=jnp.int32)
    tile_expert = jnp.minimum(jnp.sum((tile_end[None, :] <= tile_ids[:, None]).astype(jnp.int32), axis=1),
                              N_EXPERTS - 1)
    n_tiles = tile_end[-1:].astype(jnp.int32)
    return src_token, dst_row, w_slot.reshape(p_rows, 1), tile_expert, tile_valid, n_tiles


def kernel(x, w_in, gla_wa2, gla_ba, gla_norm, gm_ln_g, gm_ln_b, gm_ws, gm_bs, gm_norm, sc_conv, sc_norm, mla_q_norm, mla_kv_norm, mla_wuq, mla_wukv, mla_norm, w_o, ln1_g, ln1_b, router_g_w, router_g_b, router_e_w, router_e_b, exp_w_gate, exp_w_up, exp_w_down, ln2_g, ln2_b):
    bsz, s, _ = x.shape
    assert bsz == 1
    xc = x.reshape(s, D_MODEL)
    cos_t, sin_t = _rope_tables(s)
    row = lambda v: v.reshape(1, -1)
    for l in range(DEPTH):
        p = _mm_in(xc, _pack_w_in(w_in[l]))

        wa2p = jnp.concatenate(
            [gla_wa2[l], jnp.zeros((LANE - GLA_GATE_RANK, GLA_HEADS * GLA_DK), F32)], axis=0).astype(BF16)
        out_a = _gla(p, wa2p, row(gla_ba[l]), row(gla_norm[l]))

        gm_bias = jnp.repeat(gm_bs[l].T, GM_CH, axis=1)
        out_b = _gmlp(p, row(gm_ln_g[l]), row(gm_ln_b[l]), gm_ws[l], gm_bias, row(gm_norm[l]))

        out_c = _sconv(p, sc_conv[l], row(sc_norm[l]))

        wq, wqs, wk, wv = _pack_mla_weights(mla_wuq[l], mla_wukv[l])
        q_att, k_att, v_att = _mla_proj(p, row(mla_q_norm[l]), row(mla_kv_norm[l]), wq, wqs, wk, wv, cos_t, sin_t)
        out_d = _flash(q_att, k_att, v_att, row(mla_norm[l]))

        y = _mm_out(out_a, out_b, out_c, out_d, w_o[l].astype(BF16), xc)

        wr = jnp.concatenate([router_g_w[l], router_e_w[l],
                              jnp.zeros((D_MODEL, LANE - MOE_GROUPS - N_EXPERTS), F32)], axis=1)
        wr_hi = wr.astype(BF16)
        wr_lo = (wr - wr_hi.astype(F32)).astype(BF16)
        rb = jnp.concatenate([router_g_b[l], router_e_b[l], jnp.zeros((LANE - MOE_GROUPS - N_EXPERTS,), F32)])
        x1, x1_tok, route = _ln_route(y, row(ln1_g[l]), row(ln1_b[l]), wr_hi, wr_lo, row(rb))

        src_token, dst_row, w_slot, tile_expert, tile_valid, n_tiles = _dispatch_plan(route, s)
        y_tok = _moe(tile_expert, tile_valid, n_tiles, src_token, dst_row, x1_tok, exp_w_gate[l].astype(BF16),
                     exp_w_up[l].astype(BF16), exp_w_down[l].astype(BF16), w_slot)
        xc = _ln_add(x1, y_tok, row(ln2_g[l]), row(ln2_b[l]))
    return xc.reshape(bsz, s, D_MODEL)
```

```python
import functools

import jax
import jax.numpy as jnp
from jax import lax
from jax.experimental import pallas as pl
from jax.experimental.pallas import tpu as pltpu

F32 = jnp.float32
BF16 = jnp.bfloat16

D_MODEL = 4096
DEPTH = 2
GROUP_W = 1024

GLA_HEADS = 4
GLA_DK = 128
GLA_DV = 256
GLA_GATE_RANK = 16
GLA_TAU = 16.0
GLA_CHUNK = 64
GLA_SUB = 16

GM_GROUPS = 8
GM_CH = 128
GM_CHUNK = 128

MLA_HEADS = 8
MLA_NOPE = 128
MLA_ROPE = 64
MLA_V = 128
MLA_Q_LORA = 768
MLA_KV_LORA = 256
ROPE_BASE = 10000.0
MLA_QK_PAD = 256

MOE_GROUPS = 4
MOE_PER_GROUP = 8
N_EXPERTS = 32
D_EXPERT = 512

ALPHA = (2.0 * DEPTH) ** 0.25
LOG2_E = 1.4426950408889634

LANE = 128
MIB = 1024 * 1024

COL_CQ = 0
COL_CKV = 768
COL_Q = 1024
COL_K = 1536
COL_V = 2048
COL_R = 3072
COL_U = 4096
COL_VG = 5120
COL_CB = 6144
COL_CC = 7168
COL_CH = 8192
COL_KR = 9216
COL_KRS = 9344
COL_GL = 9472
P_COLS = 9728

MOE_TILE = 256
SCATTER_UNROLL = 8
TOK_CHUNKS = D_MODEL // LANE


def _cparams(sem, vmem_mib):
    return pltpu.CompilerParams(dimension_semantics=sem, vmem_limit_bytes=vmem_mib * MIB)


def _dot(a, b):
    return jnp.dot(a, b, preferred_element_type=F32)


def _dot_nt(a, b):
    return lax.dot_general(a, b, (((1,), (1,)), ((), ())), preferred_element_type=F32)


def _dot_tn(a, b):
    return lax.dot_general(a, b, (((0,), (0,)), ((), ())), preferred_element_type=F32)


def _mm_in_kernel(x_ref, w_ref, o_ref, xb_ref):
    @pl.when(pl.program_id(1) == 0)
    def _():
        xb_ref[...] = x_ref[...].astype(BF16)

    o_ref[...] = _dot(xb_ref[...], w_ref[...]).astype(o_ref.dtype)


def _mm_in(x, w):
    s, k = x.shape
    n = w.shape[1]
    tm = min(512, s)
    tn = 512
    return pl.pallas_call(
        _mm_in_kernel,
        grid=(s // tm, n // tn),
        in_specs=[pl.BlockSpec((tm, k), lambda i, j: (i, 0)),
                  pl.BlockSpec((k, tn), lambda i, j: (0, j))],
        out_specs=pl.BlockSpec((tm, tn), lambda i, j: (i, j)),
        out_shape=jax.ShapeDtypeStruct((s, n), BF16),
        scratch_shapes=[pltpu.VMEM((tm, k), BF16)],
        compiler_params=_cparams(("parallel", "arbitrary"), 48),
        name="mm_in",
    )(x, w)


def _gla_kernel(q_ref, k_ref, v_ref, r_ref, gl_ref, wa2_ref, ba_ref, ng_ref, o_ref, s_ref, *, n_chunks):
    c_len = GLA_CHUNK

    @pl.when(pl.program_id(0) == 0)
    def _():
        s_ref[...] = jnp.zeros_like(s_ref)

    row = lax.broadcasted_iota(jnp.int32, (c_len, c_len), 0)
    col = lax.broadcasted_iota(jnp.int32, (c_len, c_len), 1)
    tril = jnp.where(col <= row, 1.0, 0.0).astype(BF16)
    sub_row = lax.broadcasted_iota(jnp.int32, (GLA_SUB, c_len), 0)
    sub_col = lax.broadcasted_iota(jnp.int32, (GLA_SUB, c_len), 1)
    n_sub = c_len // GLA_SUB

    def chunk(c, carry):
        rows = pl.ds(pl.multiple_of(c * c_len, c_len), c_len)
        logit = _dot(gl_ref[rows, :], wa2_ref[...]) + ba_ref[...]
        g = (jnp.minimum(logit, 0.0) - jnp.log(1.0 + jnp.exp(-jnp.abs(logit)))) * (1.0 / GLA_TAU)
        g_hi = g.astype(BF16)
        g_lo = (g - g_hi.astype(F32)).astype(BF16)
        b_all = _dot(tril, g_hi) + _dot(tril, g_lo)

        for h in range(GLA_HEADS):
            hs = slice(GLA_DK * h, GLA_DK * (h + 1))
            vs = slice(GLA_DV * h, GLA_DV * (h + 1))
            bh = b_all[:, hs]
            qh = q_ref[rows, hs].astype(F32) * (GLA_DK ** -0.5)
            kh = k_ref[rows, hs].astype(F32)
            vh = v_ref[rows, vs]
            state = s_ref[h]

            o = _dot((qh * jnp.exp(bh)).astype(BF16), state.astype(BF16))

            att_rows = []
            for blk in range(n_sub):
                sl = slice(GLA_SUB * blk, GLA_SUB * (blk + 1))
                b_blk = bh[sl]
                q_blk = qh[sl]
                att = jnp.zeros((GLA_SUB, c_len), F32)
                if blk > 0:
                    ref = bh[GLA_SUB * blk:GLA_SUB * blk + 1, :]
                    qs = (q_blk * jnp.exp(b_blk - ref)).astype(BF16)
                    ks = (kh * jnp.exp(jnp.minimum(ref - bh, 0.0))).astype(BF16)
                    att = jnp.where(sub_col < GLA_SUB * blk, _dot_nt(qs, ks), 0.0)
                for jj in range(GLA_SUB):
                    j = GLA_SUB * blk + jj
                    t = q_blk * kh[j:j + 1, :] * jnp.exp(b_blk - bh[j:j + 1, :])
                    rs = jnp.sum(t, axis=-1, keepdims=True)
                    att = jnp.where((sub_col == j) & (sub_row >= jj), rs, att)
                att_rows.append(att)
            att_full = jnp.concatenate(att_rows, axis=0).astype(BF16)
            o = o + _dot(att_full, vh)

            b_last = bh[c_len - 1:c_len, :]
            kd = (kh * jnp.exp(b_last - bh)).astype(BF16)
            decay_col = jnp.transpose(jnp.broadcast_to(jnp.exp(b_last), (GLA_DK, GLA_DK)))
            s_ref[h] = state * jnp.concatenate([decay_col, decay_col], axis=1) + _dot_tn(kd, vh)

            var = jnp.mean(o * o, axis=-1, keepdims=True)
            on = o * lax.rsqrt(var + 1e-6) * ng_ref[:, vs]
            rr = r_ref[rows, vs].astype(F32)
            o_ref[rows, vs] = (on * (rr / (1.0 + jnp.exp(-rr)))).astype(o_ref.dtype)
        return carry

    lax.fori_loop(0, n_chunks, chunk, 0)


def _gla(p, wa2p, ba, ng):
    s = p.shape[0]
    t = min(256, s)
    col = lambda width, off: pl.BlockSpec((t, width), lambda i: (i, off // width))
    full = lambda shape: pl.BlockSpec(shape, lambda i: (0,) * len(shape))
    return pl.pallas_call(
        functools.partial(_gla_kernel, n_chunks=t // GLA_CHUNK),
        grid=(s // t,),
        in_specs=[col(512, COL_Q), col(512, COL_K), col(1024, COL_V), col(1024, COL_R), col(LANE, COL_GL),
                  full(wa2p.shape), full(ba.shape), full(ng.shape)],
        out_specs=pl.BlockSpec((t, GROUP_W), lambda i: (i, 0)),
        out_shape=jax.ShapeDtypeStruct((s, GROUP_W), BF16),
        scratch_shapes=[pltpu.VMEM((GLA_HEADS, GLA_DK, GLA_DV), F32)],
        compiler_params=_cparams(("arbitrary",), 32),
        name="gla",
    )(p, p, p, p, p, wa2p, ba, ng)


def _gelu(x):
    return 0.5 * x * (1.0 + lax.erf(x * 0.7071067811865476))


def _gmlp_kernel(u_ref, v_ref, lg_ref, lb_ref, ws_ref, bias_ref, ng_ref, o_ref, buf_ref, *, n_chunks):
    row = lax.broadcasted_iota(jnp.int32, (GM_CHUNK, GM_CHUNK), 0)
    col = lax.broadcasted_iota(jnp.int32, (GM_CHUNK, GM_CHUNK), 1)
    causal = col <= row
    v = _gelu(v_ref[...].astype(F32))
    mu = jnp.mean(v, axis=-1, keepdims=True)
    d = v - mu
    var = jnp.mean(d * d, axis=-1, keepdims=True)
    vb = (d * lax.rsqrt(var + 1e-5) * lg_ref[...] + lb_ref[...]).astype(BF16)
    for g in range(GM_GROUPS):
        cs = slice(GM_CH * g, GM_CH * (g + 1))
        w = jnp.where(causal, ws_ref[g], 0.0).astype(BF16)
        for c in range(n_chunks):
            rs = slice(GM_CHUNK * c, GM_CHUNK * (c + 1))
            mixed = _dot(w, vb[rs, cs]) + bias_ref[:, cs]
            buf_ref[rs, cs] = _gelu(u_ref[rs, cs].astype(F32)) * mixed
    out = buf_ref[...]
    ms = jnp.mean(out * out, axis=-1, keepdims=True)
    o_ref[...] = (out * lax.rsqrt(ms + 1e-6) * ng_ref[...]).astype(o_ref.dtype)


def _gmlp(p, lg, lb, ws, bias, ng):
    s = p.shape[0]
    t = min(256, s)
    col = lambda off: pl.BlockSpec((t, GROUP_W), lambda i: (i, off // GROUP_W))
    full = lambda shape: pl.BlockSpec(shape, lambda i: (0,) * len(shape))
    return pl.pallas_call(
        functools.partial(_gmlp_kernel, n_chunks=t // GM_CHUNK),
        grid=(s // t,),
        in_specs=[col(COL_U), col(COL_VG), full(lg.shape), full(lb.shape), full(ws.shape), full(bias.shape),
                  full(ng.shape)],
        out_specs=pl.BlockSpec((t, GROUP_W), lambda i: (i, 0)),
        out_shape=jax.ShapeDtypeStruct((s, GROUP_W), BF16),
        scratch_shapes=[pltpu.VMEM((t, GROUP_W), F32)],
        compiler_params=_cparams(("parallel",), 32),
        name="gmlp",
    )(p, p, lg, lb, ws, bias, ng)


CONV_HALO = 8


def _sconv_kernel(b_ref, c_ref, h_ref, w_ref, ng_ref, o_ref, z_ref):
    t = b_ref.shape[0]

    @pl.when(pl.program_id(0) == 0)
    def _():
        z_ref[0:CONV_HALO, :] = jnp.zeros((CONV_HALO, GROUP_W), F32)

    z = c_ref[...].astype(F32) * h_ref[...].astype(F32)
    z_ref[CONV_HALO:CONV_HALO + t, :] = z
    z1 = z_ref[CONV_HALO - 1:CONV_HALO - 1 + t, :]
    z2 = z_ref[CONV_HALO - 2:CONV_HALO - 2 + t, :]
    y = w_ref[0:1, :] * z2 + w_ref[1:2, :] * z1 + w_ref[2:3, :] * z
    z_ref[0:CONV_HALO, :] = z[t - CONV_HALO:t, :]
    out = b_ref[...].astype(F32) * y
    ms = jnp.mean(out * out, axis=-1, keepdims=True)
    o_ref[...] = (out * lax.rsqrt(ms + 1e-6) * ng_ref[...]).astype(o_ref.dtype)


def _sconv(p, w, ng):
    s = p.shape[0]
    t = min(256, s)
    col = lambda off: pl.BlockSpec((t, GROUP_W), lambda i: (i, off // GROUP_W))
    full = lambda shape: pl.BlockSpec(shape, lambda i: (0,) * len(shape))
    return pl.pallas_call(
        _sconv_kernel,
        grid=(s // t,),
        in_specs=[col(COL_CB), col(COL_CC), col(COL_CH), full(w.shape), full(ng.shape)],
        out_specs=pl.BlockSpec((t, GROUP_W), lambda i: (i, 0)),
        out_shape=jax.ShapeDtypeStruct((s, GROUP_W), BF16),
        scratch_shapes=[pltpu.VMEM((t + CONV_HALO, GROUP_W), F32)],
        compiler_params=_cparams(("arbitrary",), 32),
        name="sconv",
    )(p, p, p, w, ng)


def _mla_proj_kernel(cq_ref, ckv_ref, kr_ref, krs_ref, qn_ref, kvn_ref, wq_ref, wqs_ref, wk_ref, wv_ref,
                     cos_ref, sin_ref, q_out, k_out, v_out):
    def rms(ref, g_ref):
        t = ref[...].astype(F32)
        return (t * lax.rsqrt(jnp.mean(t * t, axis=-1, keepdims=True) + 1e-6) * g_ref[...]).astype(BF16)

    cqn = rms(cq_ref, qn_ref)
    ckvn = rms(ckv_ref, kvn_ref)
    qm = _dot(cqn, wq_ref[...])
    qsw = _dot(cqn, wqs_ref[...])
    kn = _dot(ckvn, wk_ref[...])
    vv = _dot(ckvn, wv_ref[...])
    cos = cos_ref[...]
    sin = sin_ref[...]
    scale = (MLA_NOPE + MLA_ROPE) ** -0.5 * LOG2_E
    kr_rot = (kr_ref[...].astype(F32) * cos + krs_ref[...].astype(F32) * sin).astype(k_out.dtype)
    ones_col = jnp.where(lax.broadcasted_iota(jnp.int32, (cos.shape[0], LANE), 1) == 0, 1.0, 0.0).astype(v_out.dtype)
    for h in range(MLA_HEADS):
        lo = MLA_QK_PAD * h
        v_out[:, lo:lo + LANE] = vv[:, LANE * h:LANE * (h + 1)].astype(v_out.dtype)
        v_out[:, lo + LANE:lo + 2 * LANE] = ones_col
        q_out[:, lo:lo + LANE] = (qm[:, lo:lo + LANE] * scale).astype(q_out.dtype)
        q_out[:, lo + LANE:lo + 2 * LANE] = (
            (qm[:, lo + LANE:lo + 2 * LANE] * cos + qsw[:, LANE * h:LANE * (h + 1)] * sin) * scale
        ).astype(q_out.dtype)
        k_out[:, lo:lo + LANE] = kn[:, LANE * h:LANE * (h + 1)].astype(k_out.dtype)
        k_out[:, lo + LANE:lo + 2 * LANE] = kr_rot


def _mla_proj(p, qn, kvn, wq, wqs, wk, wv, cos, sin):
    s = p.shape[0]
    t = min(256, s)
    full = lambda shape: pl.BlockSpec(shape, lambda i: (0,) * len(shape))
    rowb = lambda width: pl.BlockSpec((t, width), lambda i: (i, 0))
    qk_w = MLA_HEADS * MLA_QK_PAD
    return pl.pallas_call(
        _mla_proj_kernel,
        grid=(s // t,),
        in_specs=[pl.BlockSpec((t, MLA_Q_LORA), lambda i: (i, COL_CQ // MLA_Q_LORA)),
                  pl.BlockSpec((t, MLA_KV_LORA), lambda i: (i, COL_CKV // MLA_KV_LORA)),
                  pl.BlockSpec((t, LANE), lambda i: (i, COL_KR // LANE)),
                  pl.BlockSpec((t, LANE), lambda i: (i, COL_KRS // LANE)),
                  full(qn.shape), full(kvn.shape), full(wq.shape), full(wqs.shape), full(wk.shape), full(wv.shape),
                  rowb(LANE), rowb(LANE)],
        out_specs=[rowb(qk_w), rowb(qk_w), rowb(qk_w)],
        out_shape=[jax.ShapeDtypeStruct((s, qk_w), BF16)] * 3,
        compiler_params=_cparams(("parallel",), 40),
        name="mla_proj",
    )(p, p, p, p, qn, kvn, wq, wqs, wk, wv, cos, sin)


FLASH_TQ = 512
FLASH_TK = 1024


def _flash_kernel(qi_ref, ki_ref, q_ref, k_ref, v_ref, ng_ref, o_ref, acc_ref, m_ref, *, tq, tk):
    step = pl.program_id(0)
    qi = qi_ref[step]
    ki = ki_ref[step]
    last_ki = (qi * tq) // tk

    @pl.when(ki == 0)
    def _():
        m_ref[...] = jnp.full(m_ref.shape, -jnp.inf, F32)
        acc_ref[...] = jnp.zeros_like(acc_ref)

    def accumulate(masked):
        if masked:
            row = qi * tq + lax.broadcasted_iota(jnp.int32, (tq, tk), 0)
            col = ki * tk + lax.broadcasted_iota(jnp.int32, (tq, tk), 1)
            visible = col <= row
        for h in range(MLA_HEADS):
            hs = slice(MLA_QK_PAD * h, MLA_QK_PAD * (h + 1))
            sc = _dot_nt(q_ref[:, hs], k_ref[:, hs])
            if masked:
                sc = jnp.where(visible, sc, -jnp.inf)
            m_old = m_ref[h]
            m_new = jnp.maximum(m_old, jnp.max(sc, axis=-1, keepdims=True))
            pr = jnp.exp2(sc - m_new).astype(BF16)
            acc_ref[:, hs] = jnp.exp2(m_old - m_new) * acc_ref[:, hs] + _dot(pr, v_ref[:, hs])
            m_ref[h] = m_new

    @pl.when(ki < last_ki)
    def _():
        accumulate(False)

    @pl.when(ki == last_ki)
    def _():
        accumulate(True)
        outs = []
        for h in range(MLA_HEADS):
            lo = MLA_QK_PAD * h
            outs.append(acc_ref[:, lo:lo + MLA_V] / acc_ref[:, lo + MLA_V:lo + MLA_V + 1])
        o = jnp.concatenate(outs, axis=1)
        ms = jnp.mean(o * o, axis=-1, keepdims=True)
        o_ref[...] = (o * lax.rsqrt(ms + 1e-6) * ng_ref[...]).astype(o_ref.dtype)


def _flash(q, k, v, ng):
    s = q.shape[0]
    tk = min(FLASH_TK, s)
    tq = min(FLASH_TQ, tk)
    nq = s // tq
    pairs = [(a, b) for a in range(nq) for b in range((a * tq) // tk + 1)]
    qi_tab = jnp.asarray([a for a, _ in pairs], jnp.int32)
    ki_tab = jnp.asarray([b for _, b in pairs], jnp.int32)
    qk_w = MLA_HEADS * MLA_QK_PAD
    grid_spec = pltpu.PrefetchScalarGridSpec(
        num_scalar_prefetch=2,
        grid=(len(pairs),),
        in_specs=[pl.BlockSpec((tq, qk_w), lambda t, qi, ki: (qi[t], 0)),
                  pl.BlockSpec((tk, qk_w), lambda t, qi, ki: (ki[t], 0)),
                  pl.BlockSpec((tk, qk_w), lambda t, qi, ki: (ki[t], 0)),
                  pl.BlockSpec((1, GROUP_W), lambda t, qi, ki: (0, 0))],
        out_specs=pl.BlockSpec((tq, GROUP_W), lambda t, qi, ki: (qi[t], 0)),
        scratch_shapes=[pltpu.VMEM((tq, qk_w), F32),
                        pltpu.VMEM((MLA_HEADS, tq, 1), F32)],
    )
    return pl.pallas_call(
        functools.partial(_flash_kernel, tq=tq, tk=tk),
        grid_spec=grid_spec,
        out_shape=jax.ShapeDtypeStruct((s, GROUP_W), BF16),
        compiler_params=_cparams(("arbitrary",), 56),
        name="flash",
    )(qi_tab, ki_tab, q, k, v, ng)


def _mm_out_kernel(a0_ref, a1_ref, a2_ref, a3_ref, w_ref, x_ref, o_ref):
    acc = _dot(a0_ref[...], w_ref[0:GROUP_W, :])
    acc = acc + _dot(a1_ref[...], w_ref[GROUP_W:2 * GROUP_W, :])
    acc = acc + _dot(a2_ref[...], w_ref[2 * GROUP_W:3 * GROUP_W, :])
    acc = acc + _dot(a3_ref[...], w_ref[3 * GROUP_W:4 * GROUP_W, :])
    o_ref[...] = ALPHA * x_ref[...] + acc


def _mm_out(a0, a1, a2, a3, w, x):
    s = x.shape[0]
    tm = min(512, s)
    tn = 1024
    a_spec = pl.BlockSpec((tm, GROUP_W), lambda i, j: (i, 0))
    return pl.pallas_call(
        _mm_out_kernel,
        grid=(s // tm, D_MODEL // tn),
        in_specs=[a_spec, a_spec, a_spec, a_spec,
                  pl.BlockSpec((D_MODEL, tn), lambda i, j: (0, j)),
                  pl.BlockSpec((tm, tn), lambda i, j: (i, j))],
        out_specs=pl.BlockSpec((tm, tn), lambda i, j: (i, j)),
        out_shape=jax.ShapeDtypeStruct((s, D_MODEL), F32),
        compiler_params=_cparams(("parallel", "arbitrary"), 48),
        name="mm_out",
    )(a0, a1, a2, a3, w, x)


def _layer_norm(t, g, b):
    mu = jnp.mean(t, axis=-1, keepdims=True)
    d = t - mu
    var = jnp.mean(d * d, axis=-1, keepdims=True)
    return d * lax.rsqrt(var + 1e-5) * g + b


def _ln_route_kernel(y_ref, g_ref, b_ref, wh_ref, wl_ref, rb_ref, x_out, route_out, xtok_hbm, tbuf, sem, *, tm):
    i = pl.program_id(0)
    slot = lax.rem(i, 2)

    def wait_rows(buf_slot):
        pltpu.make_async_copy(tbuf.at[buf_slot], tbuf.at[buf_slot], sem.at[buf_slot]).wait()

    x1 = _layer_norm(y_ref[...], g_ref[...], b_ref[...])
    x_out[...] = x1

    @pl.when(i >= 2)
    def _():
        wait_rows(slot)

    for c in range(TOK_CHUNKS):
        tbuf[slot, c] = x1[:, LANE * c:LANE * (c + 1)]
    for j in range(tm):
        pltpu.make_async_copy(tbuf.at[slot, :, pl.ds(j, 1), :], xtok_hbm.at[i * tm + j], sem.at[slot]).start()

    @pl.when(i == pl.num_programs(0) - 1)
    def _():
        wait_rows(slot)

        @pl.when(i >= 1)
        def _():
            wait_rows(1 - slot)

    x_hi = x1.astype(BF16)
    x_lo = (x1 - x_hi.astype(F32)).astype(BF16)
    logits = _dot(x_hi, wh_ref[...]) + _dot(x_lo, wh_ref[...]) + _dot(x_hi, wl_ref[...]) + rb_ref[...]
    lane = lax.broadcasted_iota(jnp.int32, logits.shape, 1).astype(F32)
    neg = -jnp.inf
    big = 1e9
    is_grp = lane < MOE_GROUPS
    lg = jnp.where(is_grp, logits, neg)
    mg = jnp.max(lg, axis=-1, keepdims=True)
    gsel = jnp.min(jnp.where(lg == mg, lane, big), axis=-1, keepdims=True)
    pg_sel = 1.0 / jnp.sum(jnp.where(is_grp, jnp.exp(lg - mg), 0.0), axis=-1, keepdims=True)
    lo = MOE_GROUPS + MOE_PER_GROUP * gsel
    le = jnp.where((lane >= lo) & (lane < lo + MOE_PER_GROUP), logits, neg)
    v1 = jnp.max(le, axis=-1, keepdims=True)
    i1 = jnp.min(jnp.where(le == v1, lane, big), axis=-1, keepdims=True)
    le2 = jnp.where(lane == i1, neg, le)
    v2 = jnp.max(le2, axis=-1, keepdims=True)
    i2 = jnp.min(jnp.where(le2 == v2, lane, big), axis=-1, keepdims=True)
    e = jnp.exp(v2 - v1)
    w1 = pg_sel / (1.0 + e)
    w2 = pg_sel * e / (1.0 + e)
    route = jnp.where(lane == 0.0, i1 - MOE_GROUPS,
                      jnp.where(lane == 1.0, i2 - MOE_GROUPS,
                                jnp.where(lane == 2.0, w1, jnp.where(lane == 3.0, w2, 0.0))))
    route_out[...] = route


def _ln_route(y, g, b, wh, wl, rb):
    s = y.shape[0]
    tm = min(256, s)
    full = lambda shape: pl.BlockSpec(shape, lambda i: (0,) * len(shape))
    rowb = lambda width: pl.BlockSpec((tm, width), lambda i: (i, 0))
    x1, route, x1_tok = pl.pallas_call(
        functools.partial(_ln_route_kernel, tm=tm),
        grid=(s // tm,),
        in_specs=[rowb(D_MODEL), full(g.shape), full(b.shape), full(wh.shape), full(wl.shape), full(rb.shape)],
        out_specs=[rowb(D_MODEL), rowb(LANE), pl.BlockSpec(memory_space=pl.ANY)],
        out_shape=[jax.ShapeDtypeStruct((s, D_MODEL), F32),
                   jax.ShapeDtypeStruct((s, LANE), F32),
                   jax.ShapeDtypeStruct((s, TOK_CHUNKS, 1, LANE), F32)],
        scratch_shapes=[pltpu.VMEM((2, TOK_CHUNKS, tm, LANE), F32), pltpu.SemaphoreType.DMA((2,))],
        compiler_params=_cparams(("arbitrary",), 48),
        name="ln_route",
    )(y, g, b, wh, wl, rb)
    return x1, x1_tok, route


def _moe_kernel(te_ref, tv_ref, nt_ref, src_ref, nxt_ref, dst_ref, x_hbm, wg_ref, wu_ref, wd_ref, ws_ref, y_hbm,
                gbuf, hb, obuf, gsem, ssem):
    t = pl.program_id(0)
    n_used = nt_ref[0]
    slot = lax.rem(t, 2)

    def gather_row(idx_ref, buf_slot, i):
        return pltpu.make_async_copy(x_hbm.at[idx_ref[i]], gbuf.at[buf_slot, :, pl.ds(i, 1), :], gsem.at[buf_slot])

    def wait_gather(buf_slot):
        pltpu.make_async_copy(gbuf.at[buf_slot], gbuf.at[buf_slot], gsem.at[buf_slot]).wait()

    def scatter_row(i):
        return pltpu.make_async_copy(obuf.at[:, pl.ds(i, 1), :], y_hbm.at[dst_ref[i]], ssem)

    def start_scatter(n_rows):
        def group(g, carry):
            for u in range(SCATTER_UNROLL):
                scatter_row(g * SCATTER_UNROLL + u).start()
            return carry

        def single(i, carry):
            scatter_row(i).start()
            return carry
        n_groups = n_rows // SCATTER_UNROLL
        lax.fori_loop(0, n_groups, group, 0)
        lax.fori_loop(n_groups * SCATTER_UNROLL, n_rows, single, 0)

    def wait_scatter(n_rows):
        rows = obuf.at[:, pl.ds(0, n_rows), :]
        pltpu.make_async_copy(rows, rows, ssem).wait()

    @pl.when(t == 0)
    def _():
        def body(i, carry):
            gather_row(src_ref, 0, i).start()
            return carry
        lax.fori_loop(0, MOE_TILE, body, 0, unroll=8)

    @pl.when(t < n_used)
    def _():
        wait_gather(slot)
        for c in range(TOK_CHUNKS):
            hb[:, LANE * c:LANE * (c + 1)] = gbuf[slot, c].astype(BF16)
        for i in range(MOE_TILE):
            gather_row(nxt_ref, 1 - slot, i).start()
        h = hb[...]
        gate = _dot(h, wg_ref[0])
        up = _dot(h, wu_ref[0])
        a = (gate / (1.0 + jnp.exp(-gate)) * up).astype(BF16)

        @pl.when(t > 0)
        def _():
            wait_scatter(tv_ref[jnp.maximum(t - 1, 0)])

        ws = ws_ref[...]
        for c2 in range(TOK_CHUNKS // 2):
            yc = _dot(a, wd_ref[0, :, 2 * LANE * c2:2 * LANE * (c2 + 1)]) * ws
            obuf[2 * c2] = yc[:, :LANE]
            obuf[2 * c2 + 1] = yc[:, LANE:]
        start_scatter(tv_ref[t])

        @pl.when(t == n_used - 1)
        def _():
            wait_gather(1 - slot)
            wait_scatter(tv_ref[t])


def _moe(tile_expert, tile_valid, n_tiles, src_token, dst_row, x_tok, wg, wu, wd, w_slot):
    p_rows = src_token.shape[0]
    tm = MOE_TILE
    n_blocks = p_rows // tm
    smem_blk = lambda fn: pl.BlockSpec((tm,), fn, memory_space=pltpu.SMEM)
    grid_spec = pltpu.PrefetchScalarGridSpec(
        num_scalar_prefetch=3,
        grid=(n_blocks,),
        in_specs=[smem_blk(lambda t, te, tv, nt: (t,)),
                  smem_blk(lambda t, te, tv, nt: (jnp.minimum(t + 1, n_blocks - 1),)),
                  smem_blk(lambda t, te, tv, nt: (t,)),
                  pl.BlockSpec(memory_space=pl.ANY),
                  pl.BlockSpec((1, D_MODEL, D_EXPERT), lambda t, te, tv, nt: (te[t], 0, 0)),
                  pl.BlockSpec((1, D_MODEL, D_EXPERT), lambda t, te, tv, nt: (te[t], 0, 0)),
                  pl.BlockSpec((1, D_EXPERT, D_MODEL), lambda t, te, tv, nt: (te[t], 0, 0)),
                  pl.BlockSpec((tm, 1), lambda t, te, tv, nt: (t, 0))],
        out_specs=pl.BlockSpec(memory_space=pl.ANY),
        scratch_shapes=[pltpu.VMEM((2, TOK_CHUNKS, tm, LANE), F32),
                        pltpu.VMEM((tm, D_MODEL), BF16),
                        pltpu.VMEM((TOK_CHUNKS, tm, LANE), F32),
                        pltpu.SemaphoreType.DMA((2,)),
                        pltpu.SemaphoreType.DMA(())],
    )
    return pl.pallas_call(
        _moe_kernel,
        grid_spec=grid_spec,
        out_shape=jax.ShapeDtypeStruct((2 * x_tok.shape[0], TOK_CHUNKS, 1, LANE), F32),
        compiler_params=_cparams(("arbitrary",), 56),
        name="moe",
    )(tile_expert, tile_valid, n_tiles, src_token, src_token, dst_row, x_tok, wg, wu, wd, w_slot)


def _ln_add_kernel(x_ref, g_ref, b_ref, y_hbm, o_ref, ybuf, vbuf, sem, *, tm):
    i = pl.program_id(0)
    slot = lax.rem(i, 2)

    def start_gather(tile, buf_slot):
        base = tile * (2 * tm)

        def body(j, carry):
            for k in range(2):
                pltpu.make_async_copy(y_hbm.at[base + 2 * j + k], ybuf.at[buf_slot, k, :, pl.ds(j, 1), :],
                                      sem.at[buf_slot]).start()
            return carry
        lax.fori_loop(0, tm, body, 0, unroll=4)

    @pl.when(i == 0)
    def _():
        start_gather(0, 0)

    @pl.when(i + 1 < pl.num_programs(0))
    def _():
        start_gather(i + 1, 1 - slot)

    pltpu.make_async_copy(ybuf.at[slot], ybuf.at[slot], sem.at[slot]).wait()
    for c in range(TOK_CHUNKS):
        cs = slice(LANE * c, LANE * (c + 1))
        vbuf[:, cs] = ALPHA * x_ref[:, cs] + (ybuf[slot, 0, c] + ybuf[slot, 1, c])
    o_ref[...] = _layer_norm(vbuf[...], g_ref[...], b_ref[...])


def _ln_add(x, y_tok, g, b):
    s = x.shape[0]
    tm = min(256, s)
    full = lambda shape: pl.BlockSpec(shape, lambda i: (0,) * len(shape))
    return pl.pallas_call(
        functools.partial(_ln_add_kernel, tm=tm),
        grid=(s // tm,),
        in_specs=[pl.BlockSpec((tm, D_MODEL), lambda i: (i, 0)), full(g.shape), full(b.shape),
                  pl.BlockSpec(memory_space=pl.ANY)],
        out_specs=pl.BlockSpec((tm, D_MODEL), lambda i: (i, 0)),
        out_shape=jax.ShapeDtypeStruct((s, D_MODEL), F32),
        scratch_shapes=[pltpu.VMEM((2, 2, TOK_CHUNKS, tm, LANE), F32),
                        pltpu.VMEM((tm, D_MODEL), F32),
                        pltpu.SemaphoreType.DMA((2,))],
        compiler_params=_cparams(("arbitrary",), 48),
        name="ln_add",
    )(x, g, b, y_tok)


def _pack_w_in(w):
    o = 0
    parts = {}
    for name, width in (("q", 512), ("k", 512), ("v", 1024), ("r", 1024), ("gl", GLA_GATE_RANK), ("u", 1024),
                        ("vg", 1024), ("cb", 1024), ("cc", 1024), ("ch", 1024), ("cq", MLA_Q_LORA),
                        ("ckv", MLA_KV_LORA), ("kr", MLA_ROPE)):
        parts[name] = w[:, o:o + width]
        o += width
    zeros = lambda n: jnp.zeros((w.shape[0], n), w.dtype)
    half = MLA_ROPE // 2
    kr = parts["kr"]
    cols = [parts["cq"], parts["ckv"], parts["q"], parts["k"], parts["v"], parts["r"], parts["u"], parts["vg"],
            parts["cb"], parts["cc"], parts["ch"],
            kr, zeros(LANE - MLA_ROPE),
            kr[:, half:], kr[:, :half], zeros(LANE - MLA_ROPE),
            parts["gl"], zeros(LANE - GLA_GATE_RANK)]
    packed = jnp.concatenate(cols, axis=1)
    packed = jnp.concatenate([packed, zeros(P_COLS - packed.shape[1])], axis=1)
    return packed.astype(BF16)


def _pack_mla_weights(wuq, wukv):
    half = MLA_ROPE // 2
    wq3 = wuq.reshape(MLA_Q_LORA, MLA_HEADS, MLA_NOPE + MLA_ROPE)
    rope = wq3[:, :, MLA_NOPE:]
    zq = jnp.zeros((MLA_Q_LORA, MLA_HEADS, MLA_QK_PAD - MLA_NOPE - MLA_ROPE), wuq.dtype)
    wq = jnp.concatenate([wq3, zq], axis=2).reshape(MLA_Q_LORA, MLA_HEADS * MLA_QK_PAD)
    zs = jnp.zeros((MLA_Q_LORA, MLA_HEADS, LANE - MLA_ROPE), wuq.dtype)
    wqs = jnp.concatenate([rope[:, :, half:], rope[:, :, :half], zs], axis=2).reshape(MLA_Q_LORA, MLA_HEADS * LANE)
    wkv3 = wukv.reshape(MLA_KV_LORA, MLA_HEADS, MLA_NOPE + MLA_V)
    wk = wkv3[:, :, :MLA_NOPE].reshape(MLA_KV_LORA, MLA_HEADS * MLA_NOPE)
    wv = wkv3[:, :, MLA_NOPE:].reshape(MLA_KV_LORA, MLA_HEADS * MLA_V)
    return wq.astype(BF16), wqs.astype(BF16), wk.astype(BF16), wv.astype(BF16)


def _rope_tables(s):
    pos = jnp.arange(s, dtype=F32)
    inv_freq = ROPE_BASE ** (-jnp.arange(0, MLA_ROPE, 2, dtype=F32) / MLA_ROPE)
    ang = pos[:, None] * inv_freq[None, :]
    cos, sin = jnp.cos(ang), jnp.sin(ang)
    z = jnp.zeros((s, LANE - MLA_ROPE), F32)
    return jnp.concatenate([cos, cos, z], axis=1), jnp.concatenate([-sin, sin, z], axis=1)


def _dispatch_plan(route, s):
    n_assign = 2 * s
    p_rows = n_assign + N_EXPERTS * MOE_TILE
    eid = route[:, 0:2].astype(jnp.int32).reshape(n_assign)
    wgt = route[:, 2:4].reshape(n_assign)
    onehot = (eid[:, None] == jnp.arange(N_EXPERTS, dtype=jnp.int32)[None, :]).astype(jnp.int32)
    csum = jnp.cumsum(onehot, axis=0)
    rank = jnp.sum(onehot * (csum - 1), axis=1)
    counts = csum[-1]
    tiles_e = (counts + MOE_TILE - 1) // MOE_TILE
    tile_end = jnp.cumsum(tiles_e)
    tile_start = tile_end - tiles_e
    slot = tile_start[eid] * MOE_TILE + rank
    slot_assign = jnp.full((p_rows,), -1, jnp.int32).at[slot].set(jnp.arange(n_assign, dtype=jnp.int32))
    is_pad = slot_assign < 0
    src_token = jnp.where(is_pad, 0, slot_assign // 2)
    dst_row = jnp.maximum(slot_assign, 0)
    w_slot = jnp.where(is_pad, 0.0, wgt[dst_row])
    n_blocks = p_rows // MOE_TILE
    tile_valid = jnp.sum(jnp.logical_not(is_pad).reshape(n_blocks, MOE_TILE).astype(jnp.int32), axis=1)
    tile_ids = jnp.arange(n_blocks, dtype=jnp.int32)
    tile_expert = jnp.minimum(jnp.sum((tile_end[None, :] <= tile_ids[:, None]).astype(jnp.int32), axis=1),
                              N_EXPERTS - 1)
    n_tiles = tile_end[-1:].astype(jnp.int32)
    return src_token, dst_row, w_slot.reshape(p_rows, 1), tile_expert, tile_valid, n_tiles


def kernel(x, w_in, gla_wa2, gla_ba, gla_norm, gm_ln_g, gm_ln_b, gm_ws, gm_bs, gm_norm, sc_conv, sc_norm, mla_q_norm, mla_kv_norm, mla_wuq, mla_wukv, mla_norm, w_o, ln1_g, ln1_b, router_g_w, router_g_b, router_e_w, router_e_b, exp_w_gate, exp_w_up, exp_w_down, ln2_g, ln2_b):
    bsz, s, _ = x.shape
    assert bsz == 1
    xc = x.reshape(s, D_MODEL)
    cos_t, sin_t = _rope_tables(s)
    row = lambda v: v.reshape(1, -1)
    wg_all = exp_w_gate.astype(BF16).reshape(DEPTH * N_EXPERTS, D_MODEL, D_EXPERT)
    wu_all = exp_w_up.astype(BF16).reshape(DEPTH * N_EXPERTS, D_MODEL, D_EXPERT)
    wd_all = exp_w_down.astype(BF16).reshape(DEPTH * N_EXPERTS, D_EXPERT, D_MODEL)
    for l in range(DEPTH):
        p = _mm_in(xc, _pack_w_in(w_in[l]))

        wa2p = jnp.concatenate(
            [gla_wa2[l], jnp.zeros((LANE - GLA_GATE_RANK, GLA_HEADS * GLA_DK), F32)], axis=0).astype(BF16)
        out_a = _gla(p, wa2p, row(gla_ba[l]), row(gla_norm[l]))

        gm_bias = jnp.repeat(gm_bs[l].T, GM_CH, axis=1)
        out_b = _gmlp(p, row(gm_ln_g[l]), row(gm_ln_b[l]), gm_ws[l], gm_bias, row(gm_norm[l]))

        out_c = _sconv(p, sc_conv[l], row(sc_norm[l]))

        wq, wqs, wk, wv = _pack_mla_weights(mla_wuq[l], mla_wukv[l])
        q_att, k_att, v_att = _mla_proj(p, row(mla_q_norm[l]), row(mla_kv_norm[l]), wq, wqs, wk, wv, cos_t, sin_t)
        out_d = _flash(q_att, k_att, v_att, row(mla_norm[l]))

        y = _mm_out(out_a, out_b, out_c, out_d, w_o[l].astype(BF16), xc)

        wr = jnp.concatenate([router_g_w[l], router_e_w[l],
                              jnp.zeros((D_MODEL, LANE - MOE_GROUPS - N_EXPERTS), F32)], axis=1)
        wr_hi = wr.astype(BF16)
        wr_lo = (wr - wr_hi.astype(F32)).astype(BF16)
        rb = jnp.concatenate([router_g_b[l], router_e_b[l], jnp.zeros((LANE - MOE_GROUPS - N_EXPERTS,), F32)])
        x1, x1_tok, route = _ln_route(y, row(ln1_g[l]), row(ln1_b[l]), wr_hi, wr_lo, row(rb))

        src_token, dst_row, w_slot, tile_expert, tile_valid, n_tiles = _dispatch_plan(route, s)
        y_tok = _moe(tile_expert + l * N_EXPERTS, tile_valid, n_tiles, src_token, dst_row, x1_tok,
                     wg_all, wu_all, wd_all, w_slot)
        xc = _ln_add(x1, y_tok, row(ln2_g[l]), row(ln2_b[l]))
    return xc.reshape(bsz, s, D_MODEL)
```

```python
import functools

import jax
import jax.numpy as jnp
from jax import lax
from jax.experimental import pallas as pl
from jax.experimental.pallas import tpu as pltpu

F32 = jnp.float32
BF16 = jnp.bfloat16

D_MODEL = 4096
DEPTH = 2
GROUP_W = 1024

GLA_HEADS = 4
GLA_DK = 128
GLA_DV = 256
GLA_GATE_RANK = 16
GLA_TAU = 16.0
GLA_CHUNK = 64
GLA_SUB = 16

GM_GROUPS = 8
GM_CH = 128
GM_CHUNK = 128

MLA_HEADS = 8
MLA_NOPE = 128
MLA_ROPE = 64
MLA_V = 128
MLA_Q_LORA = 768
MLA_KV_LORA = 256
ROPE_BASE = 10000.0
MLA_QK_PAD = 256

MOE_GROUPS = 4
MOE_PER_GROUP = 8
N_EXPERTS = 32
D_EXPERT = 512

ALPHA = (2.0 * DEPTH) ** 0.25
LOG2_E = 1.4426950408889634

LANE = 128
MIB = 1024 * 1024

COL_CQ = 0
COL_CKV = 768
COL_Q = 1024
COL_K = 1536
COL_V = 2048
COL_R = 3072
COL_U = 4096
COL_VG = 5120
COL_CB = 6144
COL_CC = 7168
COL_CH = 8192
COL_KR = 9216
COL_KRS = 9344
COL_GL = 9472
P_COLS = 9728

MOE_TILE = 256
SCATTER_UNROLL = 8
HALF_D = D_MODEL // 2
TOK_WORDS = HALF_D // LANE
U32 = jnp.uint32
HIGH_HALF = 0xFFFF0000


def _pack_rows(lo_f32, hi_f32):
    return (lax.bitcast_convert_type(lo_f32, U32) >> 16) | lax.bitcast_convert_type(hi_f32, U32)


def _unpack_rows(words):
    return (lax.bitcast_convert_type(words << 16, F32),
            lax.bitcast_convert_type(words & jnp.uint32(HIGH_HALF), F32))


def _cparams(sem, vmem_mib):
    return pltpu.CompilerParams(dimension_semantics=sem, vmem_limit_bytes=vmem_mib * MIB)


def _dot(a, b):
    return jnp.dot(a, b, preferred_element_type=F32)


def _dot_nt(a, b):
    return lax.dot_general(a, b, (((1,), (1,)), ((), ())), preferred_element_type=F32)


def _dot_tn(a, b):
    return lax.dot_general(a, b, (((0,), (0,)), ((), ())), preferred_element_type=F32)


def _mm_in_kernel(x_ref, w_ref, o_ref, xb_ref):
    @pl.when(pl.program_id(1) == 0)
    def _():
        xb_ref[...] = x_ref[...].astype(BF16)

    o_ref[...] = _dot(xb_ref[...], w_ref[...]).astype(o_ref.dtype)


def _mm_in(x, w):
    s, k = x.shape
    n = w.shape[1]
    tm = min(512, s)
    tn = 512
    return pl.pallas_call(
        _mm_in_kernel,
        grid=(s // tm, n // tn),
        in_specs=[pl.BlockSpec((tm, k), lambda i, j: (i, 0)),
                  pl.BlockSpec((k, tn), lambda i, j: (0, j))],
        out_specs=pl.BlockSpec((tm, tn), lambda i, j: (i, j)),
        out_shape=jax.ShapeDtypeStruct((s, n), BF16),
        scratch_shapes=[pltpu.VMEM((tm, k), BF16)],
        compiler_params=_cparams(("parallel", "arbitrary"), 48),
        name="mm_in",
    )(x, w)


def _gla_kernel(q_ref, k_ref, v_ref, r_ref, gl_ref, wa2_ref, ba_ref, ng_ref, o_ref, s_ref, *, n_chunks):
    c_len = GLA_CHUNK

    @pl.when(pl.program_id(0) == 0)
    def _():
        s_ref[...] = jnp.zeros_like(s_ref)

    row = lax.broadcasted_iota(jnp.int32, (c_len, c_len), 0)
    col = lax.broadcasted_iota(jnp.int32, (c_len, c_len), 1)
    tril = jnp.where(col <= row, 1.0, 0.0).astype(BF16)
    sub_row = lax.broadcasted_iota(jnp.int32, (GLA_SUB, c_len), 0)
    sub_col = lax.broadcasted_iota(jnp.int32, (GLA_SUB, c_len), 1)
    n_sub = c_len // GLA_SUB

    def chunk(c, carry):
        rows = pl.ds(pl.multiple_of(c * c_len, c_len), c_len)
        logit = _dot(gl_ref[rows, :], wa2_ref[...]) + ba_ref[...]
        g = (jnp.minimum(logit, 0.0) - jnp.log(1.0 + jnp.exp(-jnp.abs(logit)))) * (1.0 / GLA_TAU)
        g_hi = g.astype(BF16)
        g_lo = (g - g_hi.astype(F32)).astype(BF16)
        b_all = _dot(tril, g_hi) + _dot(tril, g_lo)

        for h in range(GLA_HEADS):
            hs = slice(GLA_DK * h, GLA_DK * (h + 1))
            vs = slice(GLA_DV * h, GLA_DV * (h + 1))
            bh = b_all[:, hs]
            qh = q_ref[rows, hs].astype(F32) * (GLA_DK ** -0.5)
            kh = k_ref[rows, hs].astype(F32)
            vh = v_ref[rows, vs]
            state = s_ref[h]

            o = _dot((qh * jnp.exp(bh)).astype(BF16), state.astype(BF16))

            att_rows = []
            for blk in range(n_sub):
                sl = slice(GLA_SUB * blk, GLA_SUB * (blk + 1))
                b_blk = bh[sl]
                q_blk = qh[sl]
                att = jnp.zeros((GLA_SUB, c_len), F32)
                if blk > 0:
                    ref = bh[GLA_SUB * blk:GLA_SUB * blk + 1, :]
                    qs = (q_blk * jnp.exp(b_blk - ref)).astype(BF16)
                    ks = (kh * jnp.exp(jnp.minimum(ref - bh, 0.0))).astype(BF16)
                    att = jnp.where(sub_col < GLA_SUB * blk, _dot_nt(qs, ks), 0.0)
                for jj in range(GLA_SUB):
                    j = GLA_SUB * blk + jj
                    t = q_blk * kh[j:j + 1, :] * jnp.exp(b_blk - bh[j:j + 1, :])
                    rs = jnp.sum(t, axis=-1, keepdims=True)
                    att = jnp.where((sub_col == j) & (sub_row >= jj), rs, att)
                att_rows.append(att)
            att_full = jnp.concatenate(att_rows, axis=0).astype(BF16)
            o = o + _dot(att_full, vh)

            b_last = bh[c_len - 1:c_len, :]
            kd = (kh * jnp.exp(b_last - bh)).astype(BF16)
            decay_col = jnp.transpose(jnp.broadcast_to(jnp.exp(b_last), (GLA_DK, GLA_DK)))
            s_ref[h] = state * jnp.concatenate([decay_col, decay_col], axis=1) + _dot_tn(kd, vh)

            var = jnp.mean(o * o, axis=-1, keepdims=True)
            on = o * lax.rsqrt(var + 1e-6) * ng_ref[:, vs]
            rr = r_ref[rows, vs].astype(F32)
            o_ref[rows, vs] = (on * (rr / (1.0 + jnp.exp(-rr)))).astype(o_ref.dtype)
        return carry

    lax.fori_loop(0, n_chunks, chunk, 0)


def _gla(p, wa2p, ba, ng):
    s = p.shape[0]
    t = min(256, s)
    col = lambda width, off: pl.BlockSpec((t, width), lambda i: (i, off // width))
    full = lambda shape: pl.BlockSpec(shape, lambda i: (0,) * len(shape))
    return pl.pallas_call(
        functools.partial(_gla_kernel, n_chunks=t // GLA_CHUNK),
        grid=(s // t,),
        in_specs=[col(512, COL_Q), col(512, COL_K), col(1024, COL_V), col(1024, COL_R), col(LANE, COL_GL),
                  full(wa2p.shape), full(ba.shape), full(ng.shape)],
        out_specs=pl.BlockSpec((t, GROUP_W), lambda i: (i, 0)),
        out_shape=jax.ShapeDtypeStruct((s, GROUP_W), BF16),
        scratch_shapes=[pltpu.VMEM((GLA_HEADS, GLA_DK, GLA_DV), F32)],
        compiler_params=_cparams(("arbitrary",), 32),
        name="gla",
    )(p, p, p, p, p, wa2p, ba, ng)


def _gelu(x):
    return 0.5 * x * (1.0 + lax.erf(x * 0.7071067811865476))


def _gmlp_kernel(u_ref, v_ref, lg_ref, lb_ref, ws_ref, bias_ref, ng_ref, o_ref, buf_ref, *, n_chunks):
    row = lax.broadcasted_iota(jnp.int32, (GM_CHUNK, GM_CHUNK), 0)
    col = lax.broadcasted_iota(jnp.int32, (GM_CHUNK, GM_CHUNK), 1)
    causal = col <= row
    v = _gelu(v_ref[...].astype(F32))
    mu = jnp.mean(v, axis=-1, keepdims=True)
    d = v - mu
    var = jnp.mean(d * d, axis=-1, keepdims=True)
    vb = (d * lax.rsqrt(var + 1e-5) * lg_ref[...] + lb_ref[...]).astype(BF16)
    for g in range(GM_GROUPS):
        cs = slice(GM_CH * g, GM_CH * (g + 1))
        w = jnp.where(causal, ws_ref[g], 0.0).astype(BF16)
        for c in range(n_chunks):
            rs = slice(GM_CHUNK * c, GM_CHUNK * (c + 1))
            mixed = _dot(w, vb[rs, cs]) + bias_ref[:, cs]
            buf_ref[rs, cs] = _gelu(u_ref[rs, cs].astype(F32)) * mixed
    out = buf_ref[...]
    ms = jnp.mean(out * out, axis=-1, keepdims=True)
    o_ref[...] = (out * lax.rsqrt(ms + 1e-6) * ng_ref[...]).astype(o_ref.dtype)


def _gmlp(p, lg, lb, ws, bias, ng):
    s = p.shape[0]
    t = min(256, s)
    col = lambda off: pl.BlockSpec((t, GROUP_W), lambda i: (i, off // GROUP_W))
    full = lambda shape: pl.BlockSpec(shape, lambda i: (0,) * len(shape))
    return pl.pallas_call(
        functools.partial(_gmlp_kernel, n_chunks=t // GM_CHUNK),
        grid=(s // t,),
        in_specs=[col(COL_U), col(COL_VG), full(lg.shape), full(lb.shape), full(ws.shape), full(bias.shape),
                  full(ng.shape)],
        out_specs=pl.BlockSpec((t, GROUP_W), lambda i: (i, 0)),
        out_shape=jax.ShapeDtypeStruct((s, GROUP_W), BF16),
        scratch_shapes=[pltpu.VMEM((t, GROUP_W), F32)],
        compiler_params=_cparams(("parallel",), 32),
        name="gmlp",
    )(p, p, lg, lb, ws, bias, ng)


CONV_HALO = 8


def _sconv_kernel(b_ref, c_ref, h_ref, w_ref, ng_ref, o_ref, z_ref):
    t = b_ref.shape[0]

    @pl.when(pl.program_id(0) == 0)
    def _():
        z_ref[0:CONV_HALO, :] = jnp.zeros((CONV_HALO, GROUP_W), F32)

    z = c_ref[...].astype(F32) * h_ref[...].astype(F32)
    z_ref[CONV_HALO:CONV_HALO + t, :] = z
    z1 = z_ref[CONV_HALO - 1:CONV_HALO - 1 + t, :]
    z2 = z_ref[CONV_HALO - 2:CONV_HALO - 2 + t, :]
    y = w_ref[0:1, :] * z2 + w_ref[1:2, :] * z1 + w_ref[2:3, :] * z
    z_ref[0:CONV_HALO, :] = z[t - CONV_HALO:t, :]
    out = b_ref[...].astype(F32) * y
    ms = jnp.mean(out * out, axis=-1, keepdims=True)
    o_ref[...] = (out * lax.rsqrt(ms + 1e-6) * ng_ref[...]).astype(o_ref.dtype)


def _sconv(p, w, ng):
    s = p.shape[0]
    t = min(256, s)
    col = lambda off: pl.BlockSpec((t, GROUP_W), lambda i: (i, off // GROUP_W))
    full = lambda shape: pl.BlockSpec(shape, lambda i: (0,) * len(shape))
    return pl.pallas_call(
        _sconv_kernel,
        grid=(s // t,),
        in_specs=[col(COL_CB), col(COL_CC), col(COL_CH), full(w.shape), full(ng.shape)],
        out_specs=pl.BlockSpec((t, GROUP_W), lambda i: (i, 0)),
        out_shape=jax.ShapeDtypeStruct((s, GROUP_W), BF16),
        scratch_shapes=[pltpu.VMEM((t + CONV_HALO, GROUP_W), F32)],
        compiler_params=_cparams(("arbitrary",), 32),
        name="sconv",
    )(p, p, p, w, ng)


def _mla_proj_kernel(cq_ref, ckv_ref, kr_ref, krs_ref, qn_ref, kvn_ref, wq_ref, wqs_ref, wk_ref, wv_ref,
                     cos_ref, sin_ref, q_out, k_out, v_out):
    def rms(ref, g_ref):
        t = ref[...].astype(F32)
        return (t * lax.rsqrt(jnp.mean(t * t, axis=-1, keepdims=True) + 1e-6) * g_ref[...]).astype(BF16)

    cqn = rms(cq_ref, qn_ref)
    ckvn = rms(ckv_ref, kvn_ref)
    qm = _dot(cqn, wq_ref[...])
    qsw = _dot(cqn, wqs_ref[...])
    kn = _dot(ckvn, wk_ref[...])
    vv = _dot(ckvn, wv_ref[...])
    cos = cos_ref[...]
    sin = sin_ref[...]
    scale = (MLA_NOPE + MLA_ROPE) ** -0.5 * LOG2_E
    kr_rot = (kr_ref[...].astype(F32) * cos + krs_ref[...].astype(F32) * sin).astype(k_out.dtype)
    ones_col = jnp.where(lax.broadcasted_iota(jnp.int32, (cos.shape[0], LANE), 1) == 0, 1.0, 0.0).astype(v_out.dtype)
    for h in range(MLA_HEADS):
        lo = MLA_QK_PAD * h
        v_out[:, lo:lo + LANE] = vv[:, LANE * h:LANE * (h + 1)].astype(v_out.dtype)
        v_out[:, lo + LANE:lo + 2 * LANE] = ones_col
        q_out[:, lo:lo + LANE] = (qm[:, lo:lo + LANE] * scale).astype(q_out.dtype)
        q_out[:, lo + LANE:lo + 2 * LANE] = (
            (qm[:, lo + LANE:lo + 2 * LANE] * cos + qsw[:, LANE * h:LANE * (h + 1)] * sin) * scale
        ).astype(q_out.dtype)
        k_out[:, lo:lo + LANE] = kn[:, LANE * h:LANE * (h + 1)].astype(k_out.dtype)
        k_out[:, lo + LANE:lo + 2 * LANE] = kr_rot


def _mla_proj(p, qn, kvn, wq, wqs, wk, wv, cos, sin):
    s = p.shape[0]
    t = min(256, s)
    full = lambda shape: pl.BlockSpec(shape, lambda i: (0,) * len(shape))
    rowb = lambda width: pl.BlockSpec((t, width), lambda i: (i, 0))
    qk_w = MLA_HEADS * MLA_QK_PAD
    return pl.pallas_call(
        _mla_proj_kernel,
        grid=(s // t,),
        in_specs=[pl.BlockSpec((t, MLA_Q_LORA), lambda i: (i, COL_CQ // MLA_Q_LORA)),
                  pl.BlockSpec((t, MLA_KV_LORA), lambda i: (i, COL_CKV // MLA_KV_LORA)),
                  pl.BlockSpec((t, LANE), lambda i: (i, COL_KR // LANE)),
                  pl.BlockSpec((t, LANE), lambda i: (i, COL_KRS // LANE)),
                  full(qn.shape), full(kvn.shape), full(wq.shape), full(wqs.shape), full(wk.shape), full(wv.shape),
                  rowb(LANE), rowb(LANE)],
        out_specs=[rowb(qk_w), rowb(qk_w), rowb(qk_w)],
        out_shape=[jax.ShapeDtypeStruct((s, qk_w), BF16)] * 3,
        compiler_params=_cparams(("parallel",), 40),
        name="mla_proj",
    )(p, p, p, p, qn, kvn, wq, wqs, wk, wv, cos, sin)


FLASH_TQ = 512
FLASH_TK = 1024


def _flash_kernel(qi_ref, ki_ref, q_ref, k_ref, v_ref, ng_ref, o_ref, acc_ref, m_ref, *, tq, tk):
    step = pl.program_id(0)
    qi = qi_ref[step]
    ki = ki_ref[step]
    last_ki = (qi * tq) // tk

    @pl.when(ki == 0)
    def _():
        m_ref[...] = jnp.full(m_ref.shape, -jnp.inf, F32)
        acc_ref[...] = jnp.zeros_like(acc_ref)

    def accumulate(masked):
        if masked:
            row = qi * tq + lax.broadcasted_iota(jnp.int32, (tq, tk), 0)
            col = ki * tk + lax.broadcasted_iota(jnp.int32, (tq, tk), 1)
            visible = col <= row
        def scores(h):
            hs = slice(MLA_QK_PAD * h, MLA_QK_PAD * (h + 1))
            return _dot_nt(q_ref[:, hs], k_ref[:, hs])

        sc = scores(0)
        for h in range(MLA_HEADS):
            hs = slice(MLA_QK_PAD * h, MLA_QK_PAD * (h + 1))
            sc_next = scores(h + 1) if h + 1 < MLA_HEADS else None
            if masked:
                sc = jnp.where(visible, sc, -jnp.inf)
            m_old = m_ref[h]
            m_new = jnp.maximum(m_old, jnp.max(sc, axis=-1, keepdims=True))
            pr = jnp.exp2(sc - m_new).astype(BF16)
            acc_ref[:, hs] = jnp.exp2(m_old - m_new) * acc_ref[:, hs] + _dot(pr, v_ref[:, hs])
            m_ref[h] = m_new
            sc = sc_next

    @pl.when(ki < last_ki)
    def _():
        accumulate(False)

    @pl.when(ki == last_ki)
    def _():
        accumulate(True)
        outs = []
        for h in range(MLA_HEADS):
            lo = MLA_QK_PAD * h
            outs.append(acc_ref[:, lo:lo + MLA_V] / acc_ref[:, lo + MLA_V:lo + MLA_V + 1])
        o = jnp.concatenate(outs, axis=1)
        ms = jnp.mean(o * o, axis=-1, keepdims=True)
        o_ref[...] = (o * lax.rsqrt(ms + 1e-6) * ng_ref[...]).astype(o_ref.dtype)


def _flash(q, k, v, ng):
    s = q.shape[0]
    tk = min(FLASH_TK, s)
    tq = min(FLASH_TQ, tk)
    nq = s // tq
    pairs = [(a, b) for a in range(nq) for b in range((a * tq) // tk + 1)]
    qi_tab = jnp.asarray([a for a, _ in pairs], jnp.int32)
    ki_tab = jnp.asarray([b for _, b in pairs], jnp.int32)
    qk_w = MLA_HEADS * MLA_QK_PAD
    grid_spec = pltpu.PrefetchScalarGridSpec(
        num_scalar_prefetch=2,
        grid=(len(pairs),),
        in_specs=[pl.BlockSpec((tq, qk_w), lambda t, qi, ki: (qi[t], 0)),
                  pl.BlockSpec((tk, qk_w), lambda t, qi, ki: (ki[t], 0)),
                  pl.BlockSpec((tk, qk_w), lambda t, qi, ki: (ki[t], 0)),
                  pl.BlockSpec((1, GROUP_W), lambda t, qi, ki: (0, 0))],
        out_specs=pl.BlockSpec((tq, GROUP_W), lambda t, qi, ki: (qi[t], 0)),
        scratch_shapes=[pltpu.VMEM((tq, qk_w), F32),
                        pltpu.VMEM((MLA_HEADS, tq, 1), F32)],
    )
    return pl.pallas_call(
        functools.partial(_flash_kernel, tq=tq, tk=tk),
        grid_spec=grid_spec,
        out_shape=jax.ShapeDtypeStruct((s, GROUP_W), BF16),
        compiler_params=_cparams(("arbitrary",), 56),
        name="flash",
    )(qi_tab, ki_tab, q, k, v, ng)


def _mm_out_kernel(a0_ref, a1_ref, a2_ref, a3_ref, w_ref, x_ref, o_ref):
    acc = _dot(a0_ref[...], w_ref[0:GROUP_W, :])
    acc = acc + _dot(a1_ref[...], w_ref[GROUP_W:2 * GROUP_W, :])
    acc = acc + _dot(a2_ref[...], w_ref[2 * GROUP_W:3 * GROUP_W, :])
    acc = acc + _dot(a3_ref[...], w_ref[3 * GROUP_W:4 * GROUP_W, :])
    o_ref[...] = ALPHA * x_ref[...] + acc


def _mm_out(a0, a1, a2, a3, w, x):
    s = x.shape[0]
    tm = min(512, s)
    tn = 1024
    a_spec = pl.BlockSpec((tm, GROUP_W), lambda i, j: (i, 0))
    return pl.pallas_call(
        _mm_out_kernel,
        grid=(s // tm, D_MODEL // tn),
        in_specs=[a_spec, a_spec, a_spec, a_spec,
                  pl.BlockSpec((D_MODEL, tn), lambda i, j: (0, j)),
                  pl.BlockSpec((tm, tn), lambda i, j: (i, j))],
        out_specs=pl.BlockSpec((tm, tn), lambda i, j: (i, j)),
        out_shape=jax.ShapeDtypeStruct((s, D_MODEL), F32),
        compiler_params=_cparams(("parallel", "arbitrary"), 48),
        name="mm_out",
    )(a0, a1, a2, a3, w, x)


def _layer_norm(t, g, b):
    mu = jnp.mean(t, axis=-1, keepdims=True)
    d = t - mu
    var = jnp.mean(d * d, axis=-1, keepdims=True)
    return d * lax.rsqrt(var + 1e-5) * g + b


def _ln_route_kernel(y_ref, g_ref, b_ref, wh_ref, wl_ref, rb_ref, x_out, route_out, xtok_hbm, tbuf, sem, *, tm):
    i = pl.program_id(0)
    slot = lax.rem(i, 2)

    def wait_rows(buf_slot):
        pltpu.make_async_copy(tbuf.at[buf_slot], tbuf.at[buf_slot], sem.at[buf_slot]).wait()

    x1 = _layer_norm(y_ref[...], g_ref[...], b_ref[...])
    x_out[...] = x1
    x_hi = x1.astype(BF16)
    x_hi32 = x_hi.astype(F32)

    @pl.when(i >= 2)
    def _():
        wait_rows(slot)

    words = _pack_rows(x_hi32[:, :HALF_D], x_hi32[:, HALF_D:])
    for c in range(TOK_WORDS):
        tbuf[slot, c] = words[:, LANE * c:LANE * (c + 1)]
    for j in range(tm):
        pltpu.make_async_copy(tbuf.at[slot, :, pl.ds(j, 1), :], xtok_hbm.at[i * tm + j], sem.at[slot]).start()

    @pl.when(i == pl.num_programs(0) - 1)
    def _():
        wait_rows(slot)

        @pl.when(i >= 1)
        def _():
            wait_rows(1 - slot)

    x_lo = (x1 - x_hi32).astype(BF16)
    logits = _dot(x_hi, wh_ref[...]) + _dot(x_lo, wh_ref[...]) + _dot(x_hi, wl_ref[...]) + rb_ref[...]
    lane = lax.broadcasted_iota(jnp.int32, logits.shape, 1).astype(F32)
    neg = -jnp.inf
    big = 1e9
    is_grp = lane < MOE_GROUPS
    lg = jnp.where(is_grp, logits, neg)
    mg = jnp.max(lg, axis=-1, keepdims=True)
    gsel = jnp.min(jnp.where(lg == mg, lane, big), axis=-1, keepdims=True)
    pg_sel = 1.0 / jnp.sum(jnp.where(is_grp, jnp.exp(lg - mg), 0.0), axis=-1, keepdims=True)
    lo = MOE_GROUPS + MOE_PER_GROUP * gsel
    le = jnp.where((lane >= lo) & (lane < lo + MOE_PER_GROUP), logits, neg)
    v1 = jnp.max(le, axis=-1, keepdims=True)
    i1 = jnp.min(jnp.where(le == v1, lane, big), axis=-1, keepdims=True)
    le2 = jnp.where(lane == i1, neg, le)
    v2 = jnp.max(le2, axis=-1, keepdims=True)
    i2 = jnp.min(jnp.where(le2 == v2, lane, big), axis=-1, keepdims=True)
    e = jnp.exp(v2 - v1)
    w1 = pg_sel / (1.0 + e)
    w2 = pg_sel * e / (1.0 + e)
    route = jnp.where(lane == 0.0, i1 - MOE_GROUPS,
                      jnp.where(lane == 1.0, i2 - MOE_GROUPS,
                                jnp.where(lane == 2.0, w1, jnp.where(lane == 3.0, w2, 0.0))))
    route_out[...] = route


def _ln_route(y, g, b, wh, wl, rb):
    s = y.shape[0]
    tm = min(256, s)
    full = lambda shape: pl.BlockSpec(shape, lambda i: (0,) * len(shape))
    rowb = lambda width: pl.BlockSpec((tm, width), lambda i: (i, 0))
    x1, route, x1_tok = pl.pallas_call(
        functools.partial(_ln_route_kernel, tm=tm),
        grid=(s // tm,),
        in_specs=[rowb(D_MODEL), full(g.shape), full(b.shape), full(wh.shape), full(wl.shape), full(rb.shape)],
        out_specs=[rowb(D_MODEL), rowb(LANE), pl.BlockSpec(memory_space=pl.ANY)],
        out_shape=[jax.ShapeDtypeStruct((s, D_MODEL), F32),
                   jax.ShapeDtypeStruct((s, LANE), F32),
                   jax.ShapeDtypeStruct((s, TOK_WORDS, 1, LANE), U32)],
        scratch_shapes=[pltpu.VMEM((2, TOK_WORDS, tm, LANE), U32), pltpu.SemaphoreType.DMA((2,))],
        compiler_params=_cparams(("arbitrary",), 48),
        name="ln_route",
    )(y, g, b, wh, wl, rb)
    return x1, x1_tok, route


def _moe_kernel(te_ref, tv_ref, nt_ref, src_ref, nxt_ref, dst_ref, x_hbm, wg_ref, wu_ref, wd_ref, ws_ref, y_hbm,
                gbuf, hb, obuf, gsem, ssem):
    t = pl.program_id(0)
    n_used = nt_ref[0]
    slot = lax.rem(t, 2)

    def gather_row(idx_ref, buf_slot, i):
        return pltpu.make_async_copy(x_hbm.at[idx_ref[i]], gbuf.at[buf_slot, :, pl.ds(i, 1), :], gsem.at[buf_slot])

    def wait_gather(buf_slot):
        pltpu.make_async_copy(gbuf.at[buf_slot], gbuf.at[buf_slot], gsem.at[buf_slot]).wait()

    def scatter_row(i):
        return pltpu.make_async_copy(obuf.at[:, pl.ds(i, 1), :], y_hbm.at[dst_ref[i]], ssem)

    def start_scatter(n_rows):
        def group(g, carry):
            for u in range(SCATTER_UNROLL):
                scatter_row(g * SCATTER_UNROLL + u).start()
            return carry

        def single(i, carry):
            scatter_row(i).start()
            return carry
        n_groups = n_rows // SCATTER_UNROLL
        lax.fori_loop(0, n_groups, group, 0)
        lax.fori_loop(n_groups * SCATTER_UNROLL, n_rows, single, 0)

    def wait_scatter(n_rows):
        rows = obuf.at[:, pl.ds(0, n_rows), :]
        pltpu.make_async_copy(rows, rows, ssem).wait()

    @pl.when(t == 0)
    def _():
        def body(i, carry):
            gather_row(src_ref, 0, i).start()
            return carry
        lax.fori_loop(0, MOE_TILE, body, 0, unroll=8)

    @pl.when(t < n_used)
    def _():
        wait_gather(slot)
        for c in range(TOK_WORDS):
            lo, hi = _unpack_rows(gbuf[slot, c])
            hb[:, LANE * c:LANE * (c + 1)] = lo.astype(BF16)
            hb[:, HALF_D + LANE * c:HALF_D + LANE * (c + 1)] = hi.astype(BF16)
        for i in range(MOE_TILE):
            gather_row(nxt_ref, 1 - slot, i).start()
        h = hb[...]
        gate = _dot(h, wg_ref[0])
        up = _dot(h, wu_ref[0])
        a = (gate / (1.0 + jnp.exp(-gate)) * up).astype(BF16)

        @pl.when(t > 0)
        def _():
            wait_scatter(tv_ref[jnp.maximum(t - 1, 0)])

        ws = ws_ref[...]
        rounded = lambda v: v.astype(BF16).astype(F32)
        for c2 in range(TOK_WORDS // 2):
            cols = slice(2 * LANE * c2, 2 * LANE * (c2 + 1))
            lo = _dot(a, wd_ref[0, :, cols]) * ws
            hi = _dot(a, wd_ref[0, :, HALF_D + cols.start:HALF_D + cols.stop]) * ws
            words = _pack_rows(rounded(lo), rounded(hi))
            obuf[2 * c2] = words[:, :LANE]
            obuf[2 * c2 + 1] = words[:, LANE:]
        start_scatter(tv_ref[t])

        @pl.when(t == n_used - 1)
        def _():
            wait_gather(1 - slot)
            wait_scatter(tv_ref[t])


def _moe(tile_expert, tile_valid, n_tiles, src_token, dst_row, x_tok, wg, wu, wd, w_slot):
    p_rows = src_token.shape[0]
    tm = MOE_TILE
    n_blocks = p_rows // tm
    smem_blk = lambda fn: pl.BlockSpec((tm,), fn, memory_space=pltpu.SMEM)
    grid_spec = pltpu.PrefetchScalarGridSpec(
        num_scalar_prefetch=3,
        grid=(n_blocks,),
        in_specs=[smem_blk(lambda t, te, tv, nt: (t,)),
                  smem_blk(lambda t, te, tv, nt: (jnp.minimum(t + 1, n_blocks - 1),)),
                  smem_blk(lambda t, te, tv, nt: (t,)),
                  pl.BlockSpec(memory_space=pl.ANY),
                  pl.BlockSpec((1, D_MODEL, D_EXPERT), lambda t, te, tv, nt: (te[t], 0, 0)),
                  pl.BlockSpec((1, D_MODEL, D_EXPERT), lambda t, te, tv, nt: (te[t], 0, 0)),
                  pl.BlockSpec((1, D_EXPERT, D_MODEL), lambda t, te, tv, nt: (te[t], 0, 0)),
                  pl.BlockSpec((tm, 1), lambda t, te, tv, nt: (t, 0))],
        out_specs=pl.BlockSpec(memory_space=pl.ANY),
        scratch_shapes=[pltpu.VMEM((2, TOK_WORDS, tm, LANE), U32),
                        pltpu.VMEM((tm, D_MODEL), BF16),
                        pltpu.VMEM((TOK_WORDS, tm, LANE), U32),
                        pltpu.SemaphoreType.DMA((2,)),
                        pltpu.SemaphoreType.DMA(())],
    )
    return pl.pallas_call(
        _moe_kernel,
        grid_spec=grid_spec,
        out_shape=jax.ShapeDtypeStruct((2 * x_tok.shape[0], TOK_WORDS, 1, LANE), U32),
        compiler_params=_cparams(("arbitrary",), 56),
        name="moe",
    )(tile_expert, tile_valid, n_tiles, src_token, src_token, dst_row, x_tok, wg, wu, wd, w_slot)


def _ln_add_kernel(x_ref, g_ref, b_ref, y_hbm, o_ref, ybuf, vbuf, sem, *, tm):
    i = pl.program_id(0)
    slot = lax.rem(i, 2)

    def start_gather(tile, buf_slot):
        base = tile * (2 * tm)

        def body(j, carry):
            for k in range(2):
                pltpu.make_async_copy(y_hbm.at[base + 2 * j + k], ybuf.at[buf_slot, k, :, pl.ds(j, 1), :],
                                      sem.at[buf_slot]).start()
            return carry
        lax.fori_loop(0, tm, body, 0, unroll=4)

    @pl.when(i == 0)
    def _():
        start_gather(0, 0)

    @pl.when(i + 1 < pl.num_programs(0))
    def _():
        start_gather(i + 1, 1 - slot)

    pltpu.make_async_copy(ybuf.at[slot], ybuf.at[slot], sem.at[slot]).wait()
    for c in range(TOK_WORDS):
        a_lo, a_hi = _unpack_rows(ybuf[slot, 0, c])
        b_lo, b_hi = _unpack_rows(ybuf[slot, 1, c])
        lo = slice(LANE * c, LANE * (c + 1))
        hi = slice(HALF_D + LANE * c, HALF_D + LANE * (c + 1))
        vbuf[:, lo] = ALPHA * x_ref[:, lo] + (a_lo + b_lo)
        vbuf[:, hi] = ALPHA * x_ref[:, hi] + (a_hi + b_hi)
    o_ref[...] = _layer_norm(vbuf[...], g_ref[...], b_ref[...])


def _ln_add(x, y_tok, g, b):
    s = x.shape[0]
    tm = min(256, s)
    full = lambda shape: pl.BlockSpec(shape, lambda i: (0,) * len(shape))
    return pl.pallas_call(
        functools.partial(_ln_add_kernel, tm=tm),
        grid=(s // tm,),
        in_specs=[pl.BlockSpec((tm, D_MODEL), lambda i: (i, 0)), full(g.shape), full(b.shape),
                  pl.BlockSpec(memory_space=pl.ANY)],
        out_specs=pl.BlockSpec((tm, D_MODEL), lambda i: (i, 0)),
        out_shape=jax.ShapeDtypeStruct((s, D_MODEL), F32),
        scratch_shapes=[pltpu.VMEM((2, 2, TOK_WORDS, tm, LANE), U32),
                        pltpu.VMEM((tm, D_MODEL), F32),
                        pltpu.SemaphoreType.DMA((2,))],
        compiler_params=_cparams(("arbitrary",), 48),
        name="ln_add",
    )(x, g, b, y_tok)


def _pack_w_in(w):
    o = 0
    parts = {}
    for name, width in (("q", 512), ("k", 512), ("v", 1024), ("r", 1024), ("gl", GLA_GATE_RANK), ("u", 1024),
                        ("vg", 1024), ("cb", 1024), ("cc", 1024), ("ch", 1024), ("cq", MLA_Q_LORA),
                        ("ckv", MLA_KV_LORA), ("kr", MLA_ROPE)):
        parts[name] = w[:, o:o + width]
        o += width
    zeros = lambda n: jnp.zeros((w.shape[0], n), w.dtype)
    half = MLA_ROPE // 2
    kr = parts["kr"]
    cols = [parts["cq"], parts["ckv"], parts["q"], parts["k"], parts["v"], parts["r"], parts["u"], parts["vg"],
            parts["cb"], parts["cc"], parts["ch"],
            kr, zeros(LANE - MLA_ROPE),
            kr[:, half:], kr[:, :half], zeros(LANE - MLA_ROPE),
            parts["gl"], zeros(LANE - GLA_GATE_RANK)]
    packed = jnp.concatenate(cols, axis=1)
    packed = jnp.concatenate([packed, zeros(P_COLS - packed.shape[1])], axis=1)
    return packed.astype(BF16)


def _pack_mla_weights(wuq, wukv):
    half = MLA_ROPE // 2
    wq3 = wuq.reshape(MLA_Q_LORA, MLA_HEADS, MLA_NOPE + MLA_ROPE)
    rope = wq3[:, :, MLA_NOPE:]
    zq = jnp.zeros((MLA_Q_LORA, MLA_HEADS, MLA_QK_PAD - MLA_NOPE - MLA_ROPE), wuq.dtype)
    wq = jnp.concatenate([wq3, zq], axis=2).reshape(MLA_Q_LORA, MLA_HEADS * MLA_QK_PAD)
    zs = jnp.zeros((MLA_Q_LORA, MLA_HEADS, LANE - MLA_ROPE), wuq.dtype)
    wqs = jnp.concatenate([rope[:, :, half:], rope[:, :, :half], zs], axis=2).reshape(MLA_Q_LORA, MLA_HEADS * LANE)
    wkv3 = wukv.reshape(MLA_KV_LORA, MLA_HEADS, MLA_NOPE + MLA_V)
    wk = wkv3[:, :, :MLA_NOPE].reshape(MLA_KV_LORA, MLA_HEADS * MLA_NOPE)
    wv = wkv3[:, :, MLA_NOPE:].reshape(MLA_KV_LORA, MLA_HEADS * MLA_V)
    return wq.astype(BF16), wqs.astype(BF16), wk.astype(BF16), wv.astype(BF16)


def _rope_tables(s):
    pos = jnp.arange(s, dtype=F32)
    inv_freq = ROPE_BASE ** (-jnp.arange(0, MLA_ROPE, 2, dtype=F32) / MLA_ROPE)
    ang = pos[:, None] * inv_freq[None, :]
    cos, sin = jnp.cos(ang), jnp.sin(ang)
    z = jnp.zeros((s, LANE - MLA_ROPE), F32)
    return jnp.concatenate([cos, cos, z], axis=1), jnp.concatenate([-sin, sin, z], axis=1)


def _dispatch_plan(route, s):
    n_assign = 2 * s
    p_rows = n_assign + N_EXPERTS * MOE_TILE
    eid = route[:, 0:2].astype(jnp.int32).reshape(n_assign)
    wgt = route[:, 2:4].reshape(n_assign)
    onehot = (eid[:, None] == jnp.arange(N_EXPERTS, dtype=jnp.int32)[None, :]).astype(jnp.int32)
    csum = jnp.cumsum(onehot, axis=0)
    rank = jnp.sum(onehot * (csum - 1), axis=1)
    counts = csum[-1]
    tiles_e = (counts + MOE_TILE - 1) // MOE_TILE
    tile_end = jnp.cumsum(tiles_e)
    tile_start = tile_end - tiles_e
    slot = tile_start[eid] * MOE_TILE + rank
    slot_assign = jnp.full((p_rows,), -1, jnp.int32).at[slot].set(jnp.arange(n_assign, dtype=jnp.int32))
    is_pad = slot_assign < 0
    src_token = jnp.where(is_pad, 0, slot_assign // 2)
    dst_row = jnp.maximum(slot_assign, 0)
    w_slot = jnp.where(is_pad, 0.0, wgt[dst_row])
    n_blocks = p_rows // MOE_TILE
    tile_valid = jnp.sum(jnp.logical_not(is_pad).reshape(n_blocks, MOE_TILE).astype(jnp.int32), axis=1)
    tile_ids = jnp.arange(n_blocks, dtype=jnp.int32)
    tile_expert = jnp.minimum(jnp.sum((tile_end[None, :] <= tile_ids[:, None]).astype(jnp.int32), axis=1),
                              N_EXPERTS - 1)
    n_tiles = tile_end[-1:].astype(jnp.int32)
    return src_token, dst_row, w_slot.reshape(p_rows, 1), tile_expert, tile_valid, n_tiles


def kernel(x, w_in, gla_wa2, gla_ba, gla_norm, gm_ln_g, gm_ln_b, gm_ws, gm_bs, gm_norm, sc_conv, sc_norm, mla_q_norm, mla_kv_norm, mla_wuq, mla_wukv, mla_norm, w_o, ln1_g, ln1_b, router_g_w, router_g_b, router_e_w, router_e_b, exp_w_gate, exp_w_up, exp_w_down, ln2_g, ln2_b):
    bsz, s, _ = x.shape
    assert bsz == 1
    xc = x.reshape(s, D_MODEL)
    cos_t, sin_t = _rope_tables(s)
    row = lambda v: v.reshape(1, -1)
    wg_all = exp_w_gate.astype(BF16).reshape(DEPTH * N_EXPERTS, D_MODEL, D_EXPERT)
    wu_all = exp_w_up.astype(BF16).reshape(DEPTH * N_EXPERTS, D_MODEL, D_EXPERT)
    wd_all = exp_w_down.astype(BF16).reshape(DEPTH * N_EXPERTS, D_EXPERT, D_MODEL)
    for l in range(DEPTH):
        p = _mm_in(xc, _pack_w_in(w_in[l]))

        wa2p = jnp.concatenate(
            [gla_wa2[l], jnp.zeros((LANE - GLA_GATE_RANK, GLA_HEADS * GLA_DK), F32)], axis=0).astype(BF16)
        out_a = _gla(p, wa2p, row(gla_ba[l]), row(gla_norm[l]))

        gm_bias = jnp.repeat(gm_bs[l].T, GM_CH, axis=1)
        out_b = _gmlp(p, row(gm_ln_g[l]), row(gm_ln_b[l]), gm_ws[l], gm_bias, row(gm_norm[l]))

        out_c = _sconv(p, sc_conv[l], row(sc_norm[l]))

        wq, wqs, wk, wv = _pack_mla_weights(mla_wuq[l], mla_wukv[l])
        q_att, k_att, v_att = _mla_proj(p, row(mla_q_norm[l]), row(mla_kv_norm[l]), wq, wqs, wk, wv, cos_t, sin_t)
        out_d = _flash(q_att, k_att, v_att, row(mla_norm[l]))

        y = _mm_out(out_a, out_b, out_c, out_d, w_o[l].astype(BF16), xc)

        wr = jnp.concatenate([router_g_w[l], router_e_w[l],
                              jnp.zeros((D_MODEL, LANE - MOE_GROUPS - N_EXPERTS), F32)], axis=1)
        wr_hi = wr.astype(BF16)
        wr_lo = (wr - wr_hi.astype(F32)).astype(BF16)
        rb = jnp.concatenate([router_g_b[l], router_e_b[l], jnp.zeros((LANE - MOE_GROUPS - N_EXPERTS,), F32)])
        x1, x1_tok, route = _ln_route(y, row(ln1_g[l]), row(ln1_b[l]), wr_hi, wr_lo, row(rb))

        src_token, dst_row, w_slot, tile_expert, tile_valid, n_tiles = _dispatch_plan(route, s)
        y_tok = _moe(tile_expert + l * N_EXPERTS, tile_valid, n_tiles, src_token, dst_row, x1_tok,
                     wg_all, wu_all, wd_all, w_slot)
        xc = _ln_add(x1, y_tok, row(ln2_g[l]), row(ln2_b[l]))
    return xc.reshape(bsz, s, D_MODEL)
```

```python
import functools

import jax
import jax.numpy as jnp
from jax import lax
from jax.experimental import pallas as pl
from jax.experimental.pallas import tpu as pltpu

F32 = jnp.float32
BF16 = jnp.bfloat16

D_MODEL = 4096
DEPTH = 2
GROUP_W = 1024

GLA_HEADS = 4
GLA_DK = 128
GLA_DV = 256
GLA_GATE_RANK = 16
GLA_TAU = 16.0
GLA_CHUNK = 64
GLA_SUB = 16

GM_GROUPS = 8
GM_CH = 128
GM_CHUNK = 128

MLA_HEADS = 8
MLA_NOPE = 128
MLA_ROPE = 64
MLA_V = 128
MLA_Q_LORA = 768
MLA_KV_LORA = 256
ROPE_BASE = 10000.0
MLA_QK_PAD = 256

MOE_GROUPS = 4
MOE_PER_GROUP = 8
N_EXPERTS = 32
D_EXPERT = 512

ALPHA = (2.0 * DEPTH) ** 0.25
LOG2_E = 1.4426950408889634

LANE = 128
MIB = 1024 * 1024

COL_CQ = 0
COL_CKV = 768
COL_Q = 1024
COL_K = 1536
COL_V = 2048
COL_R = 3072
COL_U = 4096
COL_VG = 5120
COL_CB = 6144
COL_CC = 7168
COL_CH = 8192
COL_KR = 9216
COL_KRS = 9344
COL_GL = 9472
P_COLS = 9728

MOE_TILE = 256
SCATTER_UNROLL = 8
WEIGHT_ROUND_ROWS = 512
HALF_D = D_MODEL // 2
TOK_WORDS = HALF_D // LANE
U32 = jnp.uint32
HIGH_HALF = 0xFFFF0000


def _pack_rows(lo_f32, hi_f32):
    return (lax.bitcast_convert_type(lo_f32, U32) >> 16) | lax.bitcast_convert_type(hi_f32, U32)


def _unpack_rows(words):
    return (lax.bitcast_convert_type(words << 16, F32),
            lax.bitcast_convert_type(words & jnp.uint32(HIGH_HALF), F32))


def _cparams(sem, vmem_mib):
    return pltpu.CompilerParams(dimension_semantics=sem, vmem_limit_bytes=vmem_mib * MIB)


def _dot(a, b):
    return jnp.dot(a, b, preferred_element_type=F32)


def _dot_nt(a, b):
    return lax.dot_general(a, b, (((1,), (1,)), ((), ())), preferred_element_type=F32)


def _dot_tn(a, b):
    return lax.dot_general(a, b, (((0,), (0,)), ((), ())), preferred_element_type=F32)


def _mm_in_kernel(x_ref, w_ref, o_ref, xb_ref):
    @pl.when(pl.program_id(1) == 0)
    def _():
        xb_ref[...] = x_ref[...].astype(BF16)

    o_ref[...] = _dot(xb_ref[...], w_ref[...]).astype(o_ref.dtype)


def _mm_in(x, w):
    s, k = x.shape
    n = w.shape[1]
    tm = min(1024, s)
    tn = 512
    return pl.pallas_call(
        _mm_in_kernel,
        grid=(s // tm, n // tn),
        in_specs=[pl.BlockSpec((tm, k), lambda i, j: (i, 0)),
                  pl.BlockSpec((k, tn), lambda i, j: (0, j))],
        out_specs=pl.BlockSpec((tm, tn), lambda i, j: (i, j)),
        out_shape=jax.ShapeDtypeStruct((s, n), BF16),
        scratch_shapes=[pltpu.VMEM((tm, k), BF16)],
        compiler_params=_cparams(("parallel", "arbitrary"), 58),
        name="mm_in",
    )(x, w)


def _gla_kernel(q_ref, k_ref, v_ref, r_ref, gl_ref, wa2_ref, ba_ref, ng_ref, o_ref, s_ref, *, n_chunks):
    c_len = GLA_CHUNK

    @pl.when(pl.program_id(0) == 0)
    def _():
        s_ref[...] = jnp.zeros_like(s_ref)

    row = lax.broadcasted_iota(jnp.int32, (c_len, c_len), 0)
    col = lax.broadcasted_iota(jnp.int32, (c_len, c_len), 1)
    tril = jnp.where(col <= row, 1.0, 0.0).astype(BF16)
    sub_row = lax.broadcasted_iota(jnp.int32, (GLA_SUB, c_len), 0)
    sub_col = lax.broadcasted_iota(jnp.int32, (GLA_SUB, c_len), 1)
    n_sub = c_len // GLA_SUB

    def chunk(c, carry):
        rows = pl.ds(pl.multiple_of(c * c_len, c_len), c_len)
        logit = _dot(gl_ref[rows, :], wa2_ref[...]) + ba_ref[...]
        g = (jnp.minimum(logit, 0.0) - jnp.log(1.0 + jnp.exp(-jnp.abs(logit)))) * (1.0 / GLA_TAU)
        g_hi = g.astype(BF16)
        g_lo = (g - g_hi.astype(F32)).astype(BF16)
        b_all = _dot(tril, g_hi) + _dot(tril, g_lo)

        for h in range(GLA_HEADS):
            hs = slice(GLA_DK * h, GLA_DK * (h + 1))
            vs = slice(GLA_DV * h, GLA_DV * (h + 1))
            bh = b_all[:, hs]
            qh = q_ref[rows, hs].astype(F32) * (GLA_DK ** -0.5)
            kh = k_ref[rows, hs].astype(F32)
            vh = v_ref[rows, vs]
            state = s_ref[h]

            o = _dot((qh * jnp.exp(bh)).astype(BF16), state.astype(BF16))

            att_rows = []
            for blk in range(n_sub):
                sl = slice(GLA_SUB * blk, GLA_SUB * (blk + 1))
                b_blk = bh[sl]
                q_blk = qh[sl]
                att = jnp.zeros((GLA_SUB, c_len), F32)
                if blk > 0:
                    ref = bh[GLA_SUB * blk:GLA_SUB * blk + 1, :]
                    qs = (q_blk * jnp.exp(b_blk - ref)).astype(BF16)
                    ks = (kh * jnp.exp(jnp.minimum(ref - bh, 0.0))).astype(BF16)
                    att = jnp.where(sub_col < GLA_SUB * blk, _dot_nt(qs, ks), 0.0)
                for jj in range(GLA_SUB):
                    j = GLA_SUB * blk + jj
                    t = q_blk * kh[j:j + 1, :] * jnp.exp(b_blk - bh[j:j + 1, :])
                    rs = jnp.sum(t, axis=-1, keepdims=True)
                    att = jnp.where((sub_col == j) & (sub_row >= jj), rs, att)
                att_rows.append(att)
            att_full = jnp.concatenate(att_rows, axis=0).astype(BF16)
            o = o + _dot(att_full, vh)

            b_last = bh[c_len - 1:c_len, :]
            kd = (kh * jnp.exp(b_last - bh)).astype(BF16)
            decay_col = jnp.transpose(jnp.broadcast_to(jnp.exp(b_last), (GLA_DK, GLA_DK)))
            s_ref[h] = state * jnp.concatenate([decay_col, decay_col], axis=1) + _dot_tn(kd, vh)

            var = jnp.mean(o * o, axis=-1, keepdims=True)
            on = o * lax.rsqrt(var + 1e-6) * ng_ref[:, vs]
            rr = r_ref[rows, vs].astype(F32)
            o_ref[rows, vs] = (on * (rr / (1.0 + jnp.exp(-rr)))).astype(o_ref.dtype)
        return carry

    lax.fori_loop(0, n_chunks, chunk, 0)


def _gla(p, wa2p, ba, ng):
    s = p.shape[0]
    t = min(256, s)
    col = lambda width, off: pl.BlockSpec((t, width), lambda i: (i, off // width))
    full = lambda shape: pl.BlockSpec(shape, lambda i: (0,) * len(shape))
    return pl.pallas_call(
        functools.partial(_gla_kernel, n_chunks=t // GLA_CHUNK),
        grid=(s // t,),
        in_specs=[col(512, COL_Q), col(512, COL_K), col(1024, COL_V), col(1024, COL_R), col(LANE, COL_GL),
                  full(wa2p.shape), full(ba.shape), full(ng.shape)],
        out_specs=pl.BlockSpec((t, GROUP_W), lambda i: (i, 0)),
        out_shape=jax.ShapeDtypeStruct((s, GROUP_W), BF16),
        scratch_shapes=[pltpu.VMEM((GLA_HEADS, GLA_DK, GLA_DV), F32)],
        compiler_params=_cparams(("arbitrary",), 32),
        name="gla",
    )(p, p, p, p, p, wa2p, ba, ng)


def _gelu(x):
    return 0.5 * x * (1.0 + lax.erf(x * 0.7071067811865476))


def _gmlp_kernel(u_ref, v_ref, lg_ref, lb_ref, ws_ref, bias_ref, ng_ref, o_ref, buf_ref, *, n_chunks):
    row = lax.broadcasted_iota(jnp.int32, (GM_CHUNK, GM_CHUNK), 0)
    col = lax.broadcasted_iota(jnp.int32, (GM_CHUNK, GM_CHUNK), 1)
    causal = col <= row
    v = _gelu(v_ref[...].astype(F32))
    mu = jnp.mean(v, axis=-1, keepdims=True)
    d = v - mu
    var = jnp.mean(d * d, axis=-1, keepdims=True)
    vb = (d * lax.rsqrt(var + 1e-5) * lg_ref[...] + lb_ref[...]).astype(BF16)
    for g in range(GM_GROUPS):
        cs = slice(GM_CH * g, GM_CH * (g + 1))
        w = jnp.where(causal, ws_ref[g], 0.0).astype(BF16)
        for c in range(n_chunks):
            rs = slice(GM_CHUNK * c, GM_CHUNK * (c + 1))
            mixed = _dot(w, vb[rs, cs]) + bias_ref[:, cs]
            buf_ref[rs, cs] = _gelu(u_ref[rs, cs].astype(F32)) * mixed
    out = buf_ref[...]
    ms = jnp.mean(out * out, axis=-1, keepdims=True)
    o_ref[...] = (out * lax.rsqrt(ms + 1e-6) * ng_ref[...]).astype(o_ref.dtype)


def _gmlp(p, lg, lb, ws, bias, ng):
    s = p.shape[0]
    t = min(256, s)
    col = lambda off: pl.BlockSpec((t, GROUP_W), lambda i: (i, off // GROUP_W))
    full = lambda shape: pl.BlockSpec(shape, lambda i: (0,) * len(shape))
    return pl.pallas_call(
        functools.partial(_gmlp_kernel, n_chunks=t // GM_CHUNK),
        grid=(s // t,),
        in_specs=[col(COL_U), col(COL_VG), full(lg.shape), full(lb.shape), full(ws.shape), full(bias.shape),
                  full(ng.shape)],
        out_specs=pl.BlockSpec((t, GROUP_W), lambda i: (i, 0)),
        out_shape=jax.ShapeDtypeStruct((s, GROUP_W), BF16),
        scratch_shapes=[pltpu.VMEM((t, GROUP_W), F32)],
        compiler_params=_cparams(("parallel",), 32),
        name="gmlp",
    )(p, p, lg, lb, ws, bias, ng)


CONV_HALO = 8


def _sconv_kernel(b_ref, c_ref, h_ref, w_ref, ng_ref, o_ref, z_ref):
    t = b_ref.shape[0]

    @pl.when(pl.program_id(0) == 0)
    def _():
        z_ref[0:CONV_HALO, :] = jnp.zeros((CONV_HALO, GROUP_W), F32)

    z = c_ref[...].astype(F32) * h_ref[...].astype(F32)
    z_ref[CONV_HALO:CONV_HALO + t, :] = z
    z1 = z_ref[CONV_HALO - 1:CONV_HALO - 1 + t, :]
    z2 = z_ref[CONV_HALO - 2:CONV_HALO - 2 + t, :]
    y = w_ref[0:1, :] * z2 + w_ref[1:2, :] * z1 + w_ref[2:3, :] * z
    z_ref[0:CONV_HALO, :] = z[t - CONV_HALO:t, :]
    out = b_ref[...].astype(F32) * y
    ms = jnp.mean(out * out, axis=-1, keepdims=True)
    o_ref[...] = (out * lax.rsqrt(ms + 1e-6) * ng_ref[...]).astype(o_ref.dtype)


def _sconv(p, w, ng):
    s = p.shape[0]
    t = min(256, s)
    col = lambda off: pl.BlockSpec((t, GROUP_W), lambda i: (i, off // GROUP_W))
    full = lambda shape: pl.BlockSpec(shape, lambda i: (0,) * len(shape))
    return pl.pallas_call(
        _sconv_kernel,
        grid=(s // t,),
        in_specs=[col(COL_CB), col(COL_CC), col(COL_CH), full(w.shape), full(ng.shape)],
        out_specs=pl.BlockSpec((t, GROUP_W), lambda i: (i, 0)),
        out_shape=jax.ShapeDtypeStruct((s, GROUP_W), BF16),
        scratch_shapes=[pltpu.VMEM((t + CONV_HALO, GROUP_W), F32)],
        compiler_params=_cparams(("arbitrary",), 32),
        name="sconv",
    )(p, p, p, w, ng)


def _mla_proj_kernel(cq_ref, ckv_ref, kr_ref, krs_ref, qn_ref, kvn_ref, wq_ref, wqs_ref, wk_ref, wv_ref,
                     cos_ref, sin_ref, q_out, k_out, v_out):
    def rms(ref, g_ref):
        t = ref[...].astype(F32)
        return (t * lax.rsqrt(jnp.mean(t * t, axis=-1, keepdims=True) + 1e-6) * g_ref[...]).astype(BF16)

    cqn = rms(cq_ref, qn_ref)
    ckvn = rms(ckv_ref, kvn_ref)
    qm = _dot(cqn, wq_ref[...])
    qsw = _dot(cqn, wqs_ref[...])
    kn = _dot(ckvn, wk_ref[...])
    vv = _dot(ckvn, wv_ref[...])
    cos = cos_ref[...]
    sin = sin_ref[...]
    scale = (MLA_NOPE + MLA_ROPE) ** -0.5 * LOG2_E
    kr_rot = (kr_ref[...].astype(F32) * cos + krs_ref[...].astype(F32) * sin).astype(k_out.dtype)
    ones_col = jnp.where(lax.broadcasted_iota(jnp.int32, (cos.shape[0], LANE), 1) == 0, 1.0, 0.0).astype(v_out.dtype)
    for h in range(MLA_HEADS):
        lo = MLA_QK_PAD * h
        v_out[:, lo:lo + LANE] = vv[:, LANE * h:LANE * (h + 1)].astype(v_out.dtype)
        v_out[:, lo + LANE:lo + 2 * LANE] = ones_col
        q_out[:, lo:lo + LANE] = (qm[:, lo:lo + LANE] * scale).astype(q_out.dtype)
        q_out[:, lo + LANE:lo + 2 * LANE] = (
            (qm[:, lo + LANE:lo + 2 * LANE] * cos + qsw[:, LANE * h:LANE * (h + 1)] * sin) * scale
        ).astype(q_out.dtype)
        k_out[:, lo:lo + LANE] = kn[:, LANE * h:LANE * (h + 1)].astype(k_out.dtype)
        k_out[:, lo + LANE:lo + 2 * LANE] = kr_rot


def _mla_proj(p, qn, kvn, wq, wqs, wk, wv, cos, sin):
    s = p.shape[0]
    t = min(256, s)
    full = lambda shape: pl.BlockSpec(shape, lambda i: (0,) * len(shape))
    rowb = lambda width: pl.BlockSpec((t, width), lambda i: (i, 0))
    qk_w = MLA_HEADS * MLA_QK_PAD
    return pl.pallas_call(
        _mla_proj_kernel,
        grid=(s // t,),
        in_specs=[pl.BlockSpec((t, MLA_Q_LORA), lambda i: (i, COL_CQ // MLA_Q_LORA)),
                  pl.BlockSpec((t, MLA_KV_LORA), lambda i: (i, COL_CKV // MLA_KV_LORA)),
                  pl.BlockSpec((t, LANE), lambda i: (i, COL_KR // LANE)),
                  pl.BlockSpec((t, LANE), lambda i: (i, COL_KRS // LANE)),
                  full(qn.shape), full(kvn.shape), full(wq.shape), full(wqs.shape), full(wk.shape), full(wv.shape),
                  rowb(LANE), rowb(LANE)],
        out_specs=[rowb(qk_w), rowb(qk_w), rowb(qk_w)],
        out_shape=[jax.ShapeDtypeStruct((s, qk_w), BF16)] * 3,
        compiler_params=_cparams(("parallel",), 40),
        name="mla_proj",
    )(p, p, p, p, qn, kvn, wq, wqs, wk, wv, cos, sin)


FLASH_TQ = 512
FLASH_TK = 1024


def _flash_kernel(qi_ref, ki_ref, q_ref, k_ref, v_ref, ng_ref, o_ref, acc_ref, m_ref, *, tq, tk):
    step = pl.program_id(0)
    qi = qi_ref[step]
    ki = ki_ref[step]
    last_ki = (qi * tq) // tk

    @pl.when(ki == 0)
    def _():
        m_ref[...] = jnp.full(m_ref.shape, -jnp.inf, F32)
        acc_ref[...] = jnp.zeros_like(acc_ref)

    def accumulate(masked):
        if masked:
            row = qi * tq + lax.broadcasted_iota(jnp.int32, (tq, tk), 0)
            col = ki * tk + lax.broadcasted_iota(jnp.int32, (tq, tk), 1)
            visible = col <= row
        def scores(h):
            hs = slice(MLA_QK_PAD * h, MLA_QK_PAD * (h + 1))
            return _dot_nt(q_ref[:, hs], k_ref[:, hs])

        sc = scores(0)
        for h in range(MLA_HEADS):
            hs = slice(MLA_QK_PAD * h, MLA_QK_PAD * (h + 1))
            sc_next = scores(h + 1) if h + 1 < MLA_HEADS else None
            if masked:
                sc = jnp.where(visible, sc, -jnp.inf)
            m_old = m_ref[h]
            m_new = jnp.maximum(m_old, jnp.max(sc, axis=-1, keepdims=True))
            pr = jnp.exp2(sc - m_new).astype(BF16)
            acc_ref[:, hs] = jnp.exp2(m_old - m_new) * acc_ref[:, hs] + _dot(pr, v_ref[:, hs])
            m_ref[h] = m_new
            sc = sc_next

    @pl.when(ki < last_ki)
    def _():
        accumulate(False)

    @pl.when(ki == last_ki)
    def _():
        accumulate(True)
        outs = []
        for h in range(MLA_HEADS):
            lo = MLA_QK_PAD * h
            outs.append(acc_ref[:, lo:lo + MLA_V] / acc_ref[:, lo + MLA_V:lo + MLA_V + 1])
        o = jnp.concatenate(outs, axis=1)
        ms = jnp.mean(o * o, axis=-1, keepdims=True)
        o_ref[...] = (o * lax.rsqrt(ms + 1e-6) * ng_ref[...]).astype(o_ref.dtype)


def _flash(q, k, v, ng):
    s = q.shape[0]
    tk = min(FLASH_TK, s)
    tq = min(FLASH_TQ, tk)
    nq = s // tq
    pairs = [(a, b) for a in range(nq) for b in range((a * tq) // tk + 1)]
    qi_tab = jnp.asarray([a for a, _ in pairs], jnp.int32)
    ki_tab = jnp.asarray([b for _, b in pairs], jnp.int32)
    qk_w = MLA_HEADS * MLA_QK_PAD
    grid_spec = pltpu.PrefetchScalarGridSpec(
        num_scalar_prefetch=2,
        grid=(len(pairs),),
        in_specs=[pl.BlockSpec((tq, qk_w), lambda t, qi, ki: (qi[t], 0)),
                  pl.BlockSpec((tk, qk_w), lambda t, qi, ki: (ki[t], 0)),
                  pl.BlockSpec((tk, qk_w), lambda t, qi, ki: (ki[t], 0)),
                  pl.BlockSpec((1, GROUP_W), lambda t, qi, ki: (0, 0))],
        out_specs=pl.BlockSpec((tq, GROUP_W), lambda t, qi, ki: (qi[t], 0)),
        scratch_shapes=[pltpu.VMEM((tq, qk_w), F32),
                        pltpu.VMEM((MLA_HEADS, tq, 1), F32)],
    )
    return pl.pallas_call(
        functools.partial(_flash_kernel, tq=tq, tk=tk),
        grid_spec=grid_spec,
        out_shape=jax.ShapeDtypeStruct((s, GROUP_W), BF16),
        compiler_params=_cparams(("arbitrary",), 56),
        name="flash",
    )(qi_tab, ki_tab, q, k, v, ng)


def _mm_out_kernel(a0_ref, a1_ref, a2_ref, a3_ref, w_ref, x_ref, o_ref):
    acc = _dot(a0_ref[...], w_ref[0:GROUP_W, :])
    acc = acc + _dot(a1_ref[...], w_ref[GROUP_W:2 * GROUP_W, :])
    acc = acc + _dot(a2_ref[...], w_ref[2 * GROUP_W:3 * GROUP_W, :])
    acc = acc + _dot(a3_ref[...], w_ref[3 * GROUP_W:4 * GROUP_W, :])
    o_ref[...] = ALPHA * x_ref[...] + acc


def _mm_out(a0, a1, a2, a3, w, x):
    s = x.shape[0]
    tm = min(512, s)
    tn = 1024
    a_spec = pl.BlockSpec((tm, GROUP_W), lambda i, j: (i, 0))
    return pl.pallas_call(
        _mm_out_kernel,
        grid=(s // tm, D_MODEL // tn),
        in_specs=[a_spec, a_spec, a_spec, a_spec,
                  pl.BlockSpec((D_MODEL, tn), lambda i, j: (0, j)),
                  pl.BlockSpec((tm, tn), lambda i, j: (i, j))],
        out_specs=pl.BlockSpec((tm, tn), lambda i, j: (i, j)),
        out_shape=jax.ShapeDtypeStruct((s, D_MODEL), F32),
        compiler_params=_cparams(("parallel", "arbitrary"), 48),
        name="mm_out",
    )(a0, a1, a2, a3, w, x)


def _layer_norm(t, g, b):
    mu = jnp.mean(t, axis=-1, keepdims=True)
    d = t - mu
    var = jnp.mean(d * d, axis=-1, keepdims=True)
    return d * lax.rsqrt(var + 1e-5) * g + b


def _ln_route_kernel(y_ref, g_ref, b_ref, wh_ref, wl_ref, rb_ref, x_out, route_out, xtok_hbm, tbuf, sem, *, tm):
    i = pl.program_id(0)
    slot = lax.rem(i, 2)

    def wait_rows(buf_slot):
        pltpu.make_async_copy(tbuf.at[buf_slot], tbuf.at[buf_slot], sem.at[buf_slot]).wait()

    x1 = _layer_norm(y_ref[...], g_ref[...], b_ref[...])
    x_out[...] = x1
    x_hi = x1.astype(BF16)
    x_hi32 = x_hi.astype(F32)

    @pl.when(i >= 2)
    def _():
        wait_rows(slot)

    words = _pack_rows(x_hi32[:, :HALF_D], x_hi32[:, HALF_D:])
    for c in range(TOK_WORDS):
        tbuf[slot, c] = words[:, LANE * c:LANE * (c + 1)]
    for j in range(tm):
        pltpu.make_async_copy(tbuf.at[slot, :, pl.ds(j, 1), :], xtok_hbm.at[i * tm + j], sem.at[slot]).start()

    @pl.when(i == pl.num_programs(0) - 1)
    def _():
        wait_rows(slot)

        @pl.when(i >= 1)
        def _():
            wait_rows(1 - slot)

    x_lo = (x1 - x_hi32).astype(BF16)
    logits = _dot(x_hi, wh_ref[...]) + _dot(x_lo, wh_ref[...]) + _dot(x_hi, wl_ref[...]) + rb_ref[...]
    lane = lax.broadcasted_iota(jnp.int32, logits.shape, 1).astype(F32)
    neg = -jnp.inf
    big = 1e9
    is_grp = lane < MOE_GROUPS
    lg = jnp.where(is_grp, logits, neg)
    mg = jnp.max(lg, axis=-1, keepdims=True)
    gsel = jnp.min(jnp.where(lg == mg, lane, big), axis=-1, keepdims=True)
    pg_sel = 1.0 / jnp.sum(jnp.where(is_grp, jnp.exp(lg - mg), 0.0), axis=-1, keepdims=True)
    lo = MOE_GROUPS + MOE_PER_GROUP * gsel
    le = jnp.where((lane >= lo) & (lane < lo + MOE_PER_GROUP), logits, neg)
    v1 = jnp.max(le, axis=-1, keepdims=True)
    i1 = jnp.min(jnp.where(le == v1, lane, big), axis=-1, keepdims=True)
    le2 = jnp.where(lane == i1, neg, le)
    v2 = jnp.max(le2, axis=-1, keepdims=True)
    i2 = jnp.min(jnp.where(le2 == v2, lane, big), axis=-1, keepdims=True)
    e = jnp.exp(v2 - v1)
    w1 = pg_sel / (1.0 + e)
    w2 = pg_sel * e / (1.0 + e)
    route = jnp.where(lane == 0.0, i1 - MOE_GROUPS,
                      jnp.where(lane == 1.0, i2 - MOE_GROUPS,
                                jnp.where(lane == 2.0, w1, jnp.where(lane == 3.0, w2, 0.0))))
    route_out[...] = route


def _ln_route(y, g, b, wh, wl, rb):
    s = y.shape[0]
    tm = min(256, s)
    full = lambda shape: pl.BlockSpec(shape, lambda i: (0,) * len(shape))
    rowb = lambda width: pl.BlockSpec((tm, width), lambda i: (i, 0))
    x1, route, x1_tok = pl.pallas_call(
        functools.partial(_ln_route_kernel, tm=tm),
        grid=(s // tm,),
        in_specs=[rowb(D_MODEL), full(g.shape), full(b.shape), full(wh.shape), full(wl.shape), full(rb.shape)],
        out_specs=[rowb(D_MODEL), rowb(LANE), pl.BlockSpec(memory_space=pl.ANY)],
        out_shape=[jax.ShapeDtypeStruct((s, D_MODEL), F32),
                   jax.ShapeDtypeStruct((s, LANE), F32),
                   jax.ShapeDtypeStruct((s, TOK_WORDS, 1, LANE), U32)],
        scratch_shapes=[pltpu.VMEM((2, TOK_WORDS, tm, LANE), U32), pltpu.SemaphoreType.DMA((2,))],
        compiler_params=_cparams(("arbitrary",), 48),
        name="ln_route",
    )(y, g, b, wh, wl, rb)
    return x1, x1_tok, route


def _moe_kernel(te_ref, ne_ref, tv_ref, nt_ref, src_ref, nxt_ref, dst_ref, x_hbm, wg_hbm, wu_hbm, wd_hbm, ws_ref,
                y_hbm, gbuf, hb, obuf, stg_g, stg_u, stg_d, wg_ref, wu_ref, wd_ref, gsem, ssem, wsem):
    t = pl.program_id(0)
    n_used = nt_ref[0]
    slot = lax.rem(t, 2)

    def weight_copies(e):
        return (pltpu.make_async_copy(wg_hbm.at[e], stg_g, wsem.at[0]),
                pltpu.make_async_copy(wu_hbm.at[e], stg_u, wsem.at[1]),
                pltpu.make_async_copy(wd_hbm.at[e], stg_d, wsem.at[2]))

    def round_weights(stg, dst, rows):
        def body(r, carry):
            rs = pl.ds(pl.multiple_of(r * rows, rows), rows)
            dst[rs, :] = stg[rs, :].astype(BF16)
            return carry
        lax.fori_loop(0, stg.shape[0] // rows, body, 0)

    @pl.when(t == 0)
    def _():
        for cp in weight_copies(te_ref[0]):
            cp.start()

    @pl.when(t < n_used)
    def _():
        expert = te_ref[t]
        first_tile_of_expert = jnp.logical_or(t == 0, expert != te_ref[jnp.maximum(t - 1, 0)])

        @pl.when(first_tile_of_expert)
        def _():
            for cp in weight_copies(expert):
                cp.wait()
            round_weights(stg_g, wg_ref, WEIGHT_ROUND_ROWS)
            round_weights(stg_u, wu_ref, WEIGHT_ROUND_ROWS)
            round_weights(stg_d, wd_ref, WEIGHT_ROUND_ROWS * D_EXPERT // D_MODEL)

            @pl.when(ne_ref[t] >= 0)
            def _():
                for cp in weight_copies(ne_ref[t]):
                    cp.start()

    def gather_row(idx_ref, buf_slot, i):
        return pltpu.make_async_copy(x_hbm.at[idx_ref[i]], gbuf.at[buf_slot, :, pl.ds(i, 1), :], gsem.at[buf_slot])

    def wait_gather(buf_slot):
        pltpu.make_async_copy(gbuf.at[buf_slot], gbuf.at[buf_slot], gsem.at[buf_slot]).wait()

    def scatter_row(i):
        return pltpu.make_async_copy(obuf.at[:, pl.ds(i, 1), :], y_hbm.at[dst_ref[i]], ssem)

    def start_scatter(n_rows):
        def group(g, carry):
            for u in range(SCATTER_UNROLL):
                scatter_row(g * SCATTER_UNROLL + u).start()
            return carry

        def single(i, carry):
            scatter_row(i).start()
            return carry
        n_groups = n_rows // SCATTER_UNROLL
        lax.fori_loop(0, n_groups, group, 0)
        lax.fori_loop(n_groups * SCATTER_UNROLL, n_rows, single, 0)

    def wait_scatter(n_rows):
        rows = obuf.at[:, pl.ds(0, n_rows), :]
        pltpu.make_async_copy(rows, rows, ssem).wait()

    @pl.when(t == 0)
    def _():
        def body(i, carry):
            gather_row(src_ref, 0, i).start()
            return carry
        lax.fori_loop(0, MOE_TILE, body, 0, unroll=8)

    @pl.when(t < n_used)
    def _():
        wait_gather(slot)
        for c in range(TOK_WORDS):
            lo, hi = _unpack_rows(gbuf[slot, c])
            hb[:, LANE * c:LANE * (c + 1)] = lo.astype(BF16)
            hb[:, HALF_D + LANE * c:HALF_D + LANE * (c + 1)] = hi.astype(BF16)
        for i in range(MOE_TILE):
            gather_row(nxt_ref, 1 - slot, i).start()
        h = hb[...]
        gate = _dot(h, wg_ref[...])
        up = _dot(h, wu_ref[...])
        a = (gate / (1.0 + jnp.exp(-gate)) * up).astype(BF16)

        @pl.when(t > 0)
        def _():
            wait_scatter(tv_ref[jnp.maximum(t - 1, 0)])

        ws = ws_ref[...]
        rounded = lambda v: v.astype(BF16).astype(F32)
        for c2 in range(TOK_WORDS // 2):
            cols = slice(2 * LANE * c2, 2 * LANE * (c2 + 1))
            lo = _dot(a, wd_ref[:, cols]) * ws
            hi = _dot(a, wd_ref[:, HALF_D + cols.start:HALF_D + cols.stop]) * ws
            words = _pack_rows(rounded(lo), rounded(hi))
            obuf[2 * c2] = words[:, :LANE]
            obuf[2 * c2 + 1] = words[:, LANE:]
        start_scatter(tv_ref[t])

        @pl.when(t == n_used - 1)
        def _():
            wait_gather(1 - slot)
            wait_scatter(tv_ref[t])


def _moe(tile_expert, next_expert, tile_valid, n_tiles, src_token, dst_row, x_tok, wg, wu, wd, w_slot):
    p_rows = src_token.shape[0]
    tm = MOE_TILE
    n_blocks = p_rows // tm
    smem_blk = lambda fn: pl.BlockSpec((tm,), fn, memory_space=pltpu.SMEM)
    hbm = pl.BlockSpec(memory_space=pl.ANY)
    grid_spec = pltpu.PrefetchScalarGridSpec(
        num_scalar_prefetch=4,
        grid=(n_blocks,),
        in_specs=[smem_blk(lambda t, te, ne, tv, nt: (t,)),
                  smem_blk(lambda t, te, ne, tv, nt: (jnp.minimum(t + 1, n_blocks - 1),)),
                  smem_blk(lambda t, te, ne, tv, nt: (t,)),
                  hbm, hbm, hbm, hbm,
                  pl.BlockSpec((tm, 1), lambda t, te, ne, tv, nt: (t, 0))],
        out_specs=hbm,
        scratch_shapes=[pltpu.VMEM((2, TOK_WORDS, tm, LANE), U32),
                        pltpu.VMEM((tm, D_MODEL), BF16),
                        pltpu.VMEM((TOK_WORDS, tm, LANE), U32),
                        pltpu.VMEM((D_MODEL, D_EXPERT), F32),
                        pltpu.VMEM((D_MODEL, D_EXPERT), F32),
                        pltpu.VMEM((D_EXPERT, D_MODEL), F32),
                        pltpu.VMEM((D_MODEL, D_EXPERT), BF16),
                        pltpu.VMEM((D_MODEL, D_EXPERT), BF16),
                        pltpu.VMEM((D_EXPERT, D_MODEL), BF16),
                        pltpu.SemaphoreType.DMA((2,)),
                        pltpu.SemaphoreType.DMA(()),
                        pltpu.SemaphoreType.DMA((3,))],
    )
    return pl.pallas_call(
        _moe_kernel,
        grid_spec=grid_spec,
        out_shape=jax.ShapeDtypeStruct((2 * x_tok.shape[0], TOK_WORDS, 1, LANE), U32),
        compiler_params=_cparams(("arbitrary",), 58),
        name="moe",
    )(tile_expert, next_expert, tile_valid, n_tiles, src_token, src_token, dst_row, x_tok, wg, wu, wd, w_slot)


def _ln_add_kernel(x_ref, g_ref, b_ref, y_hbm, o_ref, ybuf, vbuf, sem, *, tm):
    i = pl.program_id(0)
    slot = lax.rem(i, 2)

    def start_gather(tile, buf_slot):
        base = tile * (2 * tm)

        def body(j, carry):
            for k in range(2):
                pltpu.make_async_copy(y_hbm.at[base + 2 * j + k], ybuf.at[buf_slot, k, :, pl.ds(j, 1), :],
                                      sem.at[buf_slot]).start()
            return carry
        lax.fori_loop(0, tm, body, 0, unroll=4)

    @pl.when(i == 0)
    def _():
        start_gather(0, 0)

    @pl.when(i + 1 < pl.num_programs(0))
    def _():
        start_gather(i + 1, 1 - slot)

    pltpu.make_async_copy(ybuf.at[slot], ybuf.at[slot], sem.at[slot]).wait()
    for c in range(TOK_WORDS):
        a_lo, a_hi = _unpack_rows(ybuf[slot, 0, c])
        b_lo, b_hi = _unpack_rows(ybuf[slot, 1, c])
        lo = slice(LANE * c, LANE * (c + 1))
        hi = slice(HALF_D + LANE * c, HALF_D + LANE * (c + 1))
        vbuf[:, lo] = ALPHA * x_ref[:, lo] + (a_lo + b_lo)
        vbuf[:, hi] = ALPHA * x_ref[:, hi] + (a_hi + b_hi)
    o_ref[...] = _layer_norm(vbuf[...], g_ref[...], b_ref[...])


def _ln_add(x, y_tok, g, b):
    s = x.shape[0]
    tm = min(256, s)
    full = lambda shape: pl.BlockSpec(shape, lambda i: (0,) * len(shape))
    return pl.pallas_call(
        functools.partial(_ln_add_kernel, tm=tm),
        grid=(s // tm,),
        in_specs=[pl.BlockSpec((tm, D_MODEL), lambda i: (i, 0)), full(g.shape), full(b.shape),
                  pl.BlockSpec(memory_space=pl.ANY)],
        out_specs=pl.BlockSpec((tm, D_MODEL), lambda i: (i, 0)),
        out_shape=jax.ShapeDtypeStruct((s, D_MODEL), F32),
        scratch_shapes=[pltpu.VMEM((2, 2, TOK_WORDS, tm, LANE), U32),
                        pltpu.VMEM((tm, D_MODEL), F32),
                        pltpu.SemaphoreType.DMA((2,))],
        compiler_params=_cparams(("arbitrary",), 48),
        name="ln_add",
    )(x, g, b, y_tok)


def _pack_w_in(w):
    o = 0
    parts = {}
    for name, width in (("q", 512), ("k", 512), ("v", 1024), ("r", 1024), ("gl", GLA_GATE_RANK), ("u", 1024),
                        ("vg", 1024), ("cb", 1024), ("cc", 1024), ("ch", 1024), ("cq", MLA_Q_LORA),
                        ("ckv", MLA_KV_LORA), ("kr", MLA_ROPE)):
        parts[name] = w[:, o:o + width]
        o += width
    zeros = lambda n: jnp.zeros((w.shape[0], n), w.dtype)
    half = MLA_ROPE // 2
    kr = parts["kr"]
    cols = [parts["cq"], parts["ckv"], parts["q"], parts["k"], parts["v"], parts["r"], parts["u"], parts["vg"],
            parts["cb"], parts["cc"], parts["ch"],
            kr, zeros(LANE - MLA_ROPE),
            kr[:, half:], kr[:, :half], zeros(LANE - MLA_ROPE),
            parts["gl"], zeros(LANE - GLA_GATE_RANK)]
    packed = jnp.concatenate(cols, axis=1)
    packed = jnp.concatenate([packed, zeros(P_COLS - packed.shape[1])], axis=1)
    return packed.astype(BF16)


def _pack_mla_weights(wuq, wukv):
    half = MLA_ROPE // 2
    wq3 = wuq.reshape(MLA_Q_LORA, MLA_HEADS, MLA_NOPE + MLA_ROPE)
    rope = wq3[:, :, MLA_NOPE:]
    zq = jnp.zeros((MLA_Q_LORA, MLA_HEADS, MLA_QK_PAD - MLA_NOPE - MLA_ROPE), wuq.dtype)
    wq = jnp.concatenate([wq3, zq], axis=2).reshape(MLA_Q_LORA, MLA_HEADS * MLA_QK_PAD)
    zs = jnp.zeros((MLA_Q_LORA, MLA_HEADS, LANE - MLA_ROPE), wuq.dtype)
    wqs = jnp.concatenate([rope[:, :, half:], rope[:, :, :half], zs], axis=2).reshape(MLA_Q_LORA, MLA_HEADS * LANE)
    wkv3 = wukv.reshape(MLA_KV_LORA, MLA_HEADS, MLA_NOPE + MLA_V)
    wk = wkv3[:, :, :MLA_NOPE].reshape(MLA_KV_LORA, MLA_HEADS * MLA_NOPE)
    wv = wkv3[:, :, MLA_NOPE:].reshape(MLA_KV_LORA, MLA_HEADS * MLA_V)
    return wq.astype(BF16), wqs.astype(BF16), wk.astype(BF16), wv.astype(BF16)


def _rope_tables(s):
    pos = jnp.arange(s, dtype=F32)
    inv_freq = ROPE_BASE ** (-jnp.arange(0, MLA_ROPE, 2, dtype=F32) / MLA_ROPE)
    ang = pos[:, None] * inv_freq[None, :]
    cos, sin = jnp.cos(ang), jnp.sin(ang)
    z = jnp.zeros((s, LANE - MLA_ROPE), F32)
    return jnp.concatenate([cos, cos, z], axis=1), jnp.concatenate([-sin, sin, z], axis=1)


def _dispatch_plan(route, s):
    n_assign = 2 * s
    p_rows = n_assign + N_EXPERTS * MOE_TILE
    eid = route[:, 0:2].astype(jnp.int32).reshape(n_assign)
    wgt = route[:, 2:4].reshape(n_assign)
    onehot = (eid[:, None] == jnp.arange(N_EXPERTS, dtype=jnp.int32)[None, :]).astype(jnp.int32)
    csum = jnp.cumsum(onehot, axis=0)
    rank = jnp.sum(onehot * (csum - 1), axis=1)
    counts = csum[-1]
    tiles_e = (counts + MOE_TILE - 1) // MOE_TILE
    tile_end = jnp.cumsum(tiles_e)
    tile_start = tile_end - tiles_e
    slot = tile_start[eid] * MOE_TILE + rank
    slot_assign = jnp.full((p_rows,), -1, jnp.int32).at[slot].set(jnp.arange(n_assign, dtype=jnp.int32))
    is_pad = slot_assign < 0
    src_token = jnp.where(is_pad, 0, slot_assign // 2)
    dst_row = jnp.maximum(slot_assign, 0)
    w_slot = jnp.where(is_pad, 0.0, wgt[dst_row])
    n_blocks = p_rows // MOE_TILE
    tile_valid = jnp.sum(jnp.logical_not(is_pad).reshape(n_blocks, MOE_TILE).astype(jnp.int32), axis=1)
    tile_ids = jnp.arange(n_blocks, dtype=jnp.int32)
    tile_expert = jnp.minimum(jnp.sum((tile_end[None, :] <= tile_ids[:, None]).astype(jnp.int32), axis=1),
                              N_EXPERTS - 1)
    n_tiles = tile_end[-1:].astype(jnp.int32)
    experts = jnp.arange(N_EXPERTS, dtype=jnp.int32)
    later_with_tiles = (experts[None, :] > experts[:, None]) & (tiles_e[None, :] > 0)
    next_nonempty = jnp.min(jnp.where(later_with_tiles, experts[None, :], N_EXPERTS), axis=1)
    next_expert = next_nonempty[tile_expert]
    return src_token, dst_row, w_slot.reshape(p_rows, 1), tile_expert, next_expert, tile_valid, n_tiles


def kernel(x, w_in, gla_wa2, gla_ba, gla_norm, gm_ln_g, gm_ln_b, gm_ws, gm_bs, gm_norm, sc_conv, sc_norm, mla_q_norm, mla_kv_norm, mla_wuq, mla_wukv, mla_norm, w_o, ln1_g, ln1_b, router_g_w, router_g_b, router_e_w, router_e_b, exp_w_gate, exp_w_up, exp_w_down, ln2_g, ln2_b):
    bsz, s, _ = x.shape
    assert bsz == 1
    xc = x.reshape(s, D_MODEL)
    cos_t, sin_t = _rope_tables(s)
    row = lambda v: v.reshape(1, -1)
    wg_all = exp_w_gate.reshape(DEPTH * N_EXPERTS, D_MODEL, D_EXPERT)
    wu_all = exp_w_up.reshape(DEPTH * N_EXPERTS, D_MODEL, D_EXPERT)
    wd_all = exp_w_down.reshape(DEPTH * N_EXPERTS, D_EXPERT, D_MODEL)
    for l in range(DEPTH):
        p = _mm_in(xc, _pack_w_in(w_in[l]))

        wa2p = jnp.concatenate(
            [gla_wa2[l], jnp.zeros((LANE - GLA_GATE_RANK, GLA_HEADS * GLA_DK), F32)], axis=0).astype(BF16)
        out_a = _gla(p, wa2p, row(gla_ba[l]), row(gla_norm[l]))

        gm_bias = jnp.repeat(gm_bs[l].T, GM_CH, axis=1)
        out_b = _gmlp(p, row(gm_ln_g[l]), row(gm_ln_b[l]), gm_ws[l], gm_bias, row(gm_norm[l]))

        out_c = _sconv(p, sc_conv[l], row(sc_norm[l]))

        wq, wqs, wk, wv = _pack_mla_weights(mla_wuq[l], mla_wukv[l])
        q_att, k_att, v_att = _mla_proj(p, row(mla_q_norm[l]), row(mla_kv_norm[l]), wq, wqs, wk, wv, cos_t, sin_t)
        out_d = _flash(q_att, k_att, v_att, row(mla_norm[l]))

        y = _mm_out(out_a, out_b, out_c, out_d, w_o[l].astype(BF16), xc)

        wr = jnp.concatenate([router_g_w[l], router_e_w[l],
                              jnp.zeros((D_MODEL, LANE - MOE_GROUPS - N_EXPERTS), F32)], axis=1)
        wr_hi = wr.astype(BF16)
        wr_lo = (wr - wr_hi.astype(F32)).astype(BF16)
        rb = jnp.concatenate([router_g_b[l], router_e_b[l], jnp.zeros((LANE - MOE_GROUPS - N_EXPERTS,), F32)])
        x1, x1_tok, route = _ln_route(y, row(ln1_g[l]), row(ln1_b[l]), wr_hi, wr_lo, row(rb))

        src_token, dst_row, w_slot, tile_expert, next_expert, tile_valid, n_tiles = _dispatch_plan(route, s)
        next_expert = jnp.where(next_expert < N_EXPERTS, next_expert + l * N_EXPERTS, -1)
        y_tok = _moe(tile_expert + l * N_EXPERTS, next_expert, tile_valid, n_tiles, src_token, dst_row, x1_tok,
                     wg_all, wu_all, wd_all, w_slot)
        xc = _ln_add(x1, y_tok, row(ln2_g[l]), row(ln2_b[l]))
    return xc.reshape(bsz, s, D_MODEL)
```

```python
import functools

import jax
import jax.numpy as jnp
from jax import lax
from jax.experimental import pallas as pl
from jax.experimental.pallas import tpu as pltpu

F32 = jnp.float32
BF16 = jnp.bfloat16

D_MODEL = 4096
DEPTH = 2
GROUP_W = 1024

GLA_HEADS = 4
GLA_DK = 128
GLA_DV = 256
GLA_GATE_RANK = 16
GLA_TAU = 16.0
GLA_CHUNK = 64
GLA_SUB = 16

GM_GROUPS = 8
GM_CH = 128
GM_CHUNK = 128

MLA_HEADS = 8
MLA_NOPE = 128
MLA_ROPE = 64
MLA_V = 128
MLA_Q_LORA = 768
MLA_KV_LORA = 256
ROPE_BASE = 10000.0
MLA_QK_PAD = 256

MOE_GROUPS = 4
MOE_PER_GROUP = 8
N_EXPERTS = 32
D_EXPERT = 512

ALPHA = (2.0 * DEPTH) ** 0.25
LOG2_E = 1.4426950408889634

LANE = 128
MIB = 1024 * 1024

COL_CQ = 0
COL_CKV = 768
COL_Q = 1024
COL_K = 1536
COL_V = 2048
COL_R = 3072
COL_U = 4096
COL_VG = 5120
COL_CB = 6144
COL_CC = 7168
COL_CH = 8192
COL_KR = 9216
COL_KRS = 9344
COL_GL = 9472
P_COLS = 9728

MOE_TILE = 256
SCATTER_UNROLL = 8
WEIGHT_ROUND_ROWS = 512
HALF_D = D_MODEL // 2
TOK_WORDS = HALF_D // LANE
U32 = jnp.uint32
HIGH_HALF = 0xFFFF0000


def _pack_rows(lo_f32, hi_f32):
    return (lax.bitcast_convert_type(lo_f32, U32) >> 16) | lax.bitcast_convert_type(hi_f32, U32)


def _unpack_rows(words):
    return (lax.bitcast_convert_type(words << 16, F32),
            lax.bitcast_convert_type(words & jnp.uint32(HIGH_HALF), F32))


def _cparams(sem, vmem_mib):
    return pltpu.CompilerParams(dimension_semantics=sem, vmem_limit_bytes=vmem_mib * MIB)


def _dot(a, b):
    return jnp.dot(a, b, preferred_element_type=F32)


def _dot_nt(a, b):
    return lax.dot_general(a, b, (((1,), (1,)), ((), ())), preferred_element_type=F32)


def _dot_tn(a, b):
    return lax.dot_general(a, b, (((0,), (0,)), ((), ())), preferred_element_type=F32)


def _mm_in_kernel(x_ref, w_ref, o_ref, xb_ref):
    @pl.when(pl.program_id(1) == 0)
    def _():
        xb_ref[...] = x_ref[...].astype(BF16)

    o_ref[...] = _dot(xb_ref[...], w_ref[...]).astype(o_ref.dtype)


def _mm_in(x, w):
    s, k = x.shape
    n = w.shape[1]
    tm = min(1024, s)
    tn = 512
    return pl.pallas_call(
        _mm_in_kernel,
        grid=(s // tm, n // tn),
        in_specs=[pl.BlockSpec((tm, k), lambda i, j: (i, 0)),
                  pl.BlockSpec((k, tn), lambda i, j: (0, j))],
        out_specs=pl.BlockSpec((tm, tn), lambda i, j: (i, j)),
        out_shape=jax.ShapeDtypeStruct((s, n), BF16),
        scratch_shapes=[pltpu.VMEM((tm, k), BF16)],
        compiler_params=_cparams(("parallel", "arbitrary"), 58),
        name="mm_in",
    )(x, w)


def _gla_kernel(q_ref, k_ref, v_ref, r_ref, gl_ref, wa2_ref, ba_ref, ng_ref, o_ref, s_ref, *, n_chunks):
    c_len = GLA_CHUNK

    @pl.when(pl.program_id(0) == 0)
    def _():
        s_ref[...] = jnp.zeros_like(s_ref)

    row = lax.broadcasted_iota(jnp.int32, (c_len, c_len), 0)
    col = lax.broadcasted_iota(jnp.int32, (c_len, c_len), 1)
    tril = jnp.where(col <= row, 1.0, 0.0).astype(BF16)
    sub_row = lax.broadcasted_iota(jnp.int32, (GLA_SUB, c_len), 0)
    sub_col = lax.broadcasted_iota(jnp.int32, (GLA_SUB, c_len), 1)
    n_sub = c_len // GLA_SUB

    def chunk(c, carry):
        rows = pl.ds(pl.multiple_of(c * c_len, c_len), c_len)
        logit = _dot(gl_ref[rows, :], wa2_ref[...]) + ba_ref[...]
        g = (jnp.minimum(logit, 0.0) - jnp.log(1.0 + jnp.exp(-jnp.abs(logit)))) * (1.0 / GLA_TAU)
        g_hi = g.astype(BF16)
        g_lo = (g - g_hi.astype(F32)).astype(BF16)
        b_all = _dot(tril, g_hi) + _dot(tril, g_lo)

        for h in range(GLA_HEADS):
            hs = slice(GLA_DK * h, GLA_DK * (h + 1))
            vs = slice(GLA_DV * h, GLA_DV * (h + 1))
            bh = b_all[:, hs]
            qh = q_ref[rows, hs].astype(F32) * (GLA_DK ** -0.5)
            kh = k_ref[rows, hs].astype(F32)
            vh = v_ref[rows, vs]
            state = s_ref[h]

            o = _dot((qh * jnp.exp(bh)).astype(BF16), state.astype(BF16))

            att_rows = []
            for blk in range(n_sub):
                sl = slice(GLA_SUB * blk, GLA_SUB * (blk + 1))
                b_blk = bh[sl]
                q_blk = qh[sl]
                att = jnp.zeros((GLA_SUB, c_len), F32)
                if blk > 0:
                    ref = bh[GLA_SUB * blk:GLA_SUB * blk + 1, :]
                    qs = (q_blk * jnp.exp(b_blk - ref)).astype(BF16)
                    ks = (kh * jnp.exp(jnp.minimum(ref - bh, 0.0))).astype(BF16)
                    att = jnp.where(sub_col < GLA_SUB * blk, _dot_nt(qs, ks), 0.0)
                for jj in range(GLA_SUB):
                    j = GLA_SUB * blk + jj
                    t = q_blk * kh[j:j + 1, :] * jnp.exp(b_blk - bh[j:j + 1, :])
                    rs = jnp.sum(t, axis=-1, keepdims=True)
                    att = jnp.where((sub_col == j) & (sub_row >= jj), rs, att)
                att_rows.append(att)
            att_full = jnp.concatenate(att_rows, axis=0).astype(BF16)
            o = o + _dot(att_full, vh)

            b_last = bh[c_len - 1:c_len, :]
            kd = (kh * jnp.exp(b_last - bh)).astype(BF16)
            decay_col = jnp.transpose(jnp.broadcast_to(jnp.exp(b_last), (GLA_DK, GLA_DK)))
            s_ref[h] = state * jnp.concatenate([decay_col, decay_col], axis=1) + _dot_tn(kd, vh)

            var = jnp.mean(o * o, axis=-1, keepdims=True)
            on = o * lax.rsqrt(var + 1e-6) * ng_ref[:, vs]
            rr = r_ref[rows, vs].astype(F32)
            o_ref[rows, vs] = (on * (rr / (1.0 + jnp.exp(-rr)))).astype(o_ref.dtype)
        return carry

    lax.fori_loop(0, n_chunks, chunk, 0)


def _gla(p, wa2p, ba, ng):
    s = p.shape[0]
    t = min(256, s)
    col = lambda width, off: pl.BlockSpec((t, width), lambda i: (i, off // width))
    full = lambda shape: pl.BlockSpec(shape, lambda i: (0,) * len(shape))
    return pl.pallas_call(
        functools.partial(_gla_kernel, n_chunks=t // GLA_CHUNK),
        grid=(s // t,),
        in_specs=[col(512, COL_Q), col(512, COL_K), col(1024, COL_V), col(1024, COL_R), col(LANE, COL_GL),
                  full(wa2p.shape), full(ba.shape), full(ng.shape)],
        out_specs=pl.BlockSpec((t, GROUP_W), lambda i: (i, 0)),
        out_shape=jax.ShapeDtypeStruct((s, GROUP_W), BF16),
        scratch_shapes=[pltpu.VMEM((GLA_HEADS, GLA_DK, GLA_DV), F32)],
        compiler_params=_cparams(("arbitrary",), 32),
        name="gla",
    )(p, p, p, p, p, wa2p, ba, ng)


def _gelu(x):
    return 0.5 * x * (1.0 + lax.erf(x * 0.7071067811865476))


def _gmlp_kernel(u_ref, v_ref, lg_ref, lb_ref, ws_ref, bias_ref, ng_ref, o_ref, buf_ref, *, n_chunks):
    row = lax.broadcasted_iota(jnp.int32, (GM_CHUNK, GM_CHUNK), 0)
    col = lax.broadcasted_iota(jnp.int32, (GM_CHUNK, GM_CHUNK), 1)
    causal = col <= row
    v = _gelu(v_ref[...].astype(F32))
    mu = jnp.mean(v, axis=-1, keepdims=True)
    d = v - mu
    var = jnp.mean(d * d, axis=-1, keepdims=True)
    vb = (d * lax.rsqrt(var + 1e-5) * lg_ref[...] + lb_ref[...]).astype(BF16)
    for g in range(GM_GROUPS):
        cs = slice(GM_CH * g, GM_CH * (g + 1))
        w = jnp.where(causal, ws_ref[g], 0.0).astype(BF16)
        for c in range(n_chunks):
            rs = slice(GM_CHUNK * c, GM_CHUNK * (c + 1))
            mixed = _dot(w, vb[rs, cs]) + bias_ref[:, cs]
            buf_ref[rs, cs] = _gelu(u_ref[rs, cs].astype(F32)) * mixed
    out = buf_ref[...]
    ms = jnp.mean(out * out, axis=-1, keepdims=True)
    o_ref[...] = (out * lax.rsqrt(ms + 1e-6) * ng_ref[...]).astype(o_ref.dtype)


def _gmlp(p, lg, lb, ws, bias, ng):
    s = p.shape[0]
    t = min(256, s)
    col = lambda off: pl.BlockSpec((t, GROUP_W), lambda i: (i, off // GROUP_W))
    full = lambda shape: pl.BlockSpec(shape, lambda i: (0,) * len(shape))
    return pl.pallas_call(
        functools.partial(_gmlp_kernel, n_chunks=t // GM_CHUNK),
        grid=(s // t,),
        in_specs=[col(COL_U), col(COL_VG), full(lg.shape), full(lb.shape), full(ws.shape), full(bias.shape),
                  full(ng.shape)],
        out_specs=pl.BlockSpec((t, GROUP_W), lambda i: (i, 0)),
        out_shape=jax.ShapeDtypeStruct((s, GROUP_W), BF16),
        scratch_shapes=[pltpu.VMEM((t, GROUP_W), F32)],
        compiler_params=_cparams(("parallel",), 32),
        name="gmlp",
    )(p, p, lg, lb, ws, bias, ng)


CONV_HALO = 8


def _sconv_kernel(b_ref, c_ref, h_ref, w_ref, ng_ref, o_ref, z_ref):
    t = b_ref.shape[0]

    @pl.when(pl.program_id(0) == 0)
    def _():
        z_ref[0:CONV_HALO, :] = jnp.zeros((CONV_HALO, GROUP_W), F32)

    z = c_ref[...].astype(F32) * h_ref[...].astype(F32)
    z_ref[CONV_HALO:CONV_HALO + t, :] = z
    z1 = z_ref[CONV_HALO - 1:CONV_HALO - 1 + t, :]
    z2 = z_ref[CONV_HALO - 2:CONV_HALO - 2 + t, :]
    y = w_ref[0:1, :] * z2 + w_ref[1:2, :] * z1 + w_ref[2:3, :] * z
    z_ref[0:CONV_HALO, :] = z[t - CONV_HALO:t, :]
    out = b_ref[...].astype(F32) * y
    ms = jnp.mean(out * out, axis=-1, keepdims=True)
    o_ref[...] = (out * lax.rsqrt(ms + 1e-6) * ng_ref[...]).astype(o_ref.dtype)


def _sconv(p, w, ng):
    s = p.shape[0]
    t = min(256, s)
    col = lambda off: pl.BlockSpec((t, GROUP_W), lambda i: (i, off // GROUP_W))
    full = lambda shape: pl.BlockSpec(shape, lambda i: (0,) * len(shape))
    return pl.pallas_call(
        _sconv_kernel,
        grid=(s // t,),
        in_specs=[col(COL_CB), col(COL_CC), col(COL_CH), full(w.shape), full(ng.shape)],
        out_specs=pl.BlockSpec((t, GROUP_W), lambda i: (i, 0)),
        out_shape=jax.ShapeDtypeStruct((s, GROUP_W), BF16),
        scratch_shapes=[pltpu.VMEM((t + CONV_HALO, GROUP_W), F32)],
        compiler_params=_cparams(("arbitrary",), 32),
        name="sconv",
    )(p, p, p, w, ng)


def _mla_proj_kernel(cq_ref, ckv_ref, kr_ref, krs_ref, qn_ref, kvn_ref, wq_ref, wqs_ref, wk_ref, wv_ref,
                     cos_ref, sin_ref, q_out, k_out, v_out):
    def rms(ref, g_ref):
        t = ref[...].astype(F32)
        return (t * lax.rsqrt(jnp.mean(t * t, axis=-1, keepdims=True) + 1e-6) * g_ref[...]).astype(BF16)

    cqn = rms(cq_ref, qn_ref)
    ckvn = rms(ckv_ref, kvn_ref)
    qm = _dot(cqn, wq_ref[...])
    qsw = _dot(cqn, wqs_ref[...])
    kn = _dot(ckvn, wk_ref[...])
    vv = _dot(ckvn, wv_ref[...])
    cos = cos_ref[...]
    sin = sin_ref[...]
    scale = (MLA_NOPE + MLA_ROPE) ** -0.5 * LOG2_E
    kr_rot = (kr_ref[...].astype(F32) * cos + krs_ref[...].astype(F32) * sin).astype(k_out.dtype)
    ones_col = jnp.where(lax.broadcasted_iota(jnp.int32, (cos.shape[0], LANE), 1) == 0, 1.0, 0.0).astype(v_out.dtype)
    for h in range(MLA_HEADS):
        lo = MLA_QK_PAD * h
        v_out[:, lo:lo + LANE] = vv[:, LANE * h:LANE * (h + 1)].astype(v_out.dtype)
        v_out[:, lo + LANE:lo + 2 * LANE] = ones_col
        q_out[:, lo:lo + LANE] = (qm[:, lo:lo + LANE] * scale).astype(q_out.dtype)
        q_out[:, lo + LANE:lo + 2 * LANE] = (
            (qm[:, lo + LANE:lo + 2 * LANE] * cos + qsw[:, LANE * h:LANE * (h + 1)] * sin) * scale
        ).astype(q_out.dtype)
        k_out[:, lo:lo + LANE] = kn[:, LANE * h:LANE * (h + 1)].astype(k_out.dtype)
        k_out[:, lo + LANE:lo + 2 * LANE] = kr_rot


def _mla_proj(p, qn, kvn, wq, wqs, wk, wv, cos, sin):
    s = p.shape[0]
    t = min(256, s)
    full = lambda shape: pl.BlockSpec(shape, lambda i: (0,) * len(shape))
    rowb = lambda width: pl.BlockSpec((t, width), lambda i: (i, 0))
    qk_w = MLA_HEADS * MLA_QK_PAD
    return pl.pallas_call(
        _mla_proj_kernel,
        grid=(s // t,),
        in_specs=[pl.BlockSpec((t, MLA_Q_LORA), lambda i: (i, COL_CQ // MLA_Q_LORA)),
                  pl.BlockSpec((t, MLA_KV_LORA), lambda i: (i, COL_CKV // MLA_KV_LORA)),
                  pl.BlockSpec((t, LANE), lambda i: (i, COL_KR // LANE)),
                  pl.BlockSpec((t, LANE), lambda i: (i, COL_KRS // LANE)),
                  full(qn.shape), full(kvn.shape), full(wq.shape), full(wqs.shape), full(wk.shape), full(wv.shape),
                  rowb(LANE), rowb(LANE)],
        out_specs=[rowb(qk_w), rowb(qk_w), rowb(qk_w)],
        out_shape=[jax.ShapeDtypeStruct((s, qk_w), BF16)] * 3,
        compiler_params=_cparams(("parallel",), 40),
        name="mla_proj",
    )(p, p, p, p, qn, kvn, wq, wqs, wk, wv, cos, sin)


FLASH_TQ = 256
FLASH_TK = 2048


def _flash_kernel(qi_ref, ki_ref, q_ref, k_ref, v_ref, ng_ref, o_ref, acc_ref, m_ref, *, tq, tk):
    step = pl.program_id(0)
    qi = qi_ref[step]
    ki = ki_ref[step]
    last_ki = (qi * tq) // tk

    @pl.when(ki == 0)
    def _():
        m_ref[...] = jnp.full(m_ref.shape, -jnp.inf, F32)
        acc_ref[...] = jnp.zeros_like(acc_ref)

    def accumulate(masked, k0, kn):
        keys = slice(k0, k0 + kn)
        if masked:
            row = qi * tq + lax.broadcasted_iota(jnp.int32, (tq, kn), 0)
            col = ki * tk + k0 + lax.broadcasted_iota(jnp.int32, (tq, kn), 1)
            visible = col <= row

        def scores(h):
            hs = slice(MLA_QK_PAD * h, MLA_QK_PAD * (h + 1))
            return _dot_nt(q_ref[:, hs], k_ref[keys, hs])

        sc = scores(0)
        for h in range(MLA_HEADS):
            hs = slice(MLA_QK_PAD * h, MLA_QK_PAD * (h + 1))
            sc_next = scores(h + 1) if h + 1 < MLA_HEADS else None
            if masked:
                sc = jnp.where(visible, sc, -jnp.inf)
            m_old = m_ref[h]
            m_new = jnp.maximum(m_old, jnp.max(sc, axis=-1, keepdims=True))
            pr = jnp.exp2(sc - m_new).astype(BF16)
            acc_ref[:, hs] = jnp.exp2(m_old - m_new) * acc_ref[:, hs] + _dot(pr, v_ref[keys, hs])
            m_ref[h] = m_new
            sc = sc_next

    @pl.when(ki < last_ki)
    def _():
        accumulate(False, 0, tk)

    @pl.when(ki == last_ki)
    def _():
        half = tk // 2
        accumulate(True, 0, half)

        @pl.when(qi * tq + tq - 1 >= ki * tk + half)
        def _():
            accumulate(True, half, half)

        outs = []
        for h in range(MLA_HEADS):
            lo = MLA_QK_PAD * h
            outs.append(acc_ref[:, lo:lo + MLA_V] / acc_ref[:, lo + MLA_V:lo + MLA_V + 1])
        o = jnp.concatenate(outs, axis=1)
        ms = jnp.mean(o * o, axis=-1, keepdims=True)
        o_ref[...] = (o * lax.rsqrt(ms + 1e-6) * ng_ref[...]).astype(o_ref.dtype)


def _flash(q, k, v, ng):
    s = q.shape[0]
    tk = min(FLASH_TK, s)
    tq = min(FLASH_TQ, tk)
    nq = s // tq
    pairs = [(a, b) for a in range(nq) for b in range((a * tq) // tk + 1)]
    qi_tab = jnp.asarray([a for a, _ in pairs], jnp.int32)
    ki_tab = jnp.asarray([b for _, b in pairs], jnp.int32)
    qk_w = MLA_HEADS * MLA_QK_PAD
    grid_spec = pltpu.PrefetchScalarGridSpec(
        num_scalar_prefetch=2,
        grid=(len(pairs),),
        in_specs=[pl.BlockSpec((tq, qk_w), lambda t, qi, ki: (qi[t], 0)),
                  pl.BlockSpec((tk, qk_w), lambda t, qi, ki: (ki[t], 0)),
                  pl.BlockSpec((tk, qk_w), lambda t, qi, ki: (ki[t], 0)),
                  pl.BlockSpec((1, GROUP_W), lambda t, qi, ki: (0, 0))],
        out_specs=pl.BlockSpec((tq, GROUP_W), lambda t, qi, ki: (qi[t], 0)),
        scratch_shapes=[pltpu.VMEM((tq, qk_w), F32),
                        pltpu.VMEM((MLA_HEADS, tq, 1), F32)],
    )
    return pl.pallas_call(
        functools.partial(_flash_kernel, tq=tq, tk=tk),
        grid_spec=grid_spec,
        out_shape=jax.ShapeDtypeStruct((s, GROUP_W), BF16),
        compiler_params=_cparams(("arbitrary",), 56),
        name="flash",
    )(qi_tab, ki_tab, q, k, v, ng)


def _mm_out_kernel(a0_ref, a1_ref, a2_ref, a3_ref, w_ref, x_ref, o_ref):
    acc = _dot(a0_ref[...], w_ref[0:GROUP_W, :])
    acc = acc + _dot(a1_ref[...], w_ref[GROUP_W:2 * GROUP_W, :])
    acc = acc + _dot(a2_ref[...], w_ref[2 * GROUP_W:3 * GROUP_W, :])
    acc = acc + _dot(a3_ref[...], w_ref[3 * GROUP_W:4 * GROUP_W, :])
    o_ref[...] = ALPHA * x_ref[...] + acc


def _mm_out(a0, a1, a2, a3, w, x):
    s = x.shape[0]
    tm = min(512, s)
    tn = 1024
    a_spec = pl.BlockSpec((tm, GROUP_W), lambda i, j: (i, 0))
    return pl.pallas_call(
        _mm_out_kernel,
        grid=(s // tm, D_MODEL // tn),
        in_specs=[a_spec, a_spec, a_spec, a_spec,
                  pl.BlockSpec((D_MODEL, tn), lambda i, j: (0, j)),
                  pl.BlockSpec((tm, tn), lambda i, j: (i, j))],
        out_specs=pl.BlockSpec((tm, tn), lambda i, j: (i, j)),
        out_shape=jax.ShapeDtypeStruct((s, D_MODEL), F32),
        compiler_params=_cparams(("parallel", "arbitrary"), 48),
        name="mm_out",
    )(a0, a1, a2, a3, w, x)


def _layer_norm(t, g, b):
    mu = jnp.mean(t, axis=-1, keepdims=True)
    d = t - mu
    var = jnp.mean(d * d, axis=-1, keepdims=True)
    return d * lax.rsqrt(var + 1e-5) * g + b


def _ln_route_kernel(y_ref, g_ref, b_ref, wh_ref, wl_ref, rb_ref, x_out, route_out, xtok_hbm, tbuf, sem, *, tm):
    i = pl.program_id(0)
    slot = lax.rem(i, 2)

    def wait_rows(buf_slot):
        pltpu.make_async_copy(tbuf.at[buf_slot], tbuf.at[buf_slot], sem.at[buf_slot]).wait()

    x1 = _layer_norm(y_ref[...], g_ref[...], b_ref[...])
    x_out[...] = x1
    x_hi = x1.astype(BF16)
    x_hi32 = x_hi.astype(F32)

    @pl.when(i >= 2)
    def _():
        wait_rows(slot)

    words = _pack_rows(x_hi32[:, :HALF_D], x_hi32[:, HALF_D:])
    for c in range(TOK_WORDS):
        tbuf[slot, c] = words[:, LANE * c:LANE * (c + 1)]
    for j in range(tm):
        pltpu.make_async_copy(tbuf.at[slot, :, pl.ds(j, 1), :], xtok_hbm.at[i * tm + j], sem.at[slot]).start()

    @pl.when(i == pl.num_programs(0) - 1)
    def _():
        wait_rows(slot)

        @pl.when(i >= 1)
        def _():
            wait_rows(1 - slot)

    x_lo = (x1 - x_hi32).astype(BF16)
    logits = _dot(x_hi, wh_ref[...]) + _dot(x_lo, wh_ref[...]) + _dot(x_hi, wl_ref[...]) + rb_ref[...]
    lane = lax.broadcasted_iota(jnp.int32, logits.shape, 1).astype(F32)
    neg = -jnp.inf
    big = 1e9
    is_grp = lane < MOE_GROUPS
    lg = jnp.where(is_grp, logits, neg)
    mg = jnp.max(lg, axis=-1, keepdims=True)
    gsel = jnp.min(jnp.where(lg == mg, lane, big), axis=-1, keepdims=True)
    pg_sel = 1.0 / jnp.sum(jnp.where(is_grp, jnp.exp(lg - mg), 0.0), axis=-1, keepdims=True)
    lo = MOE_GROUPS + MOE_PER_GROUP * gsel
    le = jnp.where((lane >= lo) & (lane < lo + MOE_PER_GROUP), logits, neg)
    v1 = jnp.max(le, axis=-1, keepdims=True)
    i1 = jnp.min(jnp.where(le == v1, lane, big), axis=-1, keepdims=True)
    le2 = jnp.where(lane == i1, neg, le)
    v2 = jnp.max(le2, axis=-1, keepdims=True)
    i2 = jnp.min(jnp.where(le2 == v2, lane, big), axis=-1, keepdims=True)
    e = jnp.exp(v2 - v1)
    w1 = pg_sel / (1.0 + e)
    w2 = pg_sel * e / (1.0 + e)
    route = jnp.where(lane == 0.0, i1 - MOE_GROUPS,
                      jnp.where(lane == 1.0, i2 - MOE_GROUPS,
                                jnp.where(lane == 2.0, w1, jnp.where(lane == 3.0, w2, 0.0))))
    route_out[...] = route


def _ln_route(y, g, b, wh, wl, rb):
    s = y.shape[0]
    tm = min(256, s)
    full = lambda shape: pl.BlockSpec(shape, lambda i: (0,) * len(shape))
    rowb = lambda width: pl.BlockSpec((tm, width), lambda i: (i, 0))
    x1, route, x1_tok = pl.pallas_call(
        functools.partial(_ln_route_kernel, tm=tm),
        grid=(s // tm,),
        in_specs=[rowb(D_MODEL), full(g.shape), full(b.shape), full(wh.shape), full(wl.shape), full(rb.shape)],
        out_specs=[rowb(D_MODEL), rowb(LANE), pl.BlockSpec(memory_space=pl.ANY)],
        out_shape=[jax.ShapeDtypeStruct((s, D_MODEL), F32),
                   jax.ShapeDtypeStruct((s, LANE), F32),
                   jax.ShapeDtypeStruct((s, TOK_WORDS, 1, LANE), U32)],
        scratch_shapes=[pltpu.VMEM((2, TOK_WORDS, tm, LANE), U32), pltpu.SemaphoreType.DMA((2,))],
        compiler_params=_cparams(("arbitrary",), 48),
        name="ln_route",
    )(y, g, b, wh, wl, rb)
    return x1, x1_tok, route


def _moe_kernel(te_ref, ne_ref, tv_ref, nt_ref, src_ref, nxt_ref, dst_ref, x_hbm, wg_hbm, wu_hbm, wd_hbm, ws_ref,
                y_hbm, gbuf, hb, obuf, stg_g, stg_u, stg_d, wg_ref, wu_ref, wd_ref, gsem, ssem, wsem):
    t = pl.program_id(0)
    n_used = nt_ref[0]
    slot = lax.rem(t, 2)

    def weight_copies(e):
        return (pltpu.make_async_copy(wg_hbm.at[e], stg_g, wsem.at[0]),
                pltpu.make_async_copy(wu_hbm.at[e], stg_u, wsem.at[1]),
                pltpu.make_async_copy(wd_hbm.at[e], stg_d, wsem.at[2]))

    def round_weights(stg, dst, rows):
        def body(r, carry):
            rs = pl.ds(pl.multiple_of(r * rows, rows), rows)
            dst[rs, :] = stg[rs, :].astype(BF16)
            return carry
        lax.fori_loop(0, stg.shape[0] // rows, body, 0)

    @pl.when(t == 0)
    def _():
        for cp in weight_copies(te_ref[0]):
            cp.start()

    @pl.when(t < n_used)
    def _():
        expert = te_ref[t]
        first_tile_of_expert = jnp.logical_or(t == 0, expert != te_ref[jnp.maximum(t - 1, 0)])

        @pl.when(first_tile_of_expert)
        def _():
            for cp in weight_copies(expert):
                cp.wait()
            round_weights(stg_g, wg_ref, WEIGHT_ROUND_ROWS)
            round_weights(stg_u, wu_ref, WEIGHT_ROUND_ROWS)
            round_weights(stg_d, wd_ref, WEIGHT_ROUND_ROWS * D_EXPERT // D_MODEL)

            @pl.when(ne_ref[t] >= 0)
            def _():
                for cp in weight_copies(ne_ref[t]):
                    cp.start()

    def gather_row(idx_ref, buf_slot, i):
        return pltpu.make_async_copy(x_hbm.at[idx_ref[i]], gbuf.at[buf_slot, :, pl.ds(i, 1), :], gsem.at[buf_slot])

    def wait_gather(buf_slot):
        pltpu.make_async_copy(gbuf.at[buf_slot], gbuf.at[buf_slot], gsem.at[buf_slot]).wait()

    def scatter_row(i):
        return pltpu.make_async_copy(obuf.at[:, pl.ds(i, 1), :], y_hbm.at[dst_ref[i]], ssem)

    def start_scatter(n_rows):
        def group(g, carry):
            for u in range(SCATTER_UNROLL):
                scatter_row(g * SCATTER_UNROLL + u).start()
            return carry

        def single(i, carry):
            scatter_row(i).start()
            return carry
        n_groups = n_rows // SCATTER_UNROLL
        lax.fori_loop(0, n_groups, group, 0)
        lax.fori_loop(n_groups * SCATTER_UNROLL, n_rows, single, 0)

    def wait_scatter(n_rows):
        rows = obuf.at[:, pl.ds(0, n_rows), :]
        pltpu.make_async_copy(rows, rows, ssem).wait()

    @pl.when(t == 0)
    def _():
        def body(i, carry):
            gather_row(src_ref, 0, i).start()
            return carry
        lax.fori_loop(0, MOE_TILE, body, 0, unroll=8)

    @pl.when(t < n_used)
    def _():
        wait_gather(slot)
        for c in range(TOK_WORDS):
            lo, hi = _unpack_rows(gbuf[slot, c])
            hb[:, LANE * c:LANE * (c + 1)] = lo.astype(BF16)
            hb[:, HALF_D + LANE * c:HALF_D + LANE * (c + 1)] = hi.astype(BF16)
        for i in range(MOE_TILE):
            gather_row(nxt_ref, 1 - slot, i).start()
        h = hb[...]
        gate = _dot(h, wg_ref[...])
        up = _dot(h, wu_ref[...])
        a = (gate / (1.0 + jnp.exp(-gate)) * up).astype(BF16)

        @pl.when(t > 0)
        def _():
            wait_scatter(tv_ref[jnp.maximum(t - 1, 0)])

        ws = ws_ref[...]
        rounded = lambda v: v.astype(BF16).astype(F32)
        for c2 in range(TOK_WORDS // 2):
            cols = slice(2 * LANE * c2, 2 * LANE * (c2 + 1))
            lo = _dot(a, wd_ref[:, cols]) * ws
            hi = _dot(a, wd_ref[:, HALF_D + cols.start:HALF_D + cols.stop]) * ws
            words = _pack_rows(rounded(lo), rounded(hi))
            obuf[2 * c2] = words[:, :LANE]
            obuf[2 * c2 + 1] = words[:, LANE:]
        start_scatter(tv_ref[t])

        @pl.when(t == n_used - 1)
        def _():
            wait_gather(1 - slot)
            wait_scatter(tv_ref[t])


def _moe(tile_expert, next_expert, tile_valid, n_tiles, src_token, dst_row, x_tok, wg, wu, wd, w_slot):
    p_rows = src_token.shape[0]
    tm = MOE_TILE
    n_blocks = p_rows // tm
    smem_blk = lambda fn: pl.BlockSpec((tm,), fn, memory_space=pltpu.SMEM)
    hbm = pl.BlockSpec(memory_space=pl.ANY)
    grid_spec = pltpu.PrefetchScalarGridSpec(
        num_scalar_prefetch=4,
        grid=(n_blocks,),
        in_specs=[smem_blk(lambda t, te, ne, tv, nt: (t,)),
                  smem_blk(lambda t, te, ne, tv, nt: (jnp.minimum(t + 1, n_blocks - 1),)),
                  smem_blk(lambda t, te, ne, tv, nt: (t,)),
                  hbm, hbm, hbm, hbm,
                  pl.BlockSpec((tm, 1), lambda t, te, ne, tv, nt: (t, 0))],
        out_specs=hbm,
        scratch_shapes=[pltpu.VMEM((2, TOK_WORDS, tm, LANE), U32),
                        pltpu.VMEM((tm, D_MODEL), BF16),
                        pltpu.VMEM((TOK_WORDS, tm, LANE), U32),
                        pltpu.VMEM((D_MODEL, D_EXPERT), F32),
                        pltpu.VMEM((D_MODEL, D_EXPERT), F32),
                        pltpu.VMEM((D_EXPERT, D_MODEL), F32),
                        pltpu.VMEM((D_MODEL, D_EXPERT), BF16),
                        pltpu.VMEM((D_MODEL, D_EXPERT), BF16),
                        pltpu.VMEM((D_EXPERT, D_MODEL), BF16),
                        pltpu.SemaphoreType.DMA((2,)),
                        pltpu.SemaphoreType.DMA(()),
                        pltpu.SemaphoreType.DMA((3,))],
    )
    return pl.pallas_call(
        _moe_kernel,
        grid_spec=grid_spec,
        out_shape=jax.ShapeDtypeStruct((2 * x_tok.shape[0], TOK_WORDS, 1, LANE), U32),
        compiler_params=_cparams(("arbitrary",), 58),
        name="moe",
    )(tile_expert, next_expert, tile_valid, n_tiles, src_token, src_token, dst_row, x_tok, wg, wu, wd, w_slot)


def _ln_add_kernel(x_ref, g_ref, b_ref, y_hbm, o_ref, ybuf, vbuf, sem, *, tm):
    i = pl.program_id(0)
    slot = lax.rem(i, 2)

    def start_gather(tile, buf_slot):
        base = tile * (2 * tm)

        def body(j, carry):
            for k in range(2):
                pltpu.make_async_copy(y_hbm.at[base + 2 * j + k], ybuf.at[buf_slot, k, :, pl.ds(j, 1), :],
                                      sem.at[buf_slot]).start()
            return carry
        lax.fori_loop(0, tm, body, 0, unroll=4)

    @pl.when(i == 0)
    def _():
        start_gather(0, 0)

    @pl.when(i + 1 < pl.num_programs(0))
    def _():
        start_gather(i + 1, 1 - slot)

    pltpu.make_async_copy(ybuf.at[slot], ybuf.at[slot], sem.at[slot]).wait()
    for c in range(TOK_WORDS):
        a_lo, a_hi = _unpack_rows(ybuf[slot, 0, c])
        b_lo, b_hi = _unpack_rows(ybuf[slot, 1, c])
        lo = slice(LANE * c, LANE * (c + 1))
        hi = slice(HALF_D + LANE * c, HALF_D + LANE * (c + 1))
        vbuf[:, lo] = ALPHA * x_ref[:, lo] + (a_lo + b_lo)
        vbuf[:, hi] = ALPHA * x_ref[:, hi] + (a_hi + b_hi)
    o_ref[...] = _layer_norm(vbuf[...], g_ref[...], b_ref[...])


def _ln_add(x, y_tok, g, b):
    s = x.shape[0]
    tm = min(256, s)
    full = lambda shape: pl.BlockSpec(shape, lambda i: (0,) * len(shape))
    return pl.pallas_call(
        functools.partial(_ln_add_kernel, tm=tm),
        grid=(s // tm,),
        in_specs=[pl.BlockSpec((tm, D_MODEL), lambda i: (i, 0)), full(g.shape), full(b.shape),
                  pl.BlockSpec(memory_space=pl.ANY)],
        out_specs=pl.BlockSpec((tm, D_MODEL), lambda i: (i, 0)),
        out_shape=jax.ShapeDtypeStruct((s, D_MODEL), F32),
        scratch_shapes=[pltpu.VMEM((2, 2, TOK_WORDS, tm, LANE), U32),
                        pltpu.VMEM((tm, D_MODEL), F32),
                        pltpu.SemaphoreType.DMA((2,))],
        compiler_params=_cparams(("arbitrary",), 48),
        name="ln_add",
    )(x, g, b, y_tok)


def _w_in_sources():
    src = {}
    o = 0
    for name, width in (("q", 512), ("k", 512), ("v", 1024), ("r", 1024), ("gl", GLA_GATE_RANK), ("u", 1024),
                        ("vg", 1024), ("cb", 1024), ("cc", 1024), ("ch", 1024), ("cq", MLA_Q_LORA),
                        ("ckv", MLA_KV_LORA), ("kr", MLA_ROPE)):
        src[name] = (o, width)
        o += width
    return src, o


def _pack_w_in_kernel(w_ref, o_ref):
    src, _ = _w_in_sources()
    rows = w_ref.shape[0]
    for name, dst in (("cq", COL_CQ), ("ckv", COL_CKV), ("q", COL_Q), ("k", COL_K), ("v", COL_V), ("r", COL_R),
                      ("u", COL_U), ("vg", COL_VG), ("cb", COL_CB), ("cc", COL_CC), ("ch", COL_CH)):
        s0, width = src[name]
        o_ref[:, dst:dst + width] = w_ref[:, s0:s0 + width].astype(BF16)
    zeros = lambda n: jnp.zeros((rows, n), BF16)
    half = MLA_ROPE // 2
    kr0, _ = src["kr"]
    kr = w_ref[:, kr0:kr0 + MLA_ROPE].astype(BF16)
    o_ref[:, COL_KR:COL_KR + LANE] = jnp.concatenate([kr, zeros(LANE - MLA_ROPE)], axis=1)
    o_ref[:, COL_KRS:COL_KRS + LANE] = jnp.concatenate([kr[:, half:], kr[:, :half], zeros(LANE - MLA_ROPE)], axis=1)
    gl0, _ = src["gl"]
    o_ref[:, COL_GL:COL_GL + LANE] = jnp.concatenate(
        [w_ref[:, gl0:gl0 + GLA_GATE_RANK].astype(BF16), zeros(LANE - GLA_GATE_RANK)], axis=1)
    o_ref[:, COL_GL + LANE:P_COLS] = zeros(P_COLS - COL_GL - LANE)


def _pack_w_in(w):
    k, d_in = w.shape
    assert d_in == _w_in_sources()[1]
    tr = 256
    return pl.pallas_call(
        _pack_w_in_kernel,
        grid=(k // tr,),
        in_specs=[pl.BlockSpec((tr, d_in), lambda i: (i, 0))],
        out_specs=pl.BlockSpec((tr, P_COLS), lambda i: (i, 0)),
        out_shape=jax.ShapeDtypeStruct((k, P_COLS), BF16),
        compiler_params=_cparams(("parallel",), 48),
        name="pack_w_in",
    )(w)


def _pack_mla_weights(wuq, wukv):
    half = MLA_ROPE // 2
    wq3 = wuq.reshape(MLA_Q_LORA, MLA_HEADS, MLA_NOPE + MLA_ROPE)
    rope = wq3[:, :, MLA_NOPE:]
    zq = jnp.zeros((MLA_Q_LORA, MLA_HEADS, MLA_QK_PAD - MLA_NOPE - MLA_ROPE), wuq.dtype)
    wq = jnp.concatenate([wq3, zq], axis=2).reshape(MLA_Q_LORA, MLA_HEADS * MLA_QK_PAD)
    zs = jnp.zeros((MLA_Q_LORA, MLA_HEADS, LANE - MLA_ROPE), wuq.dtype)
    wqs = jnp.concatenate([rope[:, :, half:], rope[:, :, :half], zs], axis=2).reshape(MLA_Q_LORA, MLA_HEADS * LANE)
    wkv3 = wukv.reshape(MLA_KV_LORA, MLA_HEADS, MLA_NOPE + MLA_V)
    wk = wkv3[:, :, :MLA_NOPE].reshape(MLA_KV_LORA, MLA_HEADS * MLA_NOPE)
    wv = wkv3[:, :, MLA_NOPE:].reshape(MLA_KV_LORA, MLA_HEADS * MLA_V)
    return wq.astype(BF16), wqs.astype(BF16), wk.astype(BF16), wv.astype(BF16)


def _rope_tables(s):
    pos = jnp.arange(s, dtype=F32)
    inv_freq = ROPE_BASE ** (-jnp.arange(0, MLA_ROPE, 2, dtype=F32) / MLA_ROPE)
    ang = pos[:, None] * inv_freq[None, :]
    cos, sin = jnp.cos(ang), jnp.sin(ang)
    z = jnp.zeros((s, LANE - MLA_ROPE), F32)
    return jnp.concatenate([cos, cos, z], axis=1), jnp.concatenate([-sin, sin, z], axis=1)


def _dispatch_plan(route, s):
    n_assign = 2 * s
    p_rows = n_assign + N_EXPERTS * MOE_TILE
    eid = route[:, 0:2].astype(jnp.int32).reshape(n_assign)
    wgt = route[:, 2:4].reshape(n_assign)
    onehot = (eid[:, None] == jnp.arange(N_EXPERTS, dtype=jnp.int32)[None, :]).astype(jnp.int32)
    csum = jnp.cumsum(onehot, axis=0)
    rank = jnp.sum(onehot * (csum - 1), axis=1)
    counts = csum[-1]
    tiles_e = (counts + MOE_TILE - 1) // MOE_TILE
    tile_end = jnp.cumsum(tiles_e)
    tile_start = tile_end - tiles_e
    slot = tile_start[eid] * MOE_TILE + rank
    slot_assign = jnp.full((p_rows,), -1, jnp.int32).at[slot].set(jnp.arange(n_assign, dtype=jnp.int32))
    is_pad = slot_assign < 0
    src_token = jnp.where(is_pad, 0, slot_assign // 2)
    dst_row = jnp.maximum(slot_assign, 0)
    w_slot = jnp.where(is_pad, 0.0, wgt[dst_row])
    n_blocks = p_rows // MOE_TILE
    tile_valid = jnp.sum(jnp.logical_not(is_pad).reshape(n_blocks, MOE_TILE).astype(jnp.int32), axis=1)
    tile_ids = jnp.arange(n_blocks, dtype=jnp.int32)
    tile_expert = jnp.minimum(jnp.sum((tile_end[None, :] <= tile_ids[:, None]).astype(jnp.int32), axis=1),
                              N_EXPERTS - 1)
    n_tiles = tile_end[-1:].astype(jnp.int32)
    experts = jnp.arange(N_EXPERTS, dtype=jnp.int32)
    later_with_tiles = (experts[None, :] > experts[:, None]) & (tiles_e[None, :] > 0)
    next_nonempty = jnp.min(jnp.where(later_with_tiles, experts[None, :], N_EXPERTS), axis=1)
    next_expert = next_nonempty[tile_expert]
    return src_token, dst_row, w_slot.reshape(p_rows, 1), tile_expert, next_expert, tile_valid, n_tiles


def kernel(x, w_in, gla_wa2, gla_ba, gla_norm, gm_ln_g, gm_ln_b, gm_ws, gm_bs, gm_norm, sc_conv, sc_norm, mla_q_norm, mla_kv_norm, mla_wuq, mla_wukv, mla_norm, w_o, ln1_g, ln1_b, router_g_w, router_g_b, router_e_w, router_e_b, exp_w_gate, exp_w_up, exp_w_down, ln2_g, ln2_b):
    bsz, s, _ = x.shape
    assert bsz == 1
    xc = x.reshape(s, D_MODEL)
    cos_t, sin_t = _rope_tables(s)
    row = lambda v: v.reshape(1, -1)
    wg_all = exp_w_gate.reshape(DEPTH * N_EXPERTS, D_MODEL, D_EXPERT)
    wu_all = exp_w_up.reshape(DEPTH * N_EXPERTS, D_MODEL, D_EXPERT)
    wd_all = exp_w_down.reshape(DEPTH * N_EXPERTS, D_EXPERT, D_MODEL)
    for l in range(DEPTH):
        p = _mm_in(xc, _pack_w_in(w_in[l]))

        wa2p = jnp.concatenate(
            [gla_wa2[l], jnp.zeros((LANE - GLA_GATE_RANK, GLA_HEADS * GLA_DK), F32)], axis=0).astype(BF16)
        out_a = _gla(p, wa2p, row(gla_ba[l]), row(gla_norm[l]))

        gm_bias = jnp.repeat(gm_bs[l].T, GM_CH, axis=1)
        out_b = _gmlp(p, row(gm_ln_g[l]), row(gm_ln_b[l]), gm_ws[l], gm_bias, row(gm_norm[l]))

        out_c = _sconv(p, sc_conv[l], row(sc_norm[l]))

        wq, wqs, wk, wv = _pack_mla_weights(mla_wuq[l], mla_wukv[l])
        q_att, k_att, v_att = _mla_proj(p, row(mla_q_norm[l]), row(mla_kv_norm[l]), wq, wqs, wk, wv, cos_t, sin_t)
        out_d = _flash(q_att, k_att, v_att, row(mla_norm[l]))

        y = _mm_out(out_a, out_b, out_c, out_d, w_o[l].astype(BF16), xc)

        wr = jnp.concatenate([router_g_w[l], router_e_w[l],
                              jnp.zeros((D_MODEL, LANE - MOE_GROUPS - N_EXPERTS), F32)], axis=1)
        wr_hi = wr.astype(BF16)
        wr_lo = (wr - wr_hi.astype(F32)).astype(BF16)
        rb = jnp.concatenate([router_g_b[l], router_e_b[l], jnp.zeros((LANE - MOE_GROUPS - N_EXPERTS,), F32)])
        x1, x1_tok, route = _ln_route(y, row(ln1_g[l]), row(ln1_b[l]), wr_hi, wr_lo, row(rb))

        src_token, dst_row, w_slot, tile_expert, next_expert, tile_valid, n_tiles = _dispatch_plan(route, s)
        next_expert = jnp.where(next_expert < N_EXPERTS, next_expert + l * N_EXPERTS, -1)
        y_tok = _moe(tile_expert + l * N_EXPERTS, next_expert, tile_valid, n_tiles, src_token, dst_row, x1_tok,
                     wg_all, wu_all, wd_all, w_slot)
        xc = _ln_add(x1, y_tok, row(ln2_g[l]), row(ln2_b[l]))
    return xc.reshape(bsz, s, D_MODEL)
```

```python
import functools

import jax
import jax.numpy as jnp
from jax import lax
from jax.experimental import pallas as pl
from jax.experimental.pallas import tpu as pltpu

F32 = jnp.float32
BF16 = jnp.bfloat16

D_MODEL = 4096
DEPTH = 2
GROUP_W = 1024

GLA_HEADS = 4
GLA_DK = 128
GLA_DV = 256
GLA_GATE_RANK = 16
GLA_TAU = 16.0
GLA_CHUNK = 64
GLA_SUB = 16

GM_GROUPS = 8
GM_CH = 128
GM_CHUNK = 128

MLA_HEADS = 8
MLA_NOPE = 128
MLA_ROPE = 64
MLA_V = 128
MLA_Q_LORA = 768
MLA_KV_LORA = 256
ROPE_BASE = 10000.0
MLA_QK_PAD = 256

MOE_GROUPS = 4
MOE_PER_GROUP = 8
N_EXPERTS = 32
D_EXPERT = 512

ALPHA = (2.0 * DEPTH) ** 0.25
LOG2_E = 1.4426950408889634

LANE = 128
MIB = 1024 * 1024

COL_CQ = 0
COL_CKV = 768
COL_Q = 1024
COL_K = 1536
COL_V = 2048
COL_R = 3072
COL_U = 4096
COL_VG = 5120
COL_CB = 6144
COL_CC = 7168
COL_CH = 8192
COL_KR = 9216
COL_KRS = 9344
COL_GL = 9472
P_COLS = 9728

MOE_TILE = 256
SCATTER_UNROLL = 8
WEIGHT_ROUND_ROWS = 512
HALF_D = D_MODEL // 2
TOK_WORDS = HALF_D // LANE
U32 = jnp.uint32
HIGH_HALF = 0xFFFF0000


def _pack_rows(lo_f32, hi_f32):
    return (lax.bitcast_convert_type(lo_f32, U32) >> 16) | lax.bitcast_convert_type(hi_f32, U32)


def _unpack_rows(words):
    return (lax.bitcast_convert_type(words << 16, F32),
            lax.bitcast_convert_type(words & jnp.uint32(HIGH_HALF), F32))


def _cparams(sem, vmem_mib):
    return pltpu.CompilerParams(dimension_semantics=sem, vmem_limit_bytes=vmem_mib * MIB)


def _dot(a, b):
    return jnp.dot(a, b, preferred_element_type=F32)


def _dot_nt(a, b):
    return lax.dot_general(a, b, (((1,), (1,)), ((), ())), preferred_element_type=F32)


def _dot_tn(a, b):
    return lax.dot_general(a, b, (((0,), (0,)), ((), ())), preferred_element_type=F32)


def _mm_in_kernel(x_ref, w_ref, o_ref, xb_ref):
    @pl.when(pl.program_id(1) == 0)
    def _():
        xb_ref[...] = x_ref[...].astype(BF16)

    o_ref[...] = _dot(xb_ref[...], w_ref[...]).astype(o_ref.dtype)


def _mm_in(x, w):
    s, k = x.shape
    n = w.shape[1]
    tm = min(1024, s)
    tn = 512
    return pl.pallas_call(
        _mm_in_kernel,
        grid=(s // tm, n // tn),
        in_specs=[pl.BlockSpec((tm, k), lambda i, j: (i, 0)),
                  pl.BlockSpec((k, tn), lambda i, j: (0, j))],
        out_specs=pl.BlockSpec((tm, tn), lambda i, j: (i, j)),
        out_shape=jax.ShapeDtypeStruct((s, n), BF16),
        scratch_shapes=[pltpu.VMEM((tm, k), BF16)],
        compiler_params=_cparams(("parallel", "arbitrary"), 58),
        name="mm_in",
    )(x, w)


def _gla_kernel(q_ref, k_ref, v_ref, r_ref, gl_ref, wa2_ref, ba_ref, ng_ref, o_ref, s_ref, *, n_chunks):
    c_len = GLA_CHUNK

    @pl.when(pl.program_id(0) == 0)
    def _():
        s_ref[...] = jnp.zeros_like(s_ref)

    row = lax.broadcasted_iota(jnp.int32, (c_len, c_len), 0)
    col = lax.broadcasted_iota(jnp.int32, (c_len, c_len), 1)
    tril = jnp.where(col <= row, 1.0, 0.0).astype(BF16)
    sub_row = lax.broadcasted_iota(jnp.int32, (GLA_SUB, c_len), 0)
    sub_col = lax.broadcasted_iota(jnp.int32, (GLA_SUB, c_len), 1)
    n_sub = c_len // GLA_SUB

    def chunk(c, carry):
        rows = pl.ds(pl.multiple_of(c * c_len, c_len), c_len)
        logit = _dot(gl_ref[rows, :], wa2_ref[...]) + ba_ref[...]
        g = (jnp.minimum(logit, 0.0) - jnp.log(1.0 + jnp.exp(-jnp.abs(logit)))) * (1.0 / GLA_TAU)
        g_hi = g.astype(BF16)
        g_lo = (g - g_hi.astype(F32)).astype(BF16)
        b_all = _dot(tril, g_hi) + _dot(tril, g_lo)

        for h in range(GLA_HEADS):
            hs = slice(GLA_DK * h, GLA_DK * (h + 1))
            vs = slice(GLA_DV * h, GLA_DV * (h + 1))
            bh = b_all[:, hs]
            qh = q_ref[rows, hs].astype(F32) * (GLA_DK ** -0.5)
            kh = k_ref[rows, hs].astype(F32)
            vh = v_ref[rows, vs]
            state = s_ref[h]

            o = _dot((qh * jnp.exp(bh)).astype(BF16), state.astype(BF16))

            att_rows = []
            for blk in range(n_sub):
                sl = slice(GLA_SUB * blk, GLA_SUB * (blk + 1))
                b_blk = bh[sl]
                q_blk = qh[sl]
                att = jnp.zeros((GLA_SUB, c_len), F32)
                if blk > 0:
                    ref = bh[GLA_SUB * blk:GLA_SUB * blk + 1, :]
                    qs = (q_blk * jnp.exp(b_blk - ref)).astype(BF16)
                    ks = (kh * jnp.exp(jnp.minimum(ref - bh, 0.0))).astype(BF16)
                    att = jnp.where(sub_col < GLA_SUB * blk, _dot_nt(qs, ks), 0.0)
                for jj in range(GLA_SUB):
                    j = GLA_SUB * blk + jj
                    t = q_blk * kh[j:j + 1, :] * jnp.exp(b_blk - bh[j:j + 1, :])
                    rs = jnp.sum(t, axis=-1, keepdims=True)
                    att = jnp.where((sub_col == j) & (sub_row >= jj), rs, att)
                att_rows.append(att)
            att_full = jnp.concatenate(att_rows, axis=0).astype(BF16)
            o = o + _dot(att_full, vh)

            b_last = bh[c_len - 1:c_len, :]
            kd = (kh * jnp.exp(b_last - bh)).astype(BF16)
            decay_col = jnp.transpose(jnp.broadcast_to(jnp.exp(b_last), (GLA_DK, GLA_DK)))
            s_ref[h] = state * jnp.concatenate([decay_col, decay_col], axis=1) + _dot_tn(kd, vh)

            var = jnp.mean(o * o, axis=-1, keepdims=True)
            on = o * lax.rsqrt(var + 1e-6) * ng_ref[:, vs]
            rr = r_ref[rows, vs].astype(F32)
            o_ref[rows, vs] = (on * (rr / (1.0 + jnp.exp(-rr)))).astype(o_ref.dtype)
        return carry

    lax.fori_loop(0, n_chunks, chunk, 0)


def _gla(p, wa2p, ba, ng):
    s = p.shape[0]
    t = min(256, s)
    col = lambda width, off: pl.BlockSpec((t, width), lambda i: (i, off // width))
    full = lambda shape: pl.BlockSpec(shape, lambda i: (0,) * len(shape))
    return pl.pallas_call(
        functools.partial(_gla_kernel, n_chunks=t // GLA_CHUNK),
        grid=(s // t,),
        in_specs=[col(512, COL_Q), col(512, COL_K), col(1024, COL_V), col(1024, COL_R), col(LANE, COL_GL),
                  full(wa2p.shape), full(ba.shape), full(ng.shape)],
        out_specs=pl.BlockSpec((t, GROUP_W), lambda i: (i, 0)),
        out_shape=jax.ShapeDtypeStruct((s, GROUP_W), BF16),
        scratch_shapes=[pltpu.VMEM((GLA_HEADS, GLA_DK, GLA_DV), F32)],
        compiler_params=_cparams(("arbitrary",), 32),
        name="gla",
    )(p, p, p, p, p, wa2p, ba, ng)


def _gelu(x):
    return 0.5 * x * (1.0 + lax.erf(x * 0.7071067811865476))


def _gmlp_kernel(u_ref, v_ref, lg_ref, lb_ref, ws_ref, bias_ref, ng_ref, o_ref, buf_ref, *, n_chunks):
    row = lax.broadcasted_iota(jnp.int32, (GM_CHUNK, GM_CHUNK), 0)
    col = lax.broadcasted_iota(jnp.int32, (GM_CHUNK, GM_CHUNK), 1)
    causal = col <= row
    v = _gelu(v_ref[...].astype(F32))
    mu = jnp.mean(v, axis=-1, keepdims=True)
    d = v - mu
    var = jnp.mean(d * d, axis=-1, keepdims=True)
    vb = (d * lax.rsqrt(var + 1e-5) * lg_ref[...] + lb_ref[...]).astype(BF16)
    for g in range(GM_GROUPS):
        cs = slice(GM_CH * g, GM_CH * (g + 1))
        w = jnp.where(causal, ws_ref[g], 0.0).astype(BF16)
        for c in range(n_chunks):
            rs = slice(GM_CHUNK * c, GM_CHUNK * (c + 1))
            mixed = _dot(w, vb[rs, cs]) + bias_ref[:, cs]
            buf_ref[rs, cs] = _gelu(u_ref[rs, cs].astype(F32)) * mixed
    out = buf_ref[...]
    ms = jnp.mean(out * out, axis=-1, keepdims=True)
    o_ref[...] = (out * lax.rsqrt(ms + 1e-6) * ng_ref[...]).astype(o_ref.dtype)


def _gmlp(p, lg, lb, ws, bias, ng):
    s = p.shape[0]
    t = min(256, s)
    col = lambda off: pl.BlockSpec((t, GROUP_W), lambda i: (i, off // GROUP_W))
    full = lambda shape: pl.BlockSpec(shape, lambda i: (0,) * len(shape))
    return pl.pallas_call(
        functools.partial(_gmlp_kernel, n_chunks=t // GM_CHUNK),
        grid=(s // t,),
        in_specs=[col(COL_U), col(COL_VG), full(lg.shape), full(lb.shape), full(ws.shape), full(bias.shape),
                  full(ng.shape)],
        out_specs=pl.BlockSpec((t, GROUP_W), lambda i: (i, 0)),
        out_shape=jax.ShapeDtypeStruct((s, GROUP_W), BF16),
        scratch_shapes=[pltpu.VMEM((t, GROUP_W), F32)],
        compiler_params=_cparams(("parallel",), 32),
        name="gmlp",
    )(p, p, lg, lb, ws, bias, ng)


CONV_HALO = 8


def _sconv_kernel(b_ref, c_ref, h_ref, w_ref, ng_ref, o_ref, z_ref):
    t = b_ref.shape[0]

    @pl.when(pl.program_id(0) == 0)
    def _():
        z_ref[0:CONV_HALO, :] = jnp.zeros((CONV_HALO, GROUP_W), F32)

    z = c_ref[...].astype(F32) * h_ref[...].astype(F32)
    z_ref[CONV_HALO:CONV_HALO + t, :] = z
    z1 = z_ref[CONV_HALO - 1:CONV_HALO - 1 + t, :]
    z2 = z_ref[CONV_HALO - 2:CONV_HALO - 2 + t, :]
    y = w_ref[0:1, :] * z2 + w_ref[1:2, :] * z1 + w_ref[2:3, :] * z
    z_ref[0:CONV_HALO, :] = z[t - CONV_HALO:t, :]
    out = b_ref[...].astype(F32) * y
    ms = jnp.mean(out * out, axis=-1, keepdims=True)
    o_ref[...] = (out * lax.rsqrt(ms + 1e-6) * ng_ref[...]).astype(o_ref.dtype)


def _sconv(p, w, ng):
    s = p.shape[0]
    t = min(256, s)
    col = lambda off: pl.BlockSpec((t, GROUP_W), lambda i: (i, off // GROUP_W))
    full = lambda shape: pl.BlockSpec(shape, lambda i: (0,) * len(shape))
    return pl.pallas_call(
        _sconv_kernel,
        grid=(s // t,),
        in_specs=[col(COL_CB), col(COL_CC), col(COL_CH), full(w.shape), full(ng.shape)],
        out_specs=pl.BlockSpec((t, GROUP_W), lambda i: (i, 0)),
        out_shape=jax.ShapeDtypeStruct((s, GROUP_W), BF16),
        scratch_shapes=[pltpu.VMEM((t + CONV_HALO, GROUP_W), F32)],
        compiler_params=_cparams(("arbitrary",), 32),
        name="sconv",
    )(p, p, p, w, ng)


def _mla_proj_kernel(cq_ref, ckv_ref, kr_ref, krs_ref, qn_ref, kvn_ref, wq_ref, wqs_ref, wk_ref, wv_ref,
                     cos_ref, sin_ref, q_out, k_out, v_out):
    def rms(ref, g_ref):
        t = ref[...].astype(F32)
        return (t * lax.rsqrt(jnp.mean(t * t, axis=-1, keepdims=True) + 1e-6) * g_ref[...]).astype(BF16)

    cqn = rms(cq_ref, qn_ref)
    ckvn = rms(ckv_ref, kvn_ref)
    qm = _dot(cqn, wq_ref[...])
    qsw = _dot(cqn, wqs_ref[...])
    kn = _dot(ckvn, wk_ref[...])
    vv = _dot(ckvn, wv_ref[...])
    cos = cos_ref[...]
    sin = sin_ref[...]
    scale = (MLA_NOPE + MLA_ROPE) ** -0.5 * LOG2_E
    kr_rot = (kr_ref[...].astype(F32) * cos + krs_ref[...].astype(F32) * sin).astype(k_out.dtype)
    ones_col = jnp.where(lax.broadcasted_iota(jnp.int32, (cos.shape[0], LANE), 1) == 0, 1.0, 0.0).astype(v_out.dtype)
    for h in range(MLA_HEADS):
        lo = MLA_QK_PAD * h
        v_out[:, lo:lo + LANE] = vv[:, LANE * h:LANE * (h + 1)].astype(v_out.dtype)
        v_out[:, lo + LANE:lo + 2 * LANE] = ones_col
        q_out[:, lo:lo + LANE] = (qm[:, lo:lo + LANE] * scale).astype(q_out.dtype)
        q_out[:, lo + LANE:lo + 2 * LANE] = (
            (qm[:, lo + LANE:lo + 2 * LANE] * cos + qsw[:, LANE * h:LANE * (h + 1)] * sin) * scale
        ).astype(q_out.dtype)
        k_out[:, lo:lo + LANE] = kn[:, LANE * h:LANE * (h + 1)].astype(k_out.dtype)
        k_out[:, lo + LANE:lo + 2 * LANE] = kr_rot


def _mla_proj(p, qn, kvn, wq, wqs, wk, wv, cos, sin):
    s = p.shape[0]
    t = min(256, s)
    full = lambda shape: pl.BlockSpec(shape, lambda i: (0,) * len(shape))
    rowb = lambda width: pl.BlockSpec((t, width), lambda i: (i, 0))
    qk_w = MLA_HEADS * MLA_QK_PAD
    return pl.pallas_call(
        _mla_proj_kernel,
        grid=(s // t,),
        in_specs=[pl.BlockSpec((t, MLA_Q_LORA), lambda i: (i, COL_CQ // MLA_Q_LORA)),
                  pl.BlockSpec((t, MLA_KV_LORA), lambda i: (i, COL_CKV // MLA_KV_LORA)),
                  pl.BlockSpec((t, LANE), lambda i: (i, COL_KR // LANE)),
                  pl.BlockSpec((t, LANE), lambda i: (i, COL_KRS // LANE)),
                  full(qn.shape), full(kvn.shape), full(wq.shape), full(wqs.shape), full(wk.shape), full(wv.shape),
                  rowb(LANE), rowb(LANE)],
        out_specs=[rowb(qk_w), rowb(qk_w), rowb(qk_w)],
        out_shape=[jax.ShapeDtypeStruct((s, qk_w), BF16)] * 3,
        compiler_params=_cparams(("parallel",), 40),
        name="mla_proj",
    )(p, p, p, p, qn, kvn, wq, wqs, wk, wv, cos, sin)


FLASH_TQ = 512
FLASH_TK = 1024


def _flash_kernel(qi_ref, ki_ref, q_ref, k_ref, v_ref, ng_ref, o_ref, acc_ref, m_ref, *, tq, tk):
    step = pl.program_id(0)
    qi = qi_ref[step]
    ki = ki_ref[step]
    last_ki = (qi * tq) // tk

    @pl.when(ki == 0)
    def _():
        m_ref[...] = jnp.full(m_ref.shape, -jnp.inf, F32)
        acc_ref[...] = jnp.zeros_like(acc_ref)

    def accumulate(masked, k0, kn):
        keys = slice(k0, k0 + kn)
        if masked:
            row = qi * tq + lax.broadcasted_iota(jnp.int32, (tq, kn), 0)
            col = ki * tk + k0 + lax.broadcasted_iota(jnp.int32, (tq, kn), 1)
            visible = col <= row

        def scores(h):
            hs = slice(MLA_QK_PAD * h, MLA_QK_PAD * (h + 1))
            return _dot_nt(q_ref[:, hs], k_ref[keys, hs])

        sc = scores(0)
        for h in range(MLA_HEADS):
            hs = slice(MLA_QK_PAD * h, MLA_QK_PAD * (h + 1))
            sc_next = scores(h + 1) if h + 1 < MLA_HEADS else None
            if masked:
                sc = jnp.where(visible, sc, -jnp.inf)
            m_old = m_ref[h]
            m_new = jnp.maximum(m_old, jnp.max(sc, axis=-1, keepdims=True))
            pr = jnp.exp2(sc - m_new).astype(BF16)
            acc_ref[:, hs] = jnp.exp2(m_old - m_new) * acc_ref[:, hs] + _dot(pr, v_ref[keys, hs])
            m_ref[h] = m_new
            sc = sc_next

    @pl.when(ki < last_ki)
    def _():
        accumulate(False, 0, tk)

    @pl.when(ki == last_ki)
    def _():
        half = tk // 2
        accumulate(True, 0, half)

        @pl.when(qi * tq + tq - 1 >= ki * tk + half)
        def _():
            accumulate(True, half, half)

        outs = []
        for h in range(MLA_HEADS):
            lo = MLA_QK_PAD * h
            outs.append(acc_ref[:, lo:lo + MLA_V] / acc_ref[:, lo + MLA_V:lo + MLA_V + 1])
        o = jnp.concatenate(outs, axis=1)
        ms = jnp.mean(o * o, axis=-1, keepdims=True)
        o_ref[...] = (o * lax.rsqrt(ms + 1e-6) * ng_ref[...]).astype(o_ref.dtype)


def _flash(q, k, v, ng):
    s = q.shape[0]
    tk = min(FLASH_TK, s)
    tq = min(FLASH_TQ, tk)
    nq = s // tq
    pairs = [(a, b) for a in range(nq) for b in range((a * tq) // tk + 1)]
    qi_tab = jnp.asarray([a for a, _ in pairs], jnp.int32)
    ki_tab = jnp.asarray([b for _, b in pairs], jnp.int32)
    qk_w = MLA_HEADS * MLA_QK_PAD
    grid_spec = pltpu.PrefetchScalarGridSpec(
        num_scalar_prefetch=2,
        grid=(len(pairs),),
        in_specs=[pl.BlockSpec((tq, qk_w), lambda t, qi, ki: (qi[t], 0)),
                  pl.BlockSpec((tk, qk_w), lambda t, qi, ki: (ki[t], 0)),
                  pl.BlockSpec((tk, qk_w), lambda t, qi, ki: (ki[t], 0)),
                  pl.BlockSpec((1, GROUP_W), lambda t, qi, ki: (0, 0))],
        out_specs=pl.BlockSpec((tq, GROUP_W), lambda t, qi, ki: (qi[t], 0)),
        scratch_shapes=[pltpu.VMEM((tq, qk_w), F32),
                        pltpu.VMEM((MLA_HEADS, tq, 1), F32)],
    )
    return pl.pallas_call(
        functools.partial(_flash_kernel, tq=tq, tk=tk),
        grid_spec=grid_spec,
        out_shape=jax.ShapeDtypeStruct((s, GROUP_W), BF16),
        compiler_params=_cparams(("arbitrary",), 56),
        name="flash",
    )(qi_tab, ki_tab, q, k, v, ng)


def _mm_out_kernel(a0_ref, a1_ref, a2_ref, a3_ref, w_ref, x_ref, o_ref):
    acc = _dot(a0_ref[...], w_ref[0, 0:GROUP_W, :])
    acc = acc + _dot(a1_ref[...], w_ref[0, GROUP_W:2 * GROUP_W, :])
    acc = acc + _dot(a2_ref[...], w_ref[0, 2 * GROUP_W:3 * GROUP_W, :])
    acc = acc + _dot(a3_ref[...], w_ref[0, 3 * GROUP_W:4 * GROUP_W, :])
    o_ref[...] = ALPHA * x_ref[...] + acc


def _mm_out(a0, a1, a2, a3, w, layer, x):
    s = x.shape[0]
    tm = min(512, s)
    tn = 1024
    a_spec = pl.BlockSpec((tm, GROUP_W), lambda i, j: (i, 0))
    return pl.pallas_call(
        _mm_out_kernel,
        grid=(s // tm, D_MODEL // tn),
        in_specs=[a_spec, a_spec, a_spec, a_spec,
                  pl.BlockSpec((1, D_MODEL, tn), lambda i, j: (layer, 0, j)),
                  pl.BlockSpec((tm, tn), lambda i, j: (i, j))],
        out_specs=pl.BlockSpec((tm, tn), lambda i, j: (i, j)),
        out_shape=jax.ShapeDtypeStruct((s, D_MODEL), F32),
        compiler_params=_cparams(("parallel", "arbitrary"), 48),
        name="mm_out",
    )(a0, a1, a2, a3, w, x)


def _layer_norm(t, g, b):
    mu = jnp.mean(t, axis=-1, keepdims=True)
    d = t - mu
    var = jnp.mean(d * d, axis=-1, keepdims=True)
    return d * lax.rsqrt(var + 1e-5) * g + b


def _ln_route_kernel(y_ref, g_ref, b_ref, wh_ref, wl_ref, rb_ref, x_out, route_out, xtok_hbm, tbuf, sem, *, tm):
    i = pl.program_id(0)
    slot = lax.rem(i, 2)

    def wait_rows(buf_slot):
        pltpu.make_async_copy(tbuf.at[buf_slot], tbuf.at[buf_slot], sem.at[buf_slot]).wait()

    x1 = _layer_norm(y_ref[...], g_ref[...], b_ref[...])
    x_out[...] = x1
    x_hi = x1.astype(BF16)
    x_hi32 = x_hi.astype(F32)

    @pl.when(i >= 2)
    def _():
        wait_rows(slot)

    words = _pack_rows(x_hi32[:, :HALF_D], x_hi32[:, HALF_D:])
    for c in range(TOK_WORDS):
        tbuf[slot, c] = words[:, LANE * c:LANE * (c + 1)]
    for j in range(tm):
        pltpu.make_async_copy(tbuf.at[slot, :, pl.ds(j, 1), :], xtok_hbm.at[i * tm + j], sem.at[slot]).start()

    @pl.when(i == pl.num_programs(0) - 1)
    def _():
        wait_rows(slot)

        @pl.when(i >= 1)
        def _():
            wait_rows(1 - slot)

    x_lo = (x1 - x_hi32).astype(BF16)
    logits = _dot(x_hi, wh_ref[...]) + _dot(x_lo, wh_ref[...]) + _dot(x_hi, wl_ref[...]) + rb_ref[...]
    lane = lax.broadcasted_iota(jnp.int32, logits.shape, 1).astype(F32)
    neg = -jnp.inf
    big = 1e9
    is_grp = lane < MOE_GROUPS
    lg = jnp.where(is_grp, logits, neg)
    mg = jnp.max(lg, axis=-1, keepdims=True)
    gsel = jnp.min(jnp.where(lg == mg, lane, big), axis=-1, keepdims=True)
    pg_sel = 1.0 / jnp.sum(jnp.where(is_grp, jnp.exp(lg - mg), 0.0), axis=-1, keepdims=True)
    lo = MOE_GROUPS + MOE_PER_GROUP * gsel
    le = jnp.where((lane >= lo) & (lane < lo + MOE_PER_GROUP), logits, neg)
    v1 = jnp.max(le, axis=-1, keepdims=True)
    i1 = jnp.min(jnp.where(le == v1, lane, big), axis=-1, keepdims=True)
    le2 = jnp.where(lane == i1, neg, le)
    v2 = jnp.max(le2, axis=-1, keepdims=True)
    i2 = jnp.min(jnp.where(le2 == v2, lane, big), axis=-1, keepdims=True)
    e = jnp.exp(v2 - v1)
    w1 = pg_sel / (1.0 + e)
    w2 = pg_sel * e / (1.0 + e)
    route = jnp.where(lane == 0.0, i1 - MOE_GROUPS,
                      jnp.where(lane == 1.0, i2 - MOE_GROUPS,
                                jnp.where(lane == 2.0, w1, jnp.where(lane == 3.0, w2, 0.0))))
    route_out[...] = route


def _ln_route(y, g, b, wh, wl, rb):
    s = y.shape[0]
    tm = min(256, s)
    full = lambda shape: pl.BlockSpec(shape, lambda i: (0,) * len(shape))
    rowb = lambda width: pl.BlockSpec((tm, width), lambda i: (i, 0))
    x1, route, x1_tok = pl.pallas_call(
        functools.partial(_ln_route_kernel, tm=tm),
        grid=(s // tm,),
        in_specs=[rowb(D_MODEL), full(g.shape), full(b.shape), full(wh.shape), full(wl.shape), full(rb.shape)],
        out_specs=[rowb(D_MODEL), rowb(LANE), pl.BlockSpec(memory_space=pl.ANY)],
        out_shape=[jax.ShapeDtypeStruct((s, D_MODEL), F32),
                   jax.ShapeDtypeStruct((s, LANE), F32),
                   jax.ShapeDtypeStruct((s, TOK_WORDS, 1, LANE), U32)],
        scratch_shapes=[pltpu.VMEM((2, TOK_WORDS, tm, LANE), U32), pltpu.SemaphoreType.DMA((2,))],
        compiler_params=_cparams(("arbitrary",), 48),
        name="ln_route",
    )(y, g, b, wh, wl, rb)
    return x1, x1_tok, route


def _moe_kernel(te_ref, ne_ref, tv_ref, nt_ref, src_ref, nxt_ref, dst_ref, x_hbm, wg_hbm, wu_hbm, wd_hbm, ws_ref,
                y_hbm, gbuf, hb, obuf, stg_g, stg_u, stg_d, wg_ref, wu_ref, wd_ref, gsem, ssem, wsem):
    t = pl.program_id(0)
    n_used = nt_ref[0]
    slot = lax.rem(t, 2)

    def weight_copies(e):
        return (pltpu.make_async_copy(wg_hbm.at[e], stg_g, wsem.at[0]),
                pltpu.make_async_copy(wu_hbm.at[e], stg_u, wsem.at[1]),
                pltpu.make_async_copy(wd_hbm.at[e], stg_d, wsem.at[2]))

    def round_weights(stg, dst, rows):
        def body(r, carry):
            rs = pl.ds(pl.multiple_of(r * rows, rows), rows)
            dst[rs, :] = stg[rs, :].astype(BF16)
            return carry
        lax.fori_loop(0, stg.shape[0] // rows, body, 0)

    @pl.when(t == 0)
    def _():
        for cp in weight_copies(te_ref[0]):
            cp.start()

    @pl.when(t < n_used)
    def _():
        expert = te_ref[t]
        first_tile_of_expert = jnp.logical_or(t == 0, expert != te_ref[jnp.maximum(t - 1, 0)])

        @pl.when(first_tile_of_expert)
        def _():
            for cp in weight_copies(expert):
                cp.wait()
            round_weights(stg_g, wg_ref, WEIGHT_ROUND_ROWS)
            round_weights(stg_u, wu_ref, WEIGHT_ROUND_ROWS)
            round_weights(stg_d, wd_ref, WEIGHT_ROUND_ROWS * D_EXPERT // D_MODEL)

            @pl.when(ne_ref[t] >= 0)
            def _():
                for cp in weight_copies(ne_ref[t]):
                    cp.start()

    def gather_row(idx_ref, buf_slot, i):
        return pltpu.make_async_copy(x_hbm.at[idx_ref[i]], gbuf.at[buf_slot, :, pl.ds(i, 1), :], gsem.at[buf_slot])

    def wait_gather(buf_slot):
        pltpu.make_async_copy(gbuf.at[buf_slot], gbuf.at[buf_slot], gsem.at[buf_slot]).wait()

    def scatter_row(i):
        return pltpu.make_async_copy(obuf.at[:, pl.ds(i, 1), :], y_hbm.at[dst_ref[i]], ssem)

    def start_scatter(n_rows):
        def group(g, carry):
            for u in range(SCATTER_UNROLL):
                scatter_row(g * SCATTER_UNROLL + u).start()
            return carry

        def single(i, carry):
            scatter_row(i).start()
            return carry
        n_groups = n_rows // SCATTER_UNROLL
        lax.fori_loop(0, n_groups, group, 0)
        lax.fori_loop(n_groups * SCATTER_UNROLL, n_rows, single, 0)

    def wait_scatter(n_rows):
        rows = obuf.at[:, pl.ds(0, n_rows), :]
        pltpu.make_async_copy(rows, rows, ssem).wait()

    @pl.when(t == 0)
    def _():
        def body(i, carry):
            gather_row(src_ref, 0, i).start()
            return carry
        lax.fori_loop(0, MOE_TILE, body, 0, unroll=8)

    @pl.when(t < n_used)
    def _():
        wait_gather(slot)
        for c in range(TOK_WORDS):
            lo, hi = _unpack_rows(gbuf[slot, c])
            hb[:, LANE * c:LANE * (c + 1)] = lo.astype(BF16)
            hb[:, HALF_D + LANE * c:HALF_D + LANE * (c + 1)] = hi.astype(BF16)
        for i in range(MOE_TILE):
            gather_row(nxt_ref, 1 - slot, i).start()
        h = hb[...]
        gate = _dot(h, wg_ref[...])
        up = _dot(h, wu_ref[...])
        a = (gate / (1.0 + jnp.exp(-gate)) * up).astype(BF16)

        @pl.when(t > 0)
        def _():
            wait_scatter(tv_ref[jnp.maximum(t - 1, 0)])

        ws = ws_ref[...]
        rounded = lambda v: v.astype(BF16).astype(F32)
        for c2 in range(TOK_WORDS // 2):
            cols = slice(2 * LANE * c2, 2 * LANE * (c2 + 1))
            lo = _dot(a, wd_ref[:, cols]) * ws
            hi = _dot(a, wd_ref[:, HALF_D + cols.start:HALF_D + cols.stop]) * ws
            words = _pack_rows(rounded(lo), rounded(hi))
            obuf[2 * c2] = words[:, :LANE]
            obuf[2 * c2 + 1] = words[:, LANE:]
        start_scatter(tv_ref[t])

        @pl.when(t == n_used - 1)
        def _():
            wait_gather(1 - slot)
            wait_scatter(tv_ref[t])


def _moe(tile_expert, next_expert, tile_valid, n_tiles, src_token, dst_row, x_tok, wg, wu, wd, w_slot):
    p_rows = src_token.shape[0]
    tm = MOE_TILE
    n_blocks = p_rows // tm
    smem_blk = lambda fn: pl.BlockSpec((tm,), fn, memory_space=pltpu.SMEM)
    hbm = pl.BlockSpec(memory_space=pl.ANY)
    grid_spec = pltpu.PrefetchScalarGridSpec(
        num_scalar_prefetch=4,
        grid=(n_blocks,),
        in_specs=[smem_blk(lambda t, te, ne, tv, nt: (t,)),
                  smem_blk(lambda t, te, ne, tv, nt: (jnp.minimum(t + 1, n_blocks - 1),)),
                  smem_blk(lambda t, te, ne, tv, nt: (t,)),
                  hbm, hbm, hbm, hbm,
                  pl.BlockSpec((tm, 1), lambda t, te, ne, tv, nt: (t, 0))],
        out_specs=hbm,
        scratch_shapes=[pltpu.VMEM((2, TOK_WORDS, tm, LANE), U32),
                        pltpu.VMEM((tm, D_MODEL), BF16),
                        pltpu.VMEM((TOK_WORDS, tm, LANE), U32),
                        pltpu.VMEM((D_MODEL, D_EXPERT), F32),
                        pltpu.VMEM((D_MODEL, D_EXPERT), F32),
                        pltpu.VMEM((D_EXPERT, D_MODEL), F32),
                        pltpu.VMEM((D_MODEL, D_EXPERT), BF16),
                        pltpu.VMEM((D_MODEL, D_EXPERT), BF16),
                        pltpu.VMEM((D_EXPERT, D_MODEL), BF16),
                        pltpu.SemaphoreType.DMA((2,)),
                        pltpu.SemaphoreType.DMA(()),
                        pltpu.SemaphoreType.DMA((3,))],
    )
    return pl.pallas_call(
        _moe_kernel,
        grid_spec=grid_spec,
        out_shape=jax.ShapeDtypeStruct((2 * x_tok.shape[0], TOK_WORDS, 1, LANE), U32),
        compiler_params=_cparams(("arbitrary",), 58),
        name="moe",
    )(tile_expert, next_expert, tile_valid, n_tiles, src_token, src_token, dst_row, x_tok, wg, wu, wd, w_slot)


def _ln_add_kernel(x_ref, g_ref, b_ref, y_hbm, o_ref, ybuf, vbuf, sem, *, tm):
    i = pl.program_id(0)
    slot = lax.rem(i, 2)

    def start_gather(tile, buf_slot):
        base = tile * (2 * tm)

        def body(j, carry):
            for k in range(2):
                pltpu.make_async_copy(y_hbm.at[base + 2 * j + k], ybuf.at[buf_slot, k, :, pl.ds(j, 1), :],
                                      sem.at[buf_slot]).start()
            return carry
        lax.fori_loop(0, tm, body, 0, unroll=4)

    @pl.when(i == 0)
    def _():
        start_gather(0, 0)

    @pl.when(i + 1 < pl.num_programs(0))
    def _():
        start_gather(i + 1, 1 - slot)

    pltpu.make_async_copy(ybuf.at[slot], ybuf.at[slot], sem.at[slot]).wait()
    for c in range(TOK_WORDS):
        a_lo, a_hi = _unpack_rows(ybuf[slot, 0, c])
        b_lo, b_hi = _unpack_rows(ybuf[slot, 1, c])
        lo = slice(LANE * c, LANE * (c + 1))
        hi = slice(HALF_D + LANE * c, HALF_D + LANE * (c + 1))
        vbuf[:, lo] = ALPHA * x_ref[:, lo] + (a_lo + b_lo)
        vbuf[:, hi] = ALPHA * x_ref[:, hi] + (a_hi + b_hi)
    o_ref[...] = _layer_norm(vbuf[...], g_ref[...], b_ref[...])


def _ln_add(x, y_tok, g, b):
    s = x.shape[0]
    tm = min(256, s)
    full = lambda shape: pl.BlockSpec(shape, lambda i: (0,) * len(shape))
    return pl.pallas_call(
        functools.partial(_ln_add_kernel, tm=tm),
        grid=(s // tm,),
        in_specs=[pl.BlockSpec((tm, D_MODEL), lambda i: (i, 0)), full(g.shape), full(b.shape),
                  pl.BlockSpec(memory_space=pl.ANY)],
        out_specs=pl.BlockSpec((tm, D_MODEL), lambda i: (i, 0)),
        out_shape=jax.ShapeDtypeStruct((s, D_MODEL), F32),
        scratch_shapes=[pltpu.VMEM((2, 2, TOK_WORDS, tm, LANE), U32),
                        pltpu.VMEM((tm, D_MODEL), F32),
                        pltpu.SemaphoreType.DMA((2,))],
        compiler_params=_cparams(("arbitrary",), 48),
        name="ln_add",
    )(x, g, b, y_tok)


def _w_in_sources():
    src = {}
    o = 0
    for name, width in (("q", 512), ("k", 512), ("v", 1024), ("r", 1024), ("gl", GLA_GATE_RANK), ("u", 1024),
                        ("vg", 1024), ("cb", 1024), ("cc", 1024), ("ch", 1024), ("cq", MLA_Q_LORA),
                        ("ckv", MLA_KV_LORA), ("kr", MLA_ROPE)):
        src[name] = (o, width)
        o += width
    return src, o


def _pack_w_in_kernel(w3_ref, o_ref):
    src, _ = _w_in_sources()
    w_ref = w3_ref.at[0]
    rows = w_ref.shape[0]
    for name, dst in (("cq", COL_CQ), ("ckv", COL_CKV), ("q", COL_Q), ("k", COL_K), ("v", COL_V), ("r", COL_R),
                      ("u", COL_U), ("vg", COL_VG), ("cb", COL_CB), ("cc", COL_CC), ("ch", COL_CH)):
        s0, width = src[name]
        o_ref[:, dst:dst + width] = w_ref[:, s0:s0 + width].astype(BF16)
    zeros = lambda n: jnp.zeros((rows, n), BF16)
    half = MLA_ROPE // 2
    kr0, _ = src["kr"]
    kr = w_ref[:, kr0:kr0 + MLA_ROPE].astype(BF16)
    o_ref[:, COL_KR:COL_KR + LANE] = jnp.concatenate([kr, zeros(LANE - MLA_ROPE)], axis=1)
    o_ref[:, COL_KRS:COL_KRS + LANE] = jnp.concatenate([kr[:, half:], kr[:, :half], zeros(LANE - MLA_ROPE)], axis=1)
    gl0, _ = src["gl"]
    o_ref[:, COL_GL:COL_GL + LANE] = jnp.concatenate(
        [w_ref[:, gl0:gl0 + GLA_GATE_RANK].astype(BF16), zeros(LANE - GLA_GATE_RANK)], axis=1)
    o_ref[:, COL_GL + LANE:P_COLS] = zeros(P_COLS - COL_GL - LANE)


def _pack_w_in(w, layer):
    _, k, d_in = w.shape
    assert d_in == _w_in_sources()[1]
    tr = 256
    return pl.pallas_call(
        _pack_w_in_kernel,
        grid=(k // tr,),
        in_specs=[pl.BlockSpec((1, tr, d_in), lambda i: (layer, i, 0))],
        out_specs=pl.BlockSpec((tr, P_COLS), lambda i: (i, 0)),
        out_shape=jax.ShapeDtypeStruct((k, P_COLS), BF16),
        compiler_params=_cparams(("parallel",), 48),
        name="pack_w_in",
    )(w)


def _pack_mla_weights(wuq, wukv):
    half = MLA_ROPE // 2
    wq3 = wuq.reshape(MLA_Q_LORA, MLA_HEADS, MLA_NOPE + MLA_ROPE)
    rope = wq3[:, :, MLA_NOPE:]
    zq = jnp.zeros((MLA_Q_LORA, MLA_HEADS, MLA_QK_PAD - MLA_NOPE - MLA_ROPE), wuq.dtype)
    wq = jnp.concatenate([wq3, zq], axis=2).reshape(MLA_Q_LORA, MLA_HEADS * MLA_QK_PAD)
    zs = jnp.zeros((MLA_Q_LORA, MLA_HEADS, LANE - MLA_ROPE), wuq.dtype)
    wqs = jnp.concatenate([rope[:, :, half:], rope[:, :, :half], zs], axis=2).reshape(MLA_Q_LORA, MLA_HEADS * LANE)
    wkv3 = wukv.reshape(MLA_KV_LORA, MLA_HEADS, MLA_NOPE + MLA_V)
    wk = wkv3[:, :, :MLA_NOPE].reshape(MLA_KV_LORA, MLA_HEADS * MLA_NOPE)
    wv = wkv3[:, :, MLA_NOPE:].reshape(MLA_KV_LORA, MLA_HEADS * MLA_V)
    return wq.astype(BF16), wqs.astype(BF16), wk.astype(BF16), wv.astype(BF16)


def _rope_tables(s):
    pos = jnp.arange(s, dtype=F32)
    inv_freq = ROPE_BASE ** (-jnp.arange(0, MLA_ROPE, 2, dtype=F32) / MLA_ROPE)
    ang = pos[:, None] * inv_freq[None, :]
    cos, sin = jnp.cos(ang), jnp.sin(ang)
    z = jnp.zeros((s, LANE - MLA_ROPE), F32)
    return jnp.concatenate([cos, cos, z], axis=1), jnp.concatenate([-sin, sin, z], axis=1)


def _dispatch_plan(route, s):
    n_assign = 2 * s
    p_rows = n_assign + N_EXPERTS * MOE_TILE
    eid = route[:, 0:2].astype(jnp.int32).reshape(n_assign)
    wgt = route[:, 2:4].reshape(n_assign)
    onehot = (eid[:, None] == jnp.arange(N_EXPERTS, dtype=jnp.int32)[None, :]).astype(jnp.int32)
    csum = jnp.cumsum(onehot, axis=0)
    rank = jnp.sum(onehot * (csum - 1), axis=1)
    counts = csum[-1]
    tiles_e = (counts + MOE_TILE - 1) // MOE_TILE
    tile_end = jnp.cumsum(tiles_e)
    tile_start = tile_end - tiles_e
    slot = tile_start[eid] * MOE_TILE + rank
    slot_assign = jnp.full((p_rows,), -1, jnp.int32).at[slot].set(jnp.arange(n_assign, dtype=jnp.int32))
    is_pad = slot_assign < 0
    src_token = jnp.where(is_pad, 0, slot_assign // 2)
    dst_row = jnp.maximum(slot_assign, 0)
    w_slot = jnp.where(is_pad, 0.0, wgt[dst_row])
    n_blocks = p_rows // MOE_TILE
    tile_valid = jnp.sum(jnp.logical_not(is_pad).reshape(n_blocks, MOE_TILE).astype(jnp.int32), axis=1)
    tile_ids = jnp.arange(n_blocks, dtype=jnp.int32)
    tile_expert = jnp.minimum(jnp.sum((tile_end[None, :] <= tile_ids[:, None]).astype(jnp.int32), axis=1),
                              N_EXPERTS - 1)
    n_tiles = tile_end[-1:].astype(jnp.int32)
    experts = jnp.arange(N_EXPERTS, dtype=jnp.int32)
    later_with_tiles = (experts[None, :] > experts[:, None]) & (tiles_e[None, :] > 0)
    next_nonempty = jnp.min(jnp.where(later_with_tiles, experts[None, :], N_EXPERTS), axis=1)
    next_expert = next_nonempty[tile_expert]
    return src_token, dst_row, w_slot.reshape(p_rows, 1), tile_expert, next_expert, tile_valid, n_tiles


def kernel(x, w_in, gla_wa2, gla_ba, gla_norm, gm_ln_g, gm_ln_b, gm_ws, gm_bs, gm_norm, sc_conv, sc_norm, mla_q_norm, mla_kv_norm, mla_wuq, mla_wukv, mla_norm, w_o, ln1_g, ln1_b, router_g_w, router_g_b, router_e_w, router_e_b, exp_w_gate, exp_w_up, exp_w_down, ln2_g, ln2_b):
    bsz, s, _ = x.shape
    assert bsz == 1
    xc = x.reshape(s, D_MODEL)
    cos_t, sin_t = _rope_tables(s)
    row = lambda v: v.reshape(1, -1)
    wg_all = exp_w_gate.reshape(DEPTH * N_EXPERTS, D_MODEL, D_EXPERT)
    wu_all = exp_w_up.reshape(DEPTH * N_EXPERTS, D_MODEL, D_EXPERT)
    wd_all = exp_w_down.reshape(DEPTH * N_EXPERTS, D_EXPERT, D_MODEL)
    w_o_b = w_o.astype(BF16)
    for l in range(DEPTH):
        p = _mm_in(xc, _pack_w_in(w_in, l))

        wa2p = jnp.concatenate(
            [gla_wa2[l], jnp.zeros((LANE - GLA_GATE_RANK, GLA_HEADS * GLA_DK), F32)], axis=0).astype(BF16)
        out_a = _gla(p, wa2p, row(gla_ba[l]), row(gla_norm[l]))

        gm_bias = jnp.repeat(gm_bs[l].T, GM_CH, axis=1)
        out_b = _gmlp(p, row(gm_ln_g[l]), row(gm_ln_b[l]), gm_ws[l], gm_bias, row(gm_norm[l]))

        out_c = _sconv(p, sc_conv[l], row(sc_norm[l]))

        wq, wqs, wk, wv = _pack_mla_weights(mla_wuq[l], mla_wukv[l])
        q_att, k_att, v_att = _mla_proj(p, row(mla_q_norm[l]), row(mla_kv_norm[l]), wq, wqs, wk, wv, cos_t, sin_t)
        out_d = _flash(q_att, k_att, v_att, row(mla_norm[l]))

        y = _mm_out(out_a, out_b, out_c, out_d, w_o_b, l, xc)

        wr = jnp.concatenate([router_g_w[l], router_e_w[l],
                              jnp.zeros((D_MODEL, LANE - MOE_GROUPS - N_EXPERTS), F32)], axis=1)
        wr_hi = wr.astype(BF16)
        wr_lo = (wr - wr_hi.astype(F32)).astype(BF16)
        rb = jnp.concatenate([router_g_b[l], router_e_b[l], jnp.zeros((LANE - MOE_GROUPS - N_EXPERTS,), F32)])
        x1, x1_tok, route = _ln_route(y, row(ln1_g[l]), row(ln1_b[l]), wr_hi, wr_lo, row(rb))

        src_token, dst_row, w_slot, tile_expert, next_expert, tile_valid, n_tiles = _dispatch_plan(route, s)
        next_expert = jnp.where(next_expert < N_EXPERTS, next_expert + l * N_EXPERTS, -1)
        y_tok = _moe(tile_expert + l * N_EXPERTS, next_expert, tile_valid, n_tiles, src_token, dst_row, x1_tok,
                     wg_all, wu_all, wd_all, w_slot)
        xc = _ln_add(x1, y_tok, row(ln2_g[l]), row(ln2_b[l]))
    return xc.reshape(bsz, s, D_MODEL)
```

```python
import functools

import jax
import jax.numpy as jnp
from jax import lax
from jax.experimental import pallas as pl
from jax.experimental.pallas import tpu as pltpu

F32 = jnp.float32
BF16 = jnp.bfloat16

D_MODEL = 4096
DEPTH = 2
GROUP_W = 1024

GLA_HEADS = 4
GLA_DK = 128
GLA_DV = 256
GLA_GATE_RANK = 16
GLA_TAU = 16.0
GLA_CHUNK = 64
GLA_SUB = 16

GM_GROUPS = 8
GM_CH = 128
GM_CHUNK = 128

MLA_HEADS = 8
MLA_NOPE = 128
MLA_ROPE = 64
MLA_V = 128
MLA_Q_LORA = 768
MLA_KV_LORA = 256
ROPE_BASE = 10000.0
MLA_QK_PAD = 256

MOE_GROUPS = 4
MOE_PER_GROUP = 8
N_EXPERTS = 32
D_EXPERT = 512

ALPHA = (2.0 * DEPTH) ** 0.25
LOG2_E = 1.4426950408889634

LANE = 128
MIB = 1024 * 1024

COL_CQ = 0
COL_CKV = 768
COL_Q = 1024
COL_K = 1536
COL_V = 2048
COL_R = 3072
COL_U = 4096
COL_VG = 5120
COL_CB = 6144
COL_CC = 7168
COL_CH = 8192
COL_KR = 9216
COL_KRS = 9344
COL_GL = 9472
P_COLS = 9728

MOE_TILE = 256
SCATTER_UNROLL = 8
WEIGHT_ROUND_ROWS = 512
HALF_D = D_MODEL // 2
TOK_WORDS = HALF_D // LANE
U32 = jnp.uint32
HIGH_HALF = 0xFFFF0000


def _pack_rows(lo_f32, hi_f32):
    return (lax.bitcast_convert_type(lo_f32, U32) >> 16) | lax.bitcast_convert_type(hi_f32, U32)


def _unpack_rows(words):
    return (lax.bitcast_convert_type(words << 16, F32),
            lax.bitcast_convert_type(words & jnp.uint32(HIGH_HALF), F32))


def _cparams(sem, vmem_mib):
    return pltpu.CompilerParams(dimension_semantics=sem, vmem_limit_bytes=vmem_mib * MIB)


def _dot(a, b):
    return jnp.dot(a, b, preferred_element_type=F32)


def _dot_nt(a, b):
    return lax.dot_general(a, b, (((1,), (1,)), ((), ())), preferred_element_type=F32)


def _dot_tn(a, b):
    return lax.dot_general(a, b, (((0,), (0,)), ((), ())), preferred_element_type=F32)


def _mm_in_kernel(x_ref, w_ref, o_ref, xb_ref):
    @pl.when(pl.program_id(1) == 0)
    def _():
        xb_ref[...] = x_ref[...].astype(BF16)

    o_ref[...] = _dot(xb_ref[...], w_ref[...]).astype(o_ref.dtype)


def _mm_in(x, w):
    s, k = x.shape
    n = w.shape[1]
    tm = min(1024, s)
    tn = 512
    return pl.pallas_call(
        _mm_in_kernel,
        grid=(s // tm, n // tn),
        in_specs=[pl.BlockSpec((tm, k), lambda i, j: (i, 0)),
                  pl.BlockSpec((k, tn), lambda i, j: (0, j))],
        out_specs=pl.BlockSpec((tm, tn), lambda i, j: (i, j)),
        out_shape=jax.ShapeDtypeStruct((s, n), BF16),
        scratch_shapes=[pltpu.VMEM((tm, k), BF16)],
        compiler_params=_cparams(("parallel", "arbitrary"), 58),
        name="mm_in",
    )(x, w)


def _gla_kernel(q_ref, k_ref, v_ref, r_ref, gl_ref, wa2_ref, ba_ref, ng_ref, o_ref, s_ref, *, n_chunks):
    c_len = GLA_CHUNK

    @pl.when(pl.program_id(0) == 0)
    def _():
        s_ref[...] = jnp.zeros_like(s_ref)

    row = lax.broadcasted_iota(jnp.int32, (c_len, c_len), 0)
    col = lax.broadcasted_iota(jnp.int32, (c_len, c_len), 1)
    tril = jnp.where(col <= row, 1.0, 0.0).astype(BF16)
    sub_row = lax.broadcasted_iota(jnp.int32, (GLA_SUB, c_len), 0)
    sub_col = lax.broadcasted_iota(jnp.int32, (GLA_SUB, c_len), 1)
    n_sub = c_len // GLA_SUB

    def chunk(c, carry):
        rows = pl.ds(pl.multiple_of(c * c_len, c_len), c_len)
        logit = _dot(gl_ref[rows, :], wa2_ref[...]) + ba_ref[...]
        g = (jnp.minimum(logit, 0.0) - jnp.log(1.0 + jnp.exp(-jnp.abs(logit)))) * (1.0 / GLA_TAU)
        g_hi = g.astype(BF16)
        g_lo = (g - g_hi.astype(F32)).astype(BF16)
        b_all = _dot(tril, g_hi) + _dot(tril, g_lo)

        for h in range(GLA_HEADS):
            hs = slice(GLA_DK * h, GLA_DK * (h + 1))
            vs = slice(GLA_DV * h, GLA_DV * (h + 1))
            bh = b_all[:, hs]
            qh = q_ref[rows, hs].astype(F32) * (GLA_DK ** -0.5)
            kh = k_ref[rows, hs].astype(F32)
            vh = v_ref[rows, vs]
            state = s_ref[h]

            o = _dot((qh * jnp.exp(bh)).astype(BF16), state.astype(BF16))

            att_rows = []
            for blk in range(n_sub):
                sl = slice(GLA_SUB * blk, GLA_SUB * (blk + 1))
                b_blk = bh[sl]
                q_blk = qh[sl]
                att = jnp.zeros((GLA_SUB, c_len), F32)
                if blk > 0:
                    ref = bh[GLA_SUB * blk:GLA_SUB * blk + 1, :]
                    qs = (q_blk * jnp.exp(b_blk - ref)).astype(BF16)
                    ks = (kh * jnp.exp(jnp.minimum(ref - bh, 0.0))).astype(BF16)
                    att = jnp.where(sub_col < GLA_SUB * blk, _dot_nt(qs, ks), 0.0)
                for jj in range(GLA_SUB):
                    j = GLA_SUB * blk + jj
                    t = q_blk * kh[j:j + 1, :] * jnp.exp(b_blk - bh[j:j + 1, :])
                    rs = jnp.sum(t, axis=-1, keepdims=True)
                    att = jnp.where((sub_col == j) & (sub_row >= jj), rs, att)
                att_rows.append(att)
            att_full = jnp.concatenate(att_rows, axis=0).astype(BF16)
            o = o + _dot(att_full, vh)

            b_last = bh[c_len - 1:c_len, :]
            kd = (kh * jnp.exp(b_last - bh)).astype(BF16)
            decay_col = jnp.transpose(jnp.broadcast_to(jnp.exp(b_last), (GLA_DK, GLA_DK)))
            s_ref[h] = state * jnp.concatenate([decay_col, decay_col], axis=1) + _dot_tn(kd, vh)

            var = jnp.mean(o * o, axis=-1, keepdims=True)
            on = o * lax.rsqrt(var + 1e-6) * ng_ref[:, vs]
            rr = r_ref[rows, vs].astype(F32)
            o_ref[rows, vs] = (on * (rr / (1.0 + jnp.exp(-rr)))).astype(o_ref.dtype)
        return carry

    lax.fori_loop(0, n_chunks, chunk, 0)


def _gla(p, wa2p, ba, ng):
    s = p.shape[0]
    t = min(256, s)
    col = lambda width, off: pl.BlockSpec((t, width), lambda i: (i, off // width))
    full = lambda shape: pl.BlockSpec(shape, lambda i: (0,) * len(shape))
    return pl.pallas_call(
        functools.partial(_gla_kernel, n_chunks=t // GLA_CHUNK),
        grid=(s // t,),
        in_specs=[col(512, COL_Q), col(512, COL_K), col(1024, COL_V), col(1024, COL_R), col(LANE, COL_GL),
                  full(wa2p.shape), full(ba.shape), full(ng.shape)],
        out_specs=pl.BlockSpec((t, GROUP_W), lambda i: (i, 0)),
        out_shape=jax.ShapeDtypeStruct((s, GROUP_W), BF16),
        scratch_shapes=[pltpu.VMEM((GLA_HEADS, GLA_DK, GLA_DV), F32)],
        compiler_params=_cparams(("arbitrary",), 32),
        name="gla",
    )(p, p, p, p, p, wa2p, ba, ng)


def _gelu(x):
    return 0.5 * x * (1.0 + lax.erf(x * 0.7071067811865476))


def _gmlp_kernel(u_ref, v_ref, lg_ref, lb_ref, ws_ref, bias_ref, ng_ref, o_ref, buf_ref, *, n_chunks):
    row = lax.broadcasted_iota(jnp.int32, (GM_CHUNK, GM_CHUNK), 0)
    col = lax.broadcasted_iota(jnp.int32, (GM_CHUNK, GM_CHUNK), 1)
    causal = col <= row
    v = _gelu(v_ref[...].astype(F32))
    mu = jnp.mean(v, axis=-1, keepdims=True)
    d = v - mu
    var = jnp.mean(d * d, axis=-1, keepdims=True)
    vb = (d * lax.rsqrt(var + 1e-5) * lg_ref[...] + lb_ref[...]).astype(BF16)
    for g in range(GM_GROUPS):
        cs = slice(GM_CH * g, GM_CH * (g + 1))
        w = jnp.where(causal, ws_ref[g], 0.0).astype(BF16)
        for c in range(n_chunks):
            rs = slice(GM_CHUNK * c, GM_CHUNK * (c + 1))
            mixed = _dot(w, vb[rs, cs]) + bias_ref[:, cs]
            buf_ref[rs, cs] = _gelu(u_ref[rs, cs].astype(F32)) * mixed
    out = buf_ref[...]
    ms = jnp.mean(out * out, axis=-1, keepdims=True)
    o_ref[...] = (out * lax.rsqrt(ms + 1e-6) * ng_ref[...]).astype(o_ref.dtype)


def _gmlp(p, lg, lb, ws, bias, ng):
    s = p.shape[0]
    t = min(256, s)
    col = lambda off: pl.BlockSpec((t, GROUP_W), lambda i: (i, off // GROUP_W))
    full = lambda shape: pl.BlockSpec(shape, lambda i: (0,) * len(shape))
    return pl.pallas_call(
        functools.partial(_gmlp_kernel, n_chunks=t // GM_CHUNK),
        grid=(s // t,),
        in_specs=[col(COL_U), col(COL_VG), full(lg.shape), full(lb.shape), full(ws.shape), full(bias.shape),
                  full(ng.shape)],
        out_specs=pl.BlockSpec((t, GROUP_W), lambda i: (i, 0)),
        out_shape=jax.ShapeDtypeStruct((s, GROUP_W), BF16),
        scratch_shapes=[pltpu.VMEM((t, GROUP_W), F32)],
        compiler_params=_cparams(("parallel",), 32),
        name="gmlp",
    )(p, p, lg, lb, ws, bias, ng)


CONV_HALO = 8


def _sconv_kernel(b_ref, c_ref, h_ref, w_ref, ng_ref, o_ref, z_ref):
    t = b_ref.shape[0]

    @pl.when(pl.program_id(0) == 0)
    def _():
        z_ref[0:CONV_HALO, :] = jnp.zeros((CONV_HALO, GROUP_W), F32)

    z = c_ref[...].astype(F32) * h_ref[...].astype(F32)
    z_ref[CONV_HALO:CONV_HALO + t, :] = z
    z1 = z_ref[CONV_HALO - 1:CONV_HALO - 1 + t, :]
    z2 = z_ref[CONV_HALO - 2:CONV_HALO - 2 + t, :]
    y = w_ref[0:1, :] * z2 + w_ref[1:2, :] * z1 + w_ref[2:3, :] * z
    z_ref[0:CONV_HALO, :] = z[t - CONV_HALO:t, :]
    out = b_ref[...].astype(F32) * y
    ms = jnp.mean(out * out, axis=-1, keepdims=True)
    o_ref[...] = (out * lax.rsqrt(ms + 1e-6) * ng_ref[...]).astype(o_ref.dtype)


def _sconv(p, w, ng):
    s = p.shape[0]
    t = min(256, s)
    col = lambda off: pl.BlockSpec((t, GROUP_W), lambda i: (i, off // GROUP_W))
    full = lambda shape: pl.BlockSpec(shape, lambda i: (0,) * len(shape))
    return pl.pallas_call(
        _sconv_kernel,
        grid=(s // t,),
        in_specs=[col(COL_CB), col(COL_CC), col(COL_CH), full(w.shape), full(ng.shape)],
        out_specs=pl.BlockSpec((t, GROUP_W), lambda i: (i, 0)),
        out_shape=jax.ShapeDtypeStruct((s, GROUP_W), BF16),
        scratch_shapes=[pltpu.VMEM((t + CONV_HALO, GROUP_W), F32)],
        compiler_params=_cparams(("arbitrary",), 32),
        name="sconv",
    )(p, p, p, w, ng)


def _mla_proj_kernel(cq_ref, ckv_ref, kr_ref, krs_ref, qn_ref, kvn_ref, wq_ref, wqs_ref, wk_ref, wv_ref,
                     cos_ref, sin_ref, q_out, k_out, v_out):
    def rms(ref, g_ref):
        t = ref[...].astype(F32)
        return (t * lax.rsqrt(jnp.mean(t * t, axis=-1, keepdims=True) + 1e-6) * g_ref[...]).astype(BF16)

    cqn = rms(cq_ref, qn_ref)
    ckvn = rms(ckv_ref, kvn_ref)
    qm = _dot(cqn, wq_ref[...])
    qsw = _dot(cqn, wqs_ref[...])
    kn = _dot(ckvn, wk_ref[...])
    vv = _dot(ckvn, wv_ref[...])
    cos = cos_ref[...]
    sin = sin_ref[...]
    scale = (MLA_NOPE + MLA_ROPE) ** -0.5 * LOG2_E
    kr_rot = (kr_ref[...].astype(F32) * cos + krs_ref[...].astype(F32) * sin).astype(k_out.dtype)
    ones_col = jnp.where(lax.broadcasted_iota(jnp.int32, (cos.shape[0], LANE), 1) == 0, 1.0, 0.0).astype(v_out.dtype)
    for h in range(MLA_HEADS):
        lo = MLA_QK_PAD * h
        v_out[:, lo:lo + LANE] = vv[:, LANE * h:LANE * (h + 1)].astype(v_out.dtype)
        v_out[:, lo + LANE:lo + 2 * LANE] = ones_col
        q_out[:, lo:lo + LANE] = (qm[:, lo:lo + LANE] * scale).astype(q_out.dtype)
        q_out[:, lo + LANE:lo + 2 * LANE] = (
            (qm[:, lo + LANE:lo + 2 * LANE] * cos + qsw[:, LANE * h:LANE * (h + 1)] * sin) * scale
        ).astype(q_out.dtype)
        k_out[:, lo:lo + LANE] = kn[:, LANE * h:LANE * (h + 1)].astype(k_out.dtype)
        k_out[:, lo + LANE:lo + 2 * LANE] = kr_rot


def _mla_proj(p, qn, kvn, wq, wqs, wk, wv, cos, sin):
    s = p.shape[0]
    t = min(256, s)
    full = lambda shape: pl.BlockSpec(shape, lambda i: (0,) * len(shape))
    rowb = lambda width: pl.BlockSpec((t, width), lambda i: (i, 0))
    qk_w = MLA_HEADS * MLA_QK_PAD
    return pl.pallas_call(
        _mla_proj_kernel,
        grid=(s // t,),
        in_specs=[pl.BlockSpec((t, MLA_Q_LORA), lambda i: (i, COL_CQ // MLA_Q_LORA)),
                  pl.BlockSpec((t, MLA_KV_LORA), lambda i: (i, COL_CKV // MLA_KV_LORA)),
                  pl.BlockSpec((t, LANE), lambda i: (i, COL_KR // LANE)),
                  pl.BlockSpec((t, LANE), lambda i: (i, COL_KRS // LANE)),
                  full(qn.shape), full(kvn.shape), full(wq.shape), full(wqs.shape), full(wk.shape), full(wv.shape),
                  rowb(LANE), rowb(LANE)],
        out_specs=[rowb(qk_w), rowb(qk_w), rowb(qk_w)],
        out_shape=[jax.ShapeDtypeStruct((s, qk_w), BF16)] * 3,
        compiler_params=_cparams(("parallel",), 40),
        name="mla_proj",
    )(p, p, p, p, qn, kvn, wq, wqs, wk, wv, cos, sin)


FLASH_TQ = 512
FLASH_TK = 1024


def _flash_kernel(qi_ref, ki_ref, q_ref, k_ref, v_ref, ng_ref, o_ref, acc_ref, m_ref, *, tq, tk):
    step = pl.program_id(0)
    qi = qi_ref[step]
    ki = ki_ref[step]
    last_ki = (qi * tq) // tk

    @pl.when(ki == 0)
    def _():
        m_ref[...] = jnp.full(m_ref.shape, -jnp.inf, F32)
        acc_ref[...] = jnp.zeros_like(acc_ref)

    def accumulate(masked, k0, kn):
        keys = slice(k0, k0 + kn)
        if masked:
            row = qi * tq + lax.broadcasted_iota(jnp.int32, (tq, kn), 0)
            col = ki * tk + k0 + lax.broadcasted_iota(jnp.int32, (tq, kn), 1)
            visible = col <= row

        def scores(h):
            hs = slice(MLA_QK_PAD * h, MLA_QK_PAD * (h + 1))
            return _dot_nt(q_ref[:, hs], k_ref[keys, hs])

        sc = scores(0)
        for h in range(MLA_HEADS):
            hs = slice(MLA_QK_PAD * h, MLA_QK_PAD * (h + 1))
            sc_next = scores(h + 1) if h + 1 < MLA_HEADS else None
            if masked:
                sc = jnp.where(visible, sc, -jnp.inf)
            m_old = m_ref[h]
            m_new = jnp.maximum(m_old, jnp.max(sc, axis=-1, keepdims=True))
            pr = jnp.exp2(sc - m_new).astype(BF16)
            acc_ref[:, hs] = jnp.exp2(m_old - m_new) * acc_ref[:, hs] + _dot(pr, v_ref[keys, hs])
            m_ref[h] = m_new
            sc = sc_next

    @pl.when(ki < last_ki)
    def _():
        accumulate(False, 0, tk)

    @pl.when(ki == last_ki)
    def _():
        accumulate(True, 0, tk)
        outs = []
        for h in range(MLA_HEADS):
            lo = MLA_QK_PAD * h
            outs.append(acc_ref[:, lo:lo + MLA_V] / acc_ref[:, lo + MLA_V:lo + MLA_V + 1])
        o = jnp.concatenate(outs, axis=1)
        ms = jnp.mean(o * o, axis=-1, keepdims=True)
        o_ref[...] = (o * lax.rsqrt(ms + 1e-6) * ng_ref[...]).astype(o_ref.dtype)


def _flash(q, k, v, ng):
    s = q.shape[0]
    tk = min(FLASH_TK, s)
    tq = min(FLASH_TQ, tk)
    nq = s // tq
    pairs = [(a, b) for a in range(nq) for b in range((a * tq) // tk + 1)]
    qi_tab = jnp.asarray([a for a, _ in pairs], jnp.int32)
    ki_tab = jnp.asarray([b for _, b in pairs], jnp.int32)
    qk_w = MLA_HEADS * MLA_QK_PAD
    grid_spec = pltpu.PrefetchScalarGridSpec(
        num_scalar_prefetch=2,
        grid=(len(pairs),),
        in_specs=[pl.BlockSpec((tq, qk_w), lambda t, qi, ki: (qi[t], 0)),
                  pl.BlockSpec((tk, qk_w), lambda t, qi, ki: (ki[t], 0)),
                  pl.BlockSpec((tk, qk_w), lambda t, qi, ki: (ki[t], 0)),
                  pl.BlockSpec((1, GROUP_W), lambda t, qi, ki: (0, 0))],
        out_specs=pl.BlockSpec((tq, GROUP_W), lambda t, qi, ki: (qi[t], 0)),
        scratch_shapes=[pltpu.VMEM((tq, qk_w), F32),
                        pltpu.VMEM((MLA_HEADS, tq, 1), F32)],
    )
    return pl.pallas_call(
        functools.partial(_flash_kernel, tq=tq, tk=tk),
        grid_spec=grid_spec,
        out_shape=jax.ShapeDtypeStruct((s, GROUP_W), BF16),
        compiler_params=_cparams(("arbitrary",), 56),
        name="flash",
    )(qi_tab, ki_tab, q, k, v, ng)


def _mm_out_kernel(a0_ref, a1_ref, a2_ref, a3_ref, w_ref, x_ref, o_ref):
    acc = _dot(a0_ref[...], w_ref[0, 0:GROUP_W, :])
    acc = acc + _dot(a1_ref[...], w_ref[0, GROUP_W:2 * GROUP_W, :])
    acc = acc + _dot(a2_ref[...], w_ref[0, 2 * GROUP_W:3 * GROUP_W, :])
    acc = acc + _dot(a3_ref[...], w_ref[0, 3 * GROUP_W:4 * GROUP_W, :])
    o_ref[...] = ALPHA * x_ref[...] + acc


def _mm_out(a0, a1, a2, a3, w, layer, x):
    s = x.shape[0]
    tm = min(512, s)
    tn = 1024
    a_spec = pl.BlockSpec((tm, GROUP_W), lambda i, j: (i, 0))
    return pl.pallas_call(
        _mm_out_kernel,
        grid=(s // tm, D_MODEL // tn),
        in_specs=[a_spec, a_spec, a_spec, a_spec,
                  pl.BlockSpec((1, D_MODEL, tn), lambda i, j: (layer, 0, j)),
                  pl.BlockSpec((tm, tn), lambda i, j: (i, j))],
        out_specs=pl.BlockSpec((tm, tn), lambda i, j: (i, j)),
        out_shape=jax.ShapeDtypeStruct((s, D_MODEL), F32),
        compiler_params=_cparams(("parallel", "arbitrary"), 48),
        name="mm_out",
    )(a0, a1, a2, a3, w, x)


def _layer_norm(t, g, b):
    mu = jnp.mean(t, axis=-1, keepdims=True)
    d = t - mu
    var = jnp.mean(d * d, axis=-1, keepdims=True)
    return d * lax.rsqrt(var + 1e-5) * g + b


def _ln_route_kernel(y_ref, g_ref, b_ref, wh_ref, wl_ref, rb_ref, x_out, route_out, xtok_hbm, tbuf, sem, *, tm):
    i = pl.program_id(0)
    slot = lax.rem(i, 2)

    def wait_rows(buf_slot):
        pltpu.make_async_copy(tbuf.at[buf_slot], tbuf.at[buf_slot], sem.at[buf_slot]).wait()

    x1 = _layer_norm(y_ref[...], g_ref[...], b_ref[...])
    x_out[...] = x1
    x_hi = x1.astype(BF16)
    x_hi32 = x_hi.astype(F32)

    @pl.when(i >= 2)
    def _():
        wait_rows(slot)

    words = _pack_rows(x_hi32[:, :HALF_D], x_hi32[:, HALF_D:])
    for c in range(TOK_WORDS):
        tbuf[slot, c] = words[:, LANE * c:LANE * (c + 1)]
    for j in range(tm):
        pltpu.make_async_copy(tbuf.at[slot, :, pl.ds(j, 1), :], xtok_hbm.at[i * tm + j], sem.at[slot]).start()

    @pl.when(i == pl.num_programs(0) - 1)
    def _():
        wait_rows(slot)

        @pl.when(i >= 1)
        def _():
            wait_rows(1 - slot)

    x_lo = (x1 - x_hi32).astype(BF16)
    logits = _dot(x_hi, wh_ref[...]) + _dot(x_lo, wh_ref[...]) + _dot(x_hi, wl_ref[...]) + rb_ref[...]
    lane = lax.broadcasted_iota(jnp.int32, logits.shape, 1).astype(F32)
    neg = -jnp.inf
    big = 1e9
    is_grp = lane < MOE_GROUPS
    lg = jnp.where(is_grp, logits, neg)
    mg = jnp.max(lg, axis=-1, keepdims=True)
    gsel = jnp.min(jnp.where(lg == mg, lane, big), axis=-1, keepdims=True)
    pg_sel = 1.0 / jnp.sum(jnp.where(is_grp, jnp.exp(lg - mg), 0.0), axis=-1, keepdims=True)
    lo = MOE_GROUPS + MOE_PER_GROUP * gsel
    le = jnp.where((lane >= lo) & (lane < lo + MOE_PER_GROUP), logits, neg)
    v1 = jnp.max(le, axis=-1, keepdims=True)
    i1 = jnp.min(jnp.where(le == v1, lane, big), axis=-1, keepdims=True)
    le2 = jnp.where(lane == i1, neg, le)
    v2 = jnp.max(le2, axis=-1, keepdims=True)
    i2 = jnp.min(jnp.where(le2 == v2, lane, big), axis=-1, keepdims=True)
    e = jnp.exp(v2 - v1)
    w1 = pg_sel / (1.0 + e)
    w2 = pg_sel * e / (1.0 + e)
    route = jnp.where(lane == 0.0, i1 - MOE_GROUPS,
                      jnp.where(lane == 1.0, i2 - MOE_GROUPS,
                                jnp.where(lane == 2.0, w1, jnp.where(lane == 3.0, w2, 0.0))))
    route_out[...] = route


def _ln_route(y, g, b, wh, wl, rb):
    s = y.shape[0]
    tm = min(256, s)
    full = lambda shape: pl.BlockSpec(shape, lambda i: (0,) * len(shape))
    rowb = lambda width: pl.BlockSpec((tm, width), lambda i: (i, 0))
    x1, route, x1_tok = pl.pallas_call(
        functools.partial(_ln_route_kernel, tm=tm),
        grid=(s // tm,),
        in_specs=[rowb(D_MODEL), full(g.shape), full(b.shape), full(wh.shape), full(wl.shape), full(rb.shape)],
        out_specs=[rowb(D_MODEL), rowb(LANE), pl.BlockSpec(memory_space=pl.ANY)],
        out_shape=[jax.ShapeDtypeStruct((s, D_MODEL), F32),
                   jax.ShapeDtypeStruct((s, LANE), F32),
                   jax.ShapeDtypeStruct((s, TOK_WORDS, 1, LANE), U32)],
        scratch_shapes=[pltpu.VMEM((2, TOK_WORDS, tm, LANE), U32), pltpu.SemaphoreType.DMA((2,))],
        compiler_params=_cparams(("arbitrary",), 48),
        name="ln_route",
    )(y, g, b, wh, wl, rb)
    return x1, x1_tok, route


def _moe_kernel(te_ref, ne_ref, tv_ref, nt_ref, src_ref, nxt_ref, dst_ref, x_hbm, wg_hbm, wu_hbm, wd_hbm, ws_ref,
                y_hbm, gbuf, hb, obuf, stg_g, stg_u, stg_d, wg_ref, wu_ref, wd_ref, gsem, ssem, wsem):
    t = pl.program_id(0)
    n_used = nt_ref[0]
    slot = lax.rem(t, 2)

    def weight_copies(e):
        return (pltpu.make_async_copy(wg_hbm.at[e], stg_g, wsem.at[0]),
                pltpu.make_async_copy(wu_hbm.at[e], stg_u, wsem.at[1]),
                pltpu.make_async_copy(wd_hbm.at[e], stg_d, wsem.at[2]))

    def round_weights(stg, dst, rows):
        def body(r, carry):
            rs = pl.ds(pl.multiple_of(r * rows, rows), rows)
            dst[rs, :] = stg[rs, :].astype(BF16)
            return carry
        lax.fori_loop(0, stg.shape[0] // rows, body, 0)

    @pl.when(t == 0)
    def _():
        for cp in weight_copies(te_ref[0]):
            cp.start()

    @pl.when(t < n_used)
    def _():
        expert = te_ref[t]
        first_tile_of_expert = jnp.logical_or(t == 0, expert != te_ref[jnp.maximum(t - 1, 0)])

        @pl.when(first_tile_of_expert)
        def _():
            for cp in weight_copies(expert):
                cp.wait()
            round_weights(stg_g, wg_ref, WEIGHT_ROUND_ROWS)
            round_weights(stg_u, wu_ref, WEIGHT_ROUND_ROWS)
            round_weights(stg_d, wd_ref, WEIGHT_ROUND_ROWS * D_EXPERT // D_MODEL)

            @pl.when(ne_ref[t] >= 0)
            def _():
                for cp in weight_copies(ne_ref[t]):
                    cp.start()

    def gather_row(idx_ref, buf_slot, i):
        return pltpu.make_async_copy(x_hbm.at[idx_ref[i]], gbuf.at[buf_slot, :, pl.ds(i, 1), :], gsem.at[buf_slot])

    def wait_gather(buf_slot):
        pltpu.make_async_copy(gbuf.at[buf_slot], gbuf.at[buf_slot], gsem.at[buf_slot]).wait()

    def scatter_row(i):
        return pltpu.make_async_copy(obuf.at[:, pl.ds(i, 1), :], y_hbm.at[dst_ref[i]], ssem)

    def start_scatter(n_rows):
        def group(g, carry):
            for u in range(SCATTER_UNROLL):
                scatter_row(g * SCATTER_UNROLL + u).start()
            return carry

        def single(i, carry):
            scatter_row(i).start()
            return carry
        n_groups = n_rows // SCATTER_UNROLL
        lax.fori_loop(0, n_groups, group, 0)
        lax.fori_loop(n_groups * SCATTER_UNROLL, n_rows, single, 0)

    def wait_scatter(n_rows):
        rows = obuf.at[:, pl.ds(0, n_rows), :]
        pltpu.make_async_copy(rows, rows, ssem).wait()

    @pl.when(t == 0)
    def _():
        def body(i, carry):
            gather_row(src_ref, 0, i).start()
            return carry
        lax.fori_loop(0, MOE_TILE, body, 0, unroll=8)

    @pl.when(t < n_used)
    def _():
        wait_gather(slot)
        for c in range(TOK_WORDS):
            lo, hi = _unpack_rows(gbuf[slot, c])
            hb[:, LANE * c:LANE * (c + 1)] = lo.astype(BF16)
            hb[:, HALF_D + LANE * c:HALF_D + LANE * (c + 1)] = hi.astype(BF16)
        for i in range(MOE_TILE):
            gather_row(nxt_ref, 1 - slot, i).start()
        h = hb[...]
        gate = _dot(h, wg_ref[...])
        up = _dot(h, wu_ref[...])
        a = (gate / (1.0 + jnp.exp(-gate)) * up).astype(BF16)

        @pl.when(t > 0)
        def _():
            wait_scatter(tv_ref[jnp.maximum(t - 1, 0)])

        ws = ws_ref[...]
        rounded = lambda v: v.astype(BF16).astype(F32)
        for c2 in range(TOK_WORDS // 2):
            cols = slice(2 * LANE * c2, 2 * LANE * (c2 + 1))
            lo = _dot(a, wd_ref[:, cols]) * ws
            hi = _dot(a, wd_ref[:, HALF_D + cols.start:HALF_D + cols.stop]) * ws
            words = _pack_rows(rounded(lo), rounded(hi))
            obuf[2 * c2] = words[:, :LANE]
            obuf[2 * c2 + 1] = words[:, LANE:]
        start_scatter(tv_ref[t])

        @pl.when(t == n_used - 1)
        def _():
            wait_gather(1 - slot)
            wait_scatter(tv_ref[t])


def _moe(tile_expert, next_expert, tile_valid, n_tiles, src_token, dst_row, x_tok, wg, wu, wd, w_slot):
    p_rows = src_token.shape[0]
    tm = MOE_TILE
    n_blocks = p_rows // tm
    smem_blk = lambda fn: pl.BlockSpec((tm,), fn, memory_space=pltpu.SMEM)
    hbm = pl.BlockSpec(memory_space=pl.ANY)
    grid_spec = pltpu.PrefetchScalarGridSpec(
        num_scalar_prefetch=4,
        grid=(n_blocks,),
        in_specs=[smem_blk(lambda t, te, ne, tv, nt: (t,)),
                  smem_blk(lambda t, te, ne, tv, nt: (jnp.minimum(t + 1, n_blocks - 1),)),
                  smem_blk(lambda t, te, ne, tv, nt: (t,)),
                  hbm, hbm, hbm, hbm,
                  pl.BlockSpec((tm, 1), lambda t, te, ne, tv, nt: (t, 0))],
        out_specs=hbm,
        scratch_shapes=[pltpu.VMEM((2, TOK_WORDS, tm, LANE), U32),
                        pltpu.VMEM((tm, D_MODEL), BF16),
                        pltpu.VMEM((TOK_WORDS, tm, LANE), U32),
                        pltpu.VMEM((D_MODEL, D_EXPERT), F32),
                        pltpu.VMEM((D_MODEL, D_EXPERT), F32),
                        pltpu.VMEM((D_EXPERT, D_MODEL), F32),
                        pltpu.VMEM((D_MODEL, D_EXPERT), BF16),
                        pltpu.VMEM((D_MODEL, D_EXPERT), BF16),
                        pltpu.VMEM((D_EXPERT, D_MODEL), BF16),
                        pltpu.SemaphoreType.DMA((2,)),
                        pltpu.SemaphoreType.DMA(()),
                        pltpu.SemaphoreType.DMA((3,))],
    )
    return pl.pallas_call(
        _moe_kernel,
        grid_spec=grid_spec,
        out_shape=jax.ShapeDtypeStruct((2 * x_tok.shape[0], TOK_WORDS, 1, LANE), U32),
        compiler_params=_cparams(("arbitrary",), 58),
        name="moe",
    )(tile_expert, next_expert, tile_valid, n_tiles, src_token, src_token, dst_row, x_tok, wg, wu, wd, w_slot)


def _ln_add_kernel(x_ref, g_ref, b_ref, y_hbm, o_ref, ybuf, vbuf, sem, *, tm):
    i = pl.program_id(0)
    slot = lax.rem(i, 2)

    def start_gather(tile, buf_slot):
        base = tile * (2 * tm)

        def body(j, carry):
            for k in range(2):
                pltpu.make_async_copy(y_hbm.at[base + 2 * j + k], ybuf.at[buf_slot, k, :, pl.ds(j, 1), :],
                                      sem.at[buf_slot]).start()
            return carry
        lax.fori_loop(0, tm, body, 0, unroll=4)

    @pl.when(i == 0)
    def _():
        start_gather(0, 0)

    @pl.when(i + 1 < pl.num_programs(0))
    def _():
        start_gather(i + 1, 1 - slot)

    pltpu.make_async_copy(ybuf.at[slot], ybuf.at[slot], sem.at[slot]).wait()
    for c in range(TOK_WORDS):
        a_lo, a_hi = _unpack_rows(ybuf[slot, 0, c])
        b_lo, b_hi = _unpack_rows(ybuf[slot, 1, c])
        lo = slice(LANE * c, LANE * (c + 1))
        hi = slice(HALF_D + LANE * c, HALF_D + LANE * (c + 1))
        vbuf[:, lo] = ALPHA * x_ref[:, lo] + (a_lo + b_lo)
        vbuf[:, hi] = ALPHA * x_ref[:, hi] + (a_hi + b_hi)
    o_ref[...] = _layer_norm(vbuf[...], g_ref[...], b_ref[...])


def _ln_add(x, y_tok, g, b):
    s = x.shape[0]
    tm = min(256, s)
    full = lambda shape: pl.BlockSpec(shape, lambda i: (0,) * len(shape))
    return pl.pallas_call(
        functools.partial(_ln_add_kernel, tm=tm),
        grid=(s // tm,),
        in_specs=[pl.BlockSpec((tm, D_MODEL), lambda i: (i, 0)), full(g.shape), full(b.shape),
                  pl.BlockSpec(memory_space=pl.ANY)],
        out_specs=pl.BlockSpec((tm, D_MODEL), lambda i: (i, 0)),
        out_shape=jax.ShapeDtypeStruct((s, D_MODEL), F32),
        scratch_shapes=[pltpu.VMEM((2, 2, TOK_WORDS, tm, LANE), U32),
                        pltpu.VMEM((tm, D_MODEL), F32),
                        pltpu.SemaphoreType.DMA((2,))],
        compiler_params=_cparams(("arbitrary",), 48),
        name="ln_add",
    )(x, g, b, y_tok)


def _w_in_sources():
    src = {}
    o = 0
    for name, width in (("q", 512), ("k", 512), ("v", 1024), ("r", 1024), ("gl", GLA_GATE_RANK), ("u", 1024),
                        ("vg", 1024), ("cb", 1024), ("cc", 1024), ("ch", 1024), ("cq", MLA_Q_LORA),
                        ("ckv", MLA_KV_LORA), ("kr", MLA_ROPE)):
        src[name] = (o, width)
        o += width
    return src, o


def _pack_w_in_kernel(w3_ref, o_ref):
    src, _ = _w_in_sources()
    w_ref = w3_ref.at[0]
    rows = w_ref.shape[0]
    for name, dst in (("cq", COL_CQ), ("ckv", COL_CKV), ("q", COL_Q), ("k", COL_K), ("v", COL_V), ("r", COL_R),
                      ("u", COL_U), ("vg", COL_VG), ("cb", COL_CB), ("cc", COL_CC), ("ch", COL_CH)):
        s0, width = src[name]
        o_ref[:, dst:dst + width] = w_ref[:, s0:s0 + width].astype(BF16)
    zeros = lambda n: jnp.zeros((rows, n), BF16)
    half = MLA_ROPE // 2
    kr0, _ = src["kr"]
    kr = w_ref[:, kr0:kr0 + MLA_ROPE].astype(BF16)
    o_ref[:, COL_KR:COL_KR + LANE] = jnp.concatenate([kr, zeros(LANE - MLA_ROPE)], axis=1)
    o_ref[:, COL_KRS:COL_KRS + LANE] = jnp.concatenate([kr[:, half:], kr[:, :half], zeros(LANE - MLA_ROPE)], axis=1)
    gl0, _ = src["gl"]
    o_ref[:, COL_GL:COL_GL + LANE] = jnp.concatenate(
        [w_ref[:, gl0:gl0 + GLA_GATE_RANK].astype(BF16), zeros(LANE - GLA_GATE_RANK)], axis=1)
    o_ref[:, COL_GL + LANE:P_COLS] = zeros(P_COLS - COL_GL - LANE)


def _pack_w_in(w, layer):
    _, k, d_in = w.shape
    assert d_in == _w_in_sources()[1]
    tr = 256
    return pl.pallas_call(
        _pack_w_in_kernel,
        grid=(k // tr,),
        in_specs=[pl.BlockSpec((1, tr, d_in), lambda i: (layer, i, 0))],
        out_specs=pl.BlockSpec((tr, P_COLS), lambda i: (i, 0)),
        out_shape=jax.ShapeDtypeStruct((k, P_COLS), BF16),
        compiler_params=_cparams(("parallel",), 48),
        name="pack_w_in",
    )(w)


def _pack_mla_weights(wuq, wukv):
    half = MLA_ROPE // 2
    wq3 = wuq.reshape(MLA_Q_LORA, MLA_HEADS, MLA_NOPE + MLA_ROPE)
    rope = wq3[:, :, MLA_NOPE:]
    zq = jnp.zeros((MLA_Q_LORA, MLA_HEADS, MLA_QK_PAD - MLA_NOPE - MLA_ROPE), wuq.dtype)
    wq = jnp.concatenate([wq3, zq], axis=2).reshape(MLA_Q_LORA, MLA_HEADS * MLA_QK_PAD)
    zs = jnp.zeros((MLA_Q_LORA, MLA_HEADS, LANE - MLA_ROPE), wuq.dtype)
    wqs = jnp.concatenate([rope[:, :, half:], rope[:, :, :half], zs], axis=2).reshape(MLA_Q_LORA, MLA_HEADS * LANE)
    wkv3 = wukv.reshape(MLA_KV_LORA, MLA_HEADS, MLA_NOPE + MLA_V)
    wk = wkv3[:, :, :MLA_NOPE].reshape(MLA_KV_LORA, MLA_HEADS * MLA_NOPE)
    wv = wkv3[:, :, MLA_NOPE:].reshape(MLA_KV_LORA, MLA_HEADS * MLA_V)
    return wq.astype(BF16), wqs.astype(BF16), wk.astype(BF16), wv.astype(BF16)


def _rope_tables(s):
    pos = jnp.arange(s, dtype=F32)
    inv_freq = ROPE_BASE ** (-jnp.arange(0, MLA_ROPE, 2, dtype=F32) / MLA_ROPE)
    ang = pos[:, None] * inv_freq[None, :]
    cos, sin = jnp.cos(ang), jnp.sin(ang)
    z = jnp.zeros((s, LANE - MLA_ROPE), F32)
    return jnp.concatenate([cos, cos, z], axis=1), jnp.concatenate([-sin, sin, z], axis=1)


def _dispatch_plan(route, s):
    n_assign = 2 * s
    p_rows = n_assign + N_EXPERTS * MOE_TILE
    eid = route[:, 0:2].astype(jnp.int32).reshape(n_assign)
    wgt = route[:, 2:4].reshape(n_assign)
    onehot = (eid[:, None] == jnp.arange(N_EXPERTS, dtype=jnp.int32)[None, :]).astype(jnp.int32)
    csum = jnp.cumsum(onehot, axis=0)
    rank = jnp.sum(onehot * (csum - 1), axis=1)
    counts = csum[-1]
    tiles_e = (counts + MOE_TILE - 1) // MOE_TILE
    tile_end = jnp.cumsum(tiles_e)
    tile_start = tile_end - tiles_e
    slot = tile_start[eid] * MOE_TILE + rank
    slot_assign = jnp.full((p_rows,), -1, jnp.int32).at[slot].set(jnp.arange(n_assign, dtype=jnp.int32))
    is_pad = slot_assign < 0
    src_token = jnp.where(is_pad, 0, slot_assign // 2)
    dst_row = jnp.maximum(slot_assign, 0)
    w_slot = jnp.where(is_pad, 0.0, wgt[dst_row])
    n_blocks = p_rows // MOE_TILE
    tile_valid = jnp.sum(jnp.logical_not(is_pad).reshape(n_blocks, MOE_TILE).astype(jnp.int32), axis=1)
    tile_ids = jnp.arange(n_blocks, dtype=jnp.int32)
    tile_expert = jnp.minimum(jnp.sum((tile_end[None, :] <= tile_ids[:, None]).astype(jnp.int32), axis=1),
                              N_EXPERTS - 1)
    n_tiles = tile_end[-1:].astype(jnp.int32)
    experts = jnp.arange(N_EXPERTS, dtype=jnp.int32)
    later_with_tiles = (experts[None, :] > experts[:, None]) & (tiles_e[None, :] > 0)
    next_nonempty = jnp.min(jnp.where(later_with_tiles, experts[None, :], N_EXPERTS), axis=1)
    next_expert = next_nonempty[tile_expert]
    return src_token, dst_row, w_slot.reshape(p_rows, 1), tile_expert, next_expert, tile_valid, n_tiles


def kernel(x, w_in, gla_wa2, gla_ba, gla_norm, gm_ln_g, gm_ln_b, gm_ws, gm_bs, gm_norm, sc_conv, sc_norm, mla_q_norm, mla_kv_norm, mla_wuq, mla_wukv, mla_norm, w_o, ln1_g, ln1_b, router_g_w, router_g_b, router_e_w, router_e_b, exp_w_gate, exp_w_up, exp_w_down, ln2_g, ln2_b):
    bsz, s, _ = x.shape
    assert bsz == 1
    xc = x.reshape(s, D_MODEL)
    cos_t, sin_t = _rope_tables(s)
    row = lambda v: v.reshape(1, -1)
    wg_all = exp_w_gate.reshape(DEPTH * N_EXPERTS, D_MODEL, D_EXPERT)
    wu_all = exp_w_up.reshape(DEPTH * N_EXPERTS, D_MODEL, D_EXPERT)
    wd_all = exp_w_down.reshape(DEPTH * N_EXPERTS, D_EXPERT, D_MODEL)
    w_o_b = w_o.astype(BF16)
    for l in range(DEPTH):
        p = _mm_in(xc, _pack_w_in(w_in, l))

        wa2p = jnp.concatenate(
            [gla_wa2[l], jnp.zeros((LANE - GLA_GATE_RANK, GLA_HEADS * GLA_DK), F32)], axis=0).astype(BF16)
        out_a = _gla(p, wa2p, row(gla_ba[l]), row(gla_norm[l]))

        gm_bias = jnp.repeat(gm_bs[l].T, GM_CH, axis=1)
        out_b = _gmlp(p, row(gm_ln_g[l]), row(gm_ln_b[l]), gm_ws[l], gm_bias, row(gm_norm[l]))

        out_c = _sconv(p, sc_conv[l], row(sc_norm[l]))

        wq, wqs, wk, wv = _pack_mla_weights(mla_wuq[l], mla_wukv[l])
        q_att, k_att, v_att = _mla_proj(p, row(mla_q_norm[l]), row(mla_kv_norm[l]), wq, wqs, wk, wv, cos_t, sin_t)
        out_d = _flash(q_att, k_att, v_att, row(mla_norm[l]))

        y = _mm_out(out_a, out_b, out_c, out_d, w_o_b, l, xc)

        wr = jnp.concatenate([router_g_w[l], router_e_w[l],
                              jnp.zeros((D_MODEL, LANE - MOE_GROUPS - N_EXPERTS), F32)], axis=1)
        wr_hi = wr.astype(BF16)
        wr_lo = (wr - wr_hi.astype(F32)).astype(BF16)
        rb = jnp.concatenate([router_g_b[l], router_e_b[l], jnp.zeros((LANE - MOE_GROUPS - N_EXPERTS,), F32)])
        x1, x1_tok, route = _ln_route(y, row(ln1_g[l]), row(ln1_b[l]), wr_hi, wr_lo, row(rb))

        src_token, dst_row, w_slot, tile_expert, next_expert, tile_valid, n_tiles = _dispatch_plan(route, s)
        next_expert = jnp.where(next_expert < N_EXPERTS, next_expert + l * N_EXPERTS, -1)
        y_tok = _moe(tile_expert + l * N_EXPERTS, next_expert, tile_valid, n_tiles, src_token, dst_row, x1_tok,
                     wg_all, wu_all, wd_all, w_slot)
        xc = _ln_add(x1, y_tok, row(ln2_g[l]), row(ln2_b[l]))
    return xc.reshape(bsz, s, D_MODEL)
```

```python
import functools

import jax
import jax.numpy as jnp
from jax import lax
from jax.experimental import pallas as pl
from jax.experimental.pallas import tpu as pltpu

F32 = jnp.float32
BF16 = jnp.bfloat16

D_MODEL = 4096
DEPTH = 2
GROUP_W = 1024

GLA_HEADS = 4
GLA_DK = 128
GLA_DV = 256
GLA_GATE_RANK = 16
GLA_TAU = 16.0
GLA_CHUNK = 64
GLA_SUB = 16

GM_GROUPS = 8
GM_CH = 128
GM_CHUNK = 128

MLA_HEADS = 8
MLA_NOPE = 128
MLA_ROPE = 64
MLA_V = 128
MLA_Q_LORA = 768
MLA_KV_LORA = 256
ROPE_BASE = 10000.0
MLA_QK_PAD = 256

MOE_GROUPS = 4
MOE_PER_GROUP = 8
N_EXPERTS = 32
D_EXPERT = 512

ALPHA = (2.0 * DEPTH) ** 0.25
LOG2_E = 1.4426950408889634

LANE = 128
MIB = 1024 * 1024

COL_CQ = 0
COL_CKV = 768
COL_Q = 1024
COL_K = 1536
COL_V = 2048
COL_R = 3072
COL_U = 4096
COL_VG = 5120
COL_CB = 6144
COL_CC = 7168
COL_CH = 8192
COL_KR = 9216
COL_KRS = 9344
COL_GL = 9472
P_COLS = 9728

MOE_TILE = 256
SCATTER_UNROLL = 8
WEIGHT_ROUND_ROWS = 512
HALF_D = D_MODEL // 2
TOK_WORDS = HALF_D // LANE
U32 = jnp.uint32
HIGH_HALF = 0xFFFF0000


def _pack_rows(lo_f32, hi_f32):
    return (lax.bitcast_convert_type(lo_f32, U32) >> 16) | lax.bitcast_convert_type(hi_f32, U32)


def _unpack_rows(words):
    return (lax.bitcast_convert_type(words << 16, F32),
            lax.bitcast_convert_type(words & jnp.uint32(HIGH_HALF), F32))


def _cparams(sem, vmem_mib):
    return pltpu.CompilerParams(dimension_semantics=sem, vmem_limit_bytes=vmem_mib * MIB)


def _dot(a, b):
    return jnp.dot(a, b, preferred_element_type=F32)


def _dot_nt(a, b):
    return lax.dot_general(a, b, (((1,), (1,)), ((), ())), preferred_element_type=F32)


def _dot_tn(a, b):
    return lax.dot_general(a, b, (((0,), (0,)), ((), ())), preferred_element_type=F32)


def _mm_in_kernel(x_ref, w_ref, o_ref, xb_ref):
    @pl.when(pl.program_id(1) == 0)
    def _():
        xb_ref[...] = x_ref[...].astype(BF16)

    o_ref[...] = _dot(xb_ref[...], w_ref[...]).astype(o_ref.dtype)


def _mm_in(x, w):
    s, k = x.shape
    n = w.shape[1]
    tm = min(1024, s)
    tn = 512
    return pl.pallas_call(
        _mm_in_kernel,
        grid=(s // tm, n // tn),
        in_specs=[pl.BlockSpec((tm, k), lambda i, j: (i, 0)),
                  pl.BlockSpec((k, tn), lambda i, j: (0, j))],
        out_specs=pl.BlockSpec((tm, tn), lambda i, j: (i, j)),
        out_shape=jax.ShapeDtypeStruct((s, n), BF16),
        scratch_shapes=[pltpu.VMEM((tm, k), BF16)],
        compiler_params=_cparams(("parallel", "arbitrary"), 58),
        name="mm_in",
    )(x, w)


def _gla_kernel(q_ref, k_ref, v_ref, r_ref, gl_ref, wa2_ref, ba_ref, ng_ref, o_ref, s_ref, *, n_chunks):
    c_len = GLA_CHUNK

    @pl.when(pl.program_id(0) == 0)
    def _():
        s_ref[...] = jnp.zeros_like(s_ref)

    row = lax.broadcasted_iota(jnp.int32, (c_len, c_len), 0)
    col = lax.broadcasted_iota(jnp.int32, (c_len, c_len), 1)
    tril = jnp.where(col <= row, 1.0, 0.0).astype(BF16)
    sub_row = lax.broadcasted_iota(jnp.int32, (GLA_SUB, c_len), 0)
    sub_col = lax.broadcasted_iota(jnp.int32, (GLA_SUB, c_len), 1)
    n_sub = c_len // GLA_SUB

    def chunk(c, carry):
        rows = pl.ds(pl.multiple_of(c * c_len, c_len), c_len)
        logit = _dot(gl_ref[rows, :], wa2_ref[...]) + ba_ref[...]
        g = (jnp.minimum(logit, 0.0) - jnp.log(1.0 + jnp.exp(-jnp.abs(logit)))) * (1.0 / GLA_TAU)
        g_hi = g.astype(BF16)
        g_lo = (g - g_hi.astype(F32)).astype(BF16)
        b_all = _dot(tril, g_hi) + _dot(tril, g_lo)

        for h in range(GLA_HEADS):
            hs = slice(GLA_DK * h, GLA_DK * (h + 1))
            vs = slice(GLA_DV * h, GLA_DV * (h + 1))
            bh = b_all[:, hs]
            qh = q_ref[rows, hs].astype(F32) * (GLA_DK ** -0.5)
            kh = k_ref[rows, hs].astype(F32)
            vh = v_ref[rows, vs]
            state = s_ref[h]

            o = _dot((qh * jnp.exp(bh)).astype(BF16), state.astype(BF16))

            att_rows = []
            for blk in range(n_sub):
                sl = slice(GLA_SUB * blk, GLA_SUB * (blk + 1))
                b_blk = bh[sl]
                q_blk = qh[sl]
                att = jnp.zeros((GLA_SUB, c_len), F32)
                if blk > 0:
                    ref = bh[GLA_SUB * blk:GLA_SUB * blk + 1, :]
                    qs = (q_blk * jnp.exp(b_blk - ref)).astype(BF16)
                    ks = (kh * jnp.exp(jnp.minimum(ref - bh, 0.0))).astype(BF16)
                    att = jnp.where(sub_col < GLA_SUB * blk, _dot_nt(qs, ks), 0.0)
                for jj in range(GLA_SUB):
                    j = GLA_SUB * blk + jj
                    t = q_blk * kh[j:j + 1, :] * jnp.exp(b_blk - bh[j:j + 1, :])
                    rs = jnp.sum(t, axis=-1, keepdims=True)
                    att = jnp.where((sub_col == j) & (sub_row >= jj), rs, att)
                att_rows.append(att)
            att_full = jnp.concatenate(att_rows, axis=0).astype(BF16)
            o = o + _dot(att_full, vh)

            b_last = bh[c_len - 1:c_len, :]
            kd = (kh * jnp.exp(b_last - bh)).astype(BF16)
            decay_col = jnp.transpose(jnp.broadcast_to(jnp.exp(b_last), (GLA_DK, GLA_DK)))
            s_ref[h] = state * jnp.concatenate([decay_col, decay_col], axis=1) + _dot_tn(kd, vh)

            var = jnp.mean(o * o, axis=-1, keepdims=True)
            on = o * lax.rsqrt(var + 1e-6) * ng_ref[:, vs]
            rr = r_ref[rows, vs].astype(F32)
            o_ref[rows, vs] = (on * (rr / (1.0 + jnp.exp(-rr)))).astype(o_ref.dtype)
        return carry

    lax.fori_loop(0, n_chunks, chunk, 0)


def _gla(p, wa2p, ba, ng):
    s = p.shape[0]
    t = min(256, s)
    col = lambda width, off: pl.BlockSpec((t, width), lambda i: (i, off // width))
    full = lambda shape: pl.BlockSpec(shape, lambda i: (0,) * len(shape))
    return pl.pallas_call(
        functools.partial(_gla_kernel, n_chunks=t // GLA_CHUNK),
        grid=(s // t,),
        in_specs=[col(512, COL_Q), col(512, COL_K), col(1024, COL_V), col(1024, COL_R), col(LANE, COL_GL),
                  full(wa2p.shape), full(ba.shape), full(ng.shape)],
        out_specs=pl.BlockSpec((t, GROUP_W), lambda i: (i, 0)),
        out_shape=jax.ShapeDtypeStruct((s, GROUP_W), BF16),
        scratch_shapes=[pltpu.VMEM((GLA_HEADS, GLA_DK, GLA_DV), F32)],
        compiler_params=_cparams(("arbitrary",), 32),
        name="gla",
    )(p, p, p, p, p, wa2p, ba, ng)


def _gelu(x):
    return 0.5 * x * (1.0 + lax.erf(x * 0.7071067811865476))


def _gmlp_kernel(u_ref, v_ref, lg_ref, lb_ref, ws_ref, bias_ref, ng_ref, o_ref, buf_ref, *, n_chunks):
    row = lax.broadcasted_iota(jnp.int32, (GM_CHUNK, GM_CHUNK), 0)
    col = lax.broadcasted_iota(jnp.int32, (GM_CHUNK, GM_CHUNK), 1)
    causal = col <= row
    v = _gelu(v_ref[...].astype(F32))
    mu = jnp.mean(v, axis=-1, keepdims=True)
    d = v - mu
    var = jnp.mean(d * d, axis=-1, keepdims=True)
    vb = (d * lax.rsqrt(var + 1e-5) * lg_ref[...] + lb_ref[...]).astype(BF16)
    for g in range(GM_GROUPS):
        cs = slice(GM_CH * g, GM_CH * (g + 1))
        w = jnp.where(causal, ws_ref[g], 0.0).astype(BF16)
        for c in range(n_chunks):
            rs = slice(GM_CHUNK * c, GM_CHUNK * (c + 1))
            mixed = _dot(w, vb[rs, cs]) + bias_ref[:, cs]
            buf_ref[rs, cs] = _gelu(u_ref[rs, cs].astype(F32)) * mixed
    out = buf_ref[...]
    ms = jnp.mean(out * out, axis=-1, keepdims=True)
    o_ref[...] = (out * lax.rsqrt(ms + 1e-6) * ng_ref[...]).astype(o_ref.dtype)


def _gmlp(p, lg, lb, ws, bias, ng):
    s = p.shape[0]
    t = min(256, s)
    col = lambda off: pl.BlockSpec((t, GROUP_W), lambda i: (i, off // GROUP_W))
    full = lambda shape: pl.BlockSpec(shape, lambda i: (0,) * len(shape))
    return pl.pallas_call(
        functools.partial(_gmlp_kernel, n_chunks=t // GM_CHUNK),
        grid=(s // t,),
        in_specs=[col(COL_U), col(COL_VG), full(lg.shape), full(lb.shape), full(ws.shape), full(bias.shape),
                  full(ng.shape)],
        out_specs=pl.BlockSpec((t, GROUP_W), lambda i: (i, 0)),
        out_shape=jax.ShapeDtypeStruct((s, GROUP_W), BF16),
        scratch_shapes=[pltpu.VMEM((t, GROUP_W), F32)],
        compiler_params=_cparams(("parallel",), 32),
        name="gmlp",
    )(p, p, lg, lb, ws, bias, ng)


CONV_HALO = 8


def _sconv_kernel(b_ref, c_ref, h_ref, w_ref, ng_ref, o_ref, z_ref):
    t = b_ref.shape[0]

    @pl.when(pl.program_id(0) == 0)
    def _():
        z_ref[0:CONV_HALO, :] = jnp.zeros((CONV_HALO, GROUP_W), F32)

    z = c_ref[...].astype(F32) * h_ref[...].astype(F32)
    z_ref[CONV_HALO:CONV_HALO + t, :] = z
    z1 = z_ref[CONV_HALO - 1:CONV_HALO - 1 + t, :]
    z2 = z_ref[CONV_HALO - 2:CONV_HALO - 2 + t, :]
    y = w_ref[0:1, :] * z2 + w_ref[1:2, :] * z1 + w_ref[2:3, :] * z
    z_ref[0:CONV_HALO, :] = z[t - CONV_HALO:t, :]
    out = b_ref[...].astype(F32) * y
    ms = jnp.mean(out * out, axis=-1, keepdims=True)
    o_ref[...] = (out * lax.rsqrt(ms + 1e-6) * ng_ref[...]).astype(o_ref.dtype)


def _sconv(p, w, ng):
    s = p.shape[0]
    t = min(256, s)
    col = lambda off: pl.BlockSpec((t, GROUP_W), lambda i: (i, off // GROUP_W))
    full = lambda shape: pl.BlockSpec(shape, lambda i: (0,) * len(shape))
    return pl.pallas_call(
        _sconv_kernel,
        grid=(s // t,),
        in_specs=[col(COL_CB), col(COL_CC), col(COL_CH), full(w.shape), full(ng.shape)],
        out_specs=pl.BlockSpec((t, GROUP_W), lambda i: (i, 0)),
        out_shape=jax.ShapeDtypeStruct((s, GROUP_W), BF16),
        scratch_shapes=[pltpu.VMEM((t + CONV_HALO, GROUP_W), F32)],
        compiler_params=_cparams(("arbitrary",), 32),
        name="sconv",
    )(p, p, p, w, ng)


def _mla_proj_kernel(cq_ref, ckv_ref, kr_ref, krs_ref, qn_ref, kvn_ref, wq_ref, wqs_ref, wk_ref, wv_ref,
                     cos_ref, sin_ref, q_out, k_out, v_out):
    def rms(ref, g_ref):
        t = ref[...].astype(F32)
        return (t * lax.rsqrt(jnp.mean(t * t, axis=-1, keepdims=True) + 1e-6) * g_ref[...]).astype(BF16)

    cqn = rms(cq_ref, qn_ref)
    ckvn = rms(ckv_ref, kvn_ref)
    qm = _dot(cqn, wq_ref[...])
    qsw = _dot(cqn, wqs_ref[...])
    kn = _dot(ckvn, wk_ref[...])
    vv = _dot(ckvn, wv_ref[...])
    cos = cos_ref[...]
    sin = sin_ref[...]
    scale = (MLA_NOPE + MLA_ROPE) ** -0.5 * LOG2_E
    kr_rot = (kr_ref[...].astype(F32) * cos + krs_ref[...].astype(F32) * sin).astype(k_out.dtype)
    ones_col = jnp.where(lax.broadcasted_iota(jnp.int32, (cos.shape[0], LANE), 1) == 0, 1.0, 0.0).astype(v_out.dtype)
    for h in range(MLA_HEADS):
        lo = MLA_QK_PAD * h
        v_out[:, lo:lo + LANE] = vv[:, LANE * h:LANE * (h + 1)].astype(v_out.dtype)
        v_out[:, lo + LANE:lo + 2 * LANE] = ones_col
        q_out[:, lo:lo + LANE] = (qm[:, lo:lo + LANE] * scale).astype(q_out.dtype)
        q_out[:, lo + LANE:lo + 2 * LANE] = (
            (qm[:, lo + LANE:lo + 2 * LANE] * cos + qsw[:, LANE * h:LANE * (h + 1)] * sin) * scale
        ).astype(q_out.dtype)
        k_out[:, lo:lo + LANE] = kn[:, LANE * h:LANE * (h + 1)].astype(k_out.dtype)
        k_out[:, lo + LANE:lo + 2 * LANE] = kr_rot


def _mla_proj(p, qn, kvn, wq, wqs, wk, wv, cos, sin):
    s = p.shape[0]
    t = min(256, s)
    full = lambda shape: pl.BlockSpec(shape, lambda i: (0,) * len(shape))
    rowb = lambda width: pl.BlockSpec((t, width), lambda i: (i, 0))
    qk_w = MLA_HEADS * MLA_QK_PAD
    return pl.pallas_call(
        _mla_proj_kernel,
        grid=(s // t,),
        in_specs=[pl.BlockSpec((t, MLA_Q_LORA), lambda i: (i, COL_CQ // MLA_Q_LORA)),
                  pl.BlockSpec((t, MLA_KV_LORA), lambda i: (i, COL_CKV // MLA_KV_LORA)),
                  pl.BlockSpec((t, LANE), lambda i: (i, COL_KR // LANE)),
                  pl.BlockSpec((t, LANE), lambda i: (i, COL_KRS // LANE)),
                  full(qn.shape), full(kvn.shape), full(wq.shape), full(wqs.shape), full(wk.shape), full(wv.shape),
                  rowb(LANE), rowb(LANE)],
        out_specs=[rowb(qk_w), rowb(qk_w), rowb(qk_w)],
        out_shape=[jax.ShapeDtypeStruct((s, qk_w), BF16)] * 3,
        compiler_params=_cparams(("parallel",), 40),
        name="mla_proj",
    )(p, p, p, p, qn, kvn, wq, wqs, wk, wv, cos, sin)


FLASH_TQ = 512
FLASH_TK = 1024


def _flash_kernel(qi_ref, ki_ref, q_ref, k_ref, v_ref, ng_ref, o_ref, acc_ref, m_ref, *, tq, tk):
    step = pl.program_id(0)
    qi = qi_ref[step]
    ki = ki_ref[step]
    last_ki = (qi * tq) // tk

    @pl.when(ki == 0)
    def _():
        m_ref[...] = jnp.full(m_ref.shape, -jnp.inf, F32)
        acc_ref[...] = jnp.zeros_like(acc_ref)

    def accumulate(masked, k0, kn):
        keys = slice(k0, k0 + kn)
        if masked:
            row = qi * tq + lax.broadcasted_iota(jnp.int32, (tq, kn), 0)
            col = ki * tk + k0 + lax.broadcasted_iota(jnp.int32, (tq, kn), 1)
            visible = col <= row

        def scores(h):
            hs = slice(MLA_QK_PAD * h, MLA_QK_PAD * (h + 1))
            return _dot_nt(q_ref[:, hs], k_ref[keys, hs])

        sc = scores(0)
        for h in range(MLA_HEADS):
            hs = slice(MLA_QK_PAD * h, MLA_QK_PAD * (h + 1))
            sc_next = scores(h + 1) if h + 1 < MLA_HEADS else None
            if masked:
                sc = jnp.where(visible, sc, -jnp.inf)
            m_old = m_ref[h]
            m_new = jnp.maximum(m_old, jnp.max(sc, axis=-1, keepdims=True))
            pr = jnp.exp2(sc - m_new).astype(BF16)
            acc_ref[:, hs] = jnp.exp2(m_old - m_new) * acc_ref[:, hs] + _dot(pr, v_ref[keys, hs])
            m_ref[h] = m_new
            sc = sc_next

    @pl.when(ki < last_ki)
    def _():
        accumulate(False, 0, tk)

    @pl.when(ki == last_ki)
    def _():
        accumulate(True, 0, tk)
        outs = []
        for h in range(MLA_HEADS):
            lo = MLA_QK_PAD * h
            outs.append(acc_ref[:, lo:lo + MLA_V] / acc_ref[:, lo + MLA_V:lo + MLA_V + 1])
        o = jnp.concatenate(outs, axis=1)
        ms = jnp.mean(o * o, axis=-1, keepdims=True)
        o_ref[...] = (o * lax.rsqrt(ms + 1e-6) * ng_ref[...]).astype(o_ref.dtype)


def _flash(q, k, v, ng):
    s = q.shape[0]
    tk = min(FLASH_TK, s)
    tq = min(FLASH_TQ, tk)
    nq = s // tq
    pairs = [(a, b) for a in range(nq) for b in range((a * tq) // tk + 1)]
    qi_tab = jnp.asarray([a for a, _ in pairs], jnp.int32)
    ki_tab = jnp.asarray([b for _, b in pairs], jnp.int32)
    qk_w = MLA_HEADS * MLA_QK_PAD
    grid_spec = pltpu.PrefetchScalarGridSpec(
        num_scalar_prefetch=2,
        grid=(len(pairs),),
        in_specs=[pl.BlockSpec((tq, qk_w), lambda t, qi, ki: (qi[t], 0)),
                  pl.BlockSpec((tk, qk_w), lambda t, qi, ki: (ki[t], 0)),
                  pl.BlockSpec((tk, qk_w), lambda t, qi, ki: (ki[t], 0)),
                  pl.BlockSpec((1, GROUP_W), lambda t, qi, ki: (0, 0))],
        out_specs=pl.BlockSpec((tq, GROUP_W), lambda t, qi, ki: (qi[t], 0)),
        scratch_shapes=[pltpu.VMEM((tq, qk_w), F32),
                        pltpu.VMEM((MLA_HEADS, tq, 1), F32)],
    )
    return pl.pallas_call(
        functools.partial(_flash_kernel, tq=tq, tk=tk),
        grid_spec=grid_spec,
        out_shape=jax.ShapeDtypeStruct((s, GROUP_W), BF16),
        compiler_params=_cparams(("arbitrary",), 56),
        name="flash",
    )(qi_tab, ki_tab, q, k, v, ng)


def _mm_out_kernel(a0_ref, a1_ref, a2_ref, a3_ref, w_ref, x_ref, o_ref):
    acc = _dot(a0_ref[...], w_ref[0, 0:GROUP_W, :])
    acc = acc + _dot(a1_ref[...], w_ref[0, GROUP_W:2 * GROUP_W, :])
    acc = acc + _dot(a2_ref[...], w_ref[0, 2 * GROUP_W:3 * GROUP_W, :])
    acc = acc + _dot(a3_ref[...], w_ref[0, 3 * GROUP_W:4 * GROUP_W, :])
    o_ref[...] = ALPHA * x_ref[...] + acc


def _mm_out(a0, a1, a2, a3, w, layer, x):
    s = x.shape[0]
    tm = min(512, s)
    tn = 1024
    a_spec = pl.BlockSpec((tm, GROUP_W), lambda i, j: (i, 0))
    return pl.pallas_call(
        _mm_out_kernel,
        grid=(s // tm, D_MODEL // tn),
        in_specs=[a_spec, a_spec, a_spec, a_spec,
                  pl.BlockSpec((1, D_MODEL, tn), lambda i, j: (layer, 0, j)),
                  pl.BlockSpec((tm, tn), lambda i, j: (i, j))],
        out_specs=pl.BlockSpec((tm, tn), lambda i, j: (i, j)),
        out_shape=jax.ShapeDtypeStruct((s, D_MODEL), F32),
        compiler_params=_cparams(("parallel", "arbitrary"), 48),
        name="mm_out",
    )(a0, a1, a2, a3, w, x)


def _layer_norm(t, g, b):
    mu = jnp.mean(t, axis=-1, keepdims=True)
    d = t - mu
    var = jnp.mean(d * d, axis=-1, keepdims=True)
    return d * lax.rsqrt(var + 1e-5) * g + b


def _ln_route_kernel(y_ref, g_ref, b_ref, wh_ref, wl_ref, rb_ref, x_out, route_out, xtok_hbm, tbuf, sem, *, tm):
    i = pl.program_id(0)
    slot = lax.rem(i, 2)

    def wait_rows(buf_slot):
        pltpu.make_async_copy(tbuf.at[buf_slot], tbuf.at[buf_slot], sem.at[buf_slot]).wait()

    x1 = _layer_norm(y_ref[...], g_ref[...], b_ref[...])
    x_out[...] = x1
    x_hi = x1.astype(BF16)
    x_hi32 = x_hi.astype(F32)

    @pl.when(i >= 2)
    def _():
        wait_rows(slot)

    words = _pack_rows(x_hi32[:, :HALF_D], x_hi32[:, HALF_D:])
    for c in range(TOK_WORDS):
        tbuf[slot, c] = words[:, LANE * c:LANE * (c + 1)]
    for j in range(tm):
        pltpu.make_async_copy(tbuf.at[slot, :, pl.ds(j, 1), :], xtok_hbm.at[i * tm + j], sem.at[slot]).start()

    @pl.when(i == pl.num_programs(0) - 1)
    def _():
        wait_rows(slot)

        @pl.when(i >= 1)
        def _():
            wait_rows(1 - slot)

    x_lo = (x1 - x_hi32).astype(BF16)
    logits = _dot(x_hi, wh_ref[...]) + _dot(x_lo, wh_ref[...]) + _dot(x_hi, wl_ref[...]) + rb_ref[...]
    lane = lax.broadcasted_iota(jnp.int32, logits.shape, 1).astype(F32)
    neg = -jnp.inf
    big = 1e9
    is_grp = lane < MOE_GROUPS
    lg = jnp.where(is_grp, logits, neg)
    mg = jnp.max(lg, axis=-1, keepdims=True)
    gsel = jnp.min(jnp.where(lg == mg, lane, big), axis=-1, keepdims=True)
    pg_sel = 1.0 / jnp.sum(jnp.where(is_grp, jnp.exp(lg - mg), 0.0), axis=-1, keepdims=True)
    lo = MOE_GROUPS + MOE_PER_GROUP * gsel
    le = jnp.where((lane >= lo) & (lane < lo + MOE_PER_GROUP), logits, neg)
    v1 = jnp.max(le, axis=-1, keepdims=True)
    i1 = jnp.min(jnp.where(le == v1, lane, big), axis=-1, keepdims=True)
    le2 = jnp.where(lane == i1, neg, le)
    v2 = jnp.max(le2, axis=-1, keepdims=True)
    i2 = jnp.min(jnp.where(le2 == v2, lane, big), axis=-1, keepdims=True)
    e = jnp.exp(v2 - v1)
    w1 = pg_sel / (1.0 + e)
    w2 = pg_sel * e / (1.0 + e)
    route = jnp.where(lane == 0.0, i1 - MOE_GROUPS,
                      jnp.where(lane == 1.0, i2 - MOE_GROUPS,
                                jnp.where(lane == 2.0, w1, jnp.where(lane == 3.0, w2, 0.0))))
    route_out[...] = route


def _ln_route(y, g, b, wh, wl, rb):
    s = y.shape[0]
    tm = min(256, s)
    full = lambda shape: pl.BlockSpec(shape, lambda i: (0,) * len(shape))
    rowb = lambda width: pl.BlockSpec((tm, width), lambda i: (i, 0))
    x1, route, x1_tok = pl.pallas_call(
        functools.partial(_ln_route_kernel, tm=tm),
        grid=(s // tm,),
        in_specs=[rowb(D_MODEL), full(g.shape), full(b.shape), full(wh.shape), full(wl.shape), full(rb.shape)],
        out_specs=[rowb(D_MODEL), rowb(LANE), pl.BlockSpec(memory_space=pl.ANY)],
        out_shape=[jax.ShapeDtypeStruct((s, D_MODEL), F32),
                   jax.ShapeDtypeStruct((s, LANE), F32),
                   jax.ShapeDtypeStruct((s, TOK_WORDS, 1, LANE), U32)],
        scratch_shapes=[pltpu.VMEM((2, TOK_WORDS, tm, LANE), U32), pltpu.SemaphoreType.DMA((2,))],
        compiler_params=_cparams(("arbitrary",), 48),
        name="ln_route",
    )(y, g, b, wh, wl, rb)
    return x1, x1_tok, route


def _moe_kernel(te_ref, ne_ref, tv_ref, nt_ref, src_ref, nxt_ref, dst_ref, x_hbm, wg_hbm, wu_hbm, wd_hbm, ws_ref,
                y_hbm, gbuf, hb, obuf, stg_g, stg_u, stg_d, wg_ref, wu_ref, wd_ref, gsem, ssem, wsem):
    t = pl.program_id(0)
    n_used = nt_ref[0]
    slot = lax.rem(t, 2)

    def weight_copies(e):
        return (pltpu.make_async_copy(wg_hbm.at[e], stg_g, wsem.at[0]),
                pltpu.make_async_copy(wu_hbm.at[e], stg_u, wsem.at[1]),
                pltpu.make_async_copy(wd_hbm.at[e], stg_d, wsem.at[2]))

    def round_weights(stg, dst, rows):
        def body(r, carry):
            rs = pl.ds(pl.multiple_of(r * rows, rows), rows)
            dst[rs, :] = stg[rs, :].astype(BF16)
            return carry
        lax.fori_loop(0, stg.shape[0] // rows, body, 0)

    @pl.when(t == 0)
    def _():
        for cp in weight_copies(te_ref[0]):
            cp.start()

    @pl.when(t < n_used)
    def _():
        expert = te_ref[t]
        first_tile_of_expert = jnp.logical_or(t == 0, expert != te_ref[jnp.maximum(t - 1, 0)])

        @pl.when(first_tile_of_expert)
        def _():
            for cp in weight_copies(expert):
                cp.wait()
            round_weights(stg_g, wg_ref, WEIGHT_ROUND_ROWS)
            round_weights(stg_u, wu_ref, WEIGHT_ROUND_ROWS)
            round_weights(stg_d, wd_ref, WEIGHT_ROUND_ROWS * D_EXPERT // D_MODEL)

            @pl.when(ne_ref[t] >= 0)
            def _():
                for cp in weight_copies(ne_ref[t]):
                    cp.start()

    def gather_row(idx_ref, buf_slot, i):
        return pltpu.make_async_copy(x_hbm.at[idx_ref[i]], gbuf.at[buf_slot, :, pl.ds(i, 1), :], gsem.at[buf_slot])

    def wait_gather(buf_slot):
        pltpu.make_async_copy(gbuf.at[buf_slot], gbuf.at[buf_slot], gsem.at[buf_slot]).wait()

    def scatter_row(i):
        return pltpu.make_async_copy(obuf.at[:, pl.ds(i, 1), :], y_hbm.at[dst_ref[i]], ssem)

    def start_scatter(n_rows):
        def group(g, carry):
            for u in range(SCATTER_UNROLL):
                scatter_row(g * SCATTER_UNROLL + u).start()
            return carry

        def single(i, carry):
            scatter_row(i).start()
            return carry
        n_groups = n_rows // SCATTER_UNROLL
        lax.fori_loop(0, n_groups, group, 0)
        lax.fori_loop(n_groups * SCATTER_UNROLL, n_rows, single, 0)

    def wait_scatter(n_rows):
        rows = obuf.at[:, pl.ds(0, n_rows), :]
        pltpu.make_async_copy(rows, rows, ssem).wait()

    @pl.when(t == 0)
    def _():
        def body(i, carry):
            gather_row(src_ref, 0, i).start()
            return carry
        lax.fori_loop(0, MOE_TILE, body, 0, unroll=8)

    @pl.when(t < n_used)
    def _():
        wait_gather(slot)
        for c in range(TOK_WORDS):
            lo, hi = _unpack_rows(gbuf[slot, c])
            hb[:, LANE * c:LANE * (c + 1)] = lo.astype(BF16)
            hb[:, HALF_D + LANE * c:HALF_D + LANE * (c + 1)] = hi.astype(BF16)
        for i in range(MOE_TILE):
            gather_row(nxt_ref, 1 - slot, i).start()
        h = hb[...]
        gate = _dot(h, wg_ref[...])
        up = _dot(h, wu_ref[...])
        a = (gate / (1.0 + jnp.exp(-gate)) * up).astype(BF16)

        @pl.when(t > 0)
        def _():
            wait_scatter(tv_ref[jnp.maximum(t - 1, 0)])

        ws = ws_ref[...]
        rounded = lambda v: v.astype(BF16).astype(F32)
        for c2 in range(TOK_WORDS // 2):
            cols = slice(2 * LANE * c2, 2 * LANE * (c2 + 1))
            lo = _dot(a, wd_ref[:, cols]) * ws
            hi = _dot(a, wd_ref[:, HALF_D + cols.start:HALF_D + cols.stop]) * ws
            words = _pack_rows(rounded(lo), rounded(hi))
            obuf[2 * c2] = words[:, :LANE]
            obuf[2 * c2 + 1] = words[:, LANE:]
        start_scatter(tv_ref[t])

        @pl.when(t == n_used - 1)
        def _():
            wait_gather(1 - slot)
            wait_scatter(tv_ref[t])


def _moe(tile_expert, next_expert, tile_valid, n_tiles, src_token, dst_row, x_tok, wg, wu, wd, w_slot):
    p_rows = src_token.shape[0]
    tm = MOE_TILE
    n_blocks = p_rows // tm
    smem_blk = lambda fn: pl.BlockSpec((tm,), fn, memory_space=pltpu.SMEM)
    hbm = pl.BlockSpec(memory_space=pl.ANY)
    grid_spec = pltpu.PrefetchScalarGridSpec(
        num_scalar_prefetch=4,
        grid=(n_blocks,),
        in_specs=[smem_blk(lambda t, te, ne, tv, nt: (t,)),
                  smem_blk(lambda t, te, ne, tv, nt: (jnp.minimum(t + 1, n_blocks - 1),)),
                  smem_blk(lambda t, te, ne, tv, nt: (t,)),
                  hbm, hbm, hbm, hbm,
                  pl.BlockSpec((tm, 1), lambda t, te, ne, tv, nt: (t, 0))],
        out_specs=hbm,
        scratch_shapes=[pltpu.VMEM((2, TOK_WORDS, tm, LANE), U32),
                        pltpu.VMEM((tm, D_MODEL), BF16),
                        pltpu.VMEM((TOK_WORDS, tm, LANE), U32),
                        pltpu.VMEM((D_MODEL, D_EXPERT), F32),
                        pltpu.VMEM((D_MODEL, D_EXPERT), F32),
                        pltpu.VMEM((D_EXPERT, D_MODEL), F32),
                        pltpu.VMEM((D_MODEL, D_EXPERT), BF16),
                        pltpu.VMEM((D_MODEL, D_EXPERT), BF16),
                        pltpu.VMEM((D_EXPERT, D_MODEL), BF16),
                        pltpu.SemaphoreType.DMA((2,)),
                        pltpu.SemaphoreType.DMA(()),
                        pltpu.SemaphoreType.DMA((3,))],
    )
    return pl.pallas_call(
        _moe_kernel,
        grid_spec=grid_spec,
        out_shape=jax.ShapeDtypeStruct((2 * x_tok.shape[0], TOK_WORDS, 1, LANE), U32),
        compiler_params=_cparams(("arbitrary",), 58),
        name="moe",
    )(tile_expert, next_expert, tile_valid, n_tiles, src_token, src_token, dst_row, x_tok, wg, wu, wd, w_slot)


def _ln_add_kernel(x_ref, g_ref, b_ref, y_hbm, o_ref, ybuf, vbuf, sem, *, tm):
    i = pl.program_id(0)
    slot = lax.rem(i, 2)

    def start_gather(tile, buf_slot):
        base = tile * (2 * tm)

        def body(j, carry):
            for k in range(2):
                pltpu.make_async_copy(y_hbm.at[base + 2 * j + k], ybuf.at[buf_slot, k, :, pl.ds(j, 1), :],
                                      sem.at[buf_slot]).start()
            return carry
        lax.fori_loop(0, tm, body, 0, unroll=4)

    @pl.when(i == 0)
    def _():
        start_gather(0, 0)

    @pl.when(i + 1 < pl.num_programs(0))
    def _():
        start_gather(i + 1, 1 - slot)

    pltpu.make_async_copy(ybuf.at[slot], ybuf.at[slot], sem.at[slot]).wait()
    for c in range(TOK_WORDS):
        a_lo, a_hi = _unpack_rows(ybuf[slot, 0, c])
        b_lo, b_hi = _unpack_rows(ybuf[slot, 1, c])
        lo = slice(LANE * c, LANE * (c + 1))
        hi = slice(HALF_D + LANE * c, HALF_D + LANE * (c + 1))
        vbuf[:, lo] = ALPHA * x_ref[:, lo] + (a_lo + b_lo)
        vbuf[:, hi] = ALPHA * x_ref[:, hi] + (a_hi + b_hi)
    o_ref[...] = _layer_norm(vbuf[...], g_ref[...], b_ref[...])


def _ln_add(x, y_tok, g, b):
    s = x.shape[0]
    tm = min(256, s)
    full = lambda shape: pl.BlockSpec(shape, lambda i: (0,) * len(shape))
    return pl.pallas_call(
        functools.partial(_ln_add_kernel, tm=tm),
        grid=(s // tm,),
        in_specs=[pl.BlockSpec((tm, D_MODEL), lambda i: (i, 0)), full(g.shape), full(b.shape),
                  pl.BlockSpec(memory_space=pl.ANY)],
        out_specs=pl.BlockSpec((tm, D_MODEL), lambda i: (i, 0)),
        out_shape=jax.ShapeDtypeStruct((s, D_MODEL), F32),
        scratch_shapes=[pltpu.VMEM((2, 2, TOK_WORDS, tm, LANE), U32),
                        pltpu.VMEM((tm, D_MODEL), F32),
                        pltpu.SemaphoreType.DMA((2,))],
        compiler_params=_cparams(("arbitrary",), 48),
        name="ln_add",
    )(x, g, b, y_tok)


def _w_in_sources():
    src = {}
    o = 0
    for name, width in (("q", 512), ("k", 512), ("v", 1024), ("r", 1024), ("gl", GLA_GATE_RANK), ("u", 1024),
                        ("vg", 1024), ("cb", 1024), ("cc", 1024), ("ch", 1024), ("cq", MLA_Q_LORA),
                        ("ckv", MLA_KV_LORA), ("kr", MLA_ROPE)):
        src[name] = (o, width)
        o += width
    return src, o


PACK_COLS = 512


def _w_in_segments():
    src, _ = _w_in_sources()
    half = MLA_ROPE // 2
    kr0 = src["kr"][0]
    segs = [(dst, src[name][0], src[name][1])
            for name, dst in (("cq", COL_CQ), ("ckv", COL_CKV), ("q", COL_Q), ("k", COL_K), ("v", COL_V),
                              ("r", COL_R), ("u", COL_U), ("vg", COL_VG), ("cb", COL_CB), ("cc", COL_CC),
                              ("ch", COL_CH))]
    segs += [(COL_KR, kr0, MLA_ROPE), (COL_KR + MLA_ROPE, None, LANE - MLA_ROPE),
             (COL_KRS, kr0 + half, half), (COL_KRS + half, kr0, half), (COL_KRS + MLA_ROPE, None, LANE - MLA_ROPE),
             (COL_GL, src["gl"][0], GLA_GATE_RANK), (COL_GL + GLA_GATE_RANK, None, P_COLS - COL_GL - GLA_GATE_RANK)]
    return sorted(segs)


def _pack_w_in_kernel(wt_hbm, o_hbm, stage, tbuf, in_sem, out_sem, *, layer):
    n_blocks = P_COLS // PACK_COLS
    segs = _w_in_segments()

    def pieces(j):
        lo, hi = j * PACK_COLS, (j + 1) * PACK_COLS
        out = []
        for dst, s0, width in segs:
            a, b = max(dst, lo), min(dst + width, hi)
            if a < b:
                out.append((a - lo, None if s0 is None else s0 + (a - dst), b - a))
        return out

    def loads(j, slot):
        return [pltpu.make_async_copy(wt_hbm.at[layer, pl.ds(s0, n), :], stage.at[slot, pl.ds(off, n), :],
                                      in_sem.at[slot])
                for off, s0, n in pieces(j) if s0 is not None]

    def store(j, slot):
        return pltpu.make_async_copy(tbuf.at[slot], o_hbm.at[:, pl.ds(j * PACK_COLS, PACK_COLS)], out_sem.at[slot])

    for cp in loads(0, 0):
        cp.start()
    for j in range(n_blocks):
        slot = j % 2
        if j + 1 < n_blocks:
            for cp in loads(j + 1, 1 - slot):
                cp.start()
        for cp in loads(j, slot):
            cp.wait()
        for off, s0, n in pieces(j):
            if s0 is None:
                stage[slot, off:off + n, :] = jnp.zeros((n, D_MODEL), F32)
        if j >= 2:
            store(j - 2, slot).wait()
        tbuf[slot] = jnp.transpose(stage[slot]).astype(BF16)
        store(j, slot).start()
    for j in range(max(n_blocks - 2, 0), n_blocks):
        store(j, j % 2).wait()


def _pack_w_in(wt, layer):
    _, d_in, k = wt.shape
    assert d_in == _w_in_sources()[1] and k == D_MODEL
    return pl.pallas_call(
        functools.partial(_pack_w_in_kernel, layer=layer),
        in_specs=[pl.BlockSpec(memory_space=pl.ANY)],
        out_specs=pl.BlockSpec(memory_space=pl.ANY),
        out_shape=jax.ShapeDtypeStruct((k, P_COLS), BF16),
        scratch_shapes=[pltpu.VMEM((2, PACK_COLS, D_MODEL), F32),
                        pltpu.VMEM((2, D_MODEL, PACK_COLS), BF16),
                        pltpu.SemaphoreType.DMA((2,)),
                        pltpu.SemaphoreType.DMA((2,))],
        compiler_params=pltpu.CompilerParams(vmem_limit_bytes=40 * MIB),
        name="pack_w_in",
    )(wt)


def _pack_mla_weights(wuq, wukv):
    half = MLA_ROPE // 2
    wq3 = wuq.reshape(MLA_Q_LORA, MLA_HEADS, MLA_NOPE + MLA_ROPE)
    rope = wq3[:, :, MLA_NOPE:]
    zq = jnp.zeros((MLA_Q_LORA, MLA_HEADS, MLA_QK_PAD - MLA_NOPE - MLA_ROPE), wuq.dtype)
    wq = jnp.concatenate([wq3, zq], axis=2).reshape(MLA_Q_LORA, MLA_HEADS * MLA_QK_PAD)
    zs = jnp.zeros((MLA_Q_LORA, MLA_HEADS, LANE - MLA_ROPE), wuq.dtype)
    wqs = jnp.concatenate([rope[:, :, half:], rope[:, :, :half], zs], axis=2).reshape(MLA_Q_LORA, MLA_HEADS * LANE)
    wkv3 = wukv.reshape(MLA_KV_LORA, MLA_HEADS, MLA_NOPE + MLA_V)
    wk = wkv3[:, :, :MLA_NOPE].reshape(MLA_KV_LORA, MLA_HEADS * MLA_NOPE)
    wv = wkv3[:, :, MLA_NOPE:].reshape(MLA_KV_LORA, MLA_HEADS * MLA_V)
    return wq.astype(BF16), wqs.astype(BF16), wk.astype(BF16), wv.astype(BF16)


def _rope_tables(s):
    pos = jnp.arange(s, dtype=F32)
    inv_freq = ROPE_BASE ** (-jnp.arange(0, MLA_ROPE, 2, dtype=F32) / MLA_ROPE)
    ang = pos[:, None] * inv_freq[None, :]
    cos, sin = jnp.cos(ang), jnp.sin(ang)
    z = jnp.zeros((s, LANE - MLA_ROPE), F32)
    return jnp.concatenate([cos, cos, z], axis=1), jnp.concatenate([-sin, sin, z], axis=1)


def _dispatch_plan(route, s):
    n_assign = 2 * s
    p_rows = n_assign + N_EXPERTS * MOE_TILE
    eid = route[:, 0:2].astype(jnp.int32).reshape(n_assign)
    wgt = route[:, 2:4].reshape(n_assign)
    onehot = (eid[:, None] == jnp.arange(N_EXPERTS, dtype=jnp.int32)[None, :]).astype(jnp.int32)
    csum = jnp.cumsum(onehot, axis=0)
    rank = jnp.sum(onehot * (csum - 1), axis=1)
    counts = csum[-1]
    tiles_e = (counts + MOE_TILE - 1) // MOE_TILE
    tile_end = jnp.cumsum(tiles_e)
    tile_start = tile_end - tiles_e
    slot = tile_start[eid] * MOE_TILE + rank
    slot_assign = jnp.full((p_rows,), -1, jnp.int32).at[slot].set(jnp.arange(n_assign, dtype=jnp.int32))
    is_pad = slot_assign < 0
    src_token = jnp.where(is_pad, 0, slot_assign // 2)
    dst_row = jnp.maximum(slot_assign, 0)
    w_slot = jnp.where(is_pad, 0.0, wgt[dst_row])
    n_blocks = p_rows // MOE_TILE
    tile_valid = jnp.sum(jnp.logical_not(is_pad).reshape(n_blocks, MOE_TILE).astype(jnp.int32), axis=1)
    tile_ids = jnp.arange(n_blocks, dtype=jnp.int32)
    tile_expert = jnp.minimum(jnp.sum((tile_end[None, :] <= tile_ids[:, None]).astype(jnp.int32), axis=1),
                              N_EXPERTS - 1)
    n_tiles = tile_end[-1:].astype(jnp.int32)
    experts = jnp.arange(N_EXPERTS, dtype=jnp.int32)
    later_with_tiles = (experts[None, :] > experts[:, None]) & (tiles_e[None, :] > 0)
    next_nonempty = jnp.min(jnp.where(later_with_tiles, experts[None, :], N_EXPERTS), axis=1)
    next_expert = next_nonempty[tile_expert]
    return src_token, dst_row, w_slot.reshape(p_rows, 1), tile_expert, next_expert, tile_valid, n_tiles


def kernel(x, w_in, gla_wa2, gla_ba, gla_norm, gm_ln_g, gm_ln_b, gm_ws, gm_bs, gm_norm, sc_conv, sc_norm, mla_q_norm, mla_kv_norm, mla_wuq, mla_wukv, mla_norm, w_o, ln1_g, ln1_b, router_g_w, router_g_b, router_e_w, router_e_b, exp_w_gate, exp_w_up, exp_w_down, ln2_g, ln2_b):
    bsz, s, _ = x.shape
    assert bsz == 1
    xc = x.reshape(s, D_MODEL)
    cos_t, sin_t = _rope_tables(s)
    row = lambda v: v.reshape(1, -1)
    wg_all = exp_w_gate.reshape(DEPTH * N_EXPERTS, D_MODEL, D_EXPERT)
    wu_all = exp_w_up.reshape(DEPTH * N_EXPERTS, D_MODEL, D_EXPERT)
    wd_all = exp_w_down.reshape(DEPTH * N_EXPERTS, D_EXPERT, D_MODEL)
    w_o_b = w_o.astype(BF16)
    w_in_t = jnp.swapaxes(w_in, 1, 2)
    for l in range(DEPTH):
        p = _mm_in(xc, _pack_w_in(w_in_t, l))

        wa2p = jnp.concatenate(
            [gla_wa2[l], jnp.zeros((LANE - GLA_GATE_RANK, GLA_HEADS * GLA_DK), F32)], axis=0).astype(BF16)
        out_a = _gla(p, wa2p, row(gla_ba[l]), row(gla_norm[l]))

        gm_bias = jnp.repeat(gm_bs[l].T, GM_CH, axis=1)
        out_b = _gmlp(p, row(gm_ln_g[l]), row(gm_ln_b[l]), gm_ws[l], gm_bias, row(gm_norm[l]))

        out_c = _sconv(p, sc_conv[l], row(sc_norm[l]))

        wq, wqs, wk, wv = _pack_mla_weights(mla_wuq[l], mla_wukv[l])
        q_att, k_att, v_att = _mla_proj(p, row(mla_q_norm[l]), row(mla_kv_norm[l]), wq, wqs, wk, wv, cos_t, sin_t)
        out_d = _flash(q_att, k_att, v_att, row(mla_norm[l]))

        y = _mm_out(out_a, out_b, out_c, out_d, w_o_b, l, xc)

        wr = jnp.concatenate([router_g_w[l], router_e_w[l],
                              jnp.zeros((D_MODEL, LANE - MOE_GROUPS - N_EXPERTS), F32)], axis=1)
        wr_hi = wr.astype(BF16)
        wr_lo = (wr - wr_hi.astype(F32)).astype(BF16)
        rb = jnp.concatenate([router_g_b[l], router_e_b[l], jnp.zeros((LANE - MOE_GROUPS - N_EXPERTS,), F32)])
        x1, x1_tok, route = _ln_route(y, row(ln1_g[l]), row(ln1_b[l]), wr_hi, wr_lo, row(rb))

        src_token, dst_row, w_slot, tile_expert, next_expert, tile_valid, n_tiles = _dispatch_plan(route, s)
        next_expert = jnp.where(next_expert < N_EXPERTS, next_expert + l * N_EXPERTS, -1)
        y_tok = _moe(tile_expert + l * N_EXPERTS, next_expert, tile_valid, n_tiles, src_token, dst_row, x1_tok,
                     wg_all, wu_all, wd_all, w_slot)
        xc = _ln_add(x1, y_tok, row(ln2_g[l]), row(ln2_b[l]))
    return xc.reshape(bsz, s, D_MODEL)
```

```python
import functools

import jax
import jax.numpy as jnp
from jax import lax
from jax.experimental import pallas as pl
from jax.experimental.pallas import tpu as pltpu

F32 = jnp.float32
BF16 = jnp.bfloat16

D_MODEL = 4096
DEPTH = 2
GROUP_W = 1024

GLA_HEADS = 4
GLA_DK = 128
GLA_DV = 256
GLA_GATE_RANK = 16
GLA_TAU = 16.0
GLA_CHUNK = 64
GLA_SUB = 16

GM_GROUPS = 8
GM_CH = 128
GM_CHUNK = 128

MLA_HEADS = 8
MLA_NOPE = 128
MLA_ROPE = 64
MLA_V = 128
MLA_Q_LORA = 768
MLA_KV_LORA = 256
ROPE_BASE = 10000.0
MLA_QK_PAD = 256

MOE_GROUPS = 4
MOE_PER_GROUP = 8
N_EXPERTS = 32
D_EXPERT = 512

ALPHA = (2.0 * DEPTH) ** 0.25
LOG2_E = 1.4426950408889634

LANE = 128
MIB = 1024 * 1024

COL_CQ = 0
COL_CKV = 768
COL_Q = 1024
COL_K = 1536
COL_V = 2048
COL_R = 3072
COL_U = 4096
COL_VG = 5120
COL_CB = 6144
COL_CC = 7168
COL_CH = 8192
COL_KR = 9216
COL_KRS = 9344
COL_GL = 9472
P_COLS = 9728

MOE_TILE = 256
SCATTER_UNROLL = 8
WEIGHT_ROUND_ROWS = 512
HALF_D = D_MODEL // 2
TOK_WORDS = HALF_D // LANE
U32 = jnp.uint32
HIGH_HALF = 0xFFFF0000


def _pack_rows(lo_f32, hi_f32):
    return (lax.bitcast_convert_type(lo_f32, U32) >> 16) | lax.bitcast_convert_type(hi_f32, U32)


def _unpack_rows(words):
    return (lax.bitcast_convert_type(words << 16, F32),
            lax.bitcast_convert_type(words & jnp.uint32(HIGH_HALF), F32))


def _cparams(sem, vmem_mib):
    return pltpu.CompilerParams(dimension_semantics=sem, vmem_limit_bytes=vmem_mib * MIB)


def _dot(a, b):
    return jnp.dot(a, b, preferred_element_type=F32)


def _dot_nt(a, b):
    return lax.dot_general(a, b, (((1,), (1,)), ((), ())), preferred_element_type=F32)


def _dot_tn(a, b):
    return lax.dot_general(a, b, (((0,), (0,)), ((), ())), preferred_element_type=F32)


def _mm_in_kernel(x_ref, w_ref, o_ref, xb_ref):
    @pl.when(pl.program_id(1) == 0)
    def _():
        xb_ref[...] = x_ref[...].astype(BF16)

    o_ref[...] = _dot(xb_ref[...], w_ref[...]).astype(o_ref.dtype)


def _mm_in(x, w):
    s, k = x.shape
    n = w.shape[1]
    tm = min(1024, s)
    tn = 512
    return pl.pallas_call(
        _mm_in_kernel,
        grid=(s // tm, n // tn),
        in_specs=[pl.BlockSpec((tm, k), lambda i, j: (i, 0)),
                  pl.BlockSpec((k, tn), lambda i, j: (0, j))],
        out_specs=pl.BlockSpec((tm, tn), lambda i, j: (i, j)),
        out_shape=jax.ShapeDtypeStruct((s, n), BF16),
        scratch_shapes=[pltpu.VMEM((tm, k), BF16)],
        compiler_params=_cparams(("parallel", "arbitrary"), 58),
        name="mm_in",
    )(x, w)


def _gla_kernel(q_ref, k_ref, v_ref, r_ref, gl_ref, wa2_ref, ba_ref, ng_ref, o_ref, s_ref, *, n_chunks):
    c_len = GLA_CHUNK

    @pl.when(pl.program_id(0) == 0)
    def _():
        s_ref[...] = jnp.zeros_like(s_ref)

    row = lax.broadcasted_iota(jnp.int32, (c_len, c_len), 0)
    col = lax.broadcasted_iota(jnp.int32, (c_len, c_len), 1)
    tril = jnp.where(col <= row, 1.0, 0.0).astype(BF16)
    sub_row = lax.broadcasted_iota(jnp.int32, (GLA_SUB, c_len), 0)
    sub_col = lax.broadcasted_iota(jnp.int32, (GLA_SUB, c_len), 1)
    n_sub = c_len // GLA_SUB

    def chunk(c, carry):
        rows = pl.ds(pl.multiple_of(c * c_len, c_len), c_len)
        logit = _dot(gl_ref[rows, :], wa2_ref[...]) + ba_ref[...]
        g = (jnp.minimum(logit, 0.0) - jnp.log(1.0 + jnp.exp(-jnp.abs(logit)))) * (1.0 / GLA_TAU)
        g_hi = g.astype(BF16)
        g_lo = (g - g_hi.astype(F32)).astype(BF16)
        b_all = _dot(tril, g_hi) + _dot(tril, g_lo)

        for h in range(GLA_HEADS):
            hs = slice(GLA_DK * h, GLA_DK * (h + 1))
            vs = slice(GLA_DV * h, GLA_DV * (h + 1))
            bh = b_all[:, hs]
            qh = q_ref[rows, hs].astype(F32) * (GLA_DK ** -0.5)
            kh = k_ref[rows, hs].astype(F32)
            vh = v_ref[rows, vs]
            state = s_ref[h]

            o = _dot((qh * jnp.exp(bh)).astype(BF16), state.astype(BF16))

            att_rows = []
            for blk in range(n_sub):
                sl = slice(GLA_SUB * blk, GLA_SUB * (blk + 1))
                b_blk = bh[sl]
                q_blk = qh[sl]
                att = jnp.zeros((GLA_SUB, c_len), F32)
                if blk > 0:
                    ref = bh[GLA_SUB * blk:GLA_SUB * blk + 1, :]
                    qs = (q_blk * jnp.exp(b_blk - ref)).astype(BF16)
                    ks = (kh * jnp.exp(jnp.minimum(ref - bh, 0.0))).astype(BF16)
                    att = jnp.where(sub_col < GLA_SUB * blk, _dot_nt(qs, ks), 0.0)
                for jj in range(GLA_SUB):
                    j = GLA_SUB * blk + jj
                    t = q_blk * kh[j:j + 1, :] * jnp.exp(b_blk - bh[j:j + 1, :])
                    rs = jnp.sum(t, axis=-1, keepdims=True)
                    att = jnp.where((sub_col == j) & (sub_row >= jj), rs, att)
                att_rows.append(att)
            att_full = jnp.concatenate(att_rows, axis=0).astype(BF16)
            o = o + _dot(att_full, vh)

            b_last = bh[c_len - 1:c_len, :]
            kd = (kh * jnp.exp(b_last - bh)).astype(BF16)
            decay_col = jnp.transpose(jnp.broadcast_to(jnp.exp(b_last), (GLA_DK, GLA_DK)))
            s_ref[h] = state * jnp.concatenate([decay_col, decay_col], axis=1) + _dot_tn(kd, vh)

            var = jnp.mean(o * o, axis=-1, keepdims=True)
            on = o * lax.rsqrt(var + 1e-6) * ng_ref[:, vs]
            rr = r_ref[rows, vs].astype(F32)
            o_ref[rows, vs] = (on * (rr / (1.0 + jnp.exp(-rr)))).astype(o_ref.dtype)
        return carry

    lax.fori_loop(0, n_chunks, chunk, 0)


def _gla(p, wa2p, ba, ng):
    s = p.shape[0]
    t = min(256, s)
    col = lambda width, off: pl.BlockSpec((t, width), lambda i: (i, off // width))
    full = lambda shape: pl.BlockSpec(shape, lambda i: (0,) * len(shape))
    return pl.pallas_call(
        functools.partial(_gla_kernel, n_chunks=t // GLA_CHUNK),
        grid=(s // t,),
        in_specs=[col(512, COL_Q), col(512, COL_K), col(1024, COL_V), col(1024, COL_R), col(LANE, COL_GL),
                  full(wa2p.shape), full(ba.shape), full(ng.shape)],
        out_specs=pl.BlockSpec((t, GROUP_W), lambda i: (i, 0)),
        out_shape=jax.ShapeDtypeStruct((s, GROUP_W), BF16),
        scratch_shapes=[pltpu.VMEM((GLA_HEADS, GLA_DK, GLA_DV), F32)],
        compiler_params=_cparams(("arbitrary",), 32),
        name="gla",
    )(p, p, p, p, p, wa2p, ba, ng)


def _gelu(x):
    return 0.5 * x * (1.0 + lax.erf(x * 0.7071067811865476))


def _gmlp_kernel(u_ref, v_ref, lg_ref, lb_ref, ws_ref, bias_ref, ng_ref, o_ref, buf_ref, *, n_chunks):
    row = lax.broadcasted_iota(jnp.int32, (GM_CHUNK, GM_CHUNK), 0)
    col = lax.broadcasted_iota(jnp.int32, (GM_CHUNK, GM_CHUNK), 1)
    causal = col <= row
    v = _gelu(v_ref[...].astype(F32))
    mu = jnp.mean(v, axis=-1, keepdims=True)
    d = v - mu
    var = jnp.mean(d * d, axis=-1, keepdims=True)
    vb = (d * lax.rsqrt(var + 1e-5) * lg_ref[...] + lb_ref[...]).astype(BF16)
    for g in range(GM_GROUPS):
        cs = slice(GM_CH * g, GM_CH * (g + 1))
        w = jnp.where(causal, ws_ref[g], 0.0).astype(BF16)
        for c in range(n_chunks):
            rs = slice(GM_CHUNK * c, GM_CHUNK * (c + 1))
            mixed = _dot(w, vb[rs, cs]) + bias_ref[:, cs]
            buf_ref[rs, cs] = _gelu(u_ref[rs, cs].astype(F32)) * mixed
    out = buf_ref[...]
    ms = jnp.mean(out * out, axis=-1, keepdims=True)
    o_ref[...] = (out * lax.rsqrt(ms + 1e-6) * ng_ref[...]).astype(o_ref.dtype)


def _gmlp(p, lg, lb, ws, bias, ng):
    s = p.shape[0]
    t = min(256, s)
    col = lambda off: pl.BlockSpec((t, GROUP_W), lambda i: (i, off // GROUP_W))
    full = lambda shape: pl.BlockSpec(shape, lambda i: (0,) * len(shape))
    return pl.pallas_call(
        functools.partial(_gmlp_kernel, n_chunks=t // GM_CHUNK),
        grid=(s // t,),
        in_specs=[col(COL_U), col(COL_VG), full(lg.shape), full(lb.shape), full(ws.shape), full(bias.shape),
                  full(ng.shape)],
        out_specs=pl.BlockSpec((t, GROUP_W), lambda i: (i, 0)),
        out_shape=jax.ShapeDtypeStruct((s, GROUP_W), BF16),
        scratch_shapes=[pltpu.VMEM((t, GROUP_W), F32)],
        compiler_params=_cparams(("parallel",), 32),
        name="gmlp",
    )(p, p, lg, lb, ws, bias, ng)


CONV_HALO = 8


def _sconv_kernel(b_ref, c_ref, h_ref, w_ref, ng_ref, o_ref, z_ref):
    t = b_ref.shape[0]

    @pl.when(pl.program_id(0) == 0)
    def _():
        z_ref[0:CONV_HALO, :] = jnp.zeros((CONV_HALO, GROUP_W), F32)

    z = c_ref[...].astype(F32) * h_ref[...].astype(F32)
    z_ref[CONV_HALO:CONV_HALO + t, :] = z
    z1 = z_ref[CONV_HALO - 1:CONV_HALO - 1 + t, :]
    z2 = z_ref[CONV_HALO - 2:CONV_HALO - 2 + t, :]
    y = w_ref[0:1, :] * z2 + w_ref[1:2, :] * z1 + w_ref[2:3, :] * z
    z_ref[0:CONV_HALO, :] = z[t - CONV_HALO:t, :]
    out = b_ref[...].astype(F32) * y
    ms = jnp.mean(out * out, axis=-1, keepdims=True)
    o_ref[...] = (out * lax.rsqrt(ms + 1e-6) * ng_ref[...]).astype(o_ref.dtype)


def _sconv(p, w, ng):
    s = p.shape[0]
    t = min(256, s)
    col = lambda off: pl.BlockSpec((t, GROUP_W), lambda i: (i, off // GROUP_W))
    full = lambda shape: pl.BlockSpec(shape, lambda i: (0,) * len(shape))
    return pl.pallas_call(
        _sconv_kernel,
        grid=(s // t,),
        in_specs=[col(COL_CB), col(COL_CC), col(COL_CH), full(w.shape), full(ng.shape)],
        out_specs=pl.BlockSpec((t, GROUP_W), lambda i: (i, 0)),
        out_shape=jax.ShapeDtypeStruct((s, GROUP_W), BF16),
        scratch_shapes=[pltpu.VMEM((t + CONV_HALO, GROUP_W), F32)],
        compiler_params=_cparams(("arbitrary",), 32),
        name="sconv",
    )(p, p, p, w, ng)


def _mla_proj_kernel(cq_ref, ckv_ref, kr_ref, krs_ref, qn_ref, kvn_ref, wq_ref, wqs_ref, wk_ref, wv_ref,
                     cos_ref, sin_ref, q_out, k_out, v_out):
    def rms(ref, g_ref):
        t = ref[...].astype(F32)
        return (t * lax.rsqrt(jnp.mean(t * t, axis=-1, keepdims=True) + 1e-6) * g_ref[...]).astype(BF16)

    cqn = rms(cq_ref, qn_ref)
    ckvn = rms(ckv_ref, kvn_ref)
    qm = _dot(cqn, wq_ref[...])
    qsw = _dot(cqn, wqs_ref[...])
    kn = _dot(ckvn, wk_ref[...])
    vv = _dot(ckvn, wv_ref[...])
    cos = cos_ref[...]
    sin = sin_ref[...]
    scale = (MLA_NOPE + MLA_ROPE) ** -0.5 * LOG2_E
    kr_rot = (kr_ref[...].astype(F32) * cos + krs_ref[...].astype(F32) * sin).astype(k_out.dtype)
    ones_col = jnp.where(lax.broadcasted_iota(jnp.int32, (cos.shape[0], LANE), 1) == 0, 1.0, 0.0).astype(v_out.dtype)
    for h in range(MLA_HEADS):
        lo = MLA_QK_PAD * h
        v_out[:, lo:lo + LANE] = vv[:, LANE * h:LANE * (h + 1)].astype(v_out.dtype)
        v_out[:, lo + LANE:lo + 2 * LANE] = ones_col
        q_out[:, lo:lo + LANE] = (qm[:, lo:lo + LANE] * scale).astype(q_out.dtype)
        q_out[:, lo + LANE:lo + 2 * LANE] = (
            (qm[:, lo + LANE:lo + 2 * LANE] * cos + qsw[:, LANE * h:LANE * (h + 1)] * sin) * scale
        ).astype(q_out.dtype)
        k_out[:, lo:lo + LANE] = kn[:, LANE * h:LANE * (h + 1)].astype(k_out.dtype)
        k_out[:, lo + LANE:lo + 2 * LANE] = kr_rot


def _mla_proj(p, qn, kvn, wq, wqs, wk, wv, cos, sin):
    s = p.shape[0]
    t = min(256, s)
    full = lambda shape: pl.BlockSpec(shape, lambda i: (0,) * len(shape))
    rowb = lambda width: pl.BlockSpec((t, width), lambda i: (i, 0))
    qk_w = MLA_HEADS * MLA_QK_PAD
    return pl.pallas_call(
        _mla_proj_kernel,
        grid=(s // t,),
        in_specs=[pl.BlockSpec((t, MLA_Q_LORA), lambda i: (i, COL_CQ // MLA_Q_LORA)),
                  pl.BlockSpec((t, MLA_KV_LORA), lambda i: (i, COL_CKV // MLA_KV_LORA)),
                  pl.BlockSpec((t, LANE), lambda i: (i, COL_KR // LANE)),
                  pl.BlockSpec((t, LANE), lambda i: (i, COL_KRS // LANE)),
                  full(qn.shape), full(kvn.shape), full(wq.shape), full(wqs.shape), full(wk.shape), full(wv.shape),
                  rowb(LANE), rowb(LANE)],
        out_specs=[rowb(qk_w), rowb(qk_w), rowb(qk_w)],
        out_shape=[jax.ShapeDtypeStruct((s, qk_w), BF16)] * 3,
        compiler_params=_cparams(("parallel",), 40),
        name="mla_proj",
    )(p, p, p, p, qn, kvn, wq, wqs, wk, wv, cos, sin)


FLASH_TQ = 512
FLASH_TK = 1024


def _flash_kernel(qi_ref, ki_ref, q_ref, k_ref, v_ref, ng_ref, o_ref, acc_ref, m_ref, *, tq, tk):
    step = pl.program_id(0)
    qi = qi_ref[step]
    ki = ki_ref[step]
    last_ki = (qi * tq) // tk

    @pl.when(ki == 0)
    def _():
        m_ref[...] = jnp.full(m_ref.shape, -jnp.inf, F32)
        acc_ref[...] = jnp.zeros_like(acc_ref)

    def accumulate(masked, k0, kn):
        keys = slice(k0, k0 + kn)
        if masked:
            row = qi * tq + lax.broadcasted_iota(jnp.int32, (tq, kn), 0)
            col = ki * tk + k0 + lax.broadcasted_iota(jnp.int32, (tq, kn), 1)
            visible = col <= row

        def scores(h):
            hs = slice(MLA_QK_PAD * h, MLA_QK_PAD * (h + 1))
            return _dot_nt(q_ref[:, hs], k_ref[keys, hs])

        sc = scores(0)
        for h in range(MLA_HEADS):
            hs = slice(MLA_QK_PAD * h, MLA_QK_PAD * (h + 1))
            sc_next = scores(h + 1) if h + 1 < MLA_HEADS else None
            if masked:
                sc = jnp.where(visible, sc, -jnp.inf)
            m_old = m_ref[h]
            m_new = jnp.maximum(m_old, jnp.max(sc, axis=-1, keepdims=True))
            pr = jnp.exp2(sc - m_new).astype(BF16)
            acc_ref[:, hs] = jnp.exp2(m_old - m_new) * acc_ref[:, hs] + _dot(pr, v_ref[keys, hs])
            m_ref[h] = m_new
            sc = sc_next

    @pl.when(ki < last_ki)
    def _():
        accumulate(False, 0, tk)

    @pl.when(ki == last_ki)
    def _():
        accumulate(True, 0, tk)
        outs = []
        for h in range(MLA_HEADS):
            lo = MLA_QK_PAD * h
            outs.append(acc_ref[:, lo:lo + MLA_V] / acc_ref[:, lo + MLA_V:lo + MLA_V + 1])
        o = jnp.concatenate(outs, axis=1)
        ms = jnp.mean(o * o, axis=-1, keepdims=True)
        o_ref[...] = (o * lax.rsqrt(ms + 1e-6) * ng_ref[...]).astype(o_ref.dtype)


def _flash(q, k, v, ng):
    s = q.shape[0]
    tk = min(FLASH_TK, s)
    tq = min(FLASH_TQ, tk)
    nq = s // tq
    pairs = [(a, b) for a in range(nq) for b in range((a * tq) // tk + 1)]
    qi_tab = jnp.asarray([a for a, _ in pairs], jnp.int32)
    ki_tab = jnp.asarray([b for _, b in pairs], jnp.int32)
    qk_w = MLA_HEADS * MLA_QK_PAD
    grid_spec = pltpu.PrefetchScalarGridSpec(
        num_scalar_prefetch=2,
        grid=(len(pairs),),
        in_specs=[pl.BlockSpec((tq, qk_w), lambda t, qi, ki: (qi[t], 0)),
                  pl.BlockSpec((tk, qk_w), lambda t, qi, ki: (ki[t], 0)),
                  pl.BlockSpec((tk, qk_w), lambda t, qi, ki: (ki[t], 0)),
                  pl.BlockSpec((1, GROUP_W), lambda t, qi, ki: (0, 0))],
        out_specs=pl.BlockSpec((tq, GROUP_W), lambda t, qi, ki: (qi[t], 0)),
        scratch_shapes=[pltpu.VMEM((tq, qk_w), F32),
                        pltpu.VMEM((MLA_HEADS, tq, 1), F32)],
    )
    return pl.pallas_call(
        functools.partial(_flash_kernel, tq=tq, tk=tk),
        grid_spec=grid_spec,
        out_shape=jax.ShapeDtypeStruct((s, GROUP_W), BF16),
        compiler_params=_cparams(("arbitrary",), 56),
        name="flash",
    )(qi_tab, ki_tab, q, k, v, ng)


def _mm_out_kernel(a0_ref, a1_ref, a2_ref, a3_ref, w_ref, x_ref, o_ref):
    acc = _dot(a0_ref[...], w_ref[0, 0:GROUP_W, :])
    acc = acc + _dot(a1_ref[...], w_ref[0, GROUP_W:2 * GROUP_W, :])
    acc = acc + _dot(a2_ref[...], w_ref[0, 2 * GROUP_W:3 * GROUP_W, :])
    acc = acc + _dot(a3_ref[...], w_ref[0, 3 * GROUP_W:4 * GROUP_W, :])
    o_ref[...] = ALPHA * x_ref[...] + acc


def _mm_out(a0, a1, a2, a3, w, layer, x):
    s = x.shape[0]
    tm = min(512, s)
    tn = 1024
    a_spec = pl.BlockSpec((tm, GROUP_W), lambda i, j: (i, 0))
    return pl.pallas_call(
        _mm_out_kernel,
        grid=(s // tm, D_MODEL // tn),
        in_specs=[a_spec, a_spec, a_spec, a_spec,
                  pl.BlockSpec((1, D_MODEL, tn), lambda i, j: (layer, 0, j)),
                  pl.BlockSpec((tm, tn), lambda i, j: (i, j))],
        out_specs=pl.BlockSpec((tm, tn), lambda i, j: (i, j)),
        out_shape=jax.ShapeDtypeStruct((s, D_MODEL), F32),
        compiler_params=_cparams(("parallel", "arbitrary"), 48),
        name="mm_out",
    )(a0, a1, a2, a3, w, x)


def _layer_norm(t, g, b):
    mu = jnp.mean(t, axis=-1, keepdims=True)
    d = t - mu
    var = jnp.mean(d * d, axis=-1, keepdims=True)
    return d * lax.rsqrt(var + 1e-5) * g + b


def _ln_route_kernel(y_ref, g_ref, b_ref, wh_ref, wl_ref, rb_ref, x_out, route_out, xtok_hbm, tbuf, sem, *, tm):
    i = pl.program_id(0)
    slot = lax.rem(i, 2)

    def wait_rows(buf_slot):
        pltpu.make_async_copy(tbuf.at[buf_slot], tbuf.at[buf_slot], sem.at[buf_slot]).wait()

    x1 = _layer_norm(y_ref[...], g_ref[...], b_ref[...])
    x_out[...] = x1
    x_hi = x1.astype(BF16)
    x_hi32 = x_hi.astype(F32)

    @pl.when(i >= 2)
    def _():
        wait_rows(slot)

    words = _pack_rows(x_hi32[:, :HALF_D], x_hi32[:, HALF_D:])
    for c in range(TOK_WORDS):
        tbuf[slot, c] = words[:, LANE * c:LANE * (c + 1)]
    for j in range(tm):
        pltpu.make_async_copy(tbuf.at[slot, :, pl.ds(j, 1), :], xtok_hbm.at[i * tm + j], sem.at[slot]).start()

    @pl.when(i == pl.num_programs(0) - 1)
    def _():
        wait_rows(slot)

        @pl.when(i >= 1)
        def _():
            wait_rows(1 - slot)

    x_lo = (x1 - x_hi32).astype(BF16)
    logits = _dot(x_hi, wh_ref[...]) + _dot(x_lo, wh_ref[...]) + _dot(x_hi, wl_ref[...]) + rb_ref[...]
    lane = lax.broadcasted_iota(jnp.int32, logits.shape, 1).astype(F32)
    neg = -jnp.inf
    big = 1e9
    is_grp = lane < MOE_GROUPS
    lg = jnp.where(is_grp, logits, neg)
    mg = jnp.max(lg, axis=-1, keepdims=True)
    gsel = jnp.min(jnp.where(lg == mg, lane, big), axis=-1, keepdims=True)
    pg_sel = 1.0 / jnp.sum(jnp.where(is_grp, jnp.exp(lg - mg), 0.0), axis=-1, keepdims=True)
    lo = MOE_GROUPS + MOE_PER_GROUP * gsel
    le = jnp.where((lane >= lo) & (lane < lo + MOE_PER_GROUP), logits, neg)
    v1 = jnp.max(le, axis=-1, keepdims=True)
    i1 = jnp.min(jnp.where(le == v1, lane, big), axis=-1, keepdims=True)
    le2 = jnp.where(lane == i1, neg, le)
    v2 = jnp.max(le2, axis=-1, keepdims=True)
    i2 = jnp.min(jnp.where(le2 == v2, lane, big), axis=-1, keepdims=True)
    e = jnp.exp(v2 - v1)
    w1 = pg_sel / (1.0 + e)
    w2 = pg_sel * e / (1.0 + e)
    route = jnp.where(lane == 0.0, i1 - MOE_GROUPS,
                      jnp.where(lane == 1.0, i2 - MOE_GROUPS,
                                jnp.where(lane == 2.0, w1, jnp.where(lane == 3.0, w2, 0.0))))
    route_out[...] = route


def _ln_route(y, g, b, wh, wl, rb):
    s = y.shape[0]
    tm = min(256, s)
    full = lambda shape: pl.BlockSpec(shape, lambda i: (0,) * len(shape))
    rowb = lambda width: pl.BlockSpec((tm, width), lambda i: (i, 0))
    x1, route, x1_tok = pl.pallas_call(
        functools.partial(_ln_route_kernel, tm=tm),
        grid=(s // tm,),
        in_specs=[rowb(D_MODEL), full(g.shape), full(b.shape), full(wh.shape), full(wl.shape), full(rb.shape)],
        out_specs=[rowb(D_MODEL), rowb(LANE), pl.BlockSpec(memory_space=pl.ANY)],
        out_shape=[jax.ShapeDtypeStruct((s, D_MODEL), F32),
                   jax.ShapeDtypeStruct((s, LANE), F32),
                   jax.ShapeDtypeStruct((s, TOK_WORDS, 1, LANE), U32)],
        scratch_shapes=[pltpu.VMEM((2, TOK_WORDS, tm, LANE), U32), pltpu.SemaphoreType.DMA((2,))],
        compiler_params=_cparams(("arbitrary",), 48),
        name="ln_route",
    )(y, g, b, wh, wl, rb)
    return x1, x1_tok, route


def _moe_kernel(te_ref, ne_ref, tv_ref, nt_ref, src_ref, nxt_ref, dst_ref, x_hbm, wg_hbm, wu_hbm, wd_hbm, ws_ref,
                y_hbm, gbuf, hb, obuf, stg_g, stg_u, stg_d, wg_ref, wu_ref, wd_ref, gsem, ssem, wsem):
    t = pl.program_id(0)
    n_used = nt_ref[0]
    slot = lax.rem(t, 2)

    def weight_copies(e):
        return (pltpu.make_async_copy(wg_hbm.at[e], stg_g, wsem.at[0]),
                pltpu.make_async_copy(wu_hbm.at[e], stg_u, wsem.at[1]),
                pltpu.make_async_copy(wd_hbm.at[e], stg_d, wsem.at[2]))

    def round_weights(stg, dst, rows):
        def body(r, carry):
            rs = pl.ds(pl.multiple_of(r * rows, rows), rows)
            dst[rs, :] = stg[rs, :].astype(BF16)
            return carry
        lax.fori_loop(0, stg.shape[0] // rows, body, 0)

    @pl.when(t == 0)
    def _():
        for cp in weight_copies(te_ref[0]):
            cp.start()

    @pl.when(t < n_used)
    def _():
        expert = te_ref[t]
        first_tile_of_expert = jnp.logical_or(t == 0, expert != te_ref[jnp.maximum(t - 1, 0)])

        @pl.when(first_tile_of_expert)
        def _():
            for cp in weight_copies(expert):
                cp.wait()
            round_weights(stg_g, wg_ref, WEIGHT_ROUND_ROWS)
            round_weights(stg_u, wu_ref, WEIGHT_ROUND_ROWS)
            round_weights(stg_d, wd_ref, WEIGHT_ROUND_ROWS * D_EXPERT // D_MODEL)

            @pl.when(ne_ref[t] >= 0)
            def _():
                for cp in weight_copies(ne_ref[t]):
                    cp.start()

    def gather_row(idx_ref, buf_slot, i):
        return pltpu.make_async_copy(x_hbm.at[idx_ref[i]], gbuf.at[buf_slot, :, pl.ds(i, 1), :], gsem.at[buf_slot])

    def wait_gather(buf_slot):
        pltpu.make_async_copy(gbuf.at[buf_slot], gbuf.at[buf_slot], gsem.at[buf_slot]).wait()

    def scatter_row(i):
        return pltpu.make_async_copy(obuf.at[:, pl.ds(i, 1), :], y_hbm.at[dst_ref[i]], ssem)

    def start_scatter(n_rows):
        def group(g, carry):
            for u in range(SCATTER_UNROLL):
                scatter_row(g * SCATTER_UNROLL + u).start()
            return carry

        def single(i, carry):
            scatter_row(i).start()
            return carry
        n_groups = n_rows // SCATTER_UNROLL
        lax.fori_loop(0, n_groups, group, 0)
        lax.fori_loop(n_groups * SCATTER_UNROLL, n_rows, single, 0)

    def wait_scatter(n_rows):
        rows = obuf.at[:, pl.ds(0, n_rows), :]
        pltpu.make_async_copy(rows, rows, ssem).wait()

    @pl.when(t == 0)
    def _():
        def body(i, carry):
            gather_row(src_ref, 0, i).start()
            return carry
        lax.fori_loop(0, MOE_TILE, body, 0, unroll=8)

    @pl.when(t < n_used)
    def _():
        wait_gather(slot)
        for c in range(TOK_WORDS):
            lo, hi = _unpack_rows(gbuf[slot, c])
            hb[:, LANE * c:LANE * (c + 1)] = lo.astype(BF16)
            hb[:, HALF_D + LANE * c:HALF_D + LANE * (c + 1)] = hi.astype(BF16)
        for i in range(MOE_TILE):
            gather_row(nxt_ref, 1 - slot, i).start()
        h = hb[...]
        gate = _dot(h, wg_ref[...])
        up = _dot(h, wu_ref[...])
        a = (gate / (1.0 + jnp.exp(-gate)) * up).astype(BF16)

        @pl.when(t > 0)
        def _():
            wait_scatter(tv_ref[jnp.maximum(t - 1, 0)])

        ws = ws_ref[...]
        rounded = lambda v: v.astype(BF16).astype(F32)
        for c2 in range(TOK_WORDS // 2):
            cols = slice(2 * LANE * c2, 2 * LANE * (c2 + 1))
            lo = _dot(a, wd_ref[:, cols]) * ws
            hi = _dot(a, wd_ref[:, HALF_D + cols.start:HALF_D + cols.stop]) * ws
            words = _pack_rows(rounded(lo), rounded(hi))
            obuf[2 * c2] = words[:, :LANE]
            obuf[2 * c2 + 1] = words[:, LANE:]
        start_scatter(tv_ref[t])

        @pl.when(t == n_used - 1)
        def _():
            wait_gather(1 - slot)
            wait_scatter(tv_ref[t])


def _moe(tile_expert, next_expert, tile_valid, n_tiles, src_token, dst_row, x_tok, wg, wu, wd, w_slot):
    p_rows = src_token.shape[0]
    tm = MOE_TILE
    n_blocks = p_rows // tm
    smem_blk = lambda fn: pl.BlockSpec((tm,), fn, memory_space=pltpu.SMEM)
    hbm = pl.BlockSpec(memory_space=pl.ANY)
    grid_spec = pltpu.PrefetchScalarGridSpec(
        num_scalar_prefetch=4,
        grid=(n_blocks,),
        in_specs=[smem_blk(lambda t, te, ne, tv, nt: (t,)),
                  smem_blk(lambda t, te, ne, tv, nt: (jnp.minimum(t + 1, n_blocks - 1),)),
                  smem_blk(lambda t, te, ne, tv, nt: (t,)),
                  hbm, hbm, hbm, hbm,
                  pl.BlockSpec((tm, 1), lambda t, te, ne, tv, nt: (t, 0))],
        out_specs=hbm,
        scratch_shapes=[pltpu.VMEM((2, TOK_WORDS, tm, LANE), U32),
                        pltpu.VMEM((tm, D_MODEL), BF16),
                        pltpu.VMEM((TOK_WORDS, tm, LANE), U32),
                        pltpu.VMEM((D_MODEL, D_EXPERT), F32),
                        pltpu.VMEM((D_MODEL, D_EXPERT), F32),
                        pltpu.VMEM((D_EXPERT, D_MODEL), F32),
                        pltpu.VMEM((D_MODEL, D_EXPERT), BF16),
                        pltpu.VMEM((D_MODEL, D_EXPERT), BF16),
                        pltpu.VMEM((D_EXPERT, D_MODEL), BF16),
                        pltpu.SemaphoreType.DMA((2,)),
                        pltpu.SemaphoreType.DMA(()),
                        pltpu.SemaphoreType.DMA((3,))],
    )
    return pl.pallas_call(
        _moe_kernel,
        grid_spec=grid_spec,
        out_shape=jax.ShapeDtypeStruct((2 * x_tok.shape[0], TOK_WORDS, 1, LANE), U32),
        compiler_params=_cparams(("arbitrary",), 58),
        name="moe",
    )(tile_expert, next_expert, tile_valid, n_tiles, src_token, src_token, dst_row, x_tok, wg, wu, wd, w_slot)


def _ln_add_kernel(x_ref, g_ref, b_ref, y_hbm, o_ref, ybuf, vbuf, sem, *, tm):
    i = pl.program_id(0)
    slot = lax.rem(i, 2)

    def start_gather(tile, buf_slot):
        base = tile * (2 * tm)

        def body(j, carry):
            for k in range(2):
                pltpu.make_async_copy(y_hbm.at[base + 2 * j + k], ybuf.at[buf_slot, k, :, pl.ds(j, 1), :],
                                      sem.at[buf_slot]).start()
            return carry
        lax.fori_loop(0, tm, body, 0, unroll=4)

    @pl.when(i == 0)
    def _():
        start_gather(0, 0)

    @pl.when(i + 1 < pl.num_programs(0))
    def _():
        start_gather(i + 1, 1 - slot)

    pltpu.make_async_copy(ybuf.at[slot], ybuf.at[slot], sem.at[slot]).wait()
    for c in range(TOK_WORDS):
        a_lo, a_hi = _unpack_rows(ybuf[slot, 0, c])
        b_lo, b_hi = _unpack_rows(ybuf[slot, 1, c])
        lo = slice(LANE * c, LANE * (c + 1))
        hi = slice(HALF_D + LANE * c, HALF_D + LANE * (c + 1))
        vbuf[:, lo] = ALPHA * x_ref[:, lo] + (a_lo + b_lo)
        vbuf[:, hi] = ALPHA * x_ref[:, hi] + (a_hi + b_hi)
    o_ref[...] = _layer_norm(vbuf[...], g_ref[...], b_ref[...])


def _ln_add(x, y_tok, g, b):
    s = x.shape[0]
    tm = min(256, s)
    full = lambda shape: pl.BlockSpec(shape, lambda i: (0,) * len(shape))
    return pl.pallas_call(
        functools.partial(_ln_add_kernel, tm=tm),
        grid=(s // tm,),
        in_specs=[pl.BlockSpec((tm, D_MODEL), lambda i: (i, 0)), full(g.shape), full(b.shape),
                  pl.BlockSpec(memory_space=pl.ANY)],
        out_specs=pl.BlockSpec((tm, D_MODEL), lambda i: (i, 0)),
        out_shape=jax.ShapeDtypeStruct((s, D_MODEL), F32),
        scratch_shapes=[pltpu.VMEM((2, 2, TOK_WORDS, tm, LANE), U32),
                        pltpu.VMEM((tm, D_MODEL), F32),
                        pltpu.SemaphoreType.DMA((2,))],
        compiler_params=_cparams(("arbitrary",), 48),
        name="ln_add",
    )(x, g, b, y_tok)


def _w_in_sources():
    src = {}
    o = 0
    for name, width in (("q", 512), ("k", 512), ("v", 1024), ("r", 1024), ("gl", GLA_GATE_RANK), ("u", 1024),
                        ("vg", 1024), ("cb", 1024), ("cc", 1024), ("ch", 1024), ("cq", MLA_Q_LORA),
                        ("ckv", MLA_KV_LORA), ("kr", MLA_ROPE)):
        src[name] = (o, width)
        o += width
    return src, o


PACK_COLS = 512


def _w_in_segments():
    src, _ = _w_in_sources()
    half = MLA_ROPE // 2
    kr0 = src["kr"][0]
    segs = [(dst, src[name][0], src[name][1])
            for name, dst in (("cq", COL_CQ), ("ckv", COL_CKV), ("q", COL_Q), ("k", COL_K), ("v", COL_V),
                              ("r", COL_R), ("u", COL_U), ("vg", COL_VG), ("cb", COL_CB), ("cc", COL_CC),
                              ("ch", COL_CH))]
    segs += [(COL_KR, kr0, MLA_ROPE), (COL_KR + MLA_ROPE, None, LANE - MLA_ROPE),
             (COL_KRS, kr0 + half, half), (COL_KRS + half, kr0, half), (COL_KRS + MLA_ROPE, None, LANE - MLA_ROPE),
             (COL_GL, src["gl"][0], GLA_GATE_RANK), (COL_GL + GLA_GATE_RANK, None, P_COLS - COL_GL - GLA_GATE_RANK)]
    return sorted(segs)


def _pack_w_in_kernel(wt_hbm, o_hbm, stage, tbuf, in_sem, out_sem, *, layer):
    n_blocks = P_COLS // PACK_COLS
    segs = _w_in_segments()

    def pieces(j):
        lo, hi = j * PACK_COLS, (j + 1) * PACK_COLS
        out = []
        for dst, s0, width in segs:
            a, b = max(dst, lo), min(dst + width, hi)
            if a < b:
                out.append((a - lo, None if s0 is None else s0 + (a - dst), b - a))
        return out

    def loads(j, slot):
        return [pltpu.make_async_copy(wt_hbm.at[layer, pl.ds(s0, n), :], stage.at[slot, pl.ds(off, n), :],
                                      in_sem.at[slot])
                for off, s0, n in pieces(j) if s0 is not None]

    def store(j, slot):
        return pltpu.make_async_copy(tbuf.at[slot], o_hbm.at[:, pl.ds(j * PACK_COLS, PACK_COLS)], out_sem.at[slot])

    for cp in loads(0, 0):
        cp.start()
    for j in range(n_blocks):
        slot = j % 2
        if j + 1 < n_blocks:
            for cp in loads(j + 1, 1 - slot):
                cp.start()
        for cp in loads(j, slot):
            cp.wait()
        for off, s0, n in pieces(j):
            if s0 is None:
                stage[slot, off:off + n, :] = jnp.zeros((n, D_MODEL), F32)
        if j >= 2:
            store(j - 2, slot).wait()
        tbuf[slot] = jnp.transpose(stage[slot]).astype(BF16)
        store(j, slot).start()
    for j in range(max(n_blocks - 2, 0), n_blocks):
        store(j, j % 2).wait()


def _pack_w_in(wt, layer):
    _, d_in, k = wt.shape
    assert d_in == _w_in_sources()[1] and k == D_MODEL
    return pl.pallas_call(
        functools.partial(_pack_w_in_kernel, layer=layer),
        in_specs=[pl.BlockSpec(memory_space=pl.ANY)],
        out_specs=pl.BlockSpec(memory_space=pl.ANY),
        out_shape=jax.ShapeDtypeStruct((k, P_COLS), BF16),
        scratch_shapes=[pltpu.VMEM((2, PACK_COLS, D_MODEL), F32),
                        pltpu.VMEM((2, D_MODEL, PACK_COLS), BF16),
                        pltpu.SemaphoreType.DMA((2,)),
                        pltpu.SemaphoreType.DMA((2,))],
        compiler_params=pltpu.CompilerParams(vmem_limit_bytes=40 * MIB),
        name="pack_w_in",
    )(wt)


def _pack_mla_weights(wuq, wukv):
    half = MLA_ROPE // 2
    wq3 = wuq.reshape(MLA_Q_LORA, MLA_HEADS, MLA_NOPE + MLA_ROPE)
    rope = wq3[:, :, MLA_NOPE:]
    zq = jnp.zeros((MLA_Q_LORA, MLA_HEADS, MLA_QK_PAD - MLA_NOPE - MLA_ROPE), wuq.dtype)
    wq = jnp.concatenate([wq3, zq], axis=2).reshape(MLA_Q_LORA, MLA_HEADS * MLA_QK_PAD)
    zs = jnp.zeros((MLA_Q_LORA, MLA_HEADS, LANE - MLA_ROPE), wuq.dtype)
    wqs = jnp.concatenate([rope[:, :, half:], rope[:, :, :half], zs], axis=2).reshape(MLA_Q_LORA, MLA_HEADS * LANE)
    wkv3 = wukv.reshape(MLA_KV_LORA, MLA_HEADS, MLA_NOPE + MLA_V)
    wk = wkv3[:, :, :MLA_NOPE].reshape(MLA_KV_LORA, MLA_HEADS * MLA_NOPE)
    wv = wkv3[:, :, MLA_NOPE:].reshape(MLA_KV_LORA, MLA_HEADS * MLA_V)
    return wq.astype(BF16), wqs.astype(BF16), wk.astype(BF16), wv.astype(BF16)


def _rope_tables(s):
    pos = jnp.arange(s, dtype=F32)
    inv_freq = ROPE_BASE ** (-jnp.arange(0, MLA_ROPE, 2, dtype=F32) / MLA_ROPE)
    ang = pos[:, None] * inv_freq[None, :]
    cos, sin = jnp.cos(ang), jnp.sin(ang)
    z = jnp.zeros((s, LANE - MLA_ROPE), F32)
    return jnp.concatenate([cos, cos, z], axis=1), jnp.concatenate([-sin, sin, z], axis=1)


def _dispatch_plan(route, s):
    n_assign = 2 * s
    p_rows = n_assign + N_EXPERTS * MOE_TILE
    eid = route[:, 0:2].astype(jnp.int32).reshape(n_assign)
    wgt = route[:, 2:4].reshape(n_assign)
    experts = jnp.arange(N_EXPERTS, dtype=jnp.int32)
    counts = jnp.sum((eid[:, None] == experts[None, :]).astype(jnp.int32), axis=0)
    order = jnp.argsort(eid, stable=True).astype(jnp.int32)
    first_of_expert = jnp.cumsum(counts) - counts
    tiles_e = (counts + MOE_TILE - 1) // MOE_TILE
    tile_end = jnp.cumsum(tiles_e)
    tile_start = tile_end - tiles_e
    n_blocks = p_rows // MOE_TILE
    tile_ids = jnp.arange(n_blocks, dtype=jnp.int32)
    tile_expert = jnp.minimum(jnp.sum((tile_end[None, :] <= tile_ids[:, None]).astype(jnp.int32), axis=1),
                              N_EXPERTS - 1)
    n_tiles = tile_end[-1:].astype(jnp.int32)
    slot_ids = jnp.arange(p_rows, dtype=jnp.int32)
    tile_of_slot = slot_ids // MOE_TILE
    e_slot = tile_expert[tile_of_slot]
    rank = slot_ids - tile_start[e_slot] * MOE_TILE
    is_pad = jnp.logical_or(tile_of_slot >= n_tiles[0], rank >= counts[e_slot])
    slot_assign = order[jnp.clip(first_of_expert[e_slot] + rank, 0, n_assign - 1)]
    src_token = jnp.where(is_pad, 0, slot_assign // 2)
    dst_row = jnp.where(is_pad, 0, slot_assign)
    w_slot = jnp.where(is_pad, 0.0, wgt[dst_row])
    tile_valid = jnp.sum(jnp.logical_not(is_pad).reshape(n_blocks, MOE_TILE).astype(jnp.int32), axis=1)
    later_with_tiles = (experts[None, :] > experts[:, None]) & (tiles_e[None, :] > 0)
    next_nonempty = jnp.min(jnp.where(later_with_tiles, experts[None, :], N_EXPERTS), axis=1)
    next_expert = next_nonempty[tile_expert]
    return src_token, dst_row, w_slot.reshape(p_rows, 1), tile_expert, next_expert, tile_valid, n_tiles


def kernel(x, w_in, gla_wa2, gla_ba, gla_norm, gm_ln_g, gm_ln_b, gm_ws, gm_bs, gm_norm, sc_conv, sc_norm, mla_q_norm, mla_kv_norm, mla_wuq, mla_wukv, mla_norm, w_o, ln1_g, ln1_b, router_g_w, router_g_b, router_e_w, router_e_b, exp_w_gate, exp_w_up, exp_w_down, ln2_g, ln2_b):
    bsz, s, _ = x.shape
    assert bsz == 1
    xc = x.reshape(s, D_MODEL)
    cos_t, sin_t = _rope_tables(s)
    row = lambda v: v.reshape(1, -1)
    wg_all = exp_w_gate.reshape(DEPTH * N_EXPERTS, D_MODEL, D_EXPERT)
    wu_all = exp_w_up.reshape(DEPTH * N_EXPERTS, D_MODEL, D_EXPERT)
    wd_all = exp_w_down.reshape(DEPTH * N_EXPERTS, D_EXPERT, D_MODEL)
    w_o_b = w_o.astype(BF16)
    w_in_t = jnp.swapaxes(w_in, 1, 2)
    for l in range(DEPTH):
        p = _mm_in(xc, _pack_w_in(w_in_t, l))

        wa2p = jnp.concatenate(
            [gla_wa2[l], jnp.zeros((LANE - GLA_GATE_RANK, GLA_HEADS * GLA_DK), F32)], axis=0).astype(BF16)
        out_a = _gla(p, wa2p, row(gla_ba[l]), row(gla_norm[l]))

        gm_bias = jnp.repeat(gm_bs[l].T, GM_CH, axis=1)
        out_b = _gmlp(p, row(gm_ln_g[l]), row(gm_ln_b[l]), gm_ws[l], gm_bias, row(gm_norm[l]))

        out_c = _sconv(p, sc_conv[l], row(sc_norm[l]))

        wq, wqs, wk, wv = _pack_mla_weights(mla_wuq[l], mla_wukv[l])
        q_att, k_att, v_att = _mla_proj(p, row(mla_q_norm[l]), row(mla_kv_norm[l]), wq, wqs, wk, wv, cos_t, sin_t)
        out_d = _flash(q_att, k_att, v_att, row(mla_norm[l]))

        y = _mm_out(out_a, out_b, out_c, out_d, w_o_b, l, xc)

        wr = jnp.concatenate([router_g_w[l], router_e_w[l],
                              jnp.zeros((D_MODEL, LANE - MOE_GROUPS - N_EXPERTS), F32)], axis=1)
        wr_hi = wr.astype(BF16)
        wr_lo = (wr - wr_hi.astype(F32)).astype(BF16)
        rb = jnp.concatenate([router_g_b[l], router_e_b[l], jnp.zeros((LANE - MOE_GROUPS - N_EXPERTS,), F32)])
        x1, x1_tok, route = _ln_route(y, row(ln1_g[l]), row(ln1_b[l]), wr_hi, wr_lo, row(rb))

        src_token, dst_row, w_slot, tile_expert, next_expert, tile_valid, n_tiles = _dispatch_plan(route, s)
        next_expert = jnp.where(next_expert < N_EXPERTS, next_expert + l * N_EXPERTS, -1)
        y_tok = _moe(tile_expert + l * N_EXPERTS, next_expert, tile_valid, n_tiles, src_token, dst_row, x1_tok,
                     wg_all, wu_all, wd_all, w_slot)
        xc = _ln_add(x1, y_tok, row(ln2_g[l]), row(ln2_b[l]))
    return xc.reshape(bsz, s, D_MODEL)
```

```python
import functools

import jax
import jax.numpy as jnp
from jax import lax
from jax.experimental import pallas as pl
from jax.experimental.pallas import tpu as pltpu

F32 = jnp.float32
BF16 = jnp.bfloat16

D_MODEL = 4096
DEPTH = 2
GROUP_W = 1024

GLA_HEADS = 4
GLA_DK = 128
GLA_DV = 256
GLA_GATE_RANK = 16
GLA_TAU = 16.0
GLA_CHUNK = 64
GLA_SUB = 16

GM_GROUPS = 8
GM_CH = 128
GM_CHUNK = 128

MLA_HEADS = 8
MLA_NOPE = 128
MLA_ROPE = 64
MLA_V = 128
MLA_Q_LORA = 768
MLA_KV_LORA = 256
ROPE_BASE = 10000.0
MLA_QK_PAD = 256

MOE_GROUPS = 4
MOE_PER_GROUP = 8
N_EXPERTS = 32
D_EXPERT = 512

ALPHA = (2.0 * DEPTH) ** 0.25
LOG2_E = 1.4426950408889634

LANE = 128
MIB = 1024 * 1024

COL_CQ = 0
COL_CKV = 768
COL_Q = 1024
COL_K = 1536
COL_V = 2048
COL_R = 3072
COL_U = 4096
COL_VG = 5120
COL_CB = 6144
COL_CC = 7168
COL_CH = 8192
COL_KR = 9216
COL_KRS = 9344
COL_GL = 9472
P_COLS = 9728

MOE_TILE = 256
SCATTER_UNROLL = 8
WEIGHT_ROUND_ROWS = 512
HALF_D = D_MODEL // 2
TOK_WORDS = HALF_D // LANE
U32 = jnp.uint32
HIGH_HALF = 0xFFFF0000


def _pack_rows(lo_f32, hi_f32):
    return (lax.bitcast_convert_type(lo_f32, U32) >> 16) | lax.bitcast_convert_type(hi_f32, U32)


def _unpack_rows(words):
    return (lax.bitcast_convert_type(words << 16, F32),
            lax.bitcast_convert_type(words & jnp.uint32(HIGH_HALF), F32))


def _cparams(sem, vmem_mib):
    return pltpu.CompilerParams(dimension_semantics=sem, vmem_limit_bytes=vmem_mib * MIB)


def _dot(a, b):
    return jnp.dot(a, b, preferred_element_type=F32)


def _dot_nt(a, b):
    return lax.dot_general(a, b, (((1,), (1,)), ((), ())), preferred_element_type=F32)


def _dot_tn(a, b):
    return lax.dot_general(a, b, (((0,), (0,)), ((), ())), preferred_element_type=F32)


def _mm_in_kernel(x_ref, w_ref, o_ref, xb_ref):
    @pl.when(pl.program_id(1) == 0)
    def _():
        xb_ref[...] = x_ref[...].astype(BF16)

    o_ref[...] = _dot(xb_ref[...], w_ref[...]).astype(o_ref.dtype)


def _mm_in(x, w):
    s, k = x.shape
    n = w.shape[1]
    tm = min(1024, s)
    tn = 512
    return pl.pallas_call(
        _mm_in_kernel,
        grid=(s // tm, n // tn),
        in_specs=[pl.BlockSpec((tm, k), lambda i, j: (i, 0)),
                  pl.BlockSpec((k, tn), lambda i, j: (0, j))],
        out_specs=pl.BlockSpec((tm, tn), lambda i, j: (i, j)),
        out_shape=jax.ShapeDtypeStruct((s, n), BF16),
        scratch_shapes=[pltpu.VMEM((tm, k), BF16)],
        compiler_params=_cparams(("parallel", "arbitrary"), 58),
        name="mm_in",
    )(x, w)


def _gla_kernel(q_ref, k_ref, v_ref, r_ref, gl_ref, wa2_ref, ba_ref, ng_ref, o_ref, s_ref, *, n_chunks):
    c_len = GLA_CHUNK

    @pl.when(pl.program_id(0) == 0)
    def _():
        s_ref[...] = jnp.zeros_like(s_ref)

    row = lax.broadcasted_iota(jnp.int32, (c_len, c_len), 0)
    col = lax.broadcasted_iota(jnp.int32, (c_len, c_len), 1)
    tril = jnp.where(col <= row, 1.0, 0.0).astype(BF16)
    sub_row = lax.broadcasted_iota(jnp.int32, (GLA_SUB, c_len), 0)
    sub_col = lax.broadcasted_iota(jnp.int32, (GLA_SUB, c_len), 1)
    n_sub = c_len // GLA_SUB

    def chunk(c, carry):
        rows = pl.ds(pl.multiple_of(c * c_len, c_len), c_len)
        logit = _dot(gl_ref[rows, :], wa2_ref[...]) + ba_ref[...]
        g = (jnp.minimum(logit, 0.0) - jnp.log(1.0 + jnp.exp(-jnp.abs(logit)))) * (1.0 / GLA_TAU)
        g_hi = g.astype(BF16)
        g_lo = (g - g_hi.astype(F32)).astype(BF16)
        b_all = _dot(tril, g_hi) + _dot(tril, g_lo)

        for h in range(GLA_HEADS):
            hs = slice(GLA_DK * h, GLA_DK * (h + 1))
            vs = slice(GLA_DV * h, GLA_DV * (h + 1))
            bh = b_all[:, hs]
            qh = q_ref[rows, hs].astype(F32) * (GLA_DK ** -0.5)
            kh = k_ref[rows, hs].astype(F32)
            vh = v_ref[rows, vs]
            state = s_ref[h]

            o = _dot((qh * jnp.exp(bh)).astype(BF16), state.astype(BF16))

            att_rows = []
            for blk in range(n_sub):
                sl = slice(GLA_SUB * blk, GLA_SUB * (blk + 1))
                b_blk = bh[sl]
                q_blk = qh[sl]
                att = jnp.zeros((GLA_SUB, c_len), F32)
                if blk > 0:
                    ref = bh[GLA_SUB * blk:GLA_SUB * blk + 1, :]
                    qs = (q_blk * jnp.exp(b_blk - ref)).astype(BF16)
                    ks = (kh * jnp.exp(jnp.minimum(ref - bh, 0.0))).astype(BF16)
                    att = jnp.where(sub_col < GLA_SUB * blk, _dot_nt(qs, ks), 0.0)
                for jj in range(GLA_SUB):
                    j = GLA_SUB * blk + jj
                    t = q_blk * kh[j:j + 1, :] * jnp.exp(b_blk - bh[j:j + 1, :])
                    rs = jnp.sum(t, axis=-1, keepdims=True)
                    att = jnp.where((sub_col == j) & (sub_row >= jj), rs, att)
                att_rows.append(att)
            att_full = jnp.concatenate(att_rows, axis=0).astype(BF16)
            o = o + _dot(att_full, vh)

            b_last = bh[c_len - 1:c_len, :]
            kd = (kh * jnp.exp(b_last - bh)).astype(BF16)
            decay_col = jnp.transpose(jnp.broadcast_to(jnp.exp(b_last), (GLA_DK, GLA_DK)))
            s_ref[h] = state * jnp.concatenate([decay_col, decay_col], axis=1) + _dot_tn(kd, vh)

            var = jnp.mean(o * o, axis=-1, keepdims=True)
            on = o * lax.rsqrt(var + 1e-6) * ng_ref[:, vs]
            rr = r_ref[rows, vs].astype(F32)
            o_ref[rows, vs] = (on * (rr / (1.0 + jnp.exp(-rr)))).astype(o_ref.dtype)
        return carry

    lax.fori_loop(0, n_chunks, chunk, 0, unroll=2)


def _gla(p, wa2p, ba, ng):
    s = p.shape[0]
    t = min(512, s)
    col = lambda width, off: pl.BlockSpec((t, width), lambda i: (i, off // width))
    full = lambda shape: pl.BlockSpec(shape, lambda i: (0,) * len(shape))
    return pl.pallas_call(
        functools.partial(_gla_kernel, n_chunks=t // GLA_CHUNK),
        grid=(s // t,),
        in_specs=[col(512, COL_Q), col(512, COL_K), col(1024, COL_V), col(1024, COL_R), col(LANE, COL_GL),
                  full(wa2p.shape), full(ba.shape), full(ng.shape)],
        out_specs=pl.BlockSpec((t, GROUP_W), lambda i: (i, 0)),
        out_shape=jax.ShapeDtypeStruct((s, GROUP_W), BF16),
        scratch_shapes=[pltpu.VMEM((GLA_HEADS, GLA_DK, GLA_DV), F32)],
        compiler_params=_cparams(("arbitrary",), 32),
        name="gla",
    )(p, p, p, p, p, wa2p, ba, ng)


def _gelu(x):
    return 0.5 * x * (1.0 + lax.erf(x * 0.7071067811865476))


def _gmlp_kernel(u_ref, v_ref, lg_ref, lb_ref, ws_ref, bias_ref, ng_ref, o_ref, buf_ref, *, n_chunks):
    row = lax.broadcasted_iota(jnp.int32, (GM_CHUNK, GM_CHUNK), 0)
    col = lax.broadcasted_iota(jnp.int32, (GM_CHUNK, GM_CHUNK), 1)
    causal = col <= row
    v = _gelu(v_ref[...].astype(F32))
    mu = jnp.mean(v, axis=-1, keepdims=True)
    d = v - mu
    var = jnp.mean(d * d, axis=-1, keepdims=True)
    vb = (d * lax.rsqrt(var + 1e-5) * lg_ref[...] + lb_ref[...]).astype(BF16)
    for g in range(GM_GROUPS):
        cs = slice(GM_CH * g, GM_CH * (g + 1))
        w = jnp.where(causal, ws_ref[g], 0.0).astype(BF16)
        for c in range(n_chunks):
            rs = slice(GM_CHUNK * c, GM_CHUNK * (c + 1))
            mixed = _dot(w, vb[rs, cs]) + bias_ref[:, cs]
            buf_ref[rs, cs] = _gelu(u_ref[rs, cs].astype(F32)) * mixed
    out = buf_ref[...]
    ms = jnp.mean(out * out, axis=-1, keepdims=True)
    o_ref[...] = (out * lax.rsqrt(ms + 1e-6) * ng_ref[...]).astype(o_ref.dtype)


def _gmlp(p, lg, lb, ws, bias, ng):
    s = p.shape[0]
    t = min(256, s)
    col = lambda off: pl.BlockSpec((t, GROUP_W), lambda i: (i, off // GROUP_W))
    full = lambda shape: pl.BlockSpec(shape, lambda i: (0,) * len(shape))
    return pl.pallas_call(
        functools.partial(_gmlp_kernel, n_chunks=t // GM_CHUNK),
        grid=(s // t,),
        in_specs=[col(COL_U), col(COL_VG), full(lg.shape), full(lb.shape), full(ws.shape), full(bias.shape),
                  full(ng.shape)],
        out_specs=pl.BlockSpec((t, GROUP_W), lambda i: (i, 0)),
        out_shape=jax.ShapeDtypeStruct((s, GROUP_W), BF16),
        scratch_shapes=[pltpu.VMEM((t, GROUP_W), F32)],
        compiler_params=_cparams(("parallel",), 32),
        name="gmlp",
    )(p, p, lg, lb, ws, bias, ng)


CONV_HALO = 8


def _sconv_kernel(b_ref, c_ref, h_ref, w_ref, ng_ref, o_ref, z_ref):
    t = b_ref.shape[0]

    @pl.when(pl.program_id(0) == 0)
    def _():
        z_ref[0:CONV_HALO, :] = jnp.zeros((CONV_HALO, GROUP_W), F32)

    z = c_ref[...].astype(F32) * h_ref[...].astype(F32)
    z_ref[CONV_HALO:CONV_HALO + t, :] = z
    z1 = z_ref[CONV_HALO - 1:CONV_HALO - 1 + t, :]
    z2 = z_ref[CONV_HALO - 2:CONV_HALO - 2 + t, :]
    y = w_ref[0:1, :] * z2 + w_ref[1:2, :] * z1 + w_ref[2:3, :] * z
    z_ref[0:CONV_HALO, :] = z[t - CONV_HALO:t, :]
    out = b_ref[...].astype(F32) * y
    ms = jnp.mean(out * out, axis=-1, keepdims=True)
    o_ref[...] = (out * lax.rsqrt(ms + 1e-6) * ng_ref[...]).astype(o_ref.dtype)


def _sconv(p, w, ng):
    s = p.shape[0]
    t = min(256, s)
    col = lambda off: pl.BlockSpec((t, GROUP_W), lambda i: (i, off // GROUP_W))
    full = lambda shape: pl.BlockSpec(shape, lambda i: (0,) * len(shape))
    return pl.pallas_call(
        _sconv_kernel,
        grid=(s // t,),
        in_specs=[col(COL_CB), col(COL_CC), col(COL_CH), full(w.shape), full(ng.shape)],
        out_specs=pl.BlockSpec((t, GROUP_W), lambda i: (i, 0)),
        out_shape=jax.ShapeDtypeStruct((s, GROUP_W), BF16),
        scratch_shapes=[pltpu.VMEM((t + CONV_HALO, GROUP_W), F32)],
        compiler_params=_cparams(("arbitrary",), 32),
        name="sconv",
    )(p, p, p, w, ng)


def _mla_proj_kernel(cq_ref, ckv_ref, kr_ref, krs_ref, qn_ref, kvn_ref, wq_ref, wqs_ref, wk_ref, wv_ref,
                     cos_ref, sin_ref, q_out, k_out, v_out):
    def rms(ref, g_ref):
        t = ref[...].astype(F32)
        return (t * lax.rsqrt(jnp.mean(t * t, axis=-1, keepdims=True) + 1e-6) * g_ref[...]).astype(BF16)

    cqn = rms(cq_ref, qn_ref)
    ckvn = rms(ckv_ref, kvn_ref)
    qm = _dot(cqn, wq_ref[...])
    qsw = _dot(cqn, wqs_ref[...])
    kn = _dot(ckvn, wk_ref[...])
    vv = _dot(ckvn, wv_ref[...])
    cos = cos_ref[...]
    sin = sin_ref[...]
    scale = (MLA_NOPE + MLA_ROPE) ** -0.5 * LOG2_E
    kr_rot = (kr_ref[...].astype(F32) * cos + krs_ref[...].astype(F32) * sin).astype(k_out.dtype)
    ones_col = jnp.where(lax.broadcasted_iota(jnp.int32, (cos.shape[0], LANE), 1) == 0, 1.0, 0.0).astype(v_out.dtype)
    for h in range(MLA_HEADS):
        lo = MLA_QK_PAD * h
        v_out[:, lo:lo + LANE] = vv[:, LANE * h:LANE * (h + 1)].astype(v_out.dtype)
        v_out[:, lo + LANE:lo + 2 * LANE] = ones_col
        q_out[:, lo:lo + LANE] = (qm[:, lo:lo + LANE] * scale).astype(q_out.dtype)
        q_out[:, lo + LANE:lo + 2 * LANE] = (
            (qm[:, lo + LANE:lo + 2 * LANE] * cos + qsw[:, LANE * h:LANE * (h + 1)] * sin) * scale
        ).astype(q_out.dtype)
        k_out[:, lo:lo + LANE] = kn[:, LANE * h:LANE * (h + 1)].astype(k_out.dtype)
        k_out[:, lo + LANE:lo + 2 * LANE] = kr_rot


def _mla_proj(p, qn, kvn, wq, wqs, wk, wv, cos, sin):
    s = p.shape[0]
    t = min(256, s)
    full = lambda shape: pl.BlockSpec(shape, lambda i: (0,) * len(shape))
    rowb = lambda width: pl.BlockSpec((t, width), lambda i: (i, 0))
    qk_w = MLA_HEADS * MLA_QK_PAD
    return pl.pallas_call(
        _mla_proj_kernel,
        grid=(s // t,),
        in_specs=[pl.BlockSpec((t, MLA_Q_LORA), lambda i: (i, COL_CQ // MLA_Q_LORA)),
                  pl.BlockSpec((t, MLA_KV_LORA), lambda i: (i, COL_CKV // MLA_KV_LORA)),
                  pl.BlockSpec((t, LANE), lambda i: (i, COL_KR // LANE)),
                  pl.BlockSpec((t, LANE), lambda i: (i, COL_KRS // LANE)),
                  full(qn.shape), full(kvn.shape), full(wq.shape), full(wqs.shape), full(wk.shape), full(wv.shape),
                  rowb(LANE), rowb(LANE)],
        out_specs=[rowb(qk_w), rowb(qk_w), rowb(qk_w)],
        out_shape=[jax.ShapeDtypeStruct((s, qk_w), BF16)] * 3,
        compiler_params=_cparams(("parallel",), 40),
        name="mla_proj",
    )(p, p, p, p, qn, kvn, wq, wqs, wk, wv, cos, sin)


FLASH_TQ = 512
FLASH_TK = 1024


def _flash_kernel(qi_ref, ki_ref, q_ref, k_ref, v_ref, ng_ref, o_ref, acc_ref, m_ref, *, tq, tk):
    step = pl.program_id(0)
    qi = qi_ref[step]
    ki = ki_ref[step]
    last_ki = (qi * tq) // tk

    @pl.when(ki == 0)
    def _():
        m_ref[...] = jnp.full(m_ref.shape, -jnp.inf, F32)
        acc_ref[...] = jnp.zeros_like(acc_ref)

    def accumulate(masked, k0, kn):
        keys = slice(k0, k0 + kn)
        if masked:
            row = qi * tq + lax.broadcasted_iota(jnp.int32, (tq, kn), 0)
            col = ki * tk + k0 + lax.broadcasted_iota(jnp.int32, (tq, kn), 1)
            visible = col <= row

        def scores(h):
            hs = slice(MLA_QK_PAD * h, MLA_QK_PAD * (h + 1))
            return _dot_nt(q_ref[:, hs], k_ref[keys, hs])

        sc = scores(0)
        for h in range(MLA_HEADS):
            hs = slice(MLA_QK_PAD * h, MLA_QK_PAD * (h + 1))
            sc_next = scores(h + 1) if h + 1 < MLA_HEADS else None
            if masked:
                sc = jnp.where(visible, sc, -jnp.inf)
            m_old = m_ref[h]
            m_new = jnp.maximum(m_old, jnp.max(sc, axis=-1, keepdims=True))
            pr = jnp.exp2(sc - m_new).astype(BF16)
            acc_ref[:, hs] = jnp.exp2(m_old - m_new) * acc_ref[:, hs] + _dot(pr, v_ref[keys, hs])
            m_ref[h] = m_new
            sc = sc_next

    @pl.when(ki < last_ki)
    def _():
        accumulate(False, 0, tk)

    @pl.when(ki == last_ki)
    def _():
        accumulate(True, 0, tk)
        outs = []
        for h in range(MLA_HEADS):
            lo = MLA_QK_PAD * h
            outs.append(acc_ref[:, lo:lo + MLA_V] / acc_ref[:, lo + MLA_V:lo + MLA_V + 1])
        o = jnp.concatenate(outs, axis=1)
        ms = jnp.mean(o * o, axis=-1, keepdims=True)
        o_ref[...] = (o * lax.rsqrt(ms + 1e-6) * ng_ref[...]).astype(o_ref.dtype)


def _flash(q, k, v, ng):
    s = q.shape[0]
    tk = min(FLASH_TK, s)
    tq = min(FLASH_TQ, tk)
    nq = s // tq
    pairs = [(a, b) for a in range(nq) for b in range((a * tq) // tk + 1)]
    qi_tab = jnp.asarray([a for a, _ in pairs], jnp.int32)
    ki_tab = jnp.asarray([b for _, b in pairs], jnp.int32)
    qk_w = MLA_HEADS * MLA_QK_PAD
    grid_spec = pltpu.PrefetchScalarGridSpec(
        num_scalar_prefetch=2,
        grid=(len(pairs),),
        in_specs=[pl.BlockSpec((tq, qk_w), lambda t, qi, ki: (qi[t], 0)),
                  pl.BlockSpec((tk, qk_w), lambda t, qi, ki: (ki[t], 0)),
                  pl.BlockSpec((tk, qk_w), lambda t, qi, ki: (ki[t], 0)),
                  pl.BlockSpec((1, GROUP_W), lambda t, qi, ki: (0, 0))],
        out_specs=pl.BlockSpec((tq, GROUP_W), lambda t, qi, ki: (qi[t], 0)),
        scratch_shapes=[pltpu.VMEM((tq, qk_w), F32),
                        pltpu.VMEM((MLA_HEADS, tq, 1), F32)],
    )
    return pl.pallas_call(
        functools.partial(_flash_kernel, tq=tq, tk=tk),
        grid_spec=grid_spec,
        out_shape=jax.ShapeDtypeStruct((s, GROUP_W), BF16),
        compiler_params=_cparams(("arbitrary",), 56),
        name="flash",
    )(qi_tab, ki_tab, q, k, v, ng)


def _mm_out_kernel(a0_ref, a1_ref, a2_ref, a3_ref, w_ref, x_ref, o_ref):
    acc = _dot(a0_ref[...], w_ref[0, 0:GROUP_W, :])
    acc = acc + _dot(a1_ref[...], w_ref[0, GROUP_W:2 * GROUP_W, :])
    acc = acc + _dot(a2_ref[...], w_ref[0, 2 * GROUP_W:3 * GROUP_W, :])
    acc = acc + _dot(a3_ref[...], w_ref[0, 3 * GROUP_W:4 * GROUP_W, :])
    o_ref[...] = ALPHA * x_ref[...] + acc


def _mm_out(a0, a1, a2, a3, w, layer, x):
    s = x.shape[0]
    tm = min(512, s)
    tn = 1024
    a_spec = pl.BlockSpec((tm, GROUP_W), lambda i, j: (i, 0))
    return pl.pallas_call(
        _mm_out_kernel,
        grid=(s // tm, D_MODEL // tn),
        in_specs=[a_spec, a_spec, a_spec, a_spec,
                  pl.BlockSpec((1, D_MODEL, tn), lambda i, j: (layer, 0, j)),
                  pl.BlockSpec((tm, tn), lambda i, j: (i, j))],
        out_specs=pl.BlockSpec((tm, tn), lambda i, j: (i, j)),
        out_shape=jax.ShapeDtypeStruct((s, D_MODEL), F32),
        compiler_params=_cparams(("parallel", "arbitrary"), 48),
        name="mm_out",
    )(a0, a1, a2, a3, w, x)


def _layer_norm(t, g, b):
    mu = jnp.mean(t, axis=-1, keepdims=True)
    d = t - mu
    var = jnp.mean(d * d, axis=-1, keepdims=True)
    return d * lax.rsqrt(var + 1e-5) * g + b


def _ln_route_kernel(y_ref, g_ref, b_ref, wh_ref, wl_ref, rb_ref, x_out, route_out, xtok_hbm, tbuf, sem, *, tm):
    i = pl.program_id(0)
    slot = lax.rem(i, 2)

    def wait_rows(buf_slot):
        pltpu.make_async_copy(tbuf.at[buf_slot], tbuf.at[buf_slot], sem.at[buf_slot]).wait()

    x1 = _layer_norm(y_ref[...], g_ref[...], b_ref[...])
    x_out[...] = x1
    x_hi = x1.astype(BF16)
    x_hi32 = x_hi.astype(F32)

    @pl.when(i >= 2)
    def _():
        wait_rows(slot)

    words = _pack_rows(x_hi32[:, :HALF_D], x_hi32[:, HALF_D:])
    for c in range(TOK_WORDS):
        tbuf[slot, c] = words[:, LANE * c:LANE * (c + 1)]
    for j in range(tm):
        pltpu.make_async_copy(tbuf.at[slot, :, pl.ds(j, 1), :], xtok_hbm.at[i * tm + j], sem.at[slot]).start()

    @pl.when(i == pl.num_programs(0) - 1)
    def _():
        wait_rows(slot)

        @pl.when(i >= 1)
        def _():
            wait_rows(1 - slot)

    x_lo = (x1 - x_hi32).astype(BF16)
    logits = _dot(x_hi, wh_ref[...]) + _dot(x_lo, wh_ref[...]) + _dot(x_hi, wl_ref[...]) + rb_ref[...]
    lane = lax.broadcasted_iota(jnp.int32, logits.shape, 1).astype(F32)
    neg = -jnp.inf
    big = 1e9
    is_grp = lane < MOE_GROUPS
    lg = jnp.where(is_grp, logits, neg)
    mg = jnp.max(lg, axis=-1, keepdims=True)
    gsel = jnp.min(jnp.where(lg == mg, lane, big), axis=-1, keepdims=True)
    pg_sel = 1.0 / jnp.sum(jnp.where(is_grp, jnp.exp(lg - mg), 0.0), axis=-1, keepdims=True)
    lo = MOE_GROUPS + MOE_PER_GROUP * gsel
    le = jnp.where((lane >= lo) & (lane < lo + MOE_PER_GROUP), logits, neg)
    v1 = jnp.max(le, axis=-1, keepdims=True)
    i1 = jnp.min(jnp.where(le == v1, lane, big), axis=-1, keepdims=True)
    le2 = jnp.where(lane == i1, neg, le)
    v2 = jnp.max(le2, axis=-1, keepdims=True)
    i2 = jnp.min(jnp.where(le2 == v2, lane, big), axis=-1, keepdims=True)
    e = jnp.exp(v2 - v1)
    w1 = pg_sel / (1.0 + e)
    w2 = pg_sel * e / (1.0 + e)
    route = jnp.where(lane == 0.0, i1 - MOE_GROUPS,
                      jnp.where(lane == 1.0, i2 - MOE_GROUPS,
                                jnp.where(lane == 2.0, w1, jnp.where(lane == 3.0, w2, 0.0))))
    route_out[...] = route


def _ln_route(y, g, b, wh, wl, rb):
    s = y.shape[0]
    tm = min(256, s)
    full = lambda shape: pl.BlockSpec(shape, lambda i: (0,) * len(shape))
    rowb = lambda width: pl.BlockSpec((tm, width), lambda i: (i, 0))
    x1, route, x1_tok = pl.pallas_call(
        functools.partial(_ln_route_kernel, tm=tm),
        grid=(s // tm,),
        in_specs=[rowb(D_MODEL), full(g.shape), full(b.shape), full(wh.shape), full(wl.shape), full(rb.shape)],
        out_specs=[rowb(D_MODEL), rowb(LANE), pl.BlockSpec(memory_space=pl.ANY)],
        out_shape=[jax.ShapeDtypeStruct((s, D_MODEL), F32),
                   jax.ShapeDtypeStruct((s, LANE), F32),
                   jax.ShapeDtypeStruct((s, TOK_WORDS, 1, LANE), U32)],
        scratch_shapes=[pltpu.VMEM((2, TOK_WORDS, tm, LANE), U32), pltpu.SemaphoreType.DMA((2,))],
        compiler_params=_cparams(("arbitrary",), 48),
        name="ln_route",
    )(y, g, b, wh, wl, rb)
    return x1, x1_tok, route


def _moe_kernel(te_ref, ne_ref, tv_ref, nt_ref, src_ref, nxt_ref, dst_ref, x_hbm, wg_hbm, wu_hbm, wd_hbm, ws_ref,
                y_hbm, gbuf, hb, obuf, stg_g, stg_u, stg_d, wg_ref, wu_ref, wd_ref, gsem, ssem, wsem):
    t = pl.program_id(0)
    n_used = nt_ref[0]
    slot = lax.rem(t, 2)

    def weight_copies(e):
        return (pltpu.make_async_copy(wg_hbm.at[e], stg_g, wsem.at[0]),
                pltpu.make_async_copy(wu_hbm.at[e], stg_u, wsem.at[1]),
                pltpu.make_async_copy(wd_hbm.at[e], stg_d, wsem.at[2]))

    def round_weights(stg, dst, rows):
        def body(r, carry):
            rs = pl.ds(pl.multiple_of(r * rows, rows), rows)
            dst[rs, :] = stg[rs, :].astype(BF16)
            return carry
        lax.fori_loop(0, stg.shape[0] // rows, body, 0)

    @pl.when(t == 0)
    def _():
        for cp in weight_copies(te_ref[0]):
            cp.start()

    @pl.when(t < n_used)
    def _():
        expert = te_ref[t]
        first_tile_of_expert = jnp.logical_or(t == 0, expert != te_ref[jnp.maximum(t - 1, 0)])

        @pl.when(first_tile_of_expert)
        def _():
            for cp in weight_copies(expert):
                cp.wait()
            round_weights(stg_g, wg_ref, WEIGHT_ROUND_ROWS)
            round_weights(stg_u, wu_ref, WEIGHT_ROUND_ROWS)
            round_weights(stg_d, wd_ref, WEIGHT_ROUND_ROWS * D_EXPERT // D_MODEL)

            @pl.when(ne_ref[t] >= 0)
            def _():
                for cp in weight_copies(ne_ref[t]):
                    cp.start()

    def gather_row(idx_ref, buf_slot, i):
        return pltpu.make_async_copy(x_hbm.at[idx_ref[i]], gbuf.at[buf_slot, :, pl.ds(i, 1), :], gsem.at[buf_slot])

    def wait_gather(buf_slot):
        pltpu.make_async_copy(gbuf.at[buf_slot], gbuf.at[buf_slot], gsem.at[buf_slot]).wait()

    def scatter_row(i):
        return pltpu.make_async_copy(obuf.at[:, pl.ds(i, 1), :], y_hbm.at[dst_ref[i]], ssem)

    def start_scatter(n_rows):
        def group(g, carry):
            for u in range(SCATTER_UNROLL):
                scatter_row(g * SCATTER_UNROLL + u).start()
            return carry

        def single(i, carry):
            scatter_row(i).start()
            return carry
        n_groups = n_rows // SCATTER_UNROLL
        lax.fori_loop(0, n_groups, group, 0)
        lax.fori_loop(n_groups * SCATTER_UNROLL, n_rows, single, 0)

    def wait_scatter(n_rows):
        rows = obuf.at[:, pl.ds(0, n_rows), :]
        pltpu.make_async_copy(rows, rows, ssem).wait()

    @pl.when(t == 0)
    def _():
        def body(i, carry):
            gather_row(src_ref, 0, i).start()
            return carry
        lax.fori_loop(0, MOE_TILE, body, 0, unroll=8)

    @pl.when(t < n_used)
    def _():
        wait_gather(slot)
        for c in range(TOK_WORDS):
            lo, hi = _unpack_rows(gbuf[slot, c])
            hb[:, LANE * c:LANE * (c + 1)] = lo.astype(BF16)
            hb[:, HALF_D + LANE * c:HALF_D + LANE * (c + 1)] = hi.astype(BF16)
        for i in range(MOE_TILE):
            gather_row(nxt_ref, 1 - slot, i).start()
        h = hb[...]
        gate = _dot(h, wg_ref[...])
        up = _dot(h, wu_ref[...])
        a = (gate / (1.0 + jnp.exp(-gate)) * up).astype(BF16)

        @pl.when(t > 0)
        def _():
            wait_scatter(tv_ref[jnp.maximum(t - 1, 0)])

        ws = ws_ref[...]
        rounded = lambda v: v.astype(BF16).astype(F32)
        for c2 in range(TOK_WORDS // 2):
            cols = slice(2 * LANE * c2, 2 * LANE * (c2 + 1))
            lo = _dot(a, wd_ref[:, cols]) * ws
            hi = _dot(a, wd_ref[:, HALF_D + cols.start:HALF_D + cols.stop]) * ws
            words = _pack_rows(rounded(lo), rounded(hi))
            obuf[2 * c2] = words[:, :LANE]
            obuf[2 * c2 + 1] = words[:, LANE:]
        start_scatter(tv_ref[t])

        @pl.when(t == n_used - 1)
        def _():
            wait_gather(1 - slot)
            wait_scatter(tv_ref[t])


def _moe(tile_expert, next_expert, tile_valid, n_tiles, src_token, dst_row, x_tok, wg, wu, wd, w_slot):
    p_rows = src_token.shape[0]
    tm = MOE_TILE
    n_blocks = p_rows // tm
    smem_blk = lambda fn: pl.BlockSpec((tm,), fn, memory_space=pltpu.SMEM)
    hbm = pl.BlockSpec(memory_space=pl.ANY)
    grid_spec = pltpu.PrefetchScalarGridSpec(
        num_scalar_prefetch=4,
        grid=(n_blocks,),
        in_specs=[smem_blk(lambda t, te, ne, tv, nt: (t,)),
                  smem_blk(lambda t, te, ne, tv, nt: (jnp.minimum(t + 1, n_blocks - 1),)),
                  smem_blk(lambda t, te, ne, tv, nt: (t,)),
                  hbm, hbm, hbm, hbm,
                  pl.BlockSpec((tm, 1), lambda t, te, ne, tv, nt: (t, 0))],
        out_specs=hbm,
        scratch_shapes=[pltpu.VMEM((2, TOK_WORDS, tm, LANE), U32),
                        pltpu.VMEM((tm, D_MODEL), BF16),
                        pltpu.VMEM((TOK_WORDS, tm, LANE), U32),
                        pltpu.VMEM((D_MODEL, D_EXPERT), F32),
                        pltpu.VMEM((D_MODEL, D_EXPERT), F32),
                        pltpu.VMEM((D_EXPERT, D_MODEL), F32),
                        pltpu.VMEM((D_MODEL, D_EXPERT), BF16),
                        pltpu.VMEM((D_MODEL, D_EXPERT), BF16),
                        pltpu.VMEM((D_EXPERT, D_MODEL), BF16),
                        pltpu.SemaphoreType.DMA((2,)),
                        pltpu.SemaphoreType.DMA(()),
                        pltpu.SemaphoreType.DMA((3,))],
    )
    return pl.pallas_call(
        _moe_kernel,
        grid_spec=grid_spec,
        out_shape=jax.ShapeDtypeStruct((2 * x_tok.shape[0], TOK_WORDS, 1, LANE), U32),
        compiler_params=_cparams(("arbitrary",), 58),
        name="moe",
    )(tile_expert, next_expert, tile_valid, n_tiles, src_token, src_token, dst_row, x_tok, wg, wu, wd, w_slot)


def _ln_add_kernel(x_ref, g_ref, b_ref, y_hbm, o_ref, ybuf, vbuf, sem, *, tm):
    i = pl.program_id(0)
    slot = lax.rem(i, 2)

    def start_gather(tile, buf_slot):
        base = tile * (2 * tm)

        def body(j, carry):
            for k in range(2):
                pltpu.make_async_copy(y_hbm.at[base + 2 * j + k], ybuf.at[buf_slot, k, :, pl.ds(j, 1), :],
                                      sem.at[buf_slot]).start()
            return carry
        lax.fori_loop(0, tm, body, 0, unroll=4)

    @pl.when(i == 0)
    def _():
        start_gather(0, 0)

    @pl.when(i + 1 < pl.num_programs(0))
    def _():
        start_gather(i + 1, 1 - slot)

    pltpu.make_async_copy(ybuf.at[slot], ybuf.at[slot], sem.at[slot]).wait()
    for c in range(TOK_WORDS):
        a_lo, a_hi = _unpack_rows(ybuf[slot, 0, c])
        b_lo, b_hi = _unpack_rows(ybuf[slot, 1, c])
        lo = slice(LANE * c, LANE * (c + 1))
        hi = slice(HALF_D + LANE * c, HALF_D + LANE * (c + 1))
        vbuf[:, lo] = ALPHA * x_ref[:, lo] + (a_lo + b_lo)
        vbuf[:, hi] = ALPHA * x_ref[:, hi] + (a_hi + b_hi)
    o_ref[...] = _layer_norm(vbuf[...], g_ref[...], b_ref[...])


def _ln_add(x, y_tok, g, b):
    s = x.shape[0]
    tm = min(256, s)
    full = lambda shape: pl.BlockSpec(shape, lambda i: (0,) * len(shape))
    return pl.pallas_call(
        functools.partial(_ln_add_kernel, tm=tm),
        grid=(s // tm,),
        in_specs=[pl.BlockSpec((tm, D_MODEL), lambda i: (i, 0)), full(g.shape), full(b.shape),
                  pl.BlockSpec(memory_space=pl.ANY)],
        out_specs=pl.BlockSpec((tm, D_MODEL), lambda i: (i, 0)),
        out_shape=jax.ShapeDtypeStruct((s, D_MODEL), F32),
        scratch_shapes=[pltpu.VMEM((2, 2, TOK_WORDS, tm, LANE), U32),
                        pltpu.VMEM((tm, D_MODEL), F32),
                        pltpu.SemaphoreType.DMA((2,))],
        compiler_params=_cparams(("arbitrary",), 48),
        name="ln_add",
    )(x, g, b, y_tok)


def _w_in_sources():
    src = {}
    o = 0
    for name, width in (("q", 512), ("k", 512), ("v", 1024), ("r", 1024), ("gl", GLA_GATE_RANK), ("u", 1024),
                        ("vg", 1024), ("cb", 1024), ("cc", 1024), ("ch", 1024), ("cq", MLA_Q_LORA),
                        ("ckv", MLA_KV_LORA), ("kr", MLA_ROPE)):
        src[name] = (o, width)
        o += width
    return src, o


PACK_COLS = 512


def _w_in_segments():
    src, _ = _w_in_sources()
    half = MLA_ROPE // 2
    kr0 = src["kr"][0]
    segs = [(dst, src[name][0], src[name][1])
            for name, dst in (("cq", COL_CQ), ("ckv", COL_CKV), ("q", COL_Q), ("k", COL_K), ("v", COL_V),
                              ("r", COL_R), ("u", COL_U), ("vg", COL_VG), ("cb", COL_CB), ("cc", COL_CC),
                              ("ch", COL_CH))]
    segs += [(COL_KR, kr0, MLA_ROPE), (COL_KR + MLA_ROPE, None, LANE - MLA_ROPE),
             (COL_KRS, kr0 + half, half), (COL_KRS + half, kr0, half), (COL_KRS + MLA_ROPE, None, LANE - MLA_ROPE),
             (COL_GL, src["gl"][0], GLA_GATE_RANK), (COL_GL + GLA_GATE_RANK, None, P_COLS - COL_GL - GLA_GATE_RANK)]
    return sorted(segs)


def _pack_w_in_kernel(wt_hbm, o_hbm, stage, tbuf, in_sem, out_sem, *, layer):
    n_blocks = P_COLS // PACK_COLS
    segs = _w_in_segments()

    def pieces(j):
        lo, hi = j * PACK_COLS, (j + 1) * PACK_COLS
        out = []
        for dst, s0, width in segs:
            a, b = max(dst, lo), min(dst + width, hi)
            if a < b:
                out.append((a - lo, None if s0 is None else s0 + (a - dst), b - a))
        return out

    def loads(j, slot):
        return [pltpu.make_async_copy(wt_hbm.at[layer, pl.ds(s0, n), :], stage.at[slot, pl.ds(off, n), :],
                                      in_sem.at[slot])
                for off, s0, n in pieces(j) if s0 is not None]

    def store(j, slot):
        return pltpu.make_async_copy(tbuf.at[slot], o_hbm.at[:, pl.ds(j * PACK_COLS, PACK_COLS)], out_sem.at[slot])

    for cp in loads(0, 0):
        cp.start()
    for j in range(n_blocks):
        slot = j % 2
        if j + 1 < n_blocks:
            for cp in loads(j + 1, 1 - slot):
                cp.start()
        for cp in loads(j, slot):
            cp.wait()
        for off, s0, n in pieces(j):
            if s0 is None:
                stage[slot, off:off + n, :] = jnp.zeros((n, D_MODEL), F32)
        if j >= 2:
            store(j - 2, slot).wait()
        tbuf[slot] = jnp.transpose(stage[slot]).astype(BF16)
        store(j, slot).start()
    for j in range(max(n_blocks - 2, 0), n_blocks):
        store(j, j % 2).wait()


def _pack_w_in(wt, layer):
    _, d_in, k = wt.shape
    assert d_in == _w_in_sources()[1] and k == D_MODEL
    return pl.pallas_call(
        functools.partial(_pack_w_in_kernel, layer=layer),
        in_specs=[pl.BlockSpec(memory_space=pl.ANY)],
        out_specs=pl.BlockSpec(memory_space=pl.ANY),
        out_shape=jax.ShapeDtypeStruct((k, P_COLS), BF16),
        scratch_shapes=[pltpu.VMEM((2, PACK_COLS, D_MODEL), F32),
                        pltpu.VMEM((2, D_MODEL, PACK_COLS), BF16),
                        pltpu.SemaphoreType.DMA((2,)),
                        pltpu.SemaphoreType.DMA((2,))],
        compiler_params=pltpu.CompilerParams(vmem_limit_bytes=40 * MIB),
        name="pack_w_in",
    )(wt)


def _pack_mla_weights(wuq, wukv):
    half = MLA_ROPE // 2
    wq3 = wuq.reshape(MLA_Q_LORA, MLA_HEADS, MLA_NOPE + MLA_ROPE)
    rope = wq3[:, :, MLA_NOPE:]
    zq = jnp.zeros((MLA_Q_LORA, MLA_HEADS, MLA_QK_PAD - MLA_NOPE - MLA_ROPE), wuq.dtype)
    wq = jnp.concatenate([wq3, zq], axis=2).reshape(MLA_Q_LORA, MLA_HEADS * MLA_QK_PAD)
    zs = jnp.zeros((MLA_Q_LORA, MLA_HEADS, LANE - MLA_ROPE), wuq.dtype)
    wqs = jnp.concatenate([rope[:, :, half:], rope[:, :, :half], zs], axis=2).reshape(MLA_Q_LORA, MLA_HEADS * LANE)
    wkv3 = wukv.reshape(MLA_KV_LORA, MLA_HEADS, MLA_NOPE + MLA_V)
    wk = wkv3[:, :, :MLA_NOPE].reshape(MLA_KV_LORA, MLA_HEADS * MLA_NOPE)
    wv = wkv3[:, :, MLA_NOPE:].reshape(MLA_KV_LORA, MLA_HEADS * MLA_V)
    return wq.astype(BF16), wqs.astype(BF16), wk.astype(BF16), wv.astype(BF16)


def _rope_tables(s):
    pos = jnp.arange(s, dtype=F32)
    inv_freq = ROPE_BASE ** (-jnp.arange(0, MLA_ROPE, 2, dtype=F32) / MLA_ROPE)
    ang = pos[:, None] * inv_freq[None, :]
    cos, sin = jnp.cos(ang), jnp.sin(ang)
    z = jnp.zeros((s, LANE - MLA_ROPE), F32)
    return jnp.concatenate([cos, cos, z], axis=1), jnp.concatenate([-sin, sin, z], axis=1)


def _dispatch_plan(route, s):
    n_assign = 2 * s
    p_rows = n_assign + N_EXPERTS * MOE_TILE
    eid = route[:, 0:2].astype(jnp.int32).reshape(n_assign)
    wgt = route[:, 2:4].reshape(n_assign)
    onehot = (eid[:, None] == jnp.arange(N_EXPERTS, dtype=jnp.int32)[None, :]).astype(jnp.int32)
    csum = jnp.cumsum(onehot, axis=0)
    rank = jnp.sum(onehot * (csum - 1), axis=1)
    counts = csum[-1]
    tiles_e = (counts + MOE_TILE - 1) // MOE_TILE
    tile_end = jnp.cumsum(tiles_e)
    tile_start = tile_end - tiles_e
    slot = tile_start[eid] * MOE_TILE + rank
    slot_assign = jnp.full((p_rows,), -1, jnp.int32).at[slot].set(jnp.arange(n_assign, dtype=jnp.int32))
    is_pad = slot_assign < 0
    src_token = jnp.where(is_pad, 0, slot_assign // 2)
    dst_row = jnp.maximum(slot_assign, 0)
    w_slot = jnp.where(is_pad, 0.0, wgt[dst_row])
    n_blocks = p_rows // MOE_TILE
    tile_valid = jnp.sum(jnp.logical_not(is_pad).reshape(n_blocks, MOE_TILE).astype(jnp.int32), axis=1)
    tile_ids = jnp.arange(n_blocks, dtype=jnp.int32)
    tile_expert = jnp.minimum(jnp.sum((tile_end[None, :] <= tile_ids[:, None]).astype(jnp.int32), axis=1),
                              N_EXPERTS - 1)
    n_tiles = tile_end[-1:].astype(jnp.int32)
    experts = jnp.arange(N_EXPERTS, dtype=jnp.int32)
    later_with_tiles = (experts[None, :] > experts[:, None]) & (tiles_e[None, :] > 0)
    next_nonempty = jnp.min(jnp.where(later_with_tiles, experts[None, :], N_EXPERTS), axis=1)
    next_expert = next_nonempty[tile_expert]
    return src_token, dst_row, w_slot.reshape(p_rows, 1), tile_expert, next_expert, tile_valid, n_tiles


def kernel(x, w_in, gla_wa2, gla_ba, gla_norm, gm_ln_g, gm_ln_b, gm_ws, gm_bs, gm_norm, sc_conv, sc_norm, mla_q_norm, mla_kv_norm, mla_wuq, mla_wukv, mla_norm, w_o, ln1_g, ln1_b, router_g_w, router_g_b, router_e_w, router_e_b, exp_w_gate, exp_w_up, exp_w_down, ln2_g, ln2_b):
    bsz, s, _ = x.shape
    assert bsz == 1
    xc = x.reshape(s, D_MODEL)
    cos_t, sin_t = _rope_tables(s)
    row = lambda v: v.reshape(1, -1)
    wg_all = exp_w_gate.reshape(DEPTH * N_EXPERTS, D_MODEL, D_EXPERT)
    wu_all = exp_w_up.reshape(DEPTH * N_EXPERTS, D_MODEL, D_EXPERT)
    wd_all = exp_w_down.reshape(DEPTH * N_EXPERTS, D_EXPERT, D_MODEL)
    w_o_b = w_o.astype(BF16)
    w_in_t = jnp.swapaxes(w_in, 1, 2)
    for l in range(DEPTH):
        p = _mm_in(xc, _pack_w_in(w_in_t, l))

        wa2p = jnp.concatenate(
            [gla_wa2[l], jnp.zeros((LANE - GLA_GATE_RANK, GLA_HEADS * GLA_DK), F32)], axis=0).astype(BF16)
        out_a = _gla(p, wa2p, row(gla_ba[l]), row(gla_norm[l]))

        gm_bias = jnp.repeat(gm_bs[l].T, GM_CH, axis=1)
        out_b = _gmlp(p, row(gm_ln_g[l]), row(gm_ln_b[l]), gm_ws[l], gm_bias, row(gm_norm[l]))

        out_c = _sconv(p, sc_conv[l], row(sc_norm[l]))

        wq, wqs, wk, wv = _pack_mla_weights(mla_wuq[l], mla_wukv[l])
        q_att, k_att, v_att = _mla_proj(p, row(mla_q_norm[l]), row(mla_kv_norm[l]), wq, wqs, wk, wv, cos_t, sin_t)
        out_d = _flash(q_att, k_att, v_att, row(mla_norm[l]))

        y = _mm_out(out_a, out_b, out_c, out_d, w_o_b, l, xc)

        wr = jnp.concatenate([router_g_w[l], router_e_w[l],
                              jnp.zeros((D_MODEL, LANE - MOE_GROUPS - N_EXPERTS), F32)], axis=1)
        wr_hi = wr.astype(BF16)
        wr_lo = (wr - wr_hi.astype(F32)).astype(BF16)
        rb = jnp.concatenate([router_g_b[l], router_e_b[l], jnp.zeros((LANE - MOE_GROUPS - N_EXPERTS,), F32)])
        x1, x1_tok, route = _ln_route(y, row(ln1_g[l]), row(ln1_b[l]), wr_hi, wr_lo, row(rb))

        src_token, dst_row, w_slot, tile_expert, next_expert, tile_valid, n_tiles = _dispatch_plan(route, s)
        next_expert = jnp.where(next_expert < N_EXPERTS, next_expert + l * N_EXPERTS, -1)
        y_tok = _moe(tile_expert + l * N_EXPERTS, next_expert, tile_valid, n_tiles, src_token, dst_row, x1_tok,
                     wg_all, wu_all, wd_all, w_slot)
        xc = _ln_add(x1, y_tok, row(ln2_g[l]), row(ln2_b[l]))
    return xc.reshape(bsz, s, D_MODEL)
```

```python
import functools

import jax
import jax.numpy as jnp
from jax import lax
from jax.experimental import pallas as pl
from jax.experimental.pallas import tpu as pltpu

F32 = jnp.float32
BF16 = jnp.bfloat16

D_MODEL = 4096
DEPTH = 2
GROUP_W = 1024

GLA_HEADS = 4
GLA_DK = 128
GLA_DV = 256
GLA_GATE_RANK = 16
GLA_TAU = 16.0
GLA_CHUNK = 64
GLA_SUB = 16

GM_GROUPS = 8
GM_CH = 128
GM_CHUNK = 128

MLA_HEADS = 8
MLA_NOPE = 128
MLA_ROPE = 64
MLA_V = 128
MLA_Q_LORA = 768
MLA_KV_LORA = 256
ROPE_BASE = 10000.0
MLA_QK_PAD = 256

MOE_GROUPS = 4
MOE_PER_GROUP = 8
N_EXPERTS = 32
D_EXPERT = 512

ALPHA = (2.0 * DEPTH) ** 0.25
LOG2_E = 1.4426950408889634

LANE = 128
MIB = 1024 * 1024

COL_CQ = 0
COL_CKV = 768
COL_Q = 1024
COL_K = 1536
COL_V = 2048
COL_R = 3072
COL_U = 4096
COL_VG = 5120
COL_CB = 6144
COL_CC = 7168
COL_CH = 8192
COL_KR = 9216
COL_KRS = 9344
COL_GL = 9472
P_COLS = 9728

MOE_TILE = 256
SCATTER_UNROLL = 8
WEIGHT_ROUND_ROWS = 512
HALF_D = D_MODEL // 2
TOK_WORDS = HALF_D // LANE
U32 = jnp.uint32
HIGH_HALF = 0xFFFF0000


def _pack_rows(lo_f32, hi_f32):
    return (lax.bitcast_convert_type(lo_f32, U32) >> 16) | lax.bitcast_convert_type(hi_f32, U32)


def _unpack_rows(words):
    return (lax.bitcast_convert_type(words << 16, F32),
            lax.bitcast_convert_type(words & jnp.uint32(HIGH_HALF), F32))


def _cparams(sem, vmem_mib):
    return pltpu.CompilerParams(dimension_semantics=sem, vmem_limit_bytes=vmem_mib * MIB)


def _dot(a, b):
    return jnp.dot(a, b, preferred_element_type=F32)


def _dot_nt(a, b):
    return lax.dot_general(a, b, (((1,), (1,)), ((), ())), preferred_element_type=F32)


def _dot_tn(a, b):
    return lax.dot_general(a, b, (((0,), (0,)), ((), ())), preferred_element_type=F32)


def _mm_in_kernel(x_ref, w_ref, o_ref, xb_ref):
    @pl.when(pl.program_id(1) == 0)
    def _():
        xb_ref[...] = x_ref[...].astype(BF16)

    o_ref[...] = _dot(xb_ref[...], w_ref[...]).astype(o_ref.dtype)


def _mm_in(x, w):
    s, k = x.shape
    n = w.shape[1]
    tm = min(1024, s)
    tn = 512
    return pl.pallas_call(
        _mm_in_kernel,
        grid=(s // tm, n // tn),
        in_specs=[pl.BlockSpec((tm, k), lambda i, j: (i, 0)),
                  pl.BlockSpec((k, tn), lambda i, j: (0, j))],
        out_specs=pl.BlockSpec((tm, tn), lambda i, j: (i, j)),
        out_shape=jax.ShapeDtypeStruct((s, n), BF16),
        scratch_shapes=[pltpu.VMEM((tm, k), BF16)],
        compiler_params=_cparams(("parallel", "arbitrary"), 58),
        name="mm_in",
    )(x, w)


def _gla_kernel(q_ref, k_ref, v_ref, r_ref, gl_ref, wa2_ref, ba_ref, ng_ref, o_ref, s_ref, *, n_chunks):
    c_len = GLA_CHUNK

    @pl.when(pl.program_id(0) == 0)
    def _():
        s_ref[...] = jnp.zeros_like(s_ref)

    row = lax.broadcasted_iota(jnp.int32, (c_len, c_len), 0)
    col = lax.broadcasted_iota(jnp.int32, (c_len, c_len), 1)
    tril = jnp.where(col <= row, 1.0, 0.0).astype(BF16)
    sub_row = lax.broadcasted_iota(jnp.int32, (GLA_SUB, c_len), 0)
    sub_col = lax.broadcasted_iota(jnp.int32, (GLA_SUB, c_len), 1)
    n_sub = c_len // GLA_SUB

    def chunk(c, carry):
        rows = pl.ds(pl.multiple_of(c * c_len, c_len), c_len)
        logit = _dot(gl_ref[rows, :], wa2_ref[...]) + ba_ref[...]
        g = (jnp.minimum(logit, 0.0) - jnp.log(1.0 + jnp.exp(-jnp.abs(logit)))) * (1.0 / GLA_TAU)
        g_hi = g.astype(BF16)
        g_lo = (g - g_hi.astype(F32)).astype(BF16)
        b_all = _dot(tril, g_hi) + _dot(tril, g_lo)

        for h in range(GLA_HEADS):
            hs = slice(GLA_DK * h, GLA_DK * (h + 1))
            vs = slice(GLA_DV * h, GLA_DV * (h + 1))
            bh = b_all[:, hs]
            qh = q_ref[rows, hs].astype(F32) * (GLA_DK ** -0.5)
            kh = k_ref[rows, hs].astype(F32)
            vh = v_ref[rows, vs]
            state = s_ref[h]

            o = _dot((qh * jnp.exp(bh)).astype(BF16), state.astype(BF16))

            att_rows = []
            for blk in range(n_sub):
                sl = slice(GLA_SUB * blk, GLA_SUB * (blk + 1))
                b_blk = bh[sl]
                q_blk = qh[sl]
                att = jnp.zeros((GLA_SUB, c_len), F32)
                if blk > 0:
                    ref = bh[GLA_SUB * blk:GLA_SUB * blk + 1, :]
                    qs = (q_blk * jnp.exp(b_blk - ref)).astype(BF16)
                    ks = (kh * jnp.exp(jnp.minimum(ref - bh, 0.0))).astype(BF16)
                    att = jnp.where(sub_col < GLA_SUB * blk, _dot_nt(qs, ks), 0.0)
                for jj in range(GLA_SUB):
                    j = GLA_SUB * blk + jj
                    t = q_blk * kh[j:j + 1, :] * jnp.exp(b_blk - bh[j:j + 1, :])
                    rs = jnp.sum(t, axis=-1, keepdims=True)
                    att = jnp.where((sub_col == j) & (sub_row >= jj), rs, att)
                att_rows.append(att)
            att_full = jnp.concatenate(att_rows, axis=0).astype(BF16)
            o = o + _dot(att_full, vh)

            b_last = bh[c_len - 1:c_len, :]
            kd = (kh * jnp.exp(b_last - bh)).astype(BF16)
            decay_col = jnp.transpose(jnp.broadcast_to(jnp.exp(b_last), (GLA_DK, GLA_DK)))
            s_ref[h] = state * jnp.concatenate([decay_col, decay_col], axis=1) + _dot_tn(kd, vh)

            var = jnp.mean(o * o, axis=-1, keepdims=True)
            on = o * lax.rsqrt(var + 1e-6) * ng_ref[:, vs]
            rr = r_ref[rows, vs].astype(F32)
            o_ref[rows, vs] = (on * (rr / (1.0 + jnp.exp(-rr)))).astype(o_ref.dtype)
        return carry

    lax.fori_loop(0, n_chunks, chunk, 0, unroll=2)


def _gla(p, wa2p, ba, ng):
    s = p.shape[0]
    t = min(512, s)
    col = lambda width, off: pl.BlockSpec((t, width), lambda i: (i, off // width))
    full = lambda shape: pl.BlockSpec(shape, lambda i: (0,) * len(shape))
    return pl.pallas_call(
        functools.partial(_gla_kernel, n_chunks=t // GLA_CHUNK),
        grid=(s // t,),
        in_specs=[col(512, COL_Q), col(512, COL_K), col(1024, COL_V), col(1024, COL_R), col(LANE, COL_GL),
                  full(wa2p.shape), full(ba.shape), full(ng.shape)],
        out_specs=pl.BlockSpec((t, GROUP_W), lambda i: (i, 0)),
        out_shape=jax.ShapeDtypeStruct((s, GROUP_W), BF16),
        scratch_shapes=[pltpu.VMEM((GLA_HEADS, GLA_DK, GLA_DV), F32)],
        compiler_params=_cparams(("arbitrary",), 32),
        name="gla",
    )(p, p, p, p, p, wa2p, ba, ng)


def _gelu(x):
    return 0.5 * x * (1.0 + lax.erf(x * 0.7071067811865476))


def _gmlp_kernel(u_ref, v_ref, lg_ref, lb_ref, ws_ref, bias_ref, ng_ref, o_ref, buf_ref, *, n_chunks):
    row = lax.broadcasted_iota(jnp.int32, (GM_CHUNK, GM_CHUNK), 0)
    col = lax.broadcasted_iota(jnp.int32, (GM_CHUNK, GM_CHUNK), 1)
    causal = col <= row
    v = _gelu(v_ref[...].astype(F32))
    mu = jnp.mean(v, axis=-1, keepdims=True)
    d = v - mu
    var = jnp.mean(d * d, axis=-1, keepdims=True)
    vb = (d * lax.rsqrt(var + 1e-5) * lg_ref[...] + lb_ref[...]).astype(BF16)
    for g in range(GM_GROUPS):
        cs = slice(GM_CH * g, GM_CH * (g + 1))
        w = jnp.where(causal, ws_ref[g], 0.0).astype(BF16)
        for c in range(n_chunks):
            rs = slice(GM_CHUNK * c, GM_CHUNK * (c + 1))
            mixed = _dot(w, vb[rs, cs]) + bias_ref[:, cs]
            buf_ref[rs, cs] = _gelu(u_ref[rs, cs].astype(F32)) * mixed
    out = buf_ref[...]
    ms = jnp.mean(out * out, axis=-1, keepdims=True)
    o_ref[...] = (out * lax.rsqrt(ms + 1e-6) * ng_ref[...]).astype(o_ref.dtype)


def _gmlp(p, lg, lb, ws, bias, ng):
    s = p.shape[0]
    t = min(256, s)
    col = lambda off: pl.BlockSpec((t, GROUP_W), lambda i: (i, off // GROUP_W))
    full = lambda shape: pl.BlockSpec(shape, lambda i: (0,) * len(shape))
    return pl.pallas_call(
        functools.partial(_gmlp_kernel, n_chunks=t // GM_CHUNK),
        grid=(s // t,),
        in_specs=[col(COL_U), col(COL_VG), full(lg.shape), full(lb.shape), full(ws.shape), full(bias.shape),
                  full(ng.shape)],
        out_specs=pl.BlockSpec((t, GROUP_W), lambda i: (i, 0)),
        out_shape=jax.ShapeDtypeStruct((s, GROUP_W), BF16),
        scratch_shapes=[pltpu.VMEM((t, GROUP_W), F32)],
        compiler_params=_cparams(("parallel",), 32),
        name="gmlp",
    )(p, p, lg, lb, ws, bias, ng)


CONV_HALO = 8


def _sconv_kernel(b_ref, c_ref, h_ref, w_ref, ng_ref, o_ref, z_ref):
    t = b_ref.shape[0]

    @pl.when(pl.program_id(0) == 0)
    def _():
        z_ref[0:CONV_HALO, :] = jnp.zeros((CONV_HALO, GROUP_W), F32)

    z = c_ref[...].astype(F32) * h_ref[...].astype(F32)
    z_ref[CONV_HALO:CONV_HALO + t, :] = z
    z1 = z_ref[CONV_HALO - 1:CONV_HALO - 1 + t, :]
    z2 = z_ref[CONV_HALO - 2:CONV_HALO - 2 + t, :]
    y = w_ref[0:1, :] * z2 + w_ref[1:2, :] * z1 + w_ref[2:3, :] * z
    z_ref[0:CONV_HALO, :] = z[t - CONV_HALO:t, :]
    out = b_ref[...].astype(F32) * y
    ms = jnp.mean(out * out, axis=-1, keepdims=True)
    o_ref[...] = (out * lax.rsqrt(ms + 1e-6) * ng_ref[...]).astype(o_ref.dtype)


def _sconv(p, w, ng):
    s = p.shape[0]
    t = min(256, s)
    col = lambda off: pl.BlockSpec((t, GROUP_W), lambda i: (i, off // GROUP_W))
    full = lambda shape: pl.BlockSpec(shape, lambda i: (0,) * len(shape))
    return pl.pallas_call(
        _sconv_kernel,
        grid=(s // t,),
        in_specs=[col(COL_CB), col(COL_CC), col(COL_CH), full(w.shape), full(ng.shape)],
        out_specs=pl.BlockSpec((t, GROUP_W), lambda i: (i, 0)),
        out_shape=jax.ShapeDtypeStruct((s, GROUP_W), BF16),
        scratch_shapes=[pltpu.VMEM((t + CONV_HALO, GROUP_W), F32)],
        compiler_params=_cparams(("arbitrary",), 32),
        name="sconv",
    )(p, p, p, w, ng)


def _mla_proj_kernel(cq_ref, ckv_ref, kr_ref, krs_ref, qn_ref, kvn_ref, wq_ref, wqs_ref, wk_ref, wv_ref,
                     cos_ref, sin_ref, q_out, k_out, v_out):
    def rms(ref, g_ref):
        t = ref[...].astype(F32)
        return (t * lax.rsqrt(jnp.mean(t * t, axis=-1, keepdims=True) + 1e-6) * g_ref[...]).astype(BF16)

    cqn = rms(cq_ref, qn_ref)
    ckvn = rms(ckv_ref, kvn_ref)
    qm = _dot(cqn, wq_ref[...])
    qsw = _dot(cqn, wqs_ref[...])
    kn = _dot(ckvn, wk_ref[...])
    vv = _dot(ckvn, wv_ref[...])
    cos = cos_ref[...]
    sin = sin_ref[...]
    scale = (MLA_NOPE + MLA_ROPE) ** -0.5 * LOG2_E
    kr_rot = (kr_ref[...].astype(F32) * cos + krs_ref[...].astype(F32) * sin).astype(k_out.dtype)
    ones_col = jnp.where(lax.broadcasted_iota(jnp.int32, (cos.shape[0], LANE), 1) == 0, 1.0, 0.0).astype(v_out.dtype)
    for h in range(MLA_HEADS):
        lo = MLA_QK_PAD * h
        v_out[:, lo:lo + LANE] = vv[:, LANE * h:LANE * (h + 1)].astype(v_out.dtype)
        v_out[:, lo + LANE:lo + 2 * LANE] = ones_col
        q_out[:, lo:lo + LANE] = (qm[:, lo:lo + LANE] * scale).astype(q_out.dtype)
        q_out[:, lo + LANE:lo + 2 * LANE] = (
            (qm[:, lo + LANE:lo + 2 * LANE] * cos + qsw[:, LANE * h:LANE * (h + 1)] * sin) * scale
        ).astype(q_out.dtype)
        k_out[:, lo:lo + LANE] = kn[:, LANE * h:LANE * (h + 1)].astype(k_out.dtype)
        k_out[:, lo + LANE:lo + 2 * LANE] = kr_rot


def _mla_proj(p, qn, kvn, wq, wqs, wk, wv, cos, sin):
    s = p.shape[0]
    t = min(256, s)
    full = lambda shape: pl.BlockSpec(shape, lambda i: (0,) * len(shape))
    rowb = lambda width: pl.BlockSpec((t, width), lambda i: (i, 0))
    qk_w = MLA_HEADS * MLA_QK_PAD
    return pl.pallas_call(
        _mla_proj_kernel,
        grid=(s // t,),
        in_specs=[pl.BlockSpec((t, MLA_Q_LORA), lambda i: (i, COL_CQ // MLA_Q_LORA)),
                  pl.BlockSpec((t, MLA_KV_LORA), lambda i: (i, COL_CKV // MLA_KV_LORA)),
                  pl.BlockSpec((t, LANE), lambda i: (i, COL_KR // LANE)),
                  pl.BlockSpec((t, LANE), lambda i: (i, COL_KRS // LANE)),
                  full(qn.shape), full(kvn.shape), full(wq.shape), full(wqs.shape), full(wk.shape), full(wv.shape),
                  rowb(LANE), rowb(LANE)],
        out_specs=[rowb(qk_w), rowb(qk_w), rowb(qk_w)],
        out_shape=[jax.ShapeDtypeStruct((s, qk_w), BF16)] * 3,
        compiler_params=_cparams(("parallel",), 40),
        name="mla_proj",
    )(p, p, p, p, qn, kvn, wq, wqs, wk, wv, cos, sin)


FLASH_TQ = 512
FLASH_TK = 1024


def _flash_kernel(qi_ref, ki_ref, q_ref, k_ref, v_ref, ng_ref, o_ref, acc_ref, m_ref, *, tq, tk):
    step = pl.program_id(0)
    qi = qi_ref[step]
    ki = ki_ref[step]
    last_ki = (qi * tq) // tk

    @pl.when(ki == 0)
    def _():
        m_ref[...] = jnp.full(m_ref.shape, -jnp.inf, F32)
        acc_ref[...] = jnp.zeros_like(acc_ref)

    def accumulate(masked, k0, kn):
        keys = slice(k0, k0 + kn)
        if masked:
            row = qi * tq + lax.broadcasted_iota(jnp.int32, (tq, kn), 0)
            col = ki * tk + k0 + lax.broadcasted_iota(jnp.int32, (tq, kn), 1)
            visible = col <= row

        def scores(h):
            hs = slice(MLA_QK_PAD * h, MLA_QK_PAD * (h + 1))
            return _dot_nt(q_ref[:, hs], k_ref[keys, hs])

        sc = scores(0)
        for h in range(MLA_HEADS):
            hs = slice(MLA_QK_PAD * h, MLA_QK_PAD * (h + 1))
            sc_next = scores(h + 1) if h + 1 < MLA_HEADS else None
            if masked:
                sc = jnp.where(visible, sc, -jnp.inf)
            m_old = m_ref[h]
            m_new = jnp.maximum(m_old, jnp.max(sc, axis=-1, keepdims=True))
            pr = jnp.exp2(sc - m_new).astype(BF16)
            acc_ref[:, hs] = jnp.exp2(m_old - m_new) * acc_ref[:, hs] + _dot(pr, v_ref[keys, hs])
            m_ref[h] = m_new
            sc = sc_next

    @pl.when(ki < last_ki)
    def _():
        accumulate(False, 0, tk)

    @pl.when(ki == last_ki)
    def _():
        accumulate(True, 0, tk)
        outs = []
        for h in range(MLA_HEADS):
            lo = MLA_QK_PAD * h
            outs.append(acc_ref[:, lo:lo + MLA_V] / acc_ref[:, lo + MLA_V:lo + MLA_V + 1])
        o = jnp.concatenate(outs, axis=1)
        ms = jnp.mean(o * o, axis=-1, keepdims=True)
        o_ref[...] = (o * lax.rsqrt(ms + 1e-6) * ng_ref[...]).astype(o_ref.dtype)


def _flash(q, k, v, ng):
    s = q.shape[0]
    tk = min(FLASH_TK, s)
    tq = min(FLASH_TQ, tk)
    nq = s // tq
    pairs = [(a, b) for a in range(nq) for b in range((a * tq) // tk + 1)]
    qi_tab = jnp.asarray([a for a, _ in pairs], jnp.int32)
    ki_tab = jnp.asarray([b for _, b in pairs], jnp.int32)
    qk_w = MLA_HEADS * MLA_QK_PAD
    grid_spec = pltpu.PrefetchScalarGridSpec(
        num_scalar_prefetch=2,
        grid=(len(pairs),),
        in_specs=[pl.BlockSpec((tq, qk_w), lambda t, qi, ki: (qi[t], 0)),
                  pl.BlockSpec((tk, qk_w), lambda t, qi, ki: (ki[t], 0)),
                  pl.BlockSpec((tk, qk_w), lambda t, qi, ki: (ki[t], 0)),
                  pl.BlockSpec((1, GROUP_W), lambda t, qi, ki: (0, 0))],
        out_specs=pl.BlockSpec((tq, GROUP_W), lambda t, qi, ki: (qi[t], 0)),
        scratch_shapes=[pltpu.VMEM((tq, qk_w), F32),
                        pltpu.VMEM((MLA_HEADS, tq, 1), F32)],
    )
    return pl.pallas_call(
        functools.partial(_flash_kernel, tq=tq, tk=tk),
        grid_spec=grid_spec,
        out_shape=jax.ShapeDtypeStruct((s, GROUP_W), BF16),
        compiler_params=_cparams(("arbitrary",), 56),
        name="flash",
    )(qi_tab, ki_tab, q, k, v, ng)


def _mm_out_kernel(a0_ref, a1_ref, a2_ref, a3_ref, w_ref, x_ref, o_ref):
    acc = _dot(a0_ref[...], w_ref[0, 0:GROUP_W, :])
    acc = acc + _dot(a1_ref[...], w_ref[0, GROUP_W:2 * GROUP_W, :])
    acc = acc + _dot(a2_ref[...], w_ref[0, 2 * GROUP_W:3 * GROUP_W, :])
    acc = acc + _dot(a3_ref[...], w_ref[0, 3 * GROUP_W:4 * GROUP_W, :])
    o_ref[...] = ALPHA * x_ref[...] + acc


def _mm_out(a0, a1, a2, a3, w, layer, x):
    s = x.shape[0]
    tm = min(1024, s)
    tn = 1024
    a_spec = pl.BlockSpec((tm, GROUP_W), lambda i, j: (i, 0))
    return pl.pallas_call(
        _mm_out_kernel,
        grid=(s // tm, D_MODEL // tn),
        in_specs=[a_spec, a_spec, a_spec, a_spec,
                  pl.BlockSpec((1, D_MODEL, tn), lambda i, j: (layer, 0, j)),
                  pl.BlockSpec((tm, tn), lambda i, j: (i, j))],
        out_specs=pl.BlockSpec((tm, tn), lambda i, j: (i, j)),
        out_shape=jax.ShapeDtypeStruct((s, D_MODEL), F32),
        compiler_params=_cparams(("parallel", "arbitrary"), 58),
        name="mm_out",
    )(a0, a1, a2, a3, w, x)


def _layer_norm(t, g, b):
    mu = jnp.mean(t, axis=-1, keepdims=True)
    d = t - mu
    var = jnp.mean(d * d, axis=-1, keepdims=True)
    return d * lax.rsqrt(var + 1e-5) * g + b


def _ln_route_kernel(y_ref, g_ref, b_ref, wh_ref, wl_ref, rb_ref, x_out, route_out, xtok_hbm, tbuf, sem, *, tm):
    i = pl.program_id(0)
    slot = lax.rem(i, 2)

    def wait_rows(buf_slot):
        pltpu.make_async_copy(tbuf.at[buf_slot], tbuf.at[buf_slot], sem.at[buf_slot]).wait()

    x1 = _layer_norm(y_ref[...], g_ref[...], b_ref[...])
    x_out[...] = x1
    x_hi = x1.astype(BF16)
    x_hi32 = x_hi.astype(F32)

    @pl.when(i >= 2)
    def _():
        wait_rows(slot)

    words = _pack_rows(x_hi32[:, :HALF_D], x_hi32[:, HALF_D:])
    for c in range(TOK_WORDS):
        tbuf[slot, c] = words[:, LANE * c:LANE * (c + 1)]
    for j in range(tm):
        pltpu.make_async_copy(tbuf.at[slot, :, pl.ds(j, 1), :], xtok_hbm.at[i * tm + j], sem.at[slot]).start()

    @pl.when(i == pl.num_programs(0) - 1)
    def _():
        wait_rows(slot)

        @pl.when(i >= 1)
        def _():
            wait_rows(1 - slot)

    x_lo = (x1 - x_hi32).astype(BF16)
    logits = _dot(x_hi, wh_ref[...]) + _dot(x_lo, wh_ref[...]) + _dot(x_hi, wl_ref[...]) + rb_ref[...]
    lane = lax.broadcasted_iota(jnp.int32, logits.shape, 1).astype(F32)
    neg = -jnp.inf
    big = 1e9
    is_grp = lane < MOE_GROUPS
    lg = jnp.where(is_grp, logits, neg)
    mg = jnp.max(lg, axis=-1, keepdims=True)
    gsel = jnp.min(jnp.where(lg == mg, lane, big), axis=-1, keepdims=True)
    pg_sel = 1.0 / jnp.sum(jnp.where(is_grp, jnp.exp(lg - mg), 0.0), axis=-1, keepdims=True)
    lo = MOE_GROUPS + MOE_PER_GROUP * gsel
    le = jnp.where((lane >= lo) & (lane < lo + MOE_PER_GROUP), logits, neg)
    v1 = jnp.max(le, axis=-1, keepdims=True)
    i1 = jnp.min(jnp.where(le == v1, lane, big), axis=-1, keepdims=True)
    le2 = jnp.where(lane == i1, neg, le)
    v2 = jnp.max(le2, axis=-1, keepdims=True)
    i2 = jnp.min(jnp.where(le2 == v2, lane, big), axis=-1, keepdims=True)
    e = jnp.exp(v2 - v1)
    w1 = pg_sel / (1.0 + e)
    w2 = pg_sel * e / (1.0 + e)
    route = jnp.where(lane == 0.0, i1 - MOE_GROUPS,
                      jnp.where(lane == 1.0, i2 - MOE_GROUPS,
                                jnp.where(lane == 2.0, w1, jnp.where(lane == 3.0, w2, 0.0))))
    route_out[...] = route


def _ln_route(y, g, b, wh, wl, rb):
    s = y.shape[0]
    tm = min(256, s)
    full = lambda shape: pl.BlockSpec(shape, lambda i: (0,) * len(shape))
    rowb = lambda width: pl.BlockSpec((tm, width), lambda i: (i, 0))
    x1, route, x1_tok = pl.pallas_call(
        functools.partial(_ln_route_kernel, tm=tm),
        grid=(s // tm,),
        in_specs=[rowb(D_MODEL), full(g.shape), full(b.shape), full(wh.shape), full(wl.shape), full(rb.shape)],
        out_specs=[rowb(D_MODEL), rowb(LANE), pl.BlockSpec(memory_space=pl.ANY)],
        out_shape=[jax.ShapeDtypeStruct((s, D_MODEL), F32),
                   jax.ShapeDtypeStruct((s, LANE), F32),
                   jax.ShapeDtypeStruct((s, TOK_WORDS, 1, LANE), U32)],
        scratch_shapes=[pltpu.VMEM((2, TOK_WORDS, tm, LANE), U32), pltpu.SemaphoreType.DMA((2,))],
        compiler_params=_cparams(("arbitrary",), 48),
        name="ln_route",
    )(y, g, b, wh, wl, rb)
    return x1, x1_tok, route


def _moe_kernel(te_ref, ne_ref, tv_ref, nt_ref, src_ref, nxt_ref, dst_ref, x_hbm, wg_hbm, wu_hbm, wd_hbm, ws_ref,
                y_hbm, gbuf, hb, obuf, stg_g, stg_u, stg_d, wg_ref, wu_ref, wd_ref, gsem, ssem, wsem):
    t = pl.program_id(0)
    n_used = nt_ref[0]
    slot = lax.rem(t, 2)

    def weight_copies(e):
        return (pltpu.make_async_copy(wg_hbm.at[e], stg_g, wsem.at[0]),
                pltpu.make_async_copy(wu_hbm.at[e], stg_u, wsem.at[1]),
                pltpu.make_async_copy(wd_hbm.at[e], stg_d, wsem.at[2]))

    def round_weights(stg, dst, rows):
        def body(r, carry):
            rs = pl.ds(pl.multiple_of(r * rows, rows), rows)
            dst[rs, :] = stg[rs, :].astype(BF16)
            return carry
        lax.fori_loop(0, stg.shape[0] // rows, body, 0)

    @pl.when(t == 0)
    def _():
        for cp in weight_copies(te_ref[0]):
            cp.start()

    @pl.when(t < n_used)
    def _():
        expert = te_ref[t]
        first_tile_of_expert = jnp.logical_or(t == 0, expert != te_ref[jnp.maximum(t - 1, 0)])

        @pl.when(first_tile_of_expert)
        def _():
            for cp in weight_copies(expert):
                cp.wait()
            round_weights(stg_g, wg_ref, WEIGHT_ROUND_ROWS)
            round_weights(stg_u, wu_ref, WEIGHT_ROUND_ROWS)
            round_weights(stg_d, wd_ref, WEIGHT_ROUND_ROWS * D_EXPERT // D_MODEL)

            @pl.when(ne_ref[t] >= 0)
            def _():
                for cp in weight_copies(ne_ref[t]):
                    cp.start()

    def gather_row(idx_ref, buf_slot, i):
        return pltpu.make_async_copy(x_hbm.at[idx_ref[i]], gbuf.at[buf_slot, :, pl.ds(i, 1), :], gsem.at[buf_slot])

    def wait_gather(buf_slot):
        pltpu.make_async_copy(gbuf.at[buf_slot], gbuf.at[buf_slot], gsem.at[buf_slot]).wait()

    def scatter_row(i):
        return pltpu.make_async_copy(obuf.at[:, pl.ds(i, 1), :], y_hbm.at[dst_ref[i]], ssem)

    def start_scatter(n_rows):
        def group(g, carry):
            for u in range(SCATTER_UNROLL):
                scatter_row(g * SCATTER_UNROLL + u).start()
            return carry

        def single(i, carry):
            scatter_row(i).start()
            return carry
        n_groups = n_rows // SCATTER_UNROLL
        lax.fori_loop(0, n_groups, group, 0)
        lax.fori_loop(n_groups * SCATTER_UNROLL, n_rows, single, 0)

    def wait_scatter(n_rows):
        rows = obuf.at[:, pl.ds(0, n_rows), :]
        pltpu.make_async_copy(rows, rows, ssem).wait()

    @pl.when(t == 0)
    def _():
        def body(i, carry):
            gather_row(src_ref, 0, i).start()
            return carry
        lax.fori_loop(0, MOE_TILE, body, 0, unroll=8)

    @pl.when(t < n_used)
    def _():
        wait_gather(slot)
        for c in range(TOK_WORDS):
            lo, hi = _unpack_rows(gbuf[slot, c])
            hb[:, LANE * c:LANE * (c + 1)] = lo.astype(BF16)
            hb[:, HALF_D + LANE * c:HALF_D + LANE * (c + 1)] = hi.astype(BF16)
        for i in range(MOE_TILE):
            gather_row(nxt_ref, 1 - slot, i).start()
        h = hb[...]
        gate = _dot(h, wg_ref[...])
        up = _dot(h, wu_ref[...])
        a = (gate / (1.0 + jnp.exp(-gate)) * up).astype(BF16)

        @pl.when(t > 0)
        def _():
            wait_scatter(tv_ref[jnp.maximum(t - 1, 0)])

        ws = ws_ref[...]
        rounded = lambda v: v.astype(BF16).astype(F32)
        for c2 in range(TOK_WORDS // 2):
            cols = slice(2 * LANE * c2, 2 * LANE * (c2 + 1))
            lo = _dot(a, wd_ref[:, cols]) * ws
            hi = _dot(a, wd_ref[:, HALF_D + cols.start:HALF_D + cols.stop]) * ws
            words = _pack_rows(rounded(lo), rounded(hi))
            obuf[2 * c2] = words[:, :LANE]
            obuf[2 * c2 + 1] = words[:, LANE:]
        start_scatter(tv_ref[t])

        @pl.when(t == n_used - 1)
        def _():
            wait_gather(1 - slot)
            wait_scatter(tv_ref[t])


def _moe(tile_expert, next_expert, tile_valid, n_tiles, src_token, dst_row, x_tok, wg, wu, wd, w_slot):
    p_rows = src_token.shape[0]
    tm = MOE_TILE
    n_blocks = p_rows // tm
    smem_blk = lambda fn: pl.BlockSpec((tm,), fn, memory_space=pltpu.SMEM)
    hbm = pl.BlockSpec(memory_space=pl.ANY)
    grid_spec = pltpu.PrefetchScalarGridSpec(
        num_scalar_prefetch=4,
        grid=(n_blocks,),
        in_specs=[smem_blk(lambda t, te, ne, tv, nt: (t,)),
                  smem_blk(lambda t, te, ne, tv, nt: (jnp.minimum(t + 1, n_blocks - 1),)),
                  smem_blk(lambda t, te, ne, tv, nt: (t,)),
                  hbm, hbm, hbm, hbm,
                  pl.BlockSpec((tm, 1), lambda t, te, ne, tv, nt: (t, 0))],
        out_specs=hbm,
        scratch_shapes=[pltpu.VMEM((2, TOK_WORDS, tm, LANE), U32),
                        pltpu.VMEM((tm, D_MODEL), BF16),
                        pltpu.VMEM((TOK_WORDS, tm, LANE), U32),
                        pltpu.VMEM((D_MODEL, D_EXPERT), F32),
                        pltpu.VMEM((D_MODEL, D_EXPERT), F32),
                        pltpu.VMEM((D_EXPERT, D_MODEL), F32),
                        pltpu.VMEM((D_MODEL, D_EXPERT), BF16),
                        pltpu.VMEM((D_MODEL, D_EXPERT), BF16),
                        pltpu.VMEM((D_EXPERT, D_MODEL), BF16),
                        pltpu.SemaphoreType.DMA((2,)),
                        pltpu.SemaphoreType.DMA(()),
                        pltpu.SemaphoreType.DMA((3,))],
    )
    return pl.pallas_call(
        _moe_kernel,
        grid_spec=grid_spec,
        out_shape=jax.ShapeDtypeStruct((2 * x_tok.shape[0], TOK_WORDS, 1, LANE), U32),
        compiler_params=_cparams(("arbitrary",), 58),
        name="moe",
    )(tile_expert, next_expert, tile_valid, n_tiles, src_token, src_token, dst_row, x_tok, wg, wu, wd, w_slot)


def _ln_add_kernel(x_ref, g_ref, b_ref, y_hbm, o_ref, ybuf, vbuf, sem, *, tm):
    i = pl.program_id(0)
    slot = lax.rem(i, 2)

    def start_gather(tile, buf_slot):
        base = tile * (2 * tm)

        def body(j, carry):
            for k in range(2):
                pltpu.make_async_copy(y_hbm.at[base + 2 * j + k], ybuf.at[buf_slot, k, :, pl.ds(j, 1), :],
                                      sem.at[buf_slot]).start()
            return carry
        lax.fori_loop(0, tm, body, 0, unroll=4)

    @pl.when(i == 0)
    def _():
        start_gather(0, 0)

    @pl.when(i + 1 < pl.num_programs(0))
    def _():
        start_gather(i + 1, 1 - slot)

    pltpu.make_async_copy(ybuf.at[slot], ybuf.at[slot], sem.at[slot]).wait()
    for c in range(TOK_WORDS):
        a_lo, a_hi = _unpack_rows(ybuf[slot, 0, c])
        b_lo, b_hi = _unpack_rows(ybuf[slot, 1, c])
        lo = slice(LANE * c, LANE * (c + 1))
        hi = slice(HALF_D + LANE * c, HALF_D + LANE * (c + 1))
        vbuf[:, lo] = ALPHA * x_ref[:, lo] + (a_lo + b_lo)
        vbuf[:, hi] = ALPHA * x_ref[:, hi] + (a_hi + b_hi)
    o_ref[...] = _layer_norm(vbuf[...], g_ref[...], b_ref[...])


def _ln_add(x, y_tok, g, b):
    s = x.shape[0]
    tm = min(256, s)
    full = lambda shape: pl.BlockSpec(shape, lambda i: (0,) * len(shape))
    return pl.pallas_call(
        functools.partial(_ln_add_kernel, tm=tm),
        grid=(s // tm,),
        in_specs=[pl.BlockSpec((tm, D_MODEL), lambda i: (i, 0)), full(g.shape), full(b.shape),
                  pl.BlockSpec(memory_space=pl.ANY)],
        out_specs=pl.BlockSpec((tm, D_MODEL), lambda i: (i, 0)),
        out_shape=jax.ShapeDtypeStruct((s, D_MODEL), F32),
        scratch_shapes=[pltpu.VMEM((2, 2, TOK_WORDS, tm, LANE), U32),
                        pltpu.VMEM((tm, D_MODEL), F32),
                        pltpu.SemaphoreType.DMA((2,))],
        compiler_params=_cparams(("arbitrary",), 48),
        name="ln_add",
    )(x, g, b, y_tok)


def _w_in_sources():
    src = {}
    o = 0
    for name, width in (("q", 512), ("k", 512), ("v", 1024), ("r", 1024), ("gl", GLA_GATE_RANK), ("u", 1024),
                        ("vg", 1024), ("cb", 1024), ("cc", 1024), ("ch", 1024), ("cq", MLA_Q_LORA),
                        ("ckv", MLA_KV_LORA), ("kr", MLA_ROPE)):
        src[name] = (o, width)
        o += width
    return src, o


PACK_COLS = 512


def _w_in_segments():
    src, _ = _w_in_sources()
    half = MLA_ROPE // 2
    kr0 = src["kr"][0]
    segs = [(dst, src[name][0], src[name][1])
            for name, dst in (("cq", COL_CQ), ("ckv", COL_CKV), ("q", COL_Q), ("k", COL_K), ("v", COL_V),
                              ("r", COL_R), ("u", COL_U), ("vg", COL_VG), ("cb", COL_CB), ("cc", COL_CC),
                              ("ch", COL_CH))]
    segs += [(COL_KR, kr0, MLA_ROPE), (COL_KR + MLA_ROPE, None, LANE - MLA_ROPE),
             (COL_KRS, kr0 + half, half), (COL_KRS + half, kr0, half), (COL_KRS + MLA_ROPE, None, LANE - MLA_ROPE),
             (COL_GL, src["gl"][0], GLA_GATE_RANK), (COL_GL + GLA_GATE_RANK, None, P_COLS - COL_GL - GLA_GATE_RANK)]
    return sorted(segs)


def _pack_w_in_kernel(wt_hbm, o_hbm, stage, tbuf, in_sem, out_sem, *, layer):
    n_blocks = P_COLS // PACK_COLS
    segs = _w_in_segments()

    def pieces(j):
        lo, hi = j * PACK_COLS, (j + 1) * PACK_COLS
        out = []
        for dst, s0, width in segs:
            a, b = max(dst, lo), min(dst + width, hi)
            if a < b:
                out.append((a - lo, None if s0 is None else s0 + (a - dst), b - a))
        return out

    def loads(j, slot):
        return [pltpu.make_async_copy(wt_hbm.at[layer, pl.ds(s0, n), :], stage.at[slot, pl.ds(off, n), :],
                                      in_sem.at[slot])
                for off, s0, n in pieces(j) if s0 is not None]

    def store(j, slot):
        return pltpu.make_async_copy(tbuf.at[slot], o_hbm.at[:, pl.ds(j * PACK_COLS, PACK_COLS)], out_sem.at[slot])

    for cp in loads(0, 0):
        cp.start()
    for j in range(n_blocks):
        slot = j % 2
        if j + 1 < n_blocks:
            for cp in loads(j + 1, 1 - slot):
                cp.start()
        for cp in loads(j, slot):
            cp.wait()
        for off, s0, n in pieces(j):
            if s0 is None:
                stage[slot, off:off + n, :] = jnp.zeros((n, D_MODEL), F32)
        if j >= 2:
            store(j - 2, slot).wait()
        tbuf[slot] = jnp.transpose(stage[slot]).astype(BF16)
        store(j, slot).start()
    for j in range(max(n_blocks - 2, 0), n_blocks):
        store(j, j % 2).wait()


def _pack_w_in(wt, layer):
    _, d_in, k = wt.shape
    assert d_in == _w_in_sources()[1] and k == D_MODEL
    return pl.pallas_call(
        functools.partial(_pack_w_in_kernel, layer=layer),
        in_specs=[pl.BlockSpec(memory_space=pl.ANY)],
        out_specs=pl.BlockSpec(memory_space=pl.ANY),
        out_shape=jax.ShapeDtypeStruct((k, P_COLS), BF16),
        scratch_shapes=[pltpu.VMEM((2, PACK_COLS, D_MODEL), F32),
                        pltpu.VMEM((2, D_MODEL, PACK_COLS), BF16),
                        pltpu.SemaphoreType.DMA((2,)),
                        pltpu.SemaphoreType.DMA((2,))],
        compiler_params=pltpu.CompilerParams(vmem_limit_bytes=40 * MIB),
        name="pack_w_in",
    )(wt)


def _pack_mla_weights(wuq, wukv):
    half = MLA_ROPE // 2
    wq3 = wuq.reshape(MLA_Q_LORA, MLA_HEADS, MLA_NOPE + MLA_ROPE)
    rope = wq3[:, :, MLA_NOPE:]
    zq = jnp.zeros((MLA_Q_LORA, MLA_HEADS, MLA_QK_PAD - MLA_NOPE - MLA_ROPE), wuq.dtype)
    wq = jnp.concatenate([wq3, zq], axis=2).reshape(MLA_Q_LORA, MLA_HEADS * MLA_QK_PAD)
    zs = jnp.zeros((MLA_Q_LORA, MLA_HEADS, LANE - MLA_ROPE), wuq.dtype)
    wqs = jnp.concatenate([rope[:, :, half:], rope[:, :, :half], zs], axis=2).reshape(MLA_Q_LORA, MLA_HEADS * LANE)
    wkv3 = wukv.reshape(MLA_KV_LORA, MLA_HEADS, MLA_NOPE + MLA_V)
    wk = wkv3[:, :, :MLA_NOPE].reshape(MLA_KV_LORA, MLA_HEADS * MLA_NOPE)
    wv = wkv3[:, :, MLA_NOPE:].reshape(MLA_KV_LORA, MLA_HEADS * MLA_V)
    return wq.astype(BF16), wqs.astype(BF16), wk.astype(BF16), wv.astype(BF16)


def _rope_tables(s):
    pos = jnp.arange(s, dtype=F32)
    inv_freq = ROPE_BASE ** (-jnp.arange(0, MLA_ROPE, 2, dtype=F32) / MLA_ROPE)
    ang = pos[:, None] * inv_freq[None, :]
    cos, sin = jnp.cos(ang), jnp.sin(ang)
    z = jnp.zeros((s, LANE - MLA_ROPE), F32)
    return jnp.concatenate([cos, cos, z], axis=1), jnp.concatenate([-sin, sin, z], axis=1)


def _dispatch_plan(route, s):
    n_assign = 2 * s
    p_rows = n_assign + N_EXPERTS * MOE_TILE
    eid = route[:, 0:2].astype(jnp.int32).reshape(n_assign)
    wgt = route[:, 2:4].reshape(n_assign)
    onehot = (eid[:, None] == jnp.arange(N_EXPERTS, dtype=jnp.int32)[None, :]).astype(jnp.int32)
    csum = jnp.cumsum(onehot, axis=0)
    rank = jnp.sum(onehot * (csum - 1), axis=1)
    counts = csum[-1]
    tiles_e = (counts + MOE_TILE - 1) // MOE_TILE
    tile_end = jnp.cumsum(tiles_e)
    tile_start = tile_end - tiles_e
    slot = tile_start[eid] * MOE_TILE + rank
    slot_assign = jnp.full((p_rows,), -1, jnp.int32).at[slot].set(jnp.arange(n_assign, dtype=jnp.int32))
    is_pad = slot_assign < 0
    src_token = jnp.where(is_pad, 0, slot_assign // 2)
    dst_row = jnp.maximum(slot_assign, 0)
    w_slot = jnp.where(is_pad, 0.0, wgt[dst_row])
    n_blocks = p_rows // MOE_TILE
    tile_valid = jnp.sum(jnp.logical_not(is_pad).reshape(n_blocks, MOE_TILE).astype(jnp.int32), axis=1)
    tile_ids = jnp.arange(n_blocks, dtype=jnp.int32)
    tile_expert = jnp.minimum(jnp.sum((tile_end[None, :] <= tile_ids[:, None]).astype(jnp.int32), axis=1),
                              N_EXPERTS - 1)
    n_tiles = tile_end[-1:].astype(jnp.int32)
    experts = jnp.arange(N_EXPERTS, dtype=jnp.int32)
    later_with_tiles = (experts[None, :] > experts[:, None]) & (tiles_e[None, :] > 0)
    next_nonempty = jnp.min(jnp.where(later_with_tiles, experts[None, :], N_EXPERTS), axis=1)
    next_expert = next_nonempty[tile_expert]
    return src_token, dst_row, w_slot.reshape(p_rows, 1), tile_expert, next_expert, tile_valid, n_tiles


def kernel(x, w_in, gla_wa2, gla_ba, gla_norm, gm_ln_g, gm_ln_b, gm_ws, gm_bs, gm_norm, sc_conv, sc_norm, mla_q_norm, mla_kv_norm, mla_wuq, mla_wukv, mla_norm, w_o, ln1_g, ln1_b, router_g_w, router_g_b, router_e_w, router_e_b, exp_w_gate, exp_w_up, exp_w_down, ln2_g, ln2_b):
    bsz, s, _ = x.shape
    assert bsz == 1
    xc = x.reshape(s, D_MODEL)
    cos_t, sin_t = _rope_tables(s)
    row = lambda v: v.reshape(1, -1)
    wg_all = exp_w_gate.reshape(DEPTH * N_EXPERTS, D_MODEL, D_EXPERT)
    wu_all = exp_w_up.reshape(DEPTH * N_EXPERTS, D_MODEL, D_EXPERT)
    wd_all = exp_w_down.reshape(DEPTH * N_EXPERTS, D_EXPERT, D_MODEL)
    w_o_b = w_o.astype(BF16)
    w_in_t = jnp.swapaxes(w_in, 1, 2)
    for l in range(DEPTH):
        p = _mm_in(xc, _pack_w_in(w_in_t, l))

        wa2p = jnp.concatenate(
            [gla_wa2[l], jnp.zeros((LANE - GLA_GATE_RANK, GLA_HEADS * GLA_DK), F32)], axis=0).astype(BF16)
        out_a = _gla(p, wa2p, row(gla_ba[l]), row(gla_norm[l]))

        gm_bias = jnp.repeat(gm_bs[l].T, GM_CH, axis=1)
        out_b = _gmlp(p, row(gm_ln_g[l]), row(gm_ln_b[l]), gm_ws[l], gm_bias, row(gm_norm[l]))

        out_c = _sconv(p, sc_conv[l], row(sc_norm[l]))

        wq, wqs, wk, wv = _pack_mla_weights(mla_wuq[l], mla_wukv[l])
        q_att, k_att, v_att = _mla_proj(p, row(mla_q_norm[l]), row(mla_kv_norm[l]), wq, wqs, wk, wv, cos_t, sin_t)
        out_d = _flash(q_att, k_att, v_att, row(mla_norm[l]))

        y = _mm_out(out_a, out_b, out_c, out_d, w_o_b, l, xc)

        wr = jnp.concatenate([router_g_w[l], router_e_w[l],
                              jnp.zeros((D_MODEL, LANE - MOE_GROUPS - N_EXPERTS), F32)], axis=1)
        wr_hi = wr.astype(BF16)
        wr_lo = (wr - wr_hi.astype(F32)).astype(BF16)
        rb = jnp.concatenate([router_g_b[l], router_e_b[l], jnp.zeros((LANE - MOE_GROUPS - N_EXPERTS,), F32)])
        x1, x1_tok, route = _ln_route(y, row(ln1_g[l]), row(ln1_b[l]), wr_hi, wr_lo, row(rb))

        src_token, dst_row, w_slot, tile_expert, next_expert, tile_valid, n_tiles = _dispatch_plan(route, s)
        next_expert = jnp.where(next_expert < N_EXPERTS, next_expert + l * N_EXPERTS, -1)
        y_tok = _moe(tile_expert + l * N_EXPERTS, next_expert, tile_valid, n_tiles, src_token, dst_row, x1_tok,
                     wg_all, wu_all, wd_all, w_slot)
        xc = _ln_add(x1, y_tok, row(ln2_g[l]), row(ln2_b[l]))
    return xc.reshape(bsz, s, D_MODEL)
```

```python
import functools

import jax
import jax.numpy as jnp
from jax import lax
from jax.experimental import pallas as pl
from jax.experimental.pallas import tpu as pltpu

F32 = jnp.float32
BF16 = jnp.bfloat16

D_MODEL = 4096
DEPTH = 2
GROUP_W = 1024

GLA_HEADS = 4
GLA_DK = 128
GLA_DV = 256
GLA_GATE_RANK = 16
GLA_TAU = 16.0
GLA_CHUNK = 64
GLA_SUB = 16

GM_GROUPS = 8
GM_CH = 128
GM_CHUNK = 128

MLA_HEADS = 8
MLA_NOPE = 128
MLA_ROPE = 64
MLA_V = 128
MLA_Q_LORA = 768
MLA_KV_LORA = 256
ROPE_BASE = 10000.0
MLA_QK_PAD = 256

MOE_GROUPS = 4
MOE_PER_GROUP = 8
N_EXPERTS = 32
D_EXPERT = 512

ALPHA = (2.0 * DEPTH) ** 0.25
LOG2_E = 1.4426950408889634

LANE = 128
MIB = 1024 * 1024

COL_CQ = 0
COL_CKV = 768
COL_Q = 1024
COL_K = 1536
COL_V = 2048
COL_R = 3072
COL_U = 4096
COL_VG = 5120
COL_CB = 6144
COL_CC = 7168
COL_CH = 8192
COL_KR = 9216
COL_KRS = 9344
COL_GL = 9472
P_COLS = 9728

MOE_TILE = 256
SCATTER_UNROLL = 8
WEIGHT_ROUND_ROWS = 512
HALF_D = D_MODEL // 2
TOK_WORDS = HALF_D // LANE
U32 = jnp.uint32
HIGH_HALF = 0xFFFF0000


def _pack_rows(lo_f32, hi_f32):
    return (lax.bitcast_convert_type(lo_f32, U32) >> 16) | lax.bitcast_convert_type(hi_f32, U32)


def _unpack_rows(words):
    return (lax.bitcast_convert_type(words << 16, F32),
            lax.bitcast_convert_type(words & jnp.uint32(HIGH_HALF), F32))


def _cparams(sem, vmem_mib):
    return pltpu.CompilerParams(dimension_semantics=sem, vmem_limit_bytes=vmem_mib * MIB)


def _dot(a, b):
    return jnp.dot(a, b, preferred_element_type=F32)


def _dot_nt(a, b):
    return lax.dot_general(a, b, (((1,), (1,)), ((), ())), preferred_element_type=F32)


def _dot_tn(a, b):
    return lax.dot_general(a, b, (((0,), (0,)), ((), ())), preferred_element_type=F32)


def _mm_in_kernel(x_ref, w_ref, o_ref, xb_ref):
    @pl.when(pl.program_id(1) == 0)
    def _():
        xb_ref[...] = x_ref[...].astype(BF16)

    o_ref[...] = _dot(xb_ref[...], w_ref[...]).astype(o_ref.dtype)


def _mm_in(x, w):
    s, k = x.shape
    n = w.shape[1]
    tm = min(1024, s)
    tn = 512
    return pl.pallas_call(
        _mm_in_kernel,
        grid=(s // tm, n // tn),
        in_specs=[pl.BlockSpec((tm, k), lambda i, j: (i, 0)),
                  pl.BlockSpec((k, tn), lambda i, j: (0, j))],
        out_specs=pl.BlockSpec((tm, tn), lambda i, j: (i, j)),
        out_shape=jax.ShapeDtypeStruct((s, n), BF16),
        scratch_shapes=[pltpu.VMEM((tm, k), BF16)],
        compiler_params=_cparams(("parallel", "arbitrary"), 58),
        name="mm_in",
    )(x, w)


def _gla_kernel(q_ref, k_ref, v_ref, r_ref, gl_ref, wa2_ref, ba_ref, ng_ref, o_ref, s_ref, *, n_chunks):
    c_len = GLA_CHUNK

    @pl.when(pl.program_id(0) == 0)
    def _():
        s_ref[...] = jnp.zeros_like(s_ref)

    row = lax.broadcasted_iota(jnp.int32, (c_len, c_len), 0)
    col = lax.broadcasted_iota(jnp.int32, (c_len, c_len), 1)
    tril = jnp.where(col <= row, 1.0, 0.0).astype(BF16)
    sub_row = lax.broadcasted_iota(jnp.int32, (GLA_SUB, c_len), 0)
    sub_col = lax.broadcasted_iota(jnp.int32, (GLA_SUB, c_len), 1)
    n_sub = c_len // GLA_SUB

    def chunk(c, carry):
        rows = pl.ds(pl.multiple_of(c * c_len, c_len), c_len)
        logit = _dot(gl_ref[rows, :], wa2_ref[...]) + ba_ref[...]
        g = (jnp.minimum(logit, 0.0) - jnp.log(1.0 + jnp.exp(-jnp.abs(logit)))) * (1.0 / GLA_TAU)
        g_hi = g.astype(BF16)
        g_lo = (g - g_hi.astype(F32)).astype(BF16)
        b_all = _dot(tril, g_hi) + _dot(tril, g_lo)

        for h in range(GLA_HEADS):
            hs = slice(GLA_DK * h, GLA_DK * (h + 1))
            vs = slice(GLA_DV * h, GLA_DV * (h + 1))
            bh = b_all[:, hs]
            qh = q_ref[rows, hs].astype(F32) * (GLA_DK ** -0.5)
            kh = k_ref[rows, hs].astype(F32)
            vh = v_ref[rows, vs]
            state = s_ref[h]

            o = _dot((qh * jnp.exp(bh)).astype(BF16), state.astype(BF16))

            att_rows = []
            for blk in range(n_sub):
                sl = slice(GLA_SUB * blk, GLA_SUB * (blk + 1))
                b_blk = bh[sl]
                q_blk = qh[sl]
                att = jnp.zeros((GLA_SUB, c_len), F32)
                if blk > 0:
                    ref = bh[GLA_SUB * blk:GLA_SUB * blk + 1, :]
                    qs = (q_blk * jnp.exp(b_blk - ref)).astype(BF16)
                    ks = (kh * jnp.exp(jnp.minimum(ref - bh, 0.0))).astype(BF16)
                    att = jnp.where(sub_col < GLA_SUB * blk, _dot_nt(qs, ks), 0.0)
                for jj in range(GLA_SUB):
                    j = GLA_SUB * blk + jj
                    t = q_blk * kh[j:j + 1, :] * jnp.exp(b_blk - bh[j:j + 1, :])
                    rs = jnp.sum(t, axis=-1, keepdims=True)
                    att = jnp.where((sub_col == j) & (sub_row >= jj), rs, att)
                att_rows.append(att)
            att_full = jnp.concatenate(att_rows, axis=0).astype(BF16)
            o = o + _dot(att_full, vh)

            b_last = bh[c_len - 1:c_len, :]
            kd = (kh * jnp.exp(b_last - bh)).astype(BF16)
            decay_col = jnp.transpose(jnp.broadcast_to(jnp.exp(b_last), (GLA_DK, GLA_DK)))
            s_ref[h] = state * jnp.concatenate([decay_col, decay_col], axis=1) + _dot_tn(kd, vh)

            var = jnp.mean(o * o, axis=-1, keepdims=True)
            on = o * lax.rsqrt(var + 1e-6) * ng_ref[:, vs]
            rr = r_ref[rows, vs].astype(F32)
            o_ref[rows, vs] = (on * (rr / (1.0 + jnp.exp(-rr)))).astype(o_ref.dtype)
        return carry

    lax.fori_loop(0, n_chunks, chunk, 0, unroll=2)


def _gla(p, wa2p, ba, ng):
    s = p.shape[0]
    t = min(512, s)
    col = lambda width, off: pl.BlockSpec((t, width), lambda i: (i, off // width))
    full = lambda shape: pl.BlockSpec(shape, lambda i: (0,) * len(shape))
    return pl.pallas_call(
        functools.partial(_gla_kernel, n_chunks=t // GLA_CHUNK),
        grid=(s // t,),
        in_specs=[col(512, COL_Q), col(512, COL_K), col(1024, COL_V), col(1024, COL_R), col(LANE, COL_GL),
                  full(wa2p.shape), full(ba.shape), full(ng.shape)],
        out_specs=pl.BlockSpec((t, GROUP_W), lambda i: (i, 0)),
        out_shape=jax.ShapeDtypeStruct((s, GROUP_W), BF16),
        scratch_shapes=[pltpu.VMEM((GLA_HEADS, GLA_DK, GLA_DV), F32)],
        compiler_params=_cparams(("arbitrary",), 32),
        name="gla",
    )(p, p, p, p, p, wa2p, ba, ng)


def _gelu(x):
    return 0.5 * x * (1.0 + lax.erf(x * 0.7071067811865476))


def _gmlp_kernel(u_ref, v_ref, lg_ref, lb_ref, ws_ref, bias_ref, ng_ref, o_ref, buf_ref, *, n_chunks):
    row = lax.broadcasted_iota(jnp.int32, (GM_CHUNK, GM_CHUNK), 0)
    col = lax.broadcasted_iota(jnp.int32, (GM_CHUNK, GM_CHUNK), 1)
    causal = col <= row
    v = _gelu(v_ref[...].astype(F32))
    mu = jnp.mean(v, axis=-1, keepdims=True)
    d = v - mu
    var = jnp.mean(d * d, axis=-1, keepdims=True)
    vb = (d * lax.rsqrt(var + 1e-5) * lg_ref[...] + lb_ref[...]).astype(BF16)
    for g in range(GM_GROUPS):
        cs = slice(GM_CH * g, GM_CH * (g + 1))
        w = jnp.where(causal, ws_ref[g], 0.0).astype(BF16)
        for c in range(n_chunks):
            rs = slice(GM_CHUNK * c, GM_CHUNK * (c + 1))
            mixed = _dot(w, vb[rs, cs]) + bias_ref[:, cs]
            buf_ref[rs, cs] = _gelu(u_ref[rs, cs].astype(F32)) * mixed
    out = buf_ref[...]
    ms = jnp.mean(out * out, axis=-1, keepdims=True)
    o_ref[...] = (out * lax.rsqrt(ms + 1e-6) * ng_ref[...]).astype(o_ref.dtype)


def _gmlp(p, lg, lb, ws, bias, ng):
    s = p.shape[0]
    t = min(512, s)
    col = lambda off: pl.BlockSpec((t, GROUP_W), lambda i: (i, off // GROUP_W))
    full = lambda shape: pl.BlockSpec(shape, lambda i: (0,) * len(shape))
    return pl.pallas_call(
        functools.partial(_gmlp_kernel, n_chunks=t // GM_CHUNK),
        grid=(s // t,),
        in_specs=[col(COL_U), col(COL_VG), full(lg.shape), full(lb.shape), full(ws.shape), full(bias.shape),
                  full(ng.shape)],
        out_specs=pl.BlockSpec((t, GROUP_W), lambda i: (i, 0)),
        out_shape=jax.ShapeDtypeStruct((s, GROUP_W), BF16),
        scratch_shapes=[pltpu.VMEM((t, GROUP_W), F32)],
        compiler_params=_cparams(("parallel",), 32),
        name="gmlp",
    )(p, p, lg, lb, ws, bias, ng)


CONV_HALO = 8


def _sconv_kernel(b_ref, c_ref, h_ref, w_ref, ng_ref, o_ref, z_ref):
    t = b_ref.shape[0]

    @pl.when(pl.program_id(0) == 0)
    def _():
        z_ref[0:CONV_HALO, :] = jnp.zeros((CONV_HALO, GROUP_W), F32)

    z = c_ref[...].astype(F32) * h_ref[...].astype(F32)
    z_ref[CONV_HALO:CONV_HALO + t, :] = z
    z1 = z_ref[CONV_HALO - 1:CONV_HALO - 1 + t, :]
    z2 = z_ref[CONV_HALO - 2:CONV_HALO - 2 + t, :]
    y = w_ref[0:1, :] * z2 + w_ref[1:2, :] * z1 + w_ref[2:3, :] * z
    z_ref[0:CONV_HALO, :] = z[t - CONV_HALO:t, :]
    out = b_ref[...].astype(F32) * y
    ms = jnp.mean(out * out, axis=-1, keepdims=True)
    o_ref[...] = (out * lax.rsqrt(ms + 1e-6) * ng_ref[...]).astype(o_ref.dtype)


def _sconv(p, w, ng):
    s = p.shape[0]
    t = min(512, s)
    col = lambda off: pl.BlockSpec((t, GROUP_W), lambda i: (i, off // GROUP_W))
    full = lambda shape: pl.BlockSpec(shape, lambda i: (0,) * len(shape))
    return pl.pallas_call(
        _sconv_kernel,
        grid=(s // t,),
        in_specs=[col(COL_CB), col(COL_CC), col(COL_CH), full(w.shape), full(ng.shape)],
        out_specs=pl.BlockSpec((t, GROUP_W), lambda i: (i, 0)),
        out_shape=jax.ShapeDtypeStruct((s, GROUP_W), BF16),
        scratch_shapes=[pltpu.VMEM((t + CONV_HALO, GROUP_W), F32)],
        compiler_params=_cparams(("arbitrary",), 32),
        name="sconv",
    )(p, p, p, w, ng)


def _mla_proj_kernel(cq_ref, ckv_ref, kr_ref, krs_ref, qn_ref, kvn_ref, wq_ref, wqs_ref, wk_ref, wv_ref,
                     cos_ref, sin_ref, q_out, k_out, v_out):
    def rms(ref, g_ref):
        t = ref[...].astype(F32)
        return (t * lax.rsqrt(jnp.mean(t * t, axis=-1, keepdims=True) + 1e-6) * g_ref[...]).astype(BF16)

    cqn = rms(cq_ref, qn_ref)
    ckvn = rms(ckv_ref, kvn_ref)
    qm = _dot(cqn, wq_ref[...])
    qsw = _dot(cqn, wqs_ref[...])
    kn = _dot(ckvn, wk_ref[...])
    vv = _dot(ckvn, wv_ref[...])
    cos = cos_ref[...]
    sin = sin_ref[...]
    scale = (MLA_NOPE + MLA_ROPE) ** -0.5 * LOG2_E
    kr_rot = (kr_ref[...].astype(F32) * cos + krs_ref[...].astype(F32) * sin).astype(k_out.dtype)
    ones_col = jnp.where(lax.broadcasted_iota(jnp.int32, (cos.shape[0], LANE), 1) == 0, 1.0, 0.0).astype(v_out.dtype)
    for h in range(MLA_HEADS):
        lo = MLA_QK_PAD * h
        v_out[:, lo:lo + LANE] = vv[:, LANE * h:LANE * (h + 1)].astype(v_out.dtype)
        v_out[:, lo + LANE:lo + 2 * LANE] = ones_col
        q_out[:, lo:lo + LANE] = (qm[:, lo:lo + LANE] * scale).astype(q_out.dtype)
        q_out[:, lo + LANE:lo + 2 * LANE] = (
            (qm[:, lo + LANE:lo + 2 * LANE] * cos + qsw[:, LANE * h:LANE * (h + 1)] * sin) * scale
        ).astype(q_out.dtype)
        k_out[:, lo:lo + LANE] = kn[:, LANE * h:LANE * (h + 1)].astype(k_out.dtype)
        k_out[:, lo + LANE:lo + 2 * LANE] = kr_rot


def _mla_proj(p, qn, kvn, wq, wqs, wk, wv, cos, sin):
    s = p.shape[0]
    t = min(256, s)
    full = lambda shape: pl.BlockSpec(shape, lambda i: (0,) * len(shape))
    rowb = lambda width: pl.BlockSpec((t, width), lambda i: (i, 0))
    qk_w = MLA_HEADS * MLA_QK_PAD
    return pl.pallas_call(
        _mla_proj_kernel,
        grid=(s // t,),
        in_specs=[pl.BlockSpec((t, MLA_Q_LORA), lambda i: (i, COL_CQ // MLA_Q_LORA)),
                  pl.BlockSpec((t, MLA_KV_LORA), lambda i: (i, COL_CKV // MLA_KV_LORA)),
                  pl.BlockSpec((t, LANE), lambda i: (i, COL_KR // LANE)),
                  pl.BlockSpec((t, LANE), lambda i: (i, COL_KRS // LANE)),
                  full(qn.shape), full(kvn.shape), full(wq.shape), full(wqs.shape), full(wk.shape), full(wv.shape),
                  rowb(LANE), rowb(LANE)],
        out_specs=[rowb(qk_w), rowb(qk_w), rowb(qk_w)],
        out_shape=[jax.ShapeDtypeStruct((s, qk_w), BF16)] * 3,
        compiler_params=_cparams(("parallel",), 40),
        name="mla_proj",
    )(p, p, p, p, qn, kvn, wq, wqs, wk, wv, cos, sin)


FLASH_TQ = 512
FLASH_TK = 1024


def _flash_kernel(qi_ref, ki_ref, q_ref, k_ref, v_ref, ng_ref, o_ref, acc_ref, m_ref, *, tq, tk):
    step = pl.program_id(0)
    qi = qi_ref[step]
    ki = ki_ref[step]
    last_ki = (qi * tq) // tk

    @pl.when(ki == 0)
    def _():
        m_ref[...] = jnp.full(m_ref.shape, -jnp.inf, F32)
        acc_ref[...] = jnp.zeros_like(acc_ref)

    def accumulate(masked, k0, kn):
        keys = slice(k0, k0 + kn)
        if masked:
            row = qi * tq + lax.broadcasted_iota(jnp.int32, (tq, kn), 0)
            col = ki * tk + k0 + lax.broadcasted_iota(jnp.int32, (tq, kn), 1)
            visible = col <= row

        def scores(h):
            hs = slice(MLA_QK_PAD * h, MLA_QK_PAD * (h + 1))
            return _dot_nt(q_ref[:, hs], k_ref[keys, hs])

        sc = scores(0)
        for h in range(MLA_HEADS):
            hs = slice(MLA_QK_PAD * h, MLA_QK_PAD * (h + 1))
            sc_next = scores(h + 1) if h + 1 < MLA_HEADS else None
            if masked:
                sc = jnp.where(visible, sc, -jnp.inf)
            m_old = m_ref[h]
            m_new = jnp.maximum(m_old, jnp.max(sc, axis=-1, keepdims=True))
            pr = jnp.exp2(sc - m_new).astype(BF16)
            acc_ref[:, hs] = jnp.exp2(m_old - m_new) * acc_ref[:, hs] + _dot(pr, v_ref[keys, hs])
            m_ref[h] = m_new
            sc = sc_next

    @pl.when(ki < last_ki)
    def _():
        accumulate(False, 0, tk)

    @pl.when(ki == last_ki)
    def _():
        accumulate(True, 0, tk)
        outs = []
        for h in range(MLA_HEADS):
            lo = MLA_QK_PAD * h
            outs.append(acc_ref[:, lo:lo + MLA_V] / acc_ref[:, lo + MLA_V:lo + MLA_V + 1])
        o = jnp.concatenate(outs, axis=1)
        ms = jnp.mean(o * o, axis=-1, keepdims=True)
        o_ref[...] = (o * lax.rsqrt(ms + 1e-6) * ng_ref[...]).astype(o_ref.dtype)


def _flash(q, k, v, ng):
    s = q.shape[0]
    tk = min(FLASH_TK, s)
    tq = min(FLASH_TQ, tk)
    nq = s // tq
    pairs = [(a, b) for a in range(nq) for b in range((a * tq) // tk + 1)]
    qi_tab = jnp.asarray([a for a, _ in pairs], jnp.int32)
    ki_tab = jnp.asarray([b for _, b in pairs], jnp.int32)
    qk_w = MLA_HEADS * MLA_QK_PAD
    grid_spec = pltpu.PrefetchScalarGridSpec(
        num_scalar_prefetch=2,
        grid=(len(pairs),),
        in_specs=[pl.BlockSpec((tq, qk_w), lambda t, qi, ki: (qi[t], 0)),
                  pl.BlockSpec((tk, qk_w), lambda t, qi, ki: (ki[t], 0)),
                  pl.BlockSpec((tk, qk_w), lambda t, qi, ki: (ki[t], 0)),
                  pl.BlockSpec((1, GROUP_W), lambda t, qi, ki: (0, 0))],
        out_specs=pl.BlockSpec((tq, GROUP_W), lambda t, qi, ki: (qi[t], 0)),
        scratch_shapes=[pltpu.VMEM((tq, qk_w), F32),
                        pltpu.VMEM((MLA_HEADS, tq, 1), F32)],
    )
    return pl.pallas_call(
        functools.partial(_flash_kernel, tq=tq, tk=tk),
        grid_spec=grid_spec,
        out_shape=jax.ShapeDtypeStruct((s, GROUP_W), BF16),
        compiler_params=_cparams(("arbitrary",), 56),
        name="flash",
    )(qi_tab, ki_tab, q, k, v, ng)


def _mm_out_kernel(a0_ref, a1_ref, a2_ref, a3_ref, w_ref, x_ref, o_ref):
    acc = _dot(a0_ref[...], w_ref[0, 0:GROUP_W, :])
    acc = acc + _dot(a1_ref[...], w_ref[0, GROUP_W:2 * GROUP_W, :])
    acc = acc + _dot(a2_ref[...], w_ref[0, 2 * GROUP_W:3 * GROUP_W, :])
    acc = acc + _dot(a3_ref[...], w_ref[0, 3 * GROUP_W:4 * GROUP_W, :])
    o_ref[...] = ALPHA * x_ref[...] + acc


def _mm_out(a0, a1, a2, a3, w, layer, x):
    s = x.shape[0]
    tm = min(1024, s)
    tn = 1024
    a_spec = pl.BlockSpec((tm, GROUP_W), lambda i, j: (i, 0))
    return pl.pallas_call(
        _mm_out_kernel,
        grid=(s // tm, D_MODEL // tn),
        in_specs=[a_spec, a_spec, a_spec, a_spec,
                  pl.BlockSpec((1, D_MODEL, tn), lambda i, j: (layer, 0, j)),
                  pl.BlockSpec((tm, tn), lambda i, j: (i, j))],
        out_specs=pl.BlockSpec((tm, tn), lambda i, j: (i, j)),
        out_shape=jax.ShapeDtypeStruct((s, D_MODEL), F32),
        compiler_params=_cparams(("parallel", "arbitrary"), 58),
        name="mm_out",
    )(a0, a1, a2, a3, w, x)


def _layer_norm(t, g, b):
    mu = jnp.mean(t, axis=-1, keepdims=True)
    d = t - mu
    var = jnp.mean(d * d, axis=-1, keepdims=True)
    return d * lax.rsqrt(var + 1e-5) * g + b


def _ln_route_kernel(y_ref, g_ref, b_ref, wh_ref, wl_ref, rb_ref, x_out, route_out, xtok_hbm, tbuf, sem, *, tm):
    i = pl.program_id(0)
    slot = lax.rem(i, 2)

    def wait_rows(buf_slot):
        pltpu.make_async_copy(tbuf.at[buf_slot], tbuf.at[buf_slot], sem.at[buf_slot]).wait()

    x1 = _layer_norm(y_ref[...], g_ref[...], b_ref[...])
    x_out[...] = x1
    x_hi = x1.astype(BF16)
    x_hi32 = x_hi.astype(F32)

    @pl.when(i >= 2)
    def _():
        wait_rows(slot)

    words = _pack_rows(x_hi32[:, :HALF_D], x_hi32[:, HALF_D:])
    for c in range(TOK_WORDS):
        tbuf[slot, c] = words[:, LANE * c:LANE * (c + 1)]
    for j in range(tm):
        pltpu.make_async_copy(tbuf.at[slot, :, pl.ds(j, 1), :], xtok_hbm.at[i * tm + j], sem.at[slot]).start()

    @pl.when(i == pl.num_programs(0) - 1)
    def _():
        wait_rows(slot)

        @pl.when(i >= 1)
        def _():
            wait_rows(1 - slot)

    x_lo = (x1 - x_hi32).astype(BF16)
    logits = _dot(x_hi, wh_ref[...]) + _dot(x_lo, wh_ref[...]) + _dot(x_hi, wl_ref[...]) + rb_ref[...]
    lane = lax.broadcasted_iota(jnp.int32, logits.shape, 1).astype(F32)
    neg = -jnp.inf
    big = 1e9
    is_grp = lane < MOE_GROUPS
    lg = jnp.where(is_grp, logits, neg)
    mg = jnp.max(lg, axis=-1, keepdims=True)
    gsel = jnp.min(jnp.where(lg == mg, lane, big), axis=-1, keepdims=True)
    pg_sel = 1.0 / jnp.sum(jnp.where(is_grp, jnp.exp(lg - mg), 0.0), axis=-1, keepdims=True)
    lo = MOE_GROUPS + MOE_PER_GROUP * gsel
    le = jnp.where((lane >= lo) & (lane < lo + MOE_PER_GROUP), logits, neg)
    v1 = jnp.max(le, axis=-1, keepdims=True)
    i1 = jnp.min(jnp.where(le == v1, lane, big), axis=-1, keepdims=True)
    le2 = jnp.where(lane == i1, neg, le)
    v2 = jnp.max(le2, axis=-1, keepdims=True)
    i2 = jnp.min(jnp.where(le2 == v2, lane, big), axis=-1, keepdims=True)
    e = jnp.exp(v2 - v1)
    w1 = pg_sel / (1.0 + e)
    w2 = pg_sel * e / (1.0 + e)
    route = jnp.where(lane == 0.0, i1 - MOE_GROUPS,
                      jnp.where(lane == 1.0, i2 - MOE_GROUPS,
                                jnp.where(lane == 2.0, w1, jnp.where(lane == 3.0, w2, 0.0))))
    route_out[...] = route


def _ln_route(y, g, b, wh, wl, rb):
    s = y.shape[0]
    tm = min(256, s)
    full = lambda shape: pl.BlockSpec(shape, lambda i: (0,) * len(shape))
    rowb = lambda width: pl.BlockSpec((tm, width), lambda i: (i, 0))
    x1, route, x1_tok = pl.pallas_call(
        functools.partial(_ln_route_kernel, tm=tm),
        grid=(s // tm,),
        in_specs=[rowb(D_MODEL), full(g.shape), full(b.shape), full(wh.shape), full(wl.shape), full(rb.shape)],
        out_specs=[rowb(D_MODEL), rowb(LANE), pl.BlockSpec(memory_space=pl.ANY)],
        out_shape=[jax.ShapeDtypeStruct((s, D_MODEL), F32),
                   jax.ShapeDtypeStruct((s, LANE), F32),
                   jax.ShapeDtypeStruct((s, TOK_WORDS, 1, LANE), U32)],
        scratch_shapes=[pltpu.VMEM((2, TOK_WORDS, tm, LANE), U32), pltpu.SemaphoreType.DMA((2,))],
        compiler_params=_cparams(("arbitrary",), 48),
        name="ln_route",
    )(y, g, b, wh, wl, rb)
    return x1, x1_tok, route


def _moe_kernel(te_ref, ne_ref, tv_ref, nt_ref, src_ref, nxt_ref, dst_ref, x_hbm, wg_hbm, wu_hbm, wd_hbm, ws_ref,
                y_hbm, gbuf, hb, obuf, stg_g, stg_u, stg_d, wg_ref, wu_ref, wd_ref, gsem, ssem, wsem):
    t = pl.program_id(0)
    n_used = nt_ref[0]
    slot = lax.rem(t, 2)

    def weight_copies(e):
        return (pltpu.make_async_copy(wg_hbm.at[e], stg_g, wsem.at[0]),
                pltpu.make_async_copy(wu_hbm.at[e], stg_u, wsem.at[1]),
                pltpu.make_async_copy(wd_hbm.at[e], stg_d, wsem.at[2]))

    def round_weights(stg, dst, rows):
        def body(r, carry):
            rs = pl.ds(pl.multiple_of(r * rows, rows), rows)
            dst[rs, :] = stg[rs, :].astype(BF16)
            return carry
        lax.fori_loop(0, stg.shape[0] // rows, body, 0)

    @pl.when(t == 0)
    def _():
        for cp in weight_copies(te_ref[0]):
            cp.start()

    @pl.when(t < n_used)
    def _():
        expert = te_ref[t]
        first_tile_of_expert = jnp.logical_or(t == 0, expert != te_ref[jnp.maximum(t - 1, 0)])

        @pl.when(first_tile_of_expert)
        def _():
            for cp in weight_copies(expert):
                cp.wait()
            round_weights(stg_g, wg_ref, WEIGHT_ROUND_ROWS)
            round_weights(stg_u, wu_ref, WEIGHT_ROUND_ROWS)
            round_weights(stg_d, wd_ref, WEIGHT_ROUND_ROWS * D_EXPERT // D_MODEL)

            @pl.when(ne_ref[t] >= 0)
            def _():
                for cp in weight_copies(ne_ref[t]):
                    cp.start()

    def gather_row(idx_ref, buf_slot, i):
        return pltpu.make_async_copy(x_hbm.at[idx_ref[i]], gbuf.at[buf_slot, :, pl.ds(i, 1), :], gsem.at[buf_slot])

    def wait_gather(buf_slot):
        pltpu.make_async_copy(gbuf.at[buf_slot], gbuf.at[buf_slot], gsem.at[buf_slot]).wait()

    def scatter_row(i):
        return pltpu.make_async_copy(obuf.at[:, pl.ds(i, 1), :], y_hbm.at[dst_ref[i]], ssem)

    def start_scatter(n_rows):
        def group(g, carry):
            for u in range(SCATTER_UNROLL):
                scatter_row(g * SCATTER_UNROLL + u).start()
            return carry

        def single(i, carry):
            scatter_row(i).start()
            return carry
        n_groups = n_rows // SCATTER_UNROLL
        lax.fori_loop(0, n_groups, group, 0)
        lax.fori_loop(n_groups * SCATTER_UNROLL, n_rows, single, 0)

    def wait_scatter(n_rows):
        rows = obuf.at[:, pl.ds(0, n_rows), :]
        pltpu.make_async_copy(rows, rows, ssem).wait()

    @pl.when(t == 0)
    def _():
        def body(i, carry):
            gather_row(src_ref, 0, i).start()
            return carry
        lax.fori_loop(0, MOE_TILE, body, 0, unroll=8)

    @pl.when(t < n_used)
    def _():
        wait_gather(slot)
        for c in range(TOK_WORDS):
            lo, hi = _unpack_rows(gbuf[slot, c])
            hb[:, LANE * c:LANE * (c + 1)] = lo.astype(BF16)
            hb[:, HALF_D + LANE * c:HALF_D + LANE * (c + 1)] = hi.astype(BF16)
        for i in range(MOE_TILE):
            gather_row(nxt_ref, 1 - slot, i).start()
        h = hb[...]
        gate = _dot(h, wg_ref[...])
        up = _dot(h, wu_ref[...])
        a = (gate / (1.0 + jnp.exp(-gate)) * up).astype(BF16)

        @pl.when(t > 0)
        def _():
            wait_scatter(tv_ref[jnp.maximum(t - 1, 0)])

        ws = ws_ref[...]
        rounded = lambda v: v.astype(BF16).astype(F32)
        for c2 in range(TOK_WORDS // 2):
            cols = slice(2 * LANE * c2, 2 * LANE * (c2 + 1))
            lo = _dot(a, wd_ref[:, cols]) * ws
            hi = _dot(a, wd_ref[:, HALF_D + cols.start:HALF_D + cols.stop]) * ws
            words = _pack_rows(rounded(lo), rounded(hi))
            obuf[2 * c2] = words[:, :LANE]
            obuf[2 * c2 + 1] = words[:, LANE:]
        start_scatter(tv_ref[t])

        @pl.when(t == n_used - 1)
        def _():
            wait_gather(1 - slot)
            wait_scatter(tv_ref[t])


def _moe(tile_expert, next_expert, tile_valid, n_tiles, src_token, dst_row, x_tok, wg, wu, wd, w_slot):
    p_rows = src_token.shape[0]
    tm = MOE_TILE
    n_blocks = p_rows // tm
    smem_blk = lambda fn: pl.BlockSpec((tm,), fn, memory_space=pltpu.SMEM)
    hbm = pl.BlockSpec(memory_space=pl.ANY)
    grid_spec = pltpu.PrefetchScalarGridSpec(
        num_scalar_prefetch=4,
        grid=(n_blocks,),
        in_specs=[smem_blk(lambda t, te, ne, tv, nt: (t,)),
                  smem_blk(lambda t, te, ne, tv, nt: (jnp.minimum(t + 1, n_blocks - 1),)),
                  smem_blk(lambda t, te, ne, tv, nt: (t,)),
                  hbm, hbm, hbm, hbm,
                  pl.BlockSpec((tm, 1), lambda t, te, ne, tv, nt: (t, 0))],
        out_specs=hbm,
        scratch_shapes=[pltpu.VMEM((2, TOK_WORDS, tm, LANE), U32),
                        pltpu.VMEM((tm, D_MODEL), BF16),
                        pltpu.VMEM((TOK_WORDS, tm, LANE), U32),
                        pltpu.VMEM((D_MODEL, D_EXPERT), F32),
                        pltpu.VMEM((D_MODEL, D_EXPERT), F32),
                        pltpu.VMEM((D_EXPERT, D_MODEL), F32),
                        pltpu.VMEM((D_MODEL, D_EXPERT), BF16),
                        pltpu.VMEM((D_MODEL, D_EXPERT), BF16),
                        pltpu.VMEM((D_EXPERT, D_MODEL), BF16),
                        pltpu.SemaphoreType.DMA((2,)),
                        pltpu.SemaphoreType.DMA(()),
                        pltpu.SemaphoreType.DMA((3,))],
    )
    return pl.pallas_call(
        _moe_kernel,
        grid_spec=grid_spec,
        out_shape=jax.ShapeDtypeStruct((2 * x_tok.shape[0], TOK_WORDS, 1, LANE), U32),
        compiler_params=_cparams(("arbitrary",), 58),
        name="moe",
    )(tile_expert, next_expert, tile_valid, n_tiles, src_token, src_token, dst_row, x_tok, wg, wu, wd, w_slot)


def _ln_add_kernel(x_ref, g_ref, b_ref, y_hbm, o_ref, ybuf, vbuf, sem, *, tm):
    i = pl.program_id(0)
    slot = lax.rem(i, 2)

    def start_gather(tile, buf_slot):
        base = tile * (2 * tm)

        def body(j, carry):
            for k in range(2):
                pltpu.make_async_copy(y_hbm.at[base + 2 * j + k], ybuf.at[buf_slot, k, :, pl.ds(j, 1), :],
                                      sem.at[buf_slot]).start()
            return carry
        lax.fori_loop(0, tm, body, 0, unroll=4)

    @pl.when(i == 0)
    def _():
        start_gather(0, 0)

    @pl.when(i + 1 < pl.num_programs(0))
    def _():
        start_gather(i + 1, 1 - slot)

    pltpu.make_async_copy(ybuf.at[slot], ybuf.at[slot], sem.at[slot]).wait()
    for c in range(TOK_WORDS):
        a_lo, a_hi = _unpack_rows(ybuf[slot, 0, c])
        b_lo, b_hi = _unpack_rows(ybuf[slot, 1, c])
        lo = slice(LANE * c, LANE * (c + 1))
        hi = slice(HALF_D + LANE * c, HALF_D + LANE * (c + 1))
        vbuf[:, lo] = ALPHA * x_ref[:, lo] + (a_lo + b_lo)
        vbuf[:, hi] = ALPHA * x_ref[:, hi] + (a_hi + b_hi)
    o_ref[...] = _layer_norm(vbuf[...], g_ref[...], b_ref[...])


def _ln_add(x, y_tok, g, b):
    s = x.shape[0]
    tm = min(256, s)
    full = lambda shape: pl.BlockSpec(shape, lambda i: (0,) * len(shape))
    return pl.pallas_call(
        functools.partial(_ln_add_kernel, tm=tm),
        grid=(s // tm,),
        in_specs=[pl.BlockSpec((tm, D_MODEL), lambda i: (i, 0)), full(g.shape), full(b.shape),
                  pl.BlockSpec(memory_space=pl.ANY)],
        out_specs=pl.BlockSpec((tm, D_MODEL), lambda i: (i, 0)),
        out_shape=jax.ShapeDtypeStruct((s, D_MODEL), F32),
        scratch_shapes=[pltpu.VMEM((2, 2, TOK_WORDS, tm, LANE), U32),
                        pltpu.VMEM((tm, D_MODEL), F32),
                        pltpu.SemaphoreType.DMA((2,))],
        compiler_params=_cparams(("arbitrary",), 48),
        name="ln_add",
    )(x, g, b, y_tok)


def _w_in_sources():
    src = {}
    o = 0
    for name, width in (("q", 512), ("k", 512), ("v", 1024), ("r", 1024), ("gl", GLA_GATE_RANK), ("u", 1024),
                        ("vg", 1024), ("cb", 1024), ("cc", 1024), ("ch", 1024), ("cq", MLA_Q_LORA),
                        ("ckv", MLA_KV_LORA), ("kr", MLA_ROPE)):
        src[name] = (o, width)
        o += width
    return src, o


PACK_COLS = 512


def _w_in_segments():
    src, _ = _w_in_sources()
    half = MLA_ROPE // 2
    kr0 = src["kr"][0]
    segs = [(dst, src[name][0], src[name][1])
            for name, dst in (("cq", COL_CQ), ("ckv", COL_CKV), ("q", COL_Q), ("k", COL_K), ("v", COL_V),
                              ("r", COL_R), ("u", COL_U), ("vg", COL_VG), ("cb", COL_CB), ("cc", COL_CC),
                              ("ch", COL_CH))]
    segs += [(COL_KR, kr0, MLA_ROPE), (COL_KR + MLA_ROPE, None, LANE - MLA_ROPE),
             (COL_KRS, kr0 + half, half), (COL_KRS + half, kr0, half), (COL_KRS + MLA_ROPE, None, LANE - MLA_ROPE),
             (COL_GL, src["gl"][0], GLA_GATE_RANK), (COL_GL + GLA_GATE_RANK, None, P_COLS - COL_GL - GLA_GATE_RANK)]
    return sorted(segs)


def _pack_w_in_kernel(wt_hbm, o_hbm, stage, tbuf, in_sem, out_sem, *, layer):
    n_blocks = P_COLS // PACK_COLS
    segs = _w_in_segments()

    def pieces(j):
        lo, hi = j * PACK_COLS, (j + 1) * PACK_COLS
        out = []
        for dst, s0, width in segs:
            a, b = max(dst, lo), min(dst + width, hi)
            if a < b:
                out.append((a - lo, None if s0 is None else s0 + (a - dst), b - a))
        return out

    def loads(j, slot):
        return [pltpu.make_async_copy(wt_hbm.at[layer, pl.ds(s0, n), :], stage.at[slot, pl.ds(off, n), :],
                                      in_sem.at[slot])
                for off, s0, n in pieces(j) if s0 is not None]

    def store(j, slot):
        return pltpu.make_async_copy(tbuf.at[slot], o_hbm.at[:, pl.ds(j * PACK_COLS, PACK_COLS)], out_sem.at[slot])

    for cp in loads(0, 0):
        cp.start()
    for j in range(n_blocks):
        slot = j % 2
        if j + 1 < n_blocks:
            for cp in loads(j + 1, 1 - slot):
                cp.start()
        for cp in loads(j, slot):
            cp.wait()
        for off, s0, n in pieces(j):
            if s0 is None:
                stage[slot, off:off + n, :] = jnp.zeros((n, D_MODEL), F32)
        if j >= 2:
            store(j - 2, slot).wait()
        tbuf[slot] = jnp.transpose(stage[slot]).astype(BF16)
        store(j, slot).start()
    for j in range(max(n_blocks - 2, 0), n_blocks):
        store(j, j % 2).wait()


def _pack_w_in(wt, layer):
    _, d_in, k = wt.shape
    assert d_in == _w_in_sources()[1] and k == D_MODEL
    return pl.pallas_call(
        functools.partial(_pack_w_in_kernel, layer=layer),
        in_specs=[pl.BlockSpec(memory_space=pl.ANY)],
        out_specs=pl.BlockSpec(memory_space=pl.ANY),
        out_shape=jax.ShapeDtypeStruct((k, P_COLS), BF16),
        scratch_shapes=[pltpu.VMEM((2, PACK_COLS, D_MODEL), F32),
                        pltpu.VMEM((2, D_MODEL, PACK_COLS), BF16),
                        pltpu.SemaphoreType.DMA((2,)),
                        pltpu.SemaphoreType.DMA((2,))],
        compiler_params=pltpu.CompilerParams(vmem_limit_bytes=40 * MIB),
        name="pack_w_in",
    )(wt)


def _pack_mla_weights(wuq, wukv):
    half = MLA_ROPE // 2
    wq3 = wuq.reshape(MLA_Q_LORA, MLA_HEADS, MLA_NOPE + MLA_ROPE)
    rope = wq3[:, :, MLA_NOPE:]
    zq = jnp.zeros((MLA_Q_LORA, MLA_HEADS, MLA_QK_PAD - MLA_NOPE - MLA_ROPE), wuq.dtype)
    wq = jnp.concatenate([wq3, zq], axis=2).reshape(MLA_Q_LORA, MLA_HEADS * MLA_QK_PAD)
    zs = jnp.zeros((MLA_Q_LORA, MLA_HEADS, LANE - MLA_ROPE), wuq.dtype)
    wqs = jnp.concatenate([rope[:, :, half:], rope[:, :, :half], zs], axis=2).reshape(MLA_Q_LORA, MLA_HEADS * LANE)
    wkv3 = wukv.reshape(MLA_KV_LORA, MLA_HEADS, MLA_NOPE + MLA_V)
    wk = wkv3[:, :, :MLA_NOPE].reshape(MLA_KV_LORA, MLA_HEADS * MLA_NOPE)
    wv = wkv3[:, :, MLA_NOPE:].reshape(MLA_KV_LORA, MLA_HEADS * MLA_V)
    return wq.astype(BF16), wqs.astype(BF16), wk.astype(BF16), wv.astype(BF16)


def _rope_tables(s):
    pos = jnp.arange(s, dtype=F32)
    inv_freq = ROPE_BASE ** (-jnp.arange(0, MLA_ROPE, 2, dtype=F32) / MLA_ROPE)
    ang = pos[:, None] * inv_freq[None, :]
    cos, sin = jnp.cos(ang), jnp.sin(ang)
    z = jnp.zeros((s, LANE - MLA_ROPE), F32)
    return jnp.concatenate([cos, cos, z], axis=1), jnp.concatenate([-sin, sin, z], axis=1)


def _dispatch_plan(route, s):
    n_assign = 2 * s
    p_rows = n_assign + N_EXPERTS * MOE_TILE
    eid = route[:, 0:2].astype(jnp.int32).reshape(n_assign)
    wgt = route[:, 2:4].reshape(n_assign)
    onehot = (eid[:, None] == jnp.arange(N_EXPERTS, dtype=jnp.int32)[None, :]).astype(jnp.int32)
    csum = jnp.cumsum(onehot, axis=0)
    rank = jnp.sum(onehot * (csum - 1), axis=1)
    counts = csum[-1]
    tiles_e = (counts + MOE_TILE - 1) // MOE_TILE
    tile_end = jnp.cumsum(tiles_e)
    tile_start = tile_end - tiles_e
    slot = tile_start[eid] * MOE_TILE + rank
    slot_assign = jnp.full((p_rows,), -1, jnp.int32).at[slot].set(jnp.arange(n_assign, dtype=jnp.int32))
    is_pad = slot_assign < 0
    src_token = jnp.where(is_pad, 0, slot_assign // 2)
    dst_row = jnp.maximum(slot_assign, 0)
    w_slot = jnp.where(is_pad, 0.0, wgt[dst_row])
    n_blocks = p_rows // MOE_TILE
    tile_valid = jnp.sum(jnp.logical_not(is_pad).reshape(n_blocks, MOE_TILE).astype(jnp.int32), axis=1)
    tile_ids = jnp.arange(n_blocks, dtype=jnp.int32)
    tile_expert = jnp.minimum(jnp.sum((tile_end[None, :] <= tile_ids[:, None]).astype(jnp.int32), axis=1),
                              N_EXPERTS - 1)
    n_tiles = tile_end[-1:].astype(jnp.int32)
    experts = jnp.arange(N_EXPERTS, dtype=jnp.int32)
    later_with_tiles = (experts[None, :] > experts[:, None]) & (tiles_e[None, :] > 0)
    next_nonempty = jnp.min(jnp.where(later_with_tiles, experts[None, :], N_EXPERTS), axis=1)
    next_expert = next_nonempty[tile_expert]
    return src_token, dst_row, w_slot.reshape(p_rows, 1), tile_expert, next_expert, tile_valid, n_tiles


def kernel(x, w_in, gla_wa2, gla_ba, gla_norm, gm_ln_g, gm_ln_b, gm_ws, gm_bs, gm_norm, sc_conv, sc_norm, mla_q_norm, mla_kv_norm, mla_wuq, mla_wukv, mla_norm, w_o, ln1_g, ln1_b, router_g_w, router_g_b, router_e_w, router_e_b, exp_w_gate, exp_w_up, exp_w_down, ln2_g, ln2_b):
    bsz, s, _ = x.shape
    assert bsz == 1
    xc = x.reshape(s, D_MODEL)
    cos_t, sin_t = _rope_tables(s)
    row = lambda v: v.reshape(1, -1)
    wg_all = exp_w_gate.reshape(DEPTH * N_EXPERTS, D_MODEL, D_EXPERT)
    wu_all = exp_w_up.reshape(DEPTH * N_EXPERTS, D_MODEL, D_EXPERT)
    wd_all = exp_w_down.reshape(DEPTH * N_EXPERTS, D_EXPERT, D_MODEL)
    w_o_b = w_o.astype(BF16)
    w_in_t = jnp.swapaxes(w_in, 1, 2)
    for l in range(DEPTH):
        p = _mm_in(xc, _pack_w_in(w_in_t, l))

        wa2p = jnp.concatenate(
            [gla_wa2[l], jnp.zeros((LANE - GLA_GATE_RANK, GLA_HEADS * GLA_DK), F32)], axis=0).astype(BF16)
        out_a = _gla(p, wa2p, row(gla_ba[l]), row(gla_norm[l]))

        gm_bias = jnp.repeat(gm_bs[l].T, GM_CH, axis=1)
        out_b = _gmlp(p, row(gm_ln_g[l]), row(gm_ln_b[l]), gm_ws[l], gm_bias, row(gm_norm[l]))

        out_c = _sconv(p, sc_conv[l], row(sc_norm[l]))

        wq, wqs, wk, wv = _pack_mla_weights(mla_wuq[l], mla_wukv[l])
        q_att, k_att, v_att = _mla_proj(p, row(mla_q_norm[l]), row(mla_kv_norm[l]), wq, wqs, wk, wv, cos_t, sin_t)
        out_d = _flash(q_att, k_att, v_att, row(mla_norm[l]))

        y = _mm_out(out_a, out_b, out_c, out_d, w_o_b, l, xc)

        wr = jnp.concatenate([router_g_w[l], router_e_w[l],
                              jnp.zeros((D_MODEL, LANE - MOE_GROUPS - N_EXPERTS), F32)], axis=1)
        wr_hi = wr.astype(BF16)
        wr_lo = (wr - wr_hi.astype(F32)).astype(BF16)
        rb = jnp.concatenate([router_g_b[l], router_e_b[l], jnp.zeros((LANE - MOE_GROUPS - N_EXPERTS,), F32)])
        x1, x1_tok, route = _ln_route(y, row(ln1_g[l]), row(ln1_b[l]), wr_hi, wr_lo, row(rb))

        src_token, dst_row, w_slot, tile_expert, next_expert, tile_valid, n_tiles = _dispatch_plan(route, s)
        next_expert = jnp.where(next_expert < N_EXPERTS, next_expert + l * N_EXPERTS, -1)
        y_tok = _moe(tile_expert + l * N_EXPERTS, next_expert, tile_valid, n_tiles, src_token, dst_row, x1_tok,
                     wg_all, wu_all, wd_all, w_slot)
        xc = _ln_add(x1, y_tok, row(ln2_g[l]), row(ln2_b[l]))
    return xc.reshape(bsz, s, D_MODEL)
```
